```python
import math
import jax, jax.numpy as jnp
from jax import lax
import numpy as np

D_MODEL = 2048
BATCH = 4
SEQ = 2048
DEPTH = 4
DEC_BATCH = 32
DEC_SEQ = 1
PAST_LEN = 16384
PAGE_SIZE = 128

EPS = 1e-6
NEG_INF = -1e30
MIX_WIDTH = D_MODEL
A_GROUPS = 4
A_DIM = 128
A_WIDTH = A_GROUPS * A_DIM
CHUNK = 128
B_HEADS = 16
B_KV_HEADS = 2
B_HEAD_DIM = 64
GQA = B_HEADS // B_KV_HEADS
B_WIDTH = B_HEADS * B_HEAD_DIM
B_KV_WIDTH = B_KV_HEADS * B_HEAD_DIM
WINDOW = 128
N_BUCKETS = 32
MAX_DISTANCE = 128
C_HEADS = 4
C_QK_DIM = 64
C_V_DIM = 128
C_QK_WIDTH = C_HEADS * C_QK_DIM
C_WIDTH = C_HEADS * C_V_DIM
MLSTM_CHUNK = 64
IN_SPLITS = (A_WIDTH, A_WIDTH, B_WIDTH, B_KV_WIDTH, B_KV_WIDTH,
             C_QK_WIDTH, C_QK_WIDTH, C_WIDTH, C_WIDTH, C_HEADS, C_HEADS)
IN_WIDTH = sum(IN_SPLITS)
IN_OFFSETS = tuple(int(o) for o in np.cumsum(IN_SPLITS)[:-1])
D_FF = 5632
CONV_W = 3

kernel_name = "hybrid_sgu_swa_mlstm_convffn_step"


def rmsnorm(x, g):
    xf = x.astype(jnp.float32)
    y = xf * lax.rsqrt(jnp.mean(xf * xf, axis=-1, keepdims=True) + EPS)
    return (y * g.astype(jnp.float32)).astype(x.dtype)


def t5_bucket(dist):
    n = jnp.maximum(dist, 0)
    max_exact = N_BUCKETS // 2
    nf = jnp.maximum(n, 1).astype(jnp.float32)
    large = max_exact + (jnp.log(nf / max_exact) / math.log(MAX_DISTANCE / max_exact)
                         * (N_BUCKETS - max_exact)).astype(jnp.int32)
    return jnp.where(n < max_exact, n, jnp.minimum(large, N_BUCKETS - 1))


def rel_bias_logits(rel_bias, dist):
    b = jnp.take(rel_bias, t5_bucket(dist), axis=0).astype(jnp.float32)
    return jnp.moveaxis(b, -1, 0).reshape(B_KV_HEADS, GQA, *dist.shape)


def sink_attn(q, k, v, bias, valid, sink):
    s = jnp.einsum('...qhgd,...shd->...hgqs', q, k,
                   preferred_element_type=jnp.float32) * (B_HEAD_DIM ** -0.5) + bias
    s = jnp.where(valid, s, NEG_INF)
    sk = sink.astype(jnp.float32).reshape(B_KV_HEADS, GQA, 1, 1)
    mx = jnp.maximum(jnp.max(s, axis=-1, keepdims=True), sk)
    p = jnp.exp(s - mx)
    p = p / (jnp.sum(p, axis=-1, keepdims=True) + jnp.exp(sk - mx))
    return jnp.einsum('...hgqs,...shd->...qhgd', p.astype(v.dtype), v)


def swa_prompt(q, k, v, rel_bias, sink):
    Bn, T = q.shape[:2]
    nb = T // WINDOW
    qb = q.reshape(Bn, nb, WINDOW, B_KV_HEADS, GQA, B_HEAD_DIM)

    def band(a):
        ap = jnp.pad(a, ((0, 0), (WINDOW, 0), (0, 0), (0, 0))).reshape(
            Bn, nb + 1, WINDOW, B_KV_HEADS, B_HEAD_DIM)
        return jnp.concatenate([ap[:, :-1], ap[:, 1:]], axis=2)

    qi = jnp.arange(WINDOW)[:, None]
    kj = jnp.arange(2 * WINDOW)[None, :]
    dist = WINDOW + qi - kj
    kpos = jnp.arange(nb)[:, None, None] * WINDOW - WINDOW + kj
    valid = (dist >= 0) & (dist < WINDOW) & (kpos >= 0)
    o = sink_attn(qb, band(k), band(v), rel_bias_logits(rel_bias, dist),
                  valid[:, None, None], sink)
    return o.reshape(Bn, T, B_WIDTH)


def swa_sample(q, k, v, k_past, v_past, rel_bias, sink):
    Bn, T = q.shape[:2]
    kc = jnp.concatenate([k_past.astype(k.dtype), k], axis=1)
    vc = jnp.concatenate([v_past.astype(v.dtype), v], axis=1)
    qi = jnp.arange(T)[:, None]
    kpos = jnp.concatenate([jnp.arange(WINDOW) - WINDOW, jnp.arange(T)])[None, :]
    dist = qi - kpos
    valid = (dist >= 0) & (dist < WINDOW)
    o = sink_attn(q.reshape(Bn, T, B_KV_HEADS, GQA, B_HEAD_DIM), kc, vc,
                  rel_bias_logits(rel_bias, dist), valid, sink)
    return o.reshape(Bn, T, B_WIDTH)


def chunk_mlp(u, v, v_gain, w_s, b_s):
    Bn, T, _ = u.shape
    L = min(T, CHUNK)
    nc = T // L
    vn = rmsnorm(v.reshape(Bn, T, A_GROUPS, A_DIM), v_gain.reshape(A_GROUPS, A_DIM))
    ws = jnp.tril(w_s[:, :L, :L])
    mixv = jnp.einsum('gts,bnsgd->bntgd', ws, vn.reshape(Bn, nc, L, A_GROUPS, A_DIM)) \
        + jnp.transpose(b_s[:, :L])[:, :, None]
    out = u.reshape(Bn, nc, L, A_GROUPS, A_DIM) * mixv
    return out.reshape(Bn, T, A_WIDTH), vn.reshape(Bn, T, A_WIDTH)


def mlstm(q, k, v, ig, fg, C0, n0, m0):
    f32 = jnp.float32
    q, k, v, ig = q.astype(f32), k.astype(f32), v.astype(f32), ig.astype(f32)
    Bn, T, H, dk = q.shape
    dv = v.shape[-1]
    k = k * (dk ** -0.5)
    logf = jax.nn.log_sigmoid(fg.astype(f32))
    L = math.gcd(T, MLSTM_CHUNK)
    nc = T // L

    def chunks(a):
        return jnp.moveaxis(a.reshape(Bn, nc, L, *a.shape[2:]), 1, 0)

    causal = jnp.tril(jnp.ones((L, L), dtype=bool))[None, :, :, None]

    def step(carry, xs):
        C, n, m = carry
        qc, kc, vc, ic, lfc = xs
        b = jnp.cumsum(lfc, axis=1)
        dlog = b[:, :, None, :] - b[:, None, :, :] + ic[:, None, :, :]
        dlog = jnp.where(causal, dlog, -jnp.inf)
        inter = b + m[:, None, :]
        mt = jnp.maximum(inter, jnp.max(dlog, axis=2))
        s = jnp.einsum('bthd,bshd->btsh', qc, kc) * jnp.exp(dlog - mt[:, :, None, :])
        iw = jnp.exp(inter - mt)
        num = iw[..., None] * jnp.einsum('bthd,bhde->bthe', qc, C) \
            + jnp.einsum('btsh,bshe->bthe', s, vc)
        qn = iw * jnp.einsum('bthd,bhd->bth', qc, n) + jnp.sum(s, axis=2)
        h = num / jnp.maximum(jnp.abs(qn), jnp.exp(-mt))[..., None]
        bL = b[:, -1]
        m_new = mt[:, -1]
        wc = jnp.exp(bL[:, None] - b + ic - m_new[:, None])
        decay = jnp.exp(bL + m - m_new)
        C = decay[..., None, None] * C + jnp.einsum('bsh,bshd,bshe->bhde', wc, kc, vc)
        n = decay[..., None] * n + jnp.einsum('bsh,bshd->bhd', wc, kc)
        return (C, n, m_new), h

    (C, n, m), hs = lax.scan(step, (C0.astype(f32), n0.astype(f32), m0.astype(f32)),
                             (chunks(q), chunks(k), chunks(v), chunks(ig), chunks(logf)))
    h = jnp.moveaxis(hs, 0, 1).reshape(Bn, T, H, dv)
    return h, C, n, m


def conv_ffn(h, w_up, conv_w, conv_b, w_down, buf):
    z = h @ w_up
    T = z.shape[1]
    zp = jnp.concatenate([buf.astype(z.dtype), z], axis=1)
    zc = conv_b + sum(zp[:, j:j + T] * conv_w[j] for j in range(CONV_W))
    g, a = jnp.split(zc, 2, axis=-1)
    y = (jax.nn.silu(g) * a) @ w_down
    return y, zp[:, -(CONV_W - 1):]


def trunk_layer(x, norm1, w_in, a_v_gain, a_spatial_w, a_spatial_b, b_q_gain, b_k_gain,
                b_sinks, c_gate_bias, c_h_gain, w_out, norm2, w_up, ffn_conv_w, ffn_conv_b,
                w_down, rel_bias, k_past, v_past, C0, n0, m0, conv_buf):
    Bn, T, _ = x.shape
    dt = x.dtype
    h = rmsnorm(x, norm1)
    z = h @ w_in
    au, av, bq, bk, bv, cq, ck, cv, co, ci, cf = jnp.split(z, IN_OFFSETS, axis=-1)
    a_out, a_vrows = chunk_mlp(jax.nn.gelu(au, approximate=False), jax.nn.gelu(av, approximate=False),
                               a_v_gain, a_spatial_w, a_spatial_b)
    q = rmsnorm(bq.reshape(Bn, T, B_HEADS, B_HEAD_DIM), b_q_gain)
    k = rmsnorm(bk.reshape(Bn, T, B_KV_HEADS, B_HEAD_DIM), b_k_gain)
    v = bv.reshape(Bn, T, B_KV_HEADS, B_HEAD_DIM)
    if k_past is None:
        b_out = swa_prompt(q, k, v, rel_bias, b_sinks)
        k_new, v_new = k[:, -WINDOW:], v[:, -WINDOW:]
    else:
        b_out = swa_sample(q, k, v, k_past, v_past, rel_bias, b_sinks)
        k_new, v_new = k, v
    ig = ci + c_gate_bias[:C_HEADS]
    fg = cf + c_gate_bias[C_HEADS:]
    ch, C1, n1, m1 = mlstm(cq.reshape(Bn, T, C_HEADS, C_QK_DIM), ck.reshape(Bn, T, C_HEADS, C_QK_DIM),
                           cv.reshape(Bn, T, C_HEADS, C_V_DIM), ig, fg, C0, n0, m0)
    ch = rmsnorm(ch, c_h_gain.reshape(C_HEADS, C_V_DIM)).reshape(Bn, T, C_WIDTH)
    c_out = ch * jax.nn.sigmoid(co.astype(jnp.float32))
    mix = jnp.concatenate([a_out.astype(dt), b_out.astype(dt), c_out.astype(dt)], axis=-1) @ w_out
    x = x + mix.astype(dt)
    f, new_buf = conv_ffn(rmsnorm(x, norm2), w_up, ffn_conv_w, ffn_conv_b, w_down, conv_buf)
    x = x + f.astype(dt)
    return x, a_vrows, k_new, v_new, C1, n1, m1, new_buf


def setup_inputs(seed: int = 0) -> dict:
    key = jax.random.key(seed)
    ks = jax.random.split(key, 32)
    nrm = lambda k, s, sc: jax.random.normal(k, s, jnp.float32) * sc
    F2 = 2 * D_FF
    gate_b = jnp.concatenate([nrm(ks[15], (DEPTH, C_HEADS), 0.1),
                              3.0 + nrm(ks[16], (DEPTH, C_HEADS), 0.5)], axis=-1)
    return {
        "x_prompt": nrm(ks[0], (BATCH, SEQ, D_MODEL), 1.0),
        "x_sample": nrm(ks[1], (DEC_BATCH, DEC_SEQ, D_MODEL), 1.0),
        "cache_swa_k": nrm(ks[2], (DEPTH, DEC_BATCH, WINDOW, B_KV_HEADS, B_HEAD_DIM), 1.0),
        "cache_swa_v": nrm(ks[3], (DEPTH, DEC_BATCH, WINDOW, B_KV_HEADS, B_HEAD_DIM), 1.0),
        "state_mlstm_C": nrm(ks[4], (DEPTH, DEC_BATCH, C_HEADS, C_QK_DIM, C_V_DIM), 0.1),
        "state_mlstm_n": nrm(ks[5], (DEPTH, DEC_BATCH, C_HEADS, C_QK_DIM), 0.1),
        "state_mlstm_m": nrm(ks[6], (DEPTH, DEC_BATCH, C_HEADS), 0.5),
        "state_ffn_conv": nrm(ks[7], (DEPTH, DEC_BATCH, CONV_W - 1, F2), 1.0),
        "rel_bias": nrm(ks[8], (N_BUCKETS, B_HEADS), 0.5),
        "norm1": 1.0 + nrm(ks[9], (DEPTH, D_MODEL), 0.02),
        "w_in": nrm(ks[10], (DEPTH, D_MODEL, IN_WIDTH), D_MODEL ** -0.5),
        "a_v_gain": 1.0 + nrm(ks[11], (DEPTH, A_WIDTH), 0.02),
        "a_spatial_w": nrm(ks[12], (DEPTH, A_GROUPS, CHUNK, CHUNK), 0.5 * CHUNK ** -0.5),
        "a_spatial_b": 1.0 + nrm(ks[13], (DEPTH, A_GROUPS, CHUNK), 0.1),
        "b_q_gain": 1.0 + nrm(ks[14], (DEPTH, B_HEAD_DIM), 0.02),
        "b_k_gain": 1.0 + nrm(ks[17], (DEPTH, B_HEAD_DIM), 0.02),
        "b_sinks": nrm(ks[18], (DEPTH, B_HEADS), 0.5),
        "c_gate_bias": gate_b,
        "c_h_gain": 1.0 + nrm(ks[19], (DEPTH, C_WIDTH), 0.02),
        "w_out": nrm(ks[20], (DEPTH, MIX_WIDTH, D_MODEL), MIX_WIDTH ** -0.5),
        "norm2": 1.0 + nrm(ks[21], (DEPTH, D_MODEL), 0.02),
        "w_up": nrm(ks[22], (DEPTH, D_MODEL, F2), D_MODEL ** -0.5),
        "ffn_conv_w": nrm(ks[23], (DEPTH, CONV_W, F2), CONV_W ** -0.5),
        "ffn_conv_b": nrm(ks[24], (DEPTH, F2), 0.02),
        "w_down": nrm(ks[25], (DEPTH, D_FF, D_MODEL), D_FF ** -0.5),
    }


def reference(x_prompt, x_sample, cache_swa_k, cache_swa_v, state_mlstm_C, state_mlstm_n,
              state_mlstm_m, state_ffn_conv, rel_bias, norm1, w_in, a_v_gain, a_spatial_w,
              a_spatial_b, b_q_gain, b_k_gain, b_sinks, c_gate_bias, c_h_gain, w_out, norm2,
              w_up, ffn_conv_w, ffn_conv_b, w_down):
    Bp = x_prompt.shape[0]
    C0p = jnp.zeros((Bp, C_HEADS, C_QK_DIM, C_V_DIM), jnp.float32)
    n0p = jnp.zeros((Bp, C_HEADS, C_QK_DIM), jnp.float32)
    m0p = jnp.zeros((Bp, C_HEADS), jnp.float32)
    buf0p = jnp.zeros((Bp, CONV_W - 1, 2 * D_FF), x_prompt.dtype)
    xp, xs = x_prompt, x_sample
    P = [[] for _ in range(7)]
    S = [[] for _ in range(7)]
    for l in range(DEPTH):
        w = (norm1[l], w_in[l], a_v_gain[l], a_spatial_w[l], a_spatial_b[l], b_q_gain[l],
             b_k_gain[l], b_sinks[l], c_gate_bias[l], c_h_gain[l], w_out[l], norm2[l], w_up[l],
             ffn_conv_w[l], ffn_conv_b[l], w_down[l], rel_bias)
        outp = trunk_layer(xp, *w, None, None, C0p, n0p, m0p, buf0p)
        outs = trunk_layer(xs, *w, cache_swa_k[l], cache_swa_v[l], state_mlstm_C[l],
                           state_mlstm_n[l], state_mlstm_m[l], state_ffn_conv[l])
        xp, xs = outp[0], outs[0]
        for i in range(7):
            P[i].append(outp[i + 1])
            S[i].append(outs[i + 1])
    st = lambda lst: jnp.stack(lst, axis=0)
    return (xp, xs,
            st(P[1]), st(P[2]), st(S[1]), st(S[2]),
            st(P[3]), st(P[4]), st(P[5]),
            st(S[3]), st(S[4]), st(S[5]),
            st(P[6]), st(S[6]),
            st(S[0]))
```

```python
import functools
import math

import numpy as np
import jax
import jax.numpy as jnp
from jax import lax
from jax.experimental import pallas as pl
from jax.experimental.pallas import tpu as pltpu

f32 = jnp.float32
bf16 = jnp.bfloat16

D_MODEL = 2048
EPS = 1e-6
NEG_INF = -1e30
SQRT_HALF = 0.7071067811865476

A_GROUPS = 4
A_DIM = 128
A_WIDTH = 512
CHUNK = 128
B_HEADS = 16
B_KV_HEADS = 2
B_HEAD_DIM = 64
GQA = 8
B_WIDTH = 1024
B_KV_WIDTH = 128
WINDOW = 128
N_BUCKETS = 32
MAX_DISTANCE = 128
C_HEADS = 4
C_QK_DIM = 64
C_V_DIM = 128
C_QK_WIDTH = 256
C_WIDTH = 512
IN_SPLITS = (A_WIDTH, A_WIDTH, B_WIDTH, B_KV_WIDTH, B_KV_WIDTH,
             C_QK_WIDTH, C_QK_WIDTH, C_WIDTH, C_WIDTH, C_HEADS, C_HEADS)
IN_OFFSETS = tuple(int(o) for o in np.cumsum(IN_SPLITS)[:-1])
D_FF = 5632
CONV_W = 3

Z_WIDTH = 4096
Z_AU, Z_AV, Z_BQ, Z_CV, Z_CO, Z_CQ, Z_CK, Z_BK, Z_BV, Z_GATE = (
    0, 512, 1024, 2048, 2560, 3072, 3328, 3584, 3712, 3840)

MLSTM_L = 128
LANES = 128
SUBLANES = 8
VMEM_LIMIT = 52 * 1024 * 1024


def _cparams(sem):
    return pltpu.CompilerParams(dimension_semantics=sem, vmem_limit_bytes=VMEM_LIMIT)


def _gelu(x):
    return 0.5 * x * (1.0 + lax.erf(x * SQRT_HALF))


def _rms(x, gain):
    return x * lax.rsqrt(jnp.mean(x * x, axis=-1, keepdims=True) + EPS) * gain


def _norm_head_pairs(x, gain2):
    lo = lax.broadcasted_iota(jnp.int32, x.shape, 1) < B_HEAD_DIM
    x2 = x * x
    s_lo = jnp.sum(jnp.where(lo, x2, 0.0), axis=-1, keepdims=True)
    s_hi = jnp.sum(jnp.where(lo, 0.0, x2), axis=-1, keepdims=True)
    ms = jnp.where(lo, s_lo, s_hi) * (1.0 / B_HEAD_DIM)
    return x * lax.rsqrt(ms + EPS) * gain2


def _t5_bucket_np(dist):
    n = np.maximum(dist, 0)
    max_exact = N_BUCKETS // 2
    nf = np.maximum(n, 1).astype(np.float32)
    large = max_exact + (np.log(nf / np.float32(max_exact)) / np.float32(math.log(MAX_DISTANCE / max_exact))
                         * np.float32(N_BUCKETS - max_exact)).astype(np.int32)
    return np.where(n < max_exact, n, np.minimum(large, N_BUCKETS - 1)).astype(np.int32)


def _bias_kernel(rb_ref, bkp_ref, bks_ref, op_ref, os_ref):
    bkp = bkp_ref[...]
    bks = bks_ref[...]
    for h in range(B_HEADS):
        accp = jnp.zeros(bkp.shape, f32)
        accs = jnp.zeros(bks.shape, f32)
        for b in range(N_BUCKETS):
            val = rb_ref[b, h]
            accp = jnp.where(bkp == b, val, accp)
            accs = jnp.where(bks == b, val, accs)
        op_ref[h] = accp
        os_ref[h:h + 1, :] = accs[0:1, :]


def _bias_tables(rel_bias):
    qi = np.arange(WINDOW)[:, None]
    kj = np.arange(2 * WINDOW)[None, :]
    bkp = _t5_bucket_np(WINDOW + qi - kj)
    j = np.arange(2 * WINDOW)
    dist_s = np.where(j < WINDOW, WINDOW - j, 0)
    bks = np.broadcast_to(_t5_bucket_np(dist_s)[None, :], (SUBLANES, 2 * WINDOW)).copy()
    return pl.pallas_call(
        _bias_kernel,
        out_shape=(jax.ShapeDtypeStruct((B_HEADS, WINDOW, 2 * WINDOW), f32),
                   jax.ShapeDtypeStruct((B_HEADS, 2 * WINDOW), f32)),
        in_specs=[pl.BlockSpec(memory_space=pltpu.SMEM),
                  pl.BlockSpec(memory_space=pltpu.VMEM),
                  pl.BlockSpec(memory_space=pltpu.VMEM)],
        out_specs=(pl.BlockSpec(memory_space=pltpu.VMEM), pl.BlockSpec(memory_space=pltpu.VMEM)),
        name="bias_tables",
    )(rel_bias, jnp.asarray(bkp), jnp.asarray(bks))


def _mm_in_kernel(x_ref, g_ref, w_ref, z_ref, h_ref):
    @pl.when(pl.program_id(1) == 0)
    def _():
        h_ref[...] = _rms(x_ref[...], g_ref[...]).astype(bf16)

    z_ref[...] = jnp.dot(h_ref[...], w_ref[...], preferred_element_type=f32)


def _mm_in(x, gain, w, layer, tm, tn):
    M = x.shape[0]
    return pl.pallas_call(
        _mm_in_kernel,
        grid=(M // tm, Z_WIDTH // tn),
        in_specs=[pl.BlockSpec((tm, D_MODEL), lambda i, j: (i, 0)),
                  pl.BlockSpec((None, 1, D_MODEL), lambda i, j: (layer, 0, 0)),
                  pl.BlockSpec((None, D_MODEL, tn), lambda i, j: (layer, 0, j))],
        out_specs=pl.BlockSpec((tm, tn), lambda i, j: (i, j)),
        out_shape=jax.ShapeDtypeStruct((M, Z_WIDTH), f32),
        scratch_shapes=[pltpu.VMEM((tm, D_MODEL), bf16)],
        compiler_params=_cparams(("parallel", "arbitrary")),
        name="mm_in",
    )(x, gain, w)


def _mm_out_kernel(a_ref, b_ref, c_ref, x_ref, w_ref, o_ref, lhs_ref):
    @pl.when(pl.program_id(1) == 0)
    def _():
        lhs_ref[:, 0:A_WIDTH] = a_ref[...]
        lhs_ref[:, A_WIDTH:A_WIDTH + B_WIDTH] = b_ref[...]
        lhs_ref[:, A_WIDTH + B_WIDTH:D_MODEL] = c_ref[...]

    o_ref[...] = x_ref[...] + jnp.dot(lhs_ref[...], w_ref[...], preferred_element_type=f32)


def _mm_out(a, b, c, x, w, layer, tm, tn):
    M = x.shape[0]
    return pl.pallas_call(
        _mm_out_kernel,
        grid=(M // tm, D_MODEL // tn),
        in_specs=[pl.BlockSpec((tm, A_WIDTH), lambda i, j: (i, 0)),
                  pl.BlockSpec((tm, B_WIDTH), lambda i, j: (i, 0)),
                  pl.BlockSpec((tm, C_WIDTH), lambda i, j: (i, 0)),
                  pl.BlockSpec((tm, tn), lambda i, j: (i, j)),
                  pl.BlockSpec((None, D_MODEL, tn), lambda i, j: (layer, 0, j))],
        out_specs=pl.BlockSpec((tm, tn), lambda i, j: (i, j)),
        out_shape=jax.ShapeDtypeStruct((M, D_MODEL), f32),
        scratch_shapes=[pltpu.VMEM((tm, D_MODEL), bf16)],
        compiler_params=_cparams(("parallel", "arbitrary")),
        name="mm_out",
    )(a, b, c, x, w)


def _mm_down_kernel(a_ref, x_ref, w_ref, o_ref):
    o_ref[...] = x_ref[...] + jnp.dot(a_ref[...], w_ref[...], preferred_element_type=f32)


def _mm_down(act, x, w, layer, tm, tn):
    M = x.shape[0]
    return pl.pallas_call(
        _mm_down_kernel,
        grid=(M // tm, D_MODEL // tn),
        in_specs=[pl.BlockSpec((tm, D_FF), lambda i, j: (i, 0)),
                  pl.BlockSpec((tm, tn), lambda i, j: (i, j)),
                  pl.BlockSpec((None, D_FF, tn), lambda i, j: (layer, 0, j))],
        out_specs=pl.BlockSpec((tm, tn), lambda i, j: (i, j)),
        out_shape=jax.ShapeDtypeStruct((M, D_MODEL), f32),
        compiler_params=_cparams(("parallel", "arbitrary")),
        name="mm_down",
    )(act, x, w)


def _silu(x):
    return x * jax.nn.sigmoid(x)


def _mm_up_prompt_kernel(x_ref, n2_ref, wg_ref, wa_ref, cwg_ref, cwa_ref, cbg_ref, cba_ref,
                         act_ref, cs_ref, h_ref, carry_ref, zb_ref, *, tm, tiles_per_seq):
    i = pl.program_id(0)
    j = pl.program_id(1)

    @pl.when(j == 0)
    def _():
        h_ref[...] = _rms(x_ref[...], n2_ref[...]).astype(bf16)

    seq_start = (i % tiles_per_seq) == 0

    def conv_half(idx, w_ref, cw_ref, cb_ref):
        z = jnp.dot(h_ref[...], w_ref[...], preferred_element_type=f32)
        @pl.when(seq_start)
        def _():
            zb_ref[idx, 0:SUBLANES, :] = jnp.zeros((SUBLANES, z.shape[1]), f32)

        @pl.when(jnp.logical_not(seq_start))
        def _():
            zb_ref[idx, 0:SUBLANES, :] = carry_ref[idx, j]

        zb_ref[idx, SUBLANES:SUBLANES + tm, :] = z
        tail = z[tm - SUBLANES:tm, :]
        carry_ref[idx, j] = tail
        cs_ref[0, idx] = tail
        return (cb_ref[...]
                + zb_ref[idx, SUBLANES - 2:SUBLANES - 2 + tm, :] * cw_ref[0:1, :]
                + zb_ref[idx, SUBLANES - 1:SUBLANES - 1 + tm, :] * cw_ref[1:2, :]
                + z * cw_ref[2:3, :])

    g = conv_half(0, wg_ref, cwg_ref, cbg_ref)
    a = conv_half(1, wa_ref, cwa_ref, cba_ref)
    act_ref[...] = (_silu(g) * a).astype(bf16)


def _mm_up_prompt(x, norm2, w_up, conv_w, conv_b, layer, batch, tm, tf):
    M = x.shape[0]
    seq = M // batch
    tiles_per_seq = seq // tm
    nj = D_FF // tf
    kern = functools.partial(_mm_up_prompt_kernel, tm=tm, tiles_per_seq=tiles_per_seq)
    return pl.pallas_call(
        kern,
        grid=(M // tm, nj),
        in_specs=[pl.BlockSpec((tm, D_MODEL), lambda i, j: (i, 0)),
                  pl.BlockSpec((None, 1, D_MODEL), lambda i, j: (layer, 0, 0)),
                  pl.BlockSpec((None, D_MODEL, tf), lambda i, j: (layer, 0, j)),
                  pl.BlockSpec((None, D_MODEL, tf), lambda i, j: (layer, 0, nj + j)),
                  pl.BlockSpec((None, CONV_W, tf), lambda i, j: (layer, 0, j)),
                  pl.BlockSpec((None, CONV_W, tf), lambda i, j: (layer, 0, nj + j)),
                  pl.BlockSpec((None, 1, tf), lambda i, j: (layer, 0, j)),
                  pl.BlockSpec((None, 1, tf), lambda i, j: (layer, 0, nj + j))],
        out_specs=(pl.BlockSpec((tm, tf), lambda i, j: (i, j)),
                   pl.BlockSpec((1, 2, SUBLANES, tf), lambda i, j: (i, 0, 0, j))),
        out_shape=(jax.ShapeDtypeStruct((M, D_FF), bf16),
                   jax.ShapeDtypeStruct((M // tm, 2, SUBLANES, D_FF), f32)),
        scratch_shapes=[pltpu.VMEM((tm, D_MODEL), bf16),
                        pltpu.VMEM((2, nj, SUBLANES, tf), f32),
                        pltpu.VMEM((2, tm + SUBLANES, tf), f32)],
        compiler_params=_cparams(("arbitrary", "arbitrary")),
        name="mm_up_prompt",
    )(x, norm2, w_up, w_up, conv_w, conv_w, conv_b, conv_b)


def _mm_up_sample_kernel(x_ref, n2_ref, wg_ref, wa_ref, cwg_ref, cwa_ref, cbg_ref, cba_ref,
                         b0g_ref, b1g_ref, b0a_ref, b1a_ref, act_ref, zg_ref, za_ref, h_ref):
    @pl.when(pl.program_id(0) == 0)
    def _():
        h_ref[...] = _rms(x_ref[...], n2_ref[...]).astype(bf16)

    def conv_half(w_ref, cw_ref, cb_ref, b0_ref, b1_ref, z_ref):
        z = jnp.dot(h_ref[...], w_ref[...], preferred_element_type=f32)
        z_ref[...] = z
        return (cb_ref[...] + b0_ref[...] * cw_ref[0:1, :] + b1_ref[...] * cw_ref[1:2, :]
                + z * cw_ref[2:3, :])

    g = conv_half(wg_ref, cwg_ref, cbg_ref, b0g_ref, b1g_ref, zg_ref)
    a = conv_half(wa_ref, cwa_ref, cba_ref, b0a_ref, b1a_ref, za_ref)
    act_ref[...] = (_silu(g) * a).astype(bf16)


def _mm_up_sample(x, norm2, w_up, conv_w, conv_b, buf, layer, tf):
    M = x.shape[0]
    nj = D_FF // tf
    wspec = lambda off: pl.BlockSpec((None, D_MODEL, tf), lambda j: (layer, 0, off + j))
    cwspec = lambda off: pl.BlockSpec((None, CONV_W, tf), lambda j: (layer, 0, off + j))
    cbspec = lambda off: pl.BlockSpec((None, 1, tf), lambda j: (layer, 0, off + j))
    bufspec = lambda row, off: pl.BlockSpec((None, None, M, tf), lambda j: (layer, row, 0, off + j))
    return pl.pallas_call(
        _mm_up_sample_kernel,
        grid=(nj,),
        in_specs=[pl.BlockSpec((M, D_MODEL), lambda j: (0, 0)),
                  pl.BlockSpec((None, 1, D_MODEL), lambda j: (layer, 0, 0)),
                  wspec(0), wspec(nj), cwspec(0), cwspec(nj), cbspec(0), cbspec(nj),
                  bufspec(0, 0), bufspec(1, 0), bufspec(0, nj), bufspec(1, nj)],
        out_specs=(pl.BlockSpec((M, tf), lambda j: (0, j)),
                   pl.BlockSpec((M, tf), lambda j: (0, j)),
                   pl.BlockSpec((M, tf), lambda j: (0, j))),
        out_shape=(jax.ShapeDtypeStruct((M, D_FF), bf16),
                   jax.ShapeDtypeStruct((M, D_FF), f32),
                   jax.ShapeDtypeStruct((M, D_FF), f32)),
        scratch_shapes=[pltpu.VMEM((M, D_MODEL), bf16)],
        compiler_params=_cparams(("arbitrary",)),
        name="mm_up_sample",
    )(x, norm2, w_up, w_up, conv_w, conv_w, conv_b, conv_b, buf, buf, buf, buf)


def _mix_a_prompt_kernel(u_ref, v_ref, vg_ref, ws_ref, bs_ref, o_ref):
    row = lax.broadcasted_iota(jnp.int32, (CHUNK, CHUNK), 0)
    col = lax.broadcasted_iota(jnp.int32, (CHUNK, CHUNK), 1)
    tri = col <= row
    for g in range(A_GROUPS):
        sl = slice(g * A_DIM, (g + 1) * A_DIM)
        vn = _rms(_gelu(v_ref[:, sl]), vg_ref[:, sl])
        ws = jnp.where(tri, ws_ref[g], 0.0).astype(bf16)
        mixv = jnp.dot(ws, vn.astype(bf16), preferred_element_type=f32) + bs_ref[:, sl]
        o_ref[:, sl] = (_gelu(u_ref[:, sl]) * mixv).astype(bf16)


def _mix_a_prompt(z, v_gain, w_s, b_s_full, layer):
    M = z.shape[0]
    return pl.pallas_call(
        _mix_a_prompt_kernel,
        grid=(M // CHUNK,),
        in_specs=[pl.BlockSpec((CHUNK, A_WIDTH), lambda r: (r, Z_AU // A_WIDTH)),
                  pl.BlockSpec((CHUNK, A_WIDTH), lambda r: (r, Z_AV // A_WIDTH)),
                  pl.BlockSpec((None, 1, A_WIDTH), lambda r: (layer, 0, 0)),
                  pl.BlockSpec((None, A_GROUPS, CHUNK, CHUNK), lambda r: (layer, 0, 0, 0)),
                  pl.BlockSpec((None, CHUNK, A_WIDTH), lambda r: (layer, 0, 0))],
        out_specs=pl.BlockSpec((CHUNK, A_WIDTH), lambda r: (r, 0)),
        out_shape=jax.ShapeDtypeStruct((M, A_WIDTH), bf16),
        compiler_params=_cparams(("parallel",)),
        name="mix_a_prompt",
    )(z, z, v_gain, w_s, b_s_full)


def _mix_a_sample_kernel(u_ref, v_ref, vg_ref, ws_ref, bs_ref, o_ref, vn_ref):
    for g in range(A_GROUPS):
        sl = slice(g * A_DIM, (g + 1) * A_DIM)
        vn = _rms(_gelu(v_ref[:, sl]), vg_ref[:, sl])
        vn_ref[:, sl] = vn
        mixv = ws_ref[:, sl] * vn + bs_ref[:, sl]
        o_ref[:, sl] = (_gelu(u_ref[:, sl]) * mixv).astype(bf16)


def _mix_a_sample(z, v_gain, ws0, bs0, layer):
    M = z.shape[0]
    vec = pl.BlockSpec((None, 1, A_WIDTH), lambda r: (layer, 0, 0))
    return pl.pallas_call(
        _mix_a_sample_kernel,
        grid=(1,),
        in_specs=[pl.BlockSpec((M, A_WIDTH), lambda r: (0, Z_AU // A_WIDTH)),
                  pl.BlockSpec((M, A_WIDTH), lambda r: (0, Z_AV // A_WIDTH)),
                  vec, vec, vec],
        out_specs=(pl.BlockSpec((M, A_WIDTH), lambda r: (0, 0)),
                   pl.BlockSpec((M, A_WIDTH), lambda r: (0, 0))),
        out_shape=(jax.ShapeDtypeStruct((M, A_WIDTH), bf16),
                   jax.ShapeDtypeStruct((M, A_WIDTH), f32)),
        compiler_params=_cparams(("arbitrary",)),
        name="mix_a_sample",
    )(z, z, v_gain, ws0, bs0)


def _mix_b_prompt_kernel(sink_ref, q_ref, k_ref, v_ref, qg_ref, kg_ref, bias_ref,
                         o_ref, klast_ref, vlast_ref, kcat_ref, vcat_ref):
    i = pl.program_id(1)
    W = WINDOW
    kn = _norm_head_pairs(k_ref[...], kg_ref[...])
    v = v_ref[...]
    klast_ref[0] = kn
    vlast_ref[0] = v

    @pl.when(i == 0)
    def _():
        kcat_ref[0:W, :] = jnp.zeros((W, LANES), f32)
        vcat_ref[0:W, :] = jnp.zeros((W, LANES), f32)

    @pl.when(i > 0)
    def _():
        kcat_ref[0:W, :] = kcat_ref[W:2 * W, :]
        vcat_ref[0:W, :] = vcat_ref[W:2 * W, :]

    kcat_ref[W:2 * W, :] = kn
    vcat_ref[W:2 * W, :] = v
    kc = kcat_ref[...]
    vc = vcat_ref[...]
    kc_sw = pltpu.roll(kc, B_HEAD_DIM, 1)
    vc_sw = pltpu.roll(vc, B_HEAD_DIM, 1)
    lo = lax.broadcasted_iota(jnp.int32, (2 * W, LANES), 1) < B_HEAD_DIM

    qi = lax.broadcasted_iota(jnp.int32, (W, 2 * W), 0)
    kj = lax.broadcasted_iota(jnp.int32, (W, 2 * W), 1)
    dist = W + qi - kj
    valid = (dist >= 0) & (dist < W) & ((kj >= W) | (i > 0))

    tiles = GQA // 2
    for kvh in range(B_KV_HEADS):
        k_src, k_alt = (kc, kc_sw) if kvh == 0 else (kc_sw, kc)
        v_src, v_alt = (vc, vc_sw) if kvh == 0 else (vc_sw, vc)
        k_even = jnp.where(lo, k_src, 0.0).astype(bf16)
        k_odd = jnp.where(lo, 0.0, k_alt).astype(bf16)
        v_even = jnp.where(lo, v_src, 0.0).astype(bf16)
        v_odd = jnp.where(lo, 0.0, v_alt).astype(bf16)
        qs = jnp.concatenate(
            [_norm_head_pairs(q_ref[:, (kvh * tiles + t) * LANES:(kvh * tiles + t + 1) * LANES], qg_ref[...])
             for t in range(tiles)], axis=0).astype(bf16)
        dn = (((1,), (1,)), ((), ()))
        s_par = [lax.dot_general(qs, k_even, dn, preferred_element_type=f32),
                 lax.dot_general(qs, k_odd, dn, preferred_element_type=f32)]
        p_par = []
        for par in range(2):
            blocks = []
            for t in range(tiles):
                h = kvh * GQA + 2 * t + par
                s = s_par[par][t * W:(t + 1) * W, :] * (B_HEAD_DIM ** -0.5) + bias_ref[h]
                s = jnp.where(valid, s, NEG_INF)
                sk = sink_ref[h]
                mx = jnp.maximum(jnp.max(s, axis=-1, keepdims=True), sk)
                p = jnp.exp(s - mx)
                den = jnp.sum(p, axis=-1, keepdims=True) + jnp.exp(sk - mx)
                blocks.append((p * (1.0 / den)).astype(bf16))
            p_par.append(jnp.concatenate(blocks, axis=0))
        o = (jnp.dot(p_par[0], v_even, preferred_element_type=f32)
             + jnp.dot(p_par[1], v_odd, preferred_element_type=f32))
        for t in range(tiles):
            c0 = (kvh * tiles + t) * LANES
            o_ref[:, c0:c0 + LANES] = o[t * W:(t + 1) * W, :].astype(bf16)


def _mix_b_prompt(z, sinks, q_gain2, k_gain2, bias_p, layer, batch):
    M = z.shape[0]
    nb = M // batch // WINDOW
    return pl.pallas_call(
        _mix_b_prompt_kernel,
        grid=(batch, nb),
        in_specs=[pl.BlockSpec(memory_space=pltpu.SMEM),
                  pl.BlockSpec((WINDOW, B_WIDTH), lambda b, i: (b * nb + i, Z_BQ // B_WIDTH)),
                  pl.BlockSpec((WINDOW, B_KV_WIDTH), lambda b, i: (b * nb + i, Z_BK // B_KV_WIDTH)),
                  pl.BlockSpec((WINDOW, B_KV_WIDTH), lambda b, i: (b * nb + i, Z_BV // B_KV_WIDTH)),
                  pl.BlockSpec((None, 1, LANES), lambda b, i: (layer, 0, 0)),
                  pl.BlockSpec((None, 1, LANES), lambda b, i: (layer, 0, 0)),
                  pl.BlockSpec((B_HEADS, WINDOW, 2 * WINDOW), lambda b, i: (0, 0, 0))],
        out_specs=(pl.BlockSpec((WINDOW, B_WIDTH), lambda b, i: (b * nb + i, 0)),
                   pl.BlockSpec((1, WINDOW, B_KV_WIDTH), lambda b, i: (b, 0, 0)),
                   pl.BlockSpec((1, WINDOW, B_KV_WIDTH), lambda b, i: (b, 0, 0))),
        out_shape=(jax.ShapeDtypeStruct((M, B_WIDTH), bf16),
                   jax.ShapeDtypeStruct((batch, WINDOW, B_KV_WIDTH), f32),
                   jax.ShapeDtypeStruct((batch, WINDOW, B_KV_WIDTH), f32)),
        scratch_shapes=[pltpu.VMEM((2 * WINDOW, LANES), f32),
                        pltpu.VMEM((2 * WINDOW, LANES), f32)],
        compiler_params=_cparams(("arbitrary", "arbitrary")),
        name="mix_b_prompt",
    )(sinks, z, z, z, q_gain2, k_gain2, bias_p)


def _mix_c_prompt_kernel(q_ref, k_ref, v_ref, og_ref, g_ref, gb_ref, hg_ref,
                         out_ref, c_out, n_out, m_out, c_s, n_s, m_s):
    L = MLSTM_L
    c = pl.program_id(1)

    @pl.when(c == 0)
    def _():
        c_s[...] = jnp.zeros(c_s.shape, f32)
        n_s[...] = jnp.zeros(n_s.shape, f32)
        m_s[...] = jnp.zeros(m_s.shape, f32)

    g = g_ref[...] + gb_ref[...]
    lf = jax.nn.log_sigmoid(g)
    row = lax.broadcasted_iota(jnp.int32, (L, L), 0)
    col = lax.broadcasted_iota(jnp.int32, (L, L), 1)
    tri = col <= row
    b = jnp.dot(tri.astype(f32), lf, precision=lax.Precision.HIGHEST, preferred_element_type=f32)
    bT = b.T
    gT = g.T
    for h in range(C_HEADS):
        b_col = b[:, C_HEADS + h:C_HEADS + h + 1]
        i_col = g[:, h:h + 1]
        b_row = bT[C_HEADS + h:C_HEADS + h + 1, :]
        i_row = gT[h:h + 1, :]
        dlog = jnp.where(tri, b_col - b_row + i_row, -jnp.inf)
        m_prev = m_s[h][0:1, 0:1]
        inter = b_col + m_prev
        mt = jnp.maximum(inter, jnp.max(dlog, axis=-1, keepdims=True))
        qh = q_ref[:, h * C_QK_DIM:(h + 1) * C_QK_DIM]
        kh = k_ref[:, h * C_QK_DIM:(h + 1) * C_QK_DIM] * (C_QK_DIM ** -0.5)
        vh = v_ref[:, h * C_V_DIM:(h + 1) * C_V_DIM]
        qb, kb, vb = qh.astype(bf16), kh.astype(bf16), vh.astype(bf16)
        s = lax.dot_general(qb, kb, (((1,), (1,)), ((), ())), preferred_element_type=f32) * jnp.exp(dlog - mt)
        iw = jnp.exp(inter - mt)
        c_prev = c_s[h]
        n_prev = n_s[h]
        num = (iw * jnp.dot(qb, c_prev.astype(bf16), preferred_element_type=f32)
               + jnp.dot(s.astype(bf16), vb, preferred_element_type=f32))
        qn = iw * jnp.sum(qh * n_prev, axis=-1, keepdims=True) + jnp.sum(s, axis=-1, keepdims=True)
        hh = num / jnp.maximum(jnp.abs(qn), jnp.exp(-mt))
        m_new = mt[L - 1:L, :]
        b_last = b_col[L - 1:L, :]
        wc = jnp.exp(b_last - b_col + i_col - m_new)
        decay = jnp.exp(b_last + m_prev - m_new)
        kw = kh * wc
        c_s[h] = decay * c_prev + lax.dot_general(kw.astype(bf16), vb, (((0,), (0,)), ((), ())),
                                                  preferred_element_type=f32)
        n_s[h] = decay * n_prev + jnp.sum(kw, axis=0, keepdims=True)
        m_s[h] = jnp.broadcast_to(m_new, (SUBLANES, LANES))
        vsl = slice(h * C_V_DIM, (h + 1) * C_V_DIM)
        out_ref[:, vsl] = (_rms(hh, hg_ref[:, vsl]) * jax.nn.sigmoid(og_ref[:, vsl])).astype(bf16)

    @pl.when(c == pl.num_programs(1) - 1)
    def _():
        c_out[0] = c_s[...]
        n_out[0] = n_s[...]
        m_out[0] = m_s[...]


def _mix_c_prompt(z, gate_bias, h_gain, layer, batch):
    M = z.shape[0]
    L = MLSTM_L
    nc = M // batch // L
    return pl.pallas_call(
        _mix_c_prompt_kernel,
        grid=(batch, nc),
        in_specs=[pl.BlockSpec((L, C_QK_WIDTH), lambda b, c: (b * nc + c, Z_CQ // C_QK_WIDTH)),
                  pl.BlockSpec((L, C_QK_WIDTH), lambda b, c: (b * nc + c, Z_CK // C_QK_WIDTH)),
                  pl.BlockSpec((L, C_WIDTH), lambda b, c: (b * nc + c, Z_CV // C_WIDTH)),
                  pl.BlockSpec((L, C_WIDTH), lambda b, c: (b * nc + c, Z_CO // C_WIDTH)),
                  pl.BlockSpec((L, LANES), lambda b, c: (b * nc + c, Z_GATE // LANES)),
                  pl.BlockSpec((None, 1, LANES), lambda b, c: (layer, 0, 0)),
                  pl.BlockSpec((None, 1, C_WIDTH), lambda b, c: (layer, 0, 0))],
        out_specs=(pl.BlockSpec((L, C_WIDTH), lambda b, c: (b * nc + c, 0)),
                   pl.BlockSpec((1, C_HEADS, C_QK_DIM, C_V_DIM), lambda b, c: (b, 0, 0, 0)),
                   pl.BlockSpec((1, C_HEADS, 1, C_QK_DIM), lambda b, c: (b, 0, 0, 0)),
                   pl.BlockSpec((1, C_HEADS, SUBLANES, LANES), lambda b, c: (b, 0, 0, 0))),
        out_shape=(jax.ShapeDtypeStruct((M, C_WIDTH), bf16),
                   jax.ShapeDtypeStruct((batch, C_HEADS, C_QK_DIM, C_V_DIM), f32),
                   jax.ShapeDtypeStruct((batch, C_HEADS, 1, C_QK_DIM), f32),
                   jax.ShapeDtypeStruct((batch, C_HEADS, SUBLANES, LANES), f32)),
        scratch_shapes=[pltpu.VMEM((C_HEADS, C_QK_DIM, C_V_DIM), f32),
                        pltpu.VMEM((C_HEADS, 1, C_QK_DIM), f32),
                        pltpu.VMEM((C_HEADS, SUBLANES, LANES), f32)],
        compiler_params=_cparams(("arbitrary", "arbitrary")),
        name="mix_c_prompt",
    )(z, z, z, z, z, gate_bias, h_gain)


def _mix_bc_sample_kernel(zr_ref, q_ref, kp_ref, vp_ref, qg_ref, kg_ref, sink_ref, bias_ref,
                          cq_ref, ck_ref, cv_ref, co_ref, gb_ref, hg_ref, c0_ref, n0_ref, m0_ref,
                          bo_ref, kn_ref, co_out, c_out, n_out, m_out):
    W = WINDOW
    zr = zr_ref[0]

    qn = _rms(q_ref[0], qg_ref[...])
    hrow = lax.broadcasted_iota(jnp.int32, (B_HEADS, LANES), 0)
    lane = lax.broadcasted_iota(jnp.int32, (B_HEADS, LANES), 1)
    own = (hrow < GQA) == (lane < B_HEAD_DIM)
    q2 = jnp.where(own, jnp.concatenate([qn, qn], axis=-1), 0.0)
    k_new = _norm_head_pairs(zr[:, Z_BK:Z_BK + B_KV_WIDTH], kg_ref[...])
    v_new = zr[:, Z_BV:Z_BV + B_KV_WIDTH]
    kn_ref[0] = k_new
    scale = B_HEAD_DIM ** -0.5
    s_past = lax.dot_general(q2.astype(bf16), kp_ref[0].astype(bf16), (((1,), (1,)), ((), ())),
                             preferred_element_type=f32) * scale + bias_ref[:, 0:W]
    s_past = jnp.where(lane >= 1, s_past, NEG_INF)
    s_new = jnp.sum(q2 * k_new, axis=-1, keepdims=True) * scale + bias_ref[:, W:W + 1]
    sk = sink_ref[...]
    mx = jnp.maximum(jnp.maximum(jnp.max(s_past, axis=-1, keepdims=True), s_new), sk)
    p_past = jnp.exp(s_past - mx)
    p_new = jnp.exp(s_new - mx)
    den = jnp.sum(p_past, axis=-1, keepdims=True) + p_new + jnp.exp(sk - mx)
    inv = 1.0 / den
    o2 = (jnp.dot((p_past * inv).astype(bf16), vp_ref[0].astype(bf16), preferred_element_type=f32)
          + (p_new * inv) * v_new)
    o2_sw = pltpu.roll(o2, B_HEAD_DIM, 1)
    bo_ref[0] = jnp.where(hrow < GQA, o2, o2_sw)[:, 0:B_HEAD_DIM].astype(bf16)

    g = zr[:, Z_GATE:Z_GATE + LANES] + gb_ref[...]
    lf = jax.nn.log_sigmoid(g)
    m0 = m0_ref[0]
    lane1 = lax.broadcasted_iota(jnp.int32, (1, LANES), 1)
    m_row = jnp.zeros((1, LANES), f32)
    row8 = lax.broadcasted_iota(jnp.int32, (SUBLANES, C_QK_DIM), 0)
    c_rows = []
    for h in range(C_HEADS):
        ig = g[:, h:h + 1]
        b = lf[:, C_HEADS + h:C_HEADS + h + 1]
        m_prev = m0[:, h:h + 1]
        inter = b + m_prev
        mt = jnp.maximum(inter, ig)
        qh = cq_ref[0, h:h + 1, :]
        kh = ck_ref[0, h:h + 1, :] * (C_QK_DIM ** -0.5)
        vh = cv_ref[0, h:h + 1, :]
        c_prev = c0_ref[0, h]
        n_prev = n0_ref[0, h:h + 1, :]
        s = jnp.sum(qh * kh, axis=-1, keepdims=True) * jnp.exp(ig - mt)
        iw = jnp.exp(inter - mt)
        q8 = jnp.broadcast_to(qh, (SUBLANES, C_QK_DIM)).astype(bf16)
        qc = jnp.dot(q8, c_prev.astype(bf16), preferred_element_type=f32)[0:1, :]
        num = iw * qc + s * vh
        qn_ = iw * jnp.sum(qh * n_prev, axis=-1, keepdims=True) + s
        hh = num / jnp.maximum(jnp.abs(qn_), jnp.exp(-mt))
        wc = jnp.exp(ig - mt)
        decay = jnp.exp(inter - mt)
        kw = kh * wc
        kw8 = jnp.where(row8 == 0, jnp.broadcast_to(kw, (SUBLANES, C_QK_DIM)), 0.0).astype(bf16)
        v8 = jnp.broadcast_to(vh, (SUBLANES, C_V_DIM)).astype(bf16)
        c_out[0, h] = decay * c_prev + lax.dot_general(kw8, v8, (((0,), (0,)), ((), ())),
                                                      preferred_element_type=f32)
        n_out[0, h:h + 1, :] = decay * n_prev + kw
        m_row = jnp.where(lane1 == h, mt, m_row)
        c_rows.append(_rms(hh, hg_ref[h:h + 1, :]) * jax.nn.sigmoid(co_ref[0, h:h + 1, :]))
    co_out[0] = jnp.concatenate(c_rows, axis=0).astype(bf16)
    m_out[0] = m_row


def _mix_bc_sample(z, k_past, v_past, q_gain, k_gain2, sinks, bias_s, gate_bias, h_gain4,
                   c0, n0, m0, layer):
    R = z.shape[0]
    zr = z.reshape(R, 1, Z_WIDTH)
    q = z[:, Z_BQ:Z_BQ + B_WIDTH].reshape(R, B_HEADS, B_HEAD_DIM)
    cq = z[:, Z_CQ:Z_CQ + C_QK_WIDTH].reshape(R, C_HEADS, C_QK_DIM)
    ck = z[:, Z_CK:Z_CK + C_QK_WIDTH].reshape(R, C_HEADS, C_QK_DIM)
    cv = z[:, Z_CV:Z_CV + C_WIDTH].reshape(R, C_HEADS, C_V_DIM)
    co = z[:, Z_CO:Z_CO + C_WIDTH].reshape(R, C_HEADS, C_V_DIM)
    row3 = lambda n: pl.BlockSpec((1, 1, n), lambda r: (r, 0, 0))
    return pl.pallas_call(
        _mix_bc_sample_kernel,
        grid=(R,),
        in_specs=[row3(Z_WIDTH),
                  pl.BlockSpec((1, B_HEADS, B_HEAD_DIM), lambda r: (r, 0, 0)),
                  pl.BlockSpec((None, 1, WINDOW, B_KV_WIDTH), lambda r: (layer, r, 0, 0)),
                  pl.BlockSpec((None, 1, WINDOW, B_KV_WIDTH), lambda r: (layer, r, 0, 0)),
                  pl.BlockSpec((None, 1, B_HEAD_DIM), lambda r: (layer, 0, 0)),
                  pl.BlockSpec((None, 1, LANES), lambda r: (layer, 0, 0)),
                  pl.BlockSpec((None, B_HEADS, 1), lambda r: (layer, 0, 0)),
                  pl.BlockSpec((B_HEADS, 2 * WINDOW), lambda r: (0, 0)),
                  pl.BlockSpec((1, C_HEADS, C_QK_DIM), lambda r: (r, 0, 0)),
                  pl.BlockSpec((1, C_HEADS, C_QK_DIM), lambda r: (r, 0, 0)),
                  pl.BlockSpec((1, C_HEADS, C_V_DIM), lambda r: (r, 0, 0)),
                  pl.BlockSpec((1, C_HEADS, C_V_DIM), lambda r: (r, 0, 0)),
                  pl.BlockSpec((None, 1, LANES), lambda r: (layer, 0, 0)),
                  pl.BlockSpec((None, C_HEADS, C_V_DIM), lambda r: (layer, 0, 0)),
                  pl.BlockSpec((None, 1, C_HEADS, C_QK_DIM, C_V_DIM), lambda r: (layer, r, 0, 0, 0)),
                  pl.BlockSpec((None, 1, C_HEADS, C_QK_DIM), lambda r: (layer, r, 0, 0)),
                  pl.BlockSpec((None, 1, 1, C_HEADS), lambda r: (layer, r, 0, 0))],
        out_specs=(pl.BlockSpec((1, B_HEADS, B_HEAD_DIM), lambda r: (r, 0, 0)),
                   row3(B_KV_WIDTH),
                   pl.BlockSpec((1, C_HEADS, C_V_DIM), lambda r: (r, 0, 0)),
                   pl.BlockSpec((1, C_HEADS, C_QK_DIM, C_V_DIM), lambda r: (r, 0, 0, 0)),
                   pl.BlockSpec((1, C_HEADS, C_QK_DIM), lambda r: (r, 0, 0)),
                   row3(LANES)),
        out_shape=(jax.ShapeDtypeStruct((R, B_HEADS, B_HEAD_DIM), bf16),
                   jax.ShapeDtypeStruct((R, 1, B_KV_WIDTH), f32),
                   jax.ShapeDtypeStruct((R, C_HEADS, C_V_DIM), bf16),
                   jax.ShapeDtypeStruct((R, C_HEADS, C_QK_DIM, C_V_DIM), f32),
                   jax.ShapeDtypeStruct((R, C_HEADS, C_QK_DIM), f32),
                   jax.ShapeDtypeStruct((R, 1, LANES), f32)),
        compiler_params=_cparams(("parallel",)),
        name="mix_bc_sample",
    )(zr, q, k_past, v_past, q_gain, k_gain2, sinks, bias_s, cq, ck, cv, co, gate_bias, h_gain4,
      c0, n0, m0)


def _prep_w_in(w_in):
    au, av, bq, bk, bv, cq, ck, cv, co, ci, cf = jnp.split(w_in, IN_OFFSETS, axis=-1)
    pad = jnp.zeros(w_in.shape[:-1] + (Z_WIDTH - w_in.shape[-1],), w_in.dtype)
    return jnp.concatenate([au, av, bq, cv, co, cq, ck, bk, bv, ci, cf, pad], axis=-1).astype(bf16)


def _row_tile(m, pref):
    return pref if m % pref == 0 else m


def kernel(x_prompt, x_sample, cache_swa_k, cache_swa_v, state_mlstm_C, state_mlstm_n, state_mlstm_m, state_ffn_conv, rel_bias, norm1, w_in, a_v_gain, a_spatial_w, a_spatial_b, b_q_gain, b_k_gain, b_sinks, c_gate_bias, c_h_gain, w_out, norm2, w_up, ffn_conv_w, ffn_conv_b, w_down):
    depth = w_in.shape[0]
    Bp, T, _ = x_prompt.shape
    R = x_sample.shape[0]
    assert x_sample.shape[1] == 1 and T % CHUNK == 0 and T % MLSTM_L == 0

    w_in_p = _prep_w_in(w_in)
    w_out_b = w_out.astype(bf16)
    w_up_b = w_up.astype(bf16)
    w_down_b = w_down.astype(bf16)
    norm1_3 = norm1.reshape(depth, 1, D_MODEL)
    norm2_3 = norm2.reshape(depth, 1, D_MODEL)
    v_gain3 = a_v_gain.reshape(depth, 1, A_WIDTH)
    bs_full = jnp.repeat(jnp.swapaxes(a_spatial_b, 1, 2), A_DIM, axis=-1)
    ws0 = jnp.repeat(a_spatial_w[:, :, 0, 0], A_DIM, axis=-1).reshape(depth, 1, A_WIDTH)
    bs0 = jnp.repeat(a_spatial_b[:, :, 0], A_DIM, axis=-1).reshape(depth, 1, A_WIDTH)
    q_gain3 = b_q_gain.reshape(depth, 1, B_HEAD_DIM)
    q_gain2 = jnp.tile(b_q_gain, (1, 2)).reshape(depth, 1, LANES)
    k_gain2 = jnp.tile(b_k_gain, (1, 2)).reshape(depth, 1, LANES)
    sinks3 = b_sinks.reshape(depth, B_HEADS, 1)
    gate_b = jnp.pad(c_gate_bias, ((0, 0), (0, LANES - 2 * C_HEADS))).reshape(depth, 1, LANES)
    h_gain3 = c_h_gain.reshape(depth, 1, C_WIDTH)
    h_gain4 = c_h_gain.reshape(depth, C_HEADS, C_V_DIM)
    conv_b3 = ffn_conv_b.reshape(depth, 1, 2 * D_FF)
    k_cache = cache_swa_k.reshape(depth, R, WINDOW, B_KV_WIDTH)
    v_cache = cache_swa_v.reshape(depth, R, WINDOW, B_KV_WIDTH)
    m_state = state_mlstm_m.reshape(depth, R, 1, C_HEADS)
    conv_hist = jnp.swapaxes(state_ffn_conv, 1, 2)

    bias_p, bias_s = _bias_tables(rel_bias)

    Mp = Bp * T
    tm = _row_tile(T, 512)
    xp = x_prompt.reshape(Mp, D_MODEL)
    xs = x_sample.reshape(R, D_MODEL)
    P = [[] for _ in range(6)]
    S = [[] for _ in range(7)]
    for l in range(depth):
        z = _mm_in(xp, norm1_3, w_in_p, l, tm, 1024)
        a_o = _mix_a_prompt(z, v_gain3, a_spatial_w, bs_full, l)
        b_o, k_last, v_last = _mix_b_prompt(z, b_sinks[l], q_gain2, k_gain2, bias_p, l, Bp)
        c_o, c_st, n_st, m_st = _mix_c_prompt(z, gate_b, h_gain3, l, Bp)
        x1 = _mm_out(a_o, b_o, c_o, xp, w_out_b, l, tm, 512)
        act, cs = _mm_up_prompt(x1, norm2_3, w_up_b, ffn_conv_w, conv_b3, l, Bp, tm, 512)
        xp = _mm_down(act, x1, w_down_b, l, tm, 512)
        P[0].append(k_last.reshape(Bp, WINDOW, B_KV_HEADS, B_HEAD_DIM))
        P[1].append(v_last.reshape(Bp, WINDOW, B_KV_HEADS, B_HEAD_DIM))
        P[2].append(c_st)
        P[3].append(n_st.reshape(Bp, C_HEADS, C_QK_DIM))
        P[4].append(m_st[:, :, 0, 0])
        seq_tiles = T // tm
        tail = cs[seq_tiles - 1::seq_tiles, :, SUBLANES - (CONV_W - 1):, :]
        P[5].append(jnp.swapaxes(tail, 1, 2).reshape(Bp, CONV_W - 1, 2 * D_FF))

        zs = _mm_in(xs, norm1_3, w_in_p, l, R, 1024)
        a_s, vn_s = _mix_a_sample(zs, v_gain3, ws0, bs0, l)
        b_s, kn_s, c_s, c_new, n_new, m_new = _mix_bc_sample(
            zs, k_cache, v_cache, q_gain3, k_gain2, sinks3, bias_s, gate_b, h_gain4,
            state_mlstm_C, state_mlstm_n, m_state, l)
        x1s = _mm_out(a_s, b_s.reshape(R, B_WIDTH), c_s.reshape(R, C_WIDTH), xs, w_out_b, l, R, 512)
        act_s, zg_s, za_s = _mm_up_sample(x1s, norm2_3, w_up_b, ffn_conv_w, conv_b3, conv_hist, l, 512)
        xs = _mm_down(act_s, x1s, w_down_b, l, R, 512)
        S[0].append(vn_s.reshape(R, 1, A_WIDTH))
        S[1].append(kn_s.reshape(R, 1, B_KV_HEADS, B_HEAD_DIM))
        S[2].append(zs[:, Z_BV:Z_BV + B_KV_WIDTH].reshape(R, 1, B_KV_HEADS, B_HEAD_DIM))
        S[3].append(c_new)
        S[4].append(n_new)
        S[5].append(m_new[:, 0, 0:C_HEADS])
        z_new = jnp.concatenate([zg_s, za_s], axis=-1)
        S[6].append(jnp.stack([state_ffn_conv[l][:, CONV_W - 2], z_new], axis=1))

    st = lambda lst: jnp.stack(lst, axis=0)
    return (xp.reshape(Bp, T, D_MODEL), xs.reshape(R, 1, D_MODEL),
            st(P[0]), st(P[1]), st(S[1]), st(S[2]),
            st(P[2]), st(P[3]), st(P[4]),
            st(S[3]), st(S[4]), st(S[5]),
            st(P[5]), st(S[6]),
            st(S[0]))
```

```python
import functools
import math

import numpy as np
import jax
import jax.numpy as jnp
from jax import lax
from jax.experimental import pallas as pl
from jax.experimental.pallas import tpu as pltpu

f32 = jnp.float32
bf16 = jnp.bfloat16

D_MODEL = 2048
EPS = 1e-6
NEG_INF = -1e30
SQRT_HALF = 0.7071067811865476

A_GROUPS = 4
A_DIM = 128
A_WIDTH = 512
CHUNK = 128
B_HEADS = 16
B_KV_HEADS = 2
B_HEAD_DIM = 64
GQA = 8
B_WIDTH = 1024
B_KV_WIDTH = 128
WINDOW = 128
N_BUCKETS = 32
MAX_DISTANCE = 128
C_HEADS = 4
C_QK_DIM = 64
C_V_DIM = 128
C_QK_WIDTH = 256
C_WIDTH = 512
IN_SPLITS = (A_WIDTH, A_WIDTH, B_WIDTH, B_KV_WIDTH, B_KV_WIDTH,
             C_QK_WIDTH, C_QK_WIDTH, C_WIDTH, C_WIDTH, C_HEADS, C_HEADS)
IN_OFFSETS = tuple(int(o) for o in np.cumsum(IN_SPLITS)[:-1])
D_FF = 5632
CONV_W = 3

Z_WIDTH = 4096
Z_AU, Z_AV, Z_BQ, Z_BK, Z_BV, Z_CQ, Z_CK, Z_CV, Z_CO, Z_GATE = (0,) + IN_OFFSETS[:9]
C_PAIR = 2 * C_V_DIM

MLSTM_L = 128
UP_CHUNK = 256
LANES = 128
SUBLANES = 8
VMEM_LIMIT = 52 * 1024 * 1024


def _cparams(sem, flags=None):
    return pltpu.CompilerParams(dimension_semantics=sem, vmem_limit_bytes=VMEM_LIMIT, flags=flags)


def _gelu(x):
    return 0.5 * x * (1.0 + lax.erf(x * SQRT_HALF))


def _rms(x, gain):
    return x * lax.rsqrt(jnp.mean(x * x, axis=-1, keepdims=True) + EPS) * gain


def _norm_head_pairs(x, gain2):
    lo = lax.broadcasted_iota(jnp.int32, x.shape, 1) < B_HEAD_DIM
    x2 = x * x
    s_lo = jnp.sum(jnp.where(lo, x2, 0.0), axis=-1, keepdims=True)
    s_hi = jnp.sum(jnp.where(lo, 0.0, x2), axis=-1, keepdims=True)
    ms = jnp.where(lo, s_lo, s_hi) * (1.0 / B_HEAD_DIM)
    return x * lax.rsqrt(ms + EPS) * gain2


def _t5_bucket_np(dist):
    n = np.maximum(dist, 0)
    max_exact = N_BUCKETS // 2
    nf = np.maximum(n, 1).astype(np.float32)
    large = max_exact + (np.log(nf / np.float32(max_exact)) / np.float32(math.log(MAX_DISTANCE / max_exact))
                         * np.float32(N_BUCKETS - max_exact)).astype(np.int32)
    return np.where(n < max_exact, n, np.minimum(large, N_BUCKETS - 1)).astype(np.int32)


def _bias_kernel(rb_ref, bkp_ref, bks_ref, op_ref, os_ref):
    bkp = bkp_ref[...]
    bks = bks_ref[...]
    for h in range(B_HEADS):
        accp = jnp.zeros(bkp.shape, f32)
        accs = jnp.zeros(bks.shape, f32)
        for b in range(N_BUCKETS):
            val = rb_ref[b, h]
            accp = jnp.where(bkp == b, val, accp)
            accs = jnp.where(bks == b, val, accs)
        op_ref[h] = accp
        os_ref[h:h + 1, :] = accs[0:1, :]


def _bias_tables(rel_bias):
    qi = np.arange(WINDOW)[:, None]
    kj = np.arange(2 * WINDOW)[None, :]
    bkp = _t5_bucket_np(WINDOW + qi - kj)
    j = np.arange(2 * WINDOW)
    dist_s = np.where(j < WINDOW, WINDOW - j, 0)
    bks = np.broadcast_to(_t5_bucket_np(dist_s)[None, :], (SUBLANES, 2 * WINDOW)).copy()
    return pl.pallas_call(
        _bias_kernel,
        out_shape=(jax.ShapeDtypeStruct((B_HEADS, WINDOW, 2 * WINDOW), f32),
                   jax.ShapeDtypeStruct((B_HEADS, 2 * WINDOW), f32)),
        in_specs=[pl.BlockSpec(memory_space=pltpu.SMEM),
                  pl.BlockSpec(memory_space=pltpu.VMEM),
                  pl.BlockSpec(memory_space=pltpu.VMEM)],
        out_specs=(pl.BlockSpec(memory_space=pltpu.VMEM), pl.BlockSpec(memory_space=pltpu.VMEM)),
        name="bias_tables",
    )(rel_bias, jnp.asarray(bkp), jnp.asarray(bks))


def _mm_in_kernel(x_ref, g_ref, w_ref, z_ref, h_ref):
    @pl.when(pl.program_id(1) == 0)
    def _():
        h_ref[...] = _rms(x_ref[...], g_ref[...]).astype(bf16)

    z_ref[...] = jnp.dot(h_ref[...], w_ref[...], preferred_element_type=f32)


def _mm_in(x, gain, w, layer, tm, tn):
    M = x.shape[0]
    return pl.pallas_call(
        _mm_in_kernel,
        grid=(M // tm, Z_WIDTH // tn),
        in_specs=[pl.BlockSpec((tm, D_MODEL), lambda i, j: (i, 0)),
                  pl.BlockSpec((None, 1, D_MODEL), lambda i, j: (layer, 0, 0)),
                  pl.BlockSpec((None, D_MODEL, tn), lambda i, j: (layer, 0, j))],
        out_specs=pl.BlockSpec((tm, tn), lambda i, j: (i, j)),
        out_shape=jax.ShapeDtypeStruct((M, Z_WIDTH), f32),
        scratch_shapes=[pltpu.VMEM((tm, D_MODEL), bf16)],
        compiler_params=_cparams(("parallel", "arbitrary")),
        name="mm_in",
    )(x, gain, w)


def _mm_out_kernel(a_ref, b_ref, c_ref, x_ref, w_ref, o_ref, lhs_ref):
    @pl.when(pl.program_id(1) == 0)
    def _():
        lhs_ref[:, 0:A_WIDTH] = a_ref[...]
        lhs_ref[:, A_WIDTH:A_WIDTH + B_WIDTH] = b_ref[...]
        lhs_ref[:, A_WIDTH + B_WIDTH:D_MODEL] = c_ref[...]

    o_ref[...] = x_ref[...] + jnp.dot(lhs_ref[...], w_ref[...], preferred_element_type=f32)


def _mm_out(a, b, c, x, w, layer, tm, tn):
    M = x.shape[0]
    return pl.pallas_call(
        _mm_out_kernel,
        grid=(M // tm, D_MODEL // tn),
        in_specs=[pl.BlockSpec((tm, A_WIDTH), lambda i, j: (i, 0)),
                  pl.BlockSpec((tm, B_WIDTH), lambda i, j: (i, 0)),
                  pl.BlockSpec((tm, C_WIDTH), lambda i, j: (i, 0)),
                  pl.BlockSpec((tm, tn), lambda i, j: (i, j)),
                  pl.BlockSpec((None, D_MODEL, tn), lambda i, j: (layer, 0, j))],
        out_specs=pl.BlockSpec((tm, tn), lambda i, j: (i, j)),
        out_shape=jax.ShapeDtypeStruct((M, D_MODEL), f32),
        scratch_shapes=[pltpu.VMEM((tm, D_MODEL), bf16)],
        compiler_params=_cparams(("parallel", "arbitrary")),
        name="mm_out",
    )(a, b, c, x, w)


def _mm_down_kernel(a_ref, x_ref, w_ref, o_ref):
    o_ref[...] = x_ref[...] + jnp.dot(a_ref[...], w_ref[...], preferred_element_type=f32)


def _mm_down(act, x, w, layer, tm, tn):
    M = x.shape[0]
    return pl.pallas_call(
        _mm_down_kernel,
        grid=(M // tm, D_MODEL // tn),
        in_specs=[pl.BlockSpec((tm, D_FF), lambda i, j: (i, 0)),
                  pl.BlockSpec((tm, tn), lambda i, j: (i, j)),
                  pl.BlockSpec((None, D_FF, tn), lambda i, j: (layer, 0, j))],
        out_specs=pl.BlockSpec((tm, tn), lambda i, j: (i, j)),
        out_shape=jax.ShapeDtypeStruct((M, D_MODEL), f32),
        compiler_params=_cparams(("parallel", "arbitrary")),
        name="mm_down",
    )(act, x, w)


def _silu(x):
    return x * jax.nn.sigmoid(x)


def _mm_up_prompt_kernel(x_ref, n2_ref, wg_ref, wa_ref, cwg_ref, cwa_ref, cbg_ref, cba_ref,
                         act_ref, cs_ref, h_ref, carry_ref, zb_ref, *, tm, nj, n_steps, tiles_per_seq):
    s = pl.program_id(0)
    sa = jnp.minimum(s, n_steps - 1)
    ia = sa // nj
    ja = sa % nj
    tf = act_ref.shape[1]

    @pl.when(s == 0)
    def _():
        zb_ref[...] = jnp.zeros(zb_ref.shape, f32)
        carry_ref[...] = jnp.zeros(carry_ref.shape, f32)

    @pl.when(ja == 0)
    def _():
        h_ref[...] = _rms(x_ref[...], n2_ref[...]).astype(bf16)

    chunks = [slice(c0, c0 + UP_CHUNK) for c0 in range(0, tf, UP_CHUNK)]

    def conv(idx, cw_ref, cb_ref, cs):
        return (cb_ref[:, cs]
                + zb_ref[idx, SUBLANES - 2:SUBLANES - 2 + tm, cs] * cw_ref[0:1, cs]
                + zb_ref[idx, SUBLANES - 1:SUBLANES - 1 + tm, cs] * cw_ref[1:2, cs]
                + zb_ref[idx, SUBLANES:SUBLANES + tm, cs] * cw_ref[2:3, cs])

    for cs in chunks:
        g = conv(0, cwg_ref, cbg_ref, cs)
        a = conv(1, cwa_ref, cba_ref, cs)
        act_ref[:, cs] = (_silu(g) * a).astype(bf16)

    seq_start = (ia % tiles_per_seq) == 0
    for cs in chunks:
        for idx, w_ref in ((0, wg_ref), (1, wa_ref)):
            z = jnp.dot(h_ref[...], w_ref[:, cs], preferred_element_type=f32)
            zb_ref[idx, 0:SUBLANES, cs] = jnp.where(seq_start, 0.0, carry_ref[idx, ja, :, cs])
            zb_ref[idx, SUBLANES:SUBLANES + tm, cs] = z
            tail = z[tm - SUBLANES:tm, :]
            carry_ref[idx, ja, :, cs] = tail
            cs_ref[0, idx, :, cs] = tail


def _mm_up_prompt(x, norm2, w_up, conv_w, conv_b, layer, batch, tm, tf):
    M = x.shape[0]
    seq = M // batch
    tiles_per_seq = seq // tm
    nj = D_FF // tf
    n_steps = (M // tm) * nj
    kern = functools.partial(_mm_up_prompt_kernel, tm=tm, nj=nj, n_steps=n_steps, tiles_per_seq=tiles_per_seq)
    ia = lambda s: jnp.minimum(s, n_steps - 1) // nj
    ja = lambda s: jnp.minimum(s, n_steps - 1) % nj
    ib = lambda s: jnp.maximum(s - 1, 0) // nj
    jb = lambda s: jnp.maximum(s - 1, 0) % nj
    return pl.pallas_call(
        kern,
        grid=(n_steps + 1,),
        in_specs=[pl.BlockSpec((tm, D_MODEL), lambda s: (ia(s), 0)),
                  pl.BlockSpec((None, 1, D_MODEL), lambda s: (layer, 0, 0)),
                  pl.BlockSpec((None, D_MODEL, tf), lambda s: (layer, 0, ja(s))),
                  pl.BlockSpec((None, D_MODEL, tf), lambda s: (layer, 0, nj + ja(s))),
                  pl.BlockSpec((None, CONV_W, tf), lambda s: (layer, 0, jb(s))),
                  pl.BlockSpec((None, CONV_W, tf), lambda s: (layer, 0, nj + jb(s))),
                  pl.BlockSpec((None, 1, tf), lambda s: (layer, 0, jb(s))),
                  pl.BlockSpec((None, 1, tf), lambda s: (layer, 0, nj + jb(s)))],
        out_specs=(pl.BlockSpec((tm, tf), lambda s: (ib(s), jb(s))),
                   pl.BlockSpec((1, 2, SUBLANES, tf), lambda s: (ia(s), 0, 0, ja(s)))),
        out_shape=(jax.ShapeDtypeStruct((M, D_FF), bf16),
                   jax.ShapeDtypeStruct((M // tm, 2, SUBLANES, D_FF), f32)),
        scratch_shapes=[pltpu.VMEM((tm, D_MODEL), bf16),
                        pltpu.VMEM((2, nj, SUBLANES, tf), f32),
                        pltpu.VMEM((2, tm + SUBLANES, tf), f32)],
        compiler_params=_cparams(("arbitrary",)),
        name="mm_up_prompt",
    )(x, norm2, w_up, w_up, conv_w, conv_w, conv_b, conv_b)


def _mm_up_sample_kernel(x_ref, n2_ref, wg_ref, wa_ref, cwg_ref, cwa_ref, cbg_ref, cba_ref,
                         b0g_ref, b1g_ref, b0a_ref, b1a_ref, act_ref, zg_ref, za_ref, h_ref):
    @pl.when(pl.program_id(0) == 0)
    def _():
        h_ref[...] = _rms(x_ref[...], n2_ref[...]).astype(bf16)

    def conv_half(w_ref, cw_ref, cb_ref, b0_ref, b1_ref, z_ref):
        z = jnp.dot(h_ref[...], w_ref[...], preferred_element_type=f32)
        z_ref[...] = z
        return (cb_ref[...] + b0_ref[...] * cw_ref[0:1, :] + b1_ref[...] * cw_ref[1:2, :]
                + z * cw_ref[2:3, :])

    g = conv_half(wg_ref, cwg_ref, cbg_ref, b0g_ref, b1g_ref, zg_ref)
    a = conv_half(wa_ref, cwa_ref, cba_ref, b0a_ref, b1a_ref, za_ref)
    act_ref[...] = (_silu(g) * a).astype(bf16)


def _mm_up_sample(x, norm2, w_up, conv_w, conv_b, buf, layer, tf):
    M = x.shape[0]
    nj = D_FF // tf
    wspec = lambda off: pl.BlockSpec((None, D_MODEL, tf), lambda j: (layer, 0, off + j))
    cwspec = lambda off: pl.BlockSpec((None, CONV_W, tf), lambda j: (layer, 0, off + j))
    cbspec = lambda off: pl.BlockSpec((None, 1, tf), lambda j: (layer, 0, off + j))
    bufspec = lambda row, off: pl.BlockSpec((None, None, M, tf), lambda j: (layer, row, 0, off + j))
    return pl.pallas_call(
        _mm_up_sample_kernel,
        grid=(nj,),
        in_specs=[pl.BlockSpec((M, D_MODEL), lambda j: (0, 0)),
                  pl.BlockSpec((None, 1, D_MODEL), lambda j: (layer, 0, 0)),
                  wspec(0), wspec(nj), cwspec(0), cwspec(nj), cbspec(0), cbspec(nj),
                  bufspec(0, 0), bufspec(1, 0), bufspec(0, nj), bufspec(1, nj)],
        out_specs=(pl.BlockSpec((M, tf), lambda j: (0, j)),
                   pl.BlockSpec((M, tf), lambda j: (0, j)),
                   pl.BlockSpec((M, tf), lambda j: (0, j))),
        out_shape=(jax.ShapeDtypeStruct((M, D_FF), bf16),
                   jax.ShapeDtypeStruct((M, D_FF), f32),
                   jax.ShapeDtypeStruct((M, D_FF), f32)),
        scratch_shapes=[pltpu.VMEM((M, D_MODEL), bf16)],
        compiler_params=_cparams(("arbitrary",)),
        name="mm_up_sample",
    )(x, norm2, w_up, w_up, conv_w, conv_w, conv_b, conv_b, buf, buf, buf, buf)


def _mix_a_prompt_kernel(u_ref, v_ref, vg_ref, ws_ref, bs_ref, o_ref):
    row = lax.broadcasted_iota(jnp.int32, (CHUNK, CHUNK), 0)
    col = lax.broadcasted_iota(jnp.int32, (CHUNK, CHUNK), 1)
    tri = col <= row
    for g in range(A_GROUPS):
        sl = slice(g * A_DIM, (g + 1) * A_DIM)
        ws = jnp.where(tri, ws_ref[g], 0.0).astype(bf16)
        for r0 in range(0, u_ref.shape[0], CHUNK):
            rs = slice(r0, r0 + CHUNK)
            vn = _rms(_gelu(v_ref[rs, sl]), vg_ref[:, sl])
            mixv = jnp.dot(ws, vn.astype(bf16), preferred_element_type=f32) + bs_ref[:, sl]
            o_ref[rs, sl] = (_gelu(u_ref[rs, sl]) * mixv).astype(bf16)


def _mix_a_prompt(z, v_gain, w_s, b_s_full, layer, rows):
    M = z.shape[0]
    return pl.pallas_call(
        _mix_a_prompt_kernel,
        grid=(M // rows,),
        in_specs=[pl.BlockSpec((rows, A_WIDTH), lambda r: (r, Z_AU // A_WIDTH)),
                  pl.BlockSpec((rows, A_WIDTH), lambda r: (r, Z_AV // A_WIDTH)),
                  pl.BlockSpec((None, 1, A_WIDTH), lambda r: (layer, 0, 0)),
                  pl.BlockSpec((None, A_GROUPS, CHUNK, CHUNK), lambda r: (layer, 0, 0, 0)),
                  pl.BlockSpec((None, CHUNK, A_WIDTH), lambda r: (layer, 0, 0))],
        out_specs=pl.BlockSpec((rows, A_WIDTH), lambda r: (r, 0)),
        out_shape=jax.ShapeDtypeStruct((M, A_WIDTH), bf16),
        compiler_params=_cparams(("parallel",)),
        name="mix_a_prompt",
    )(z, z, v_gain, w_s, b_s_full)


def _mix_a_sample_kernel(u_ref, v_ref, vg_ref, ws_ref, bs_ref, o_ref, vn_ref):
    for g in range(A_GROUPS):
        sl = slice(g * A_DIM, (g + 1) * A_DIM)
        vn = _rms(_gelu(v_ref[:, sl]), vg_ref[:, sl])
        vn_ref[:, sl] = vn
        mixv = ws_ref[:, sl] * vn + bs_ref[:, sl]
        o_ref[:, sl] = (_gelu(u_ref[:, sl]) * mixv).astype(bf16)


def _mix_a_sample(z, v_gain, ws0, bs0, layer):
    M = z.shape[0]
    vec = pl.BlockSpec((None, 1, A_WIDTH), lambda r: (layer, 0, 0))
    return pl.pallas_call(
        _mix_a_sample_kernel,
        grid=(1,),
        in_specs=[pl.BlockSpec((M, A_WIDTH), lambda r: (0, Z_AU // A_WIDTH)),
                  pl.BlockSpec((M, A_WIDTH), lambda r: (0, Z_AV // A_WIDTH)),
                  vec, vec, vec],
        out_specs=(pl.BlockSpec((M, A_WIDTH), lambda r: (0, 0)),
                   pl.BlockSpec((M, A_WIDTH), lambda r: (0, 0))),
        out_shape=(jax.ShapeDtypeStruct((M, A_WIDTH), bf16),
                   jax.ShapeDtypeStruct((M, A_WIDTH), f32)),
        compiler_params=_cparams(("arbitrary",)),
        name="mix_a_sample",
    )(z, z, v_gain, ws0, bs0)


def _mix_b_prompt_kernel(sink_ref, q_ref, k_ref, v_ref, qg_ref, kg_ref, bias_ref,
                         o_ref, klast_ref, vlast_ref, kcat_ref, vcat_ref):
    i = pl.program_id(1)
    W = WINDOW
    kn = _norm_head_pairs(k_ref[...], kg_ref[...])
    v = v_ref[...]
    klast_ref[0] = kn
    vlast_ref[0] = v

    @pl.when(i == 0)
    def _():
        kcat_ref[0:W, :] = jnp.zeros((W, LANES), f32)
        vcat_ref[0:W, :] = jnp.zeros((W, LANES), f32)

    @pl.when(i > 0)
    def _():
        kcat_ref[0:W, :] = kcat_ref[W:2 * W, :]
        vcat_ref[0:W, :] = vcat_ref[W:2 * W, :]

    kcat_ref[W:2 * W, :] = kn
    vcat_ref[W:2 * W, :] = v
    kc = kcat_ref[...]
    vc = vcat_ref[...]
    kc_sw = pltpu.roll(kc, B_HEAD_DIM, 1)
    vc_sw = pltpu.roll(vc, B_HEAD_DIM, 1)
    lo = lax.broadcasted_iota(jnp.int32, (2 * W, LANES), 1) < B_HEAD_DIM

    qi = lax.broadcasted_iota(jnp.int32, (W, 2 * W), 0)
    kj = lax.broadcasted_iota(jnp.int32, (W, 2 * W), 1)
    dist = W + qi - kj
    valid = (dist >= 0) & (dist < W) & ((kj >= W) | (i > 0))

    tiles = GQA // 2
    for kvh in range(B_KV_HEADS):
        k_src, k_alt = (kc, kc_sw) if kvh == 0 else (kc_sw, kc)
        v_src, v_alt = (vc, vc_sw) if kvh == 0 else (vc_sw, vc)
        k_even = jnp.where(lo, k_src, 0.0).astype(bf16)
        k_odd = jnp.where(lo, 0.0, k_alt).astype(bf16)
        v_even = jnp.where(lo, v_src, 0.0).astype(bf16)
        v_odd = jnp.where(lo, 0.0, v_alt).astype(bf16)
        qs = jnp.concatenate(
            [_norm_head_pairs(q_ref[:, (kvh * tiles + t) * LANES:(kvh * tiles + t + 1) * LANES], qg_ref[...])
             for t in range(tiles)], axis=0).astype(bf16)
        dn = (((1,), (1,)), ((), ()))
        s_par = [lax.dot_general(qs, k_even, dn, preferred_element_type=f32),
                 lax.dot_general(qs, k_odd, dn, preferred_element_type=f32)]
        p_par = []
        for par in range(2):
            blocks = []
            for t in range(tiles):
                h = kvh * GQA + 2 * t + par
                s = s_par[par][t * W:(t + 1) * W, :] * (B_HEAD_DIM ** -0.5) + bias_ref[h]
                s = jnp.where(valid, s, NEG_INF)
                sk = sink_ref[h]
                mx = jnp.maximum(jnp.max(s, axis=-1, keepdims=True), sk)
                p = jnp.exp(s - mx)
                den = jnp.sum(p, axis=-1, keepdims=True) + jnp.exp(sk - mx)
                blocks.append((p * (1.0 / den)).astype(bf16))
            p_par.append(jnp.concatenate(blocks, axis=0))
        o = (jnp.dot(p_par[0], v_even, preferred_element_type=f32)
             + jnp.dot(p_par[1], v_odd, preferred_element_type=f32))
        for t in range(tiles):
            c0 = (kvh * tiles + t) * LANES
            o_ref[:, c0:c0 + LANES] = o[t * W:(t + 1) * W, :].astype(bf16)


def _mix_b_prompt(z, sinks, q_gain2, k_gain2, bias_p, layer, batch):
    M = z.shape[0]
    nb = M // batch // WINDOW
    return pl.pallas_call(
        _mix_b_prompt_kernel,
        grid=(batch, nb),
        in_specs=[pl.BlockSpec(memory_space=pltpu.SMEM),
                  pl.BlockSpec((WINDOW, B_WIDTH), lambda b, i: (b * nb + i, Z_BQ // B_WIDTH)),
                  pl.BlockSpec((WINDOW, B_KV_WIDTH), lambda b, i: (b * nb + i, Z_BK // B_KV_WIDTH)),
                  pl.BlockSpec((WINDOW, B_KV_WIDTH), lambda b, i: (b * nb + i, Z_BV // B_KV_WIDTH)),
                  pl.BlockSpec((None, 1, LANES), lambda b, i: (layer, 0, 0)),
                  pl.BlockSpec((None, 1, LANES), lambda b, i: (layer, 0, 0)),
                  pl.BlockSpec((B_HEADS, WINDOW, 2 * WINDOW), lambda b, i: (0, 0, 0))],
        out_specs=(pl.BlockSpec((WINDOW, B_WIDTH), lambda b, i: (b * nb + i, 0)),
                   pl.BlockSpec((1, WINDOW, B_KV_WIDTH), lambda b, i: (b, 0, 0)),
                   pl.BlockSpec((1, WINDOW, B_KV_WIDTH), lambda b, i: (b, 0, 0))),
        out_shape=(jax.ShapeDtypeStruct((M, B_WIDTH), bf16),
                   jax.ShapeDtypeStruct((batch, WINDOW, B_KV_WIDTH), f32),
                   jax.ShapeDtypeStruct((batch, WINDOW, B_KV_WIDTH), f32)),
        scratch_shapes=[pltpu.VMEM((2 * WINDOW, LANES), f32),
                        pltpu.VMEM((2 * WINDOW, LANES), f32)],
        compiler_params=_cparams(("arbitrary", "arbitrary")),
        name="mix_b_prompt",
    )(sinks, z, z, z, q_gain2, k_gain2, bias_p)


def _mix_c_prompt_kernel(q_ref, k_ref, v0_ref, v1_ref, o0_ref, o1_ref, g_ref, gb_ref, hg_ref,
                         out_ref, c_out, n_out, m_out, c_s, n_s, m_s):
    L = MLSTM_L
    c = pl.program_id(0)

    @pl.when(c == 0)
    def _():
        c_s[...] = jnp.zeros(c_s.shape, f32)
        n_s[...] = jnp.zeros(n_s.shape, f32)
        m_s[...] = jnp.zeros(m_s.shape, f32)

    row = lax.broadcasted_iota(jnp.int32, (L, L), 0)
    col = lax.broadcasted_iota(jnp.int32, (L, L), 1)
    tri = col <= row
    tri_f = tri.astype(f32)
    for bi in range(q_ref.shape[0]):
        g = g_ref[bi] + gb_ref[...]
        lf = jax.nn.log_sigmoid(g)
        b = jnp.dot(tri_f, lf, precision=lax.Precision.HIGHEST, preferred_element_type=f32)
        bT = b.T
        gT = g.T
        for h in range(C_HEADS):
            b_col = b[:, C_HEADS + h:C_HEADS + h + 1]
            i_col = g[:, h:h + 1]
            b_row = bT[C_HEADS + h:C_HEADS + h + 1, :]
            i_row = gT[h:h + 1, :]
            dlog = jnp.where(tri, b_col - b_row + i_row, -jnp.inf)
            m_prev = m_s[bi, h][0:1, 0:1]
            inter = b_col + m_prev
            mt = jnp.maximum(inter, jnp.max(dlog, axis=-1, keepdims=True))
            v_ref, o_ref = (v0_ref, o0_ref) if h < 2 else (v1_ref, o1_ref)
            vsl = slice((h % 2) * C_V_DIM, (h % 2 + 1) * C_V_DIM)
            qh = q_ref[bi, :, h * C_QK_DIM:(h + 1) * C_QK_DIM]
            kh = k_ref[bi, :, h * C_QK_DIM:(h + 1) * C_QK_DIM] * (C_QK_DIM ** -0.5)
            vh = v_ref[bi, :, vsl]
            qb, kb, vb = qh.astype(bf16), kh.astype(bf16), vh.astype(bf16)
            s = lax.dot_general(qb, kb, (((1,), (1,)), ((), ())), preferred_element_type=f32) * jnp.exp(dlog - mt)
            iw = jnp.exp(inter - mt)
            c_prev = c_s[bi, h]
            n_prev = n_s[bi, h]
            num = (iw * jnp.dot(qb, c_prev.astype(bf16), preferred_element_type=f32)
                   + jnp.dot(s.astype(bf16), vb, preferred_element_type=f32))
            qn = iw * jnp.sum(qh * n_prev, axis=-1, keepdims=True) + jnp.sum(s, axis=-1, keepdims=True)
            hh = num / jnp.maximum(jnp.abs(qn), jnp.exp(-mt))
            m_new = mt[L - 1:L, :]
            b_last = b_col[L - 1:L, :]
            wc = jnp.exp(b_last - b_col + i_col - m_new)
            decay = jnp.exp(b_last + m_prev - m_new)
            kw = kh * wc
            c_s[bi, h] = decay * c_prev + lax.dot_general(kw.astype(bf16), vb, (((0,), (0,)), ((), ())),
                                                          preferred_element_type=f32)
            n_s[bi, h] = decay * n_prev + jnp.sum(kw, axis=0, keepdims=True)
            m_s[bi, h] = jnp.broadcast_to(m_new, (SUBLANES, LANES))
            osl = slice(h * C_V_DIM, (h + 1) * C_V_DIM)
            out_ref[bi, :, osl] = (_rms(hh, hg_ref[:, osl]) * jax.nn.sigmoid(o_ref[bi, :, vsl])).astype(bf16)

    @pl.when(c == pl.num_programs(0) - 1)
    def _():
        c_out[...] = c_s[...]
        n_out[...] = n_s[...]
        m_out[...] = m_s[...]


def _mix_c_prompt(z, gate_bias, h_gain, layer, batch):
    M = z.shape[0]
    L = MLSTM_L
    T = M // batch
    z3 = z.reshape(batch, T, Z_WIDTH)
    col = lambda width, off: pl.BlockSpec((batch, L, width), lambda c: (0, c, off // width))
    whole = lambda *shape: pl.BlockSpec(shape, lambda c: (0,) * len(shape))
    out, c_st, n_st, m_st = pl.pallas_call(
        _mix_c_prompt_kernel,
        grid=(T // L,),
        in_specs=[col(C_QK_WIDTH, Z_CQ), col(C_QK_WIDTH, Z_CK),
                  col(C_PAIR, Z_CV), col(C_PAIR, Z_CV + C_PAIR),
                  col(C_PAIR, Z_CO), col(C_PAIR, Z_CO + C_PAIR),
                  col(LANES, Z_GATE),
                  pl.BlockSpec((None, 1, LANES), lambda c: (layer, 0, 0)),
                  pl.BlockSpec((None, 1, C_WIDTH), lambda c: (layer, 0, 0))],
        out_specs=(pl.BlockSpec((batch, L, C_WIDTH), lambda c: (0, c, 0)),
                   whole(batch, C_HEADS, C_QK_DIM, C_V_DIM),
                   whole(batch, C_HEADS, 1, C_QK_DIM),
                   whole(batch, C_HEADS, SUBLANES, LANES)),
        out_shape=(jax.ShapeDtypeStruct((batch, T, C_WIDTH), bf16),
                   jax.ShapeDtypeStruct((batch, C_HEADS, C_QK_DIM, C_V_DIM), f32),
                   jax.ShapeDtypeStruct((batch, C_HEADS, 1, C_QK_DIM), f32),
                   jax.ShapeDtypeStruct((batch, C_HEADS, SUBLANES, LANES), f32)),
        scratch_shapes=[pltpu.VMEM((batch, C_HEADS, C_QK_DIM, C_V_DIM), f32),
                        pltpu.VMEM((batch, C_HEADS, 1, C_QK_DIM), f32),
                        pltpu.VMEM((batch, C_HEADS, SUBLANES, LANES), f32)],
        compiler_params=_cparams(("arbitrary",)),
        name="mix_c_prompt",
    )(z3, z3, z3, z3, z3, z3, z3, gate_bias, h_gain)
    return out.reshape(M, C_WIDTH), c_st, n_st, m_st


def _mix_bc_sample_kernel(zr_ref, q_ref, kp_ref, vp_ref, qg_ref, kg_ref, sink_ref, bias_ref,
                          cq_ref, ck_ref, cv_ref, co_ref, gb_ref, hg_ref, c0_ref, n0_ref, m0_ref,
                          bo_ref, kn_ref, co_out, c_out, n_out, m_out):
    W = WINDOW
    zr = zr_ref[0]

    qn = _rms(q_ref[0], qg_ref[...])
    hrow = lax.broadcasted_iota(jnp.int32, (B_HEADS, LANES), 0)
    lane = lax.broadcasted_iota(jnp.int32, (B_HEADS, LANES), 1)
    own = (hrow < GQA) == (lane < B_HEAD_DIM)
    q2 = jnp.where(own, jnp.concatenate([qn, qn], axis=-1), 0.0)
    k_new = _norm_head_pairs(zr[:, Z_BK:Z_BK + B_KV_WIDTH], kg_ref[...])
    v_new = zr[:, Z_BV:Z_BV + B_KV_WIDTH]
    kn_ref[0] = k_new
    scale = B_HEAD_DIM ** -0.5
    s_past = lax.dot_general(q2.astype(bf16), kp_ref[0].astype(bf16), (((1,), (1,)), ((), ())),
                             preferred_element_type=f32) * scale + bias_ref[:, 0:W]
    s_past = jnp.where(lane >= 1, s_past, NEG_INF)
    s_new = jnp.sum(q2 * k_new, axis=-1, keepdims=True) * scale + bias_ref[:, W:W + 1]
    sk = sink_ref[...]
    mx = jnp.maximum(jnp.maximum(jnp.max(s_past, axis=-1, keepdims=True), s_new), sk)
    p_past = jnp.exp(s_past - mx)
    p_new = jnp.exp(s_new - mx)
    den = jnp.sum(p_past, axis=-1, keepdims=True) + p_new + jnp.exp(sk - mx)
    inv = 1.0 / den
    o2 = (jnp.dot((p_past * inv).astype(bf16), vp_ref[0].astype(bf16), preferred_element_type=f32)
          + (p_new * inv) * v_new)
    o2_sw = pltpu.roll(o2, B_HEAD_DIM, 1)
    bo_ref[0] = jnp.where(hrow < GQA, o2, o2_sw)[:, 0:B_HEAD_DIM].astype(bf16)

    g = zr[:, Z_GATE:Z_GATE + LANES] + gb_ref[...]
    lf = jax.nn.log_sigmoid(g)
    m0 = m0_ref[0]
    lane1 = lax.broadcasted_iota(jnp.int32, (1, LANES), 1)
    m_row = jnp.zeros((1, LANES), f32)
    row8 = lax.broadcasted_iota(jnp.int32, (SUBLANES, C_QK_DIM), 0)
    c_rows = []
    for h in range(C_HEADS):
        ig = g[:, h:h + 1]
        b = lf[:, C_HEADS + h:C_HEADS + h + 1]
        m_prev = m0[:, h:h + 1]
        inter = b + m_prev
        mt = jnp.maximum(inter, ig)
        qh = cq_ref[0, h:h + 1, :]
        kh = ck_ref[0, h:h + 1, :] * (C_QK_DIM ** -0.5)
        vh = cv_ref[0, h:h + 1, :]
        c_prev = c0_ref[0, h]
        n_prev = n0_ref[0, h:h + 1, :]
        s = jnp.sum(qh * kh, axis=-1, keepdims=True) * jnp.exp(ig - mt)
        iw = jnp.exp(inter - mt)
        q8 = jnp.broadcast_to(qh, (SUBLANES, C_QK_DIM)).astype(bf16)
        qc = jnp.dot(q8, c_prev.astype(bf16), preferred_element_type=f32)[0:1, :]
        num = iw * qc + s * vh
        qn_ = iw * jnp.sum(qh * n_prev, axis=-1, keepdims=True) + s
        hh = num / jnp.maximum(jnp.abs(qn_), jnp.exp(-mt))
        wc = jnp.exp(ig - mt)
        decay = jnp.exp(inter - mt)
        kw = kh * wc
        kw8 = jnp.where(row8 == 0, jnp.broadcast_to(kw, (SUBLANES, C_QK_DIM)), 0.0).astype(bf16)
        v8 = jnp.broadcast_to(vh, (SUBLANES, C_V_DIM)).astype(bf16)
        c_out[0, h] = decay * c_prev + lax.dot_general(kw8, v8, (((0,), (0,)), ((), ())),
                                                      preferred_element_type=f32)
        n_out[0, h:h + 1, :] = decay * n_prev + kw
        m_row = jnp.where(lane1 == h, mt, m_row)
        c_rows.append(_rms(hh, hg_ref[h:h + 1, :]) * jax.nn.sigmoid(co_ref[0, h:h + 1, :]))
    co_out[0] = jnp.concatenate(c_rows, axis=0).astype(bf16)
    m_out[0] = m_row


def _mix_bc_sample(z, k_past, v_past, q_gain, k_gain2, sinks, bias_s, gate_bias, h_gain4,
                   c0, n0, m0, layer):
    R = z.shape[0]
    zr = z.reshape(R, 1, Z_WIDTH)
    q = z[:, Z_BQ:Z_BQ + B_WIDTH].reshape(R, B_HEADS, B_HEAD_DIM)
    cq = z[:, Z_CQ:Z_CQ + C_QK_WIDTH].reshape(R, C_HEADS, C_QK_DIM)
    ck = z[:, Z_CK:Z_CK + C_QK_WIDTH].reshape(R, C_HEADS, C_QK_DIM)
    cv = z[:, Z_CV:Z_CV + C_WIDTH].reshape(R, C_HEADS, C_V_DIM)
    co = z[:, Z_CO:Z_CO + C_WIDTH].reshape(R, C_HEADS, C_V_DIM)
    row3 = lambda n: pl.BlockSpec((1, 1, n), lambda r: (r, 0, 0))
    return pl.pallas_call(
        _mix_bc_sample_kernel,
        grid=(R,),
        in_specs=[row3(Z_WIDTH),
                  pl.BlockSpec((1, B_HEADS, B_HEAD_DIM), lambda r: (r, 0, 0)),
                  pl.BlockSpec((None, 1, WINDOW, B_KV_WIDTH), lambda r: (layer, r, 0, 0)),
                  pl.BlockSpec((None, 1, WINDOW, B_KV_WIDTH), lambda r: (layer, r, 0, 0)),
                  pl.BlockSpec((None, 1, B_HEAD_DIM), lambda r: (layer, 0, 0)),
                  pl.BlockSpec((None, 1, LANES), lambda r: (layer, 0, 0)),
                  pl.BlockSpec((None, B_HEADS, 1), lambda r: (layer, 0, 0)),
                  pl.BlockSpec((B_HEADS, 2 * WINDOW), lambda r: (0, 0)),
                  pl.BlockSpec((1, C_HEADS, C_QK_DIM), lambda r: (r, 0, 0)),
                  pl.BlockSpec((1, C_HEADS, C_QK_DIM), lambda r: (r, 0, 0)),
                  pl.BlockSpec((1, C_HEADS, C_V_DIM), lambda r: (r, 0, 0)),
                  pl.BlockSpec((1, C_HEADS, C_V_DIM), lambda r: (r, 0, 0)),
                  pl.BlockSpec((None, 1, LANES), lambda r: (layer, 0, 0)),
                  pl.BlockSpec((None, C_HEADS, C_V_DIM), lambda r: (layer, 0, 0)),
                  pl.BlockSpec((None, 1, C_HEADS, C_QK_DIM, C_V_DIM), lambda r: (layer, r, 0, 0, 0)),
                  pl.BlockSpec((None, 1, C_HEADS, C_QK_DIM), lambda r: (layer, r, 0, 0)),
                  pl.BlockSpec((None, 1, 1, C_HEADS), lambda r: (layer, r, 0, 0))],
        out_specs=(pl.BlockSpec((1, B_HEADS, B_HEAD_DIM), lambda r: (r, 0, 0)),
                   row3(B_KV_WIDTH),
                   pl.BlockSpec((1, C_HEADS, C_V_DIM), lambda r: (r, 0, 0)),
                   pl.BlockSpec((1, C_HEADS, C_QK_DIM, C_V_DIM), lambda r: (r, 0, 0, 0)),
                   pl.BlockSpec((1, C_HEADS, C_QK_DIM), lambda r: (r, 0, 0)),
                   row3(LANES)),
        out_shape=(jax.ShapeDtypeStruct((R, B_HEADS, B_HEAD_DIM), bf16),
                   jax.ShapeDtypeStruct((R, 1, B_KV_WIDTH), f32),
                   jax.ShapeDtypeStruct((R, C_HEADS, C_V_DIM), bf16),
                   jax.ShapeDtypeStruct((R, C_HEADS, C_QK_DIM, C_V_DIM), f32),
                   jax.ShapeDtypeStruct((R, C_HEADS, C_QK_DIM), f32),
                   jax.ShapeDtypeStruct((R, 1, LANES), f32)),
        compiler_params=_cparams(("parallel",)),
        name="mix_bc_sample",
    )(zr, q, k_past, v_past, q_gain, k_gain2, sinks, bias_s, cq, ck, cv, co, gate_bias, h_gain4,
      c0, n0, m0)


def _prep_w_in(w_in):
    return jnp.pad(w_in.astype(bf16), ((0, 0), (0, 0), (0, Z_WIDTH - w_in.shape[-1])))


def _row_tile(m, pref):
    return pref if m % pref == 0 else m


def kernel(x_prompt, x_sample, cache_swa_k, cache_swa_v, state_mlstm_C, state_mlstm_n, state_mlstm_m, state_ffn_conv, rel_bias, norm1, w_in, a_v_gain, a_spatial_w, a_spatial_b, b_q_gain, b_k_gain, b_sinks, c_gate_bias, c_h_gain, w_out, norm2, w_up, ffn_conv_w, ffn_conv_b, w_down):
    depth = w_in.shape[0]
    Bp, T, _ = x_prompt.shape
    R = x_sample.shape[0]
    assert x_sample.shape[1] == 1 and T % CHUNK == 0 and T % MLSTM_L == 0

    w_in_p = _prep_w_in(w_in)
    w_out_b = w_out.astype(bf16)
    w_up_b = w_up.astype(bf16)
    w_down_b = w_down.astype(bf16)
    norm1_3 = norm1.reshape(depth, 1, D_MODEL)
    norm2_3 = norm2.reshape(depth, 1, D_MODEL)
    v_gain3 = a_v_gain.reshape(depth, 1, A_WIDTH)
    bs_full = jnp.repeat(jnp.swapaxes(a_spatial_b, 1, 2), A_DIM, axis=-1)
    ws0 = jnp.repeat(a_spatial_w[:, :, 0, 0], A_DIM, axis=-1).reshape(depth, 1, A_WIDTH)
    bs0 = jnp.repeat(a_spatial_b[:, :, 0], A_DIM, axis=-1).reshape(depth, 1, A_WIDTH)
    q_gain3 = b_q_gain.reshape(depth, 1, B_HEAD_DIM)
    q_gain2 = jnp.tile(b_q_gain, (1, 2)).reshape(depth, 1, LANES)
    k_gain2 = jnp.tile(b_k_gain, (1, 2)).reshape(depth, 1, LANES)
    sinks3 = b_sinks.reshape(depth, B_HEADS, 1)
    gate_b = jnp.pad(c_gate_bias, ((0, 0), (0, LANES - 2 * C_HEADS))).reshape(depth, 1, LANES)
    h_gain3 = c_h_gain.reshape(depth, 1, C_WIDTH)
    h_gain4 = c_h_gain.reshape(depth, C_HEADS, C_V_DIM)
    conv_b3 = ffn_conv_b.reshape(depth, 1, 2 * D_FF)
    k_cache = cache_swa_k.reshape(depth, R, WINDOW, B_KV_WIDTH)
    v_cache = cache_swa_v.reshape(depth, R, WINDOW, B_KV_WIDTH)
    m_state = state_mlstm_m.reshape(depth, R, 1, C_HEADS)
    conv_hist = jnp.swapaxes(state_ffn_conv, 1, 2)

    bias_p, bias_s = _bias_tables(rel_bias)

    Mp = Bp * T
    tm = _row_tile(T, 1024)
    a_rows = _row_tile(T, 512)
    xp = x_prompt.reshape(Mp, D_MODEL)
    xs = x_sample.reshape(R, D_MODEL)
    P = [[] for _ in range(6)]
    S = [[] for _ in range(7)]
    for l in range(depth):
        z = _mm_in(xp, norm1_3, w_in_p, l, tm, 1024)
        a_o = _mix_a_prompt(z, v_gain3, a_spatial_w, bs_full, l, a_rows)
        b_o, k_last, v_last = _mix_b_prompt(z, b_sinks[l], q_gain2, k_gain2, bias_p, l, Bp)
        c_o, c_st, n_st, m_st = _mix_c_prompt(z, gate_b, h_gain3, l, Bp)
        x1 = _mm_out(a_o, b_o, c_o, xp, w_out_b, l, tm, 1024)
        act, cs = _mm_up_prompt(x1, norm2_3, w_up_b, ffn_conv_w, conv_b3, l, Bp, tm, 512)
        xp = _mm_down(act, x1, w_down_b, l, tm, 512)
        P[0].append(k_last.reshape(Bp, WINDOW, B_KV_HEADS, B_HEAD_DIM))
        P[1].append(v_last.reshape(Bp, WINDOW, B_KV_HEADS, B_HEAD_DIM))
        P[2].append(c_st)
        P[3].append(n_st.reshape(Bp, C_HEADS, C_QK_DIM))
        P[4].append(m_st[:, :, 0, 0])
        seq_tiles = T // tm
        tail = cs[seq_tiles - 1::seq_tiles, :, SUBLANES - (CONV_W - 1):, :]
        P[5].append(jnp.swapaxes(tail, 1, 2).reshape(Bp, CONV_W - 1, 2 * D_FF))

        zs = _mm_in(xs, norm1_3, w_in_p, l, R, 1024)
        a_s, vn_s = _mix_a_sample(zs, v_gain3, ws0, bs0, l)
        b_s, kn_s, c_s, c_new, n_new, m_new = _mix_bc_sample(
            zs, k_cache, v_cache, q_gain3, k_gain2, sinks3, bias_s, gate_b, h_gain4,
            state_mlstm_C, state_mlstm_n, m_state, l)
        x1s = _mm_out(a_s, b_s.reshape(R, B_WIDTH), c_s.reshape(R, C_WIDTH), xs, w_out_b, l, R, 512)
        act_s, zg_s, za_s = _mm_up_sample(x1s, norm2_3, w_up_b, ffn_conv_w, conv_b3, conv_hist, l, 512)
        xs = _mm_down(act_s, x1s, w_down_b, l, R, 512)
        S[0].append(vn_s.reshape(R, 1, A_WIDTH))
        S[1].append(kn_s.reshape(R, 1, B_KV_HEADS, B_HEAD_DIM))
        S[2].append(zs[:, Z_BV:Z_BV + B_KV_WIDTH].reshape(R, 1, B_KV_HEADS, B_HEAD_DIM))
        S[3].append(c_new)
        S[4].append(n_new)
        S[5].append(m_new[:, 0, 0:C_HEADS])
        z_new = jnp.concatenate([zg_s, za_s], axis=-1)
        S[6].append(jnp.stack([state_ffn_conv[l][:, CONV_W - 2], z_new], axis=1))

    st = lambda lst: jnp.stack(lst, axis=0)
    return (xp.reshape(Bp, T, D_MODEL), xs.reshape(R, 1, D_MODEL),
            st(P[0]), st(P[1]), st(S[1]), st(S[2]),
            st(P[2]), st(P[3]), st(P[4]),
            st(S[3]), st(S[4]), st(S[5]),
            st(P[5]), st(S[6]),
            st(S[0]))
```

```python
import functools
import math

import numpy as np
import jax
import jax.numpy as jnp
from jax import lax
from jax.experimental import pallas as pl
from jax.experimental.pallas import tpu as pltpu

f32 = jnp.float32
bf16 = jnp.bfloat16

D_MODEL = 2048
EPS = 1e-6
NEG_INF = -1e30
SQRT_HALF = 0.7071067811865476

A_GROUPS = 4
A_DIM = 128
A_WIDTH = 512
CHUNK = 128
B_HEADS = 16
B_KV_HEADS = 2
B_HEAD_DIM = 64
GQA = 8
B_WIDTH = 1024
B_KV_WIDTH = 128
WINDOW = 128
N_BUCKETS = 32
MAX_DISTANCE = 128
C_HEADS = 4
C_QK_DIM = 64
C_V_DIM = 128
C_QK_WIDTH = 256
C_WIDTH = 512
IN_SPLITS = (A_WIDTH, A_WIDTH, B_WIDTH, B_KV_WIDTH, B_KV_WIDTH,
             C_QK_WIDTH, C_QK_WIDTH, C_WIDTH, C_WIDTH, C_HEADS, C_HEADS)
IN_OFFSETS = tuple(int(o) for o in np.cumsum(IN_SPLITS)[:-1])
D_FF = 5632
CONV_W = 3

Z_WIDTH = 4096
Z_AU, Z_AV, Z_BQ, Z_BK, Z_BV, Z_CQ, Z_CK, Z_CV, Z_CO, Z_GATE = (0,) + IN_OFFSETS[:9]
C_PAIR = 2 * C_V_DIM

MLSTM_L = 128
UP_CHUNK = 256
UP_ROWS = 256
C_SEQS = 1
LANES = 128
SUBLANES = 8
VMEM_LIMIT = 52 * 1024 * 1024


def _cparams(sem, flags=None):
    return pltpu.CompilerParams(dimension_semantics=sem, vmem_limit_bytes=VMEM_LIMIT, flags=flags)


def _gelu(x):
    return 0.5 * x * (1.0 + lax.erf(x * SQRT_HALF))


def _rms(x, gain):
    return x * lax.rsqrt(jnp.mean(x * x, axis=-1, keepdims=True) + EPS) * gain


def _norm_head_pairs(x, gain2):
    lo = lax.broadcasted_iota(jnp.int32, x.shape, 1) < B_HEAD_DIM
    x2 = x * x
    s_lo = jnp.sum(jnp.where(lo, x2, 0.0), axis=-1, keepdims=True)
    s_hi = jnp.sum(jnp.where(lo, 0.0, x2), axis=-1, keepdims=True)
    ms = jnp.where(lo, s_lo, s_hi) * (1.0 / B_HEAD_DIM)
    return x * lax.rsqrt(ms + EPS) * gain2


def _t5_bucket_np(dist):
    n = np.maximum(dist, 0)
    max_exact = N_BUCKETS // 2
    nf = np.maximum(n, 1).astype(np.float32)
    large = max_exact + (np.log(nf / np.float32(max_exact)) / np.float32(math.log(MAX_DISTANCE / max_exact))
                         * np.float32(N_BUCKETS - max_exact)).astype(np.int32)
    return np.where(n < max_exact, n, np.minimum(large, N_BUCKETS - 1)).astype(np.int32)


def _bias_kernel(rb_ref, bkp_ref, bks_ref, op_ref, os_ref):
    bkp = bkp_ref[...]
    bks = bks_ref[...]
    for h in range(B_HEADS):
        accp = jnp.zeros(bkp.shape, f32)
        accs = jnp.zeros(bks.shape, f32)
        for b in range(N_BUCKETS):
            val = rb_ref[b, h]
            accp = jnp.where(bkp == b, val, accp)
            accs = jnp.where(bks == b, val, accs)
        op_ref[h] = accp
        os_ref[h:h + 1, :] = accs[0:1, :]


def _bias_tables(rel_bias):
    qi = np.arange(WINDOW)[:, None]
    kj = np.arange(2 * WINDOW)[None, :]
    bkp = _t5_bucket_np(WINDOW + qi - kj)
    j = np.arange(2 * WINDOW)
    dist_s = np.where(j < WINDOW, WINDOW - j, 0)
    bks = np.broadcast_to(_t5_bucket_np(dist_s)[None, :], (SUBLANES, 2 * WINDOW)).copy()
    return pl.pallas_call(
        _bias_kernel,
        out_shape=(jax.ShapeDtypeStruct((B_HEADS, WINDOW, 2 * WINDOW), f32),
                   jax.ShapeDtypeStruct((B_HEADS, 2 * WINDOW), f32)),
        in_specs=[pl.BlockSpec(memory_space=pltpu.SMEM),
                  pl.BlockSpec(memory_space=pltpu.VMEM),
                  pl.BlockSpec(memory_space=pltpu.VMEM)],
        out_specs=(pl.BlockSpec(memory_space=pltpu.VMEM), pl.BlockSpec(memory_space=pltpu.VMEM)),
        name="bias_tables",
    )(rel_bias, jnp.asarray(bkp), jnp.asarray(bks))


def _mm_in_kernel(x_ref, g_ref, w_ref, z_ref, h_ref):
    @pl.when(pl.program_id(1) == 0)
    def _():
        h_ref[...] = _rms(x_ref[...], g_ref[...]).astype(bf16)

    z_ref[...] = jnp.dot(h_ref[...], w_ref[...], preferred_element_type=f32)


def _mm_in(x, gain, w, layer, tm, tn):
    M = x.shape[0]
    return pl.pallas_call(
        _mm_in_kernel,
        grid=(M // tm, Z_WIDTH // tn),
        in_specs=[pl.BlockSpec((tm, D_MODEL), lambda i, j: (i, 0)),
                  pl.BlockSpec((None, 1, D_MODEL), lambda i, j: (layer, 0, 0)),
                  pl.BlockSpec((None, D_MODEL, tn), lambda i, j: (layer, 0, j))],
        out_specs=pl.BlockSpec((tm, tn), lambda i, j: (i, j)),
        out_shape=jax.ShapeDtypeStruct((M, Z_WIDTH), f32),
        scratch_shapes=[pltpu.VMEM((tm, D_MODEL), bf16)],
        compiler_params=_cparams(("parallel", "arbitrary")),
        name="mm_in",
    )(x, gain, w)


def _mm_out_kernel(a_ref, b_ref, c_ref, x_ref, w_ref, o_ref, lhs_ref):
    @pl.when(pl.program_id(1) == 0)
    def _():
        lhs_ref[:, 0:A_WIDTH] = a_ref[...]
        lhs_ref[:, A_WIDTH:A_WIDTH + B_WIDTH] = b_ref[...]
        lhs_ref[:, A_WIDTH + B_WIDTH:D_MODEL] = c_ref[...]

    o_ref[...] = x_ref[...] + jnp.dot(lhs_ref[...], w_ref[...], preferred_element_type=f32)


def _mm_out(a, b, c, x, w, layer, tm, tn):
    M = x.shape[0]
    return pl.pallas_call(
        _mm_out_kernel,
        grid=(M // tm, D_MODEL // tn),
        in_specs=[pl.BlockSpec((tm, A_WIDTH), lambda i, j: (i, 0)),
                  pl.BlockSpec((tm, B_WIDTH), lambda i, j: (i, 0)),
                  pl.BlockSpec((tm, C_WIDTH), lambda i, j: (i, 0)),
                  pl.BlockSpec((tm, tn), lambda i, j: (i, j)),
                  pl.BlockSpec((None, D_MODEL, tn), lambda i, j: (layer, 0, j))],
        out_specs=pl.BlockSpec((tm, tn), lambda i, j: (i, j)),
        out_shape=jax.ShapeDtypeStruct((M, D_MODEL), f32),
        scratch_shapes=[pltpu.VMEM((tm, D_MODEL), bf16)],
        compiler_params=_cparams(("parallel", "arbitrary")),
        name="mm_out",
    )(a, b, c, x, w)


def _mm_down_kernel(a_ref, x_ref, w_ref, o_ref):
    o_ref[...] = x_ref[...] + jnp.dot(a_ref[...], w_ref[...], preferred_element_type=f32)


def _mm_down(act, x, w, layer, tm, tn):
    M = x.shape[0]
    return pl.pallas_call(
        _mm_down_kernel,
        grid=(M // tm, D_MODEL // tn),
        in_specs=[pl.BlockSpec((tm, D_FF), lambda i, j: (i, 0)),
                  pl.BlockSpec((tm, tn), lambda i, j: (i, j)),
                  pl.BlockSpec((None, D_FF, tn), lambda i, j: (layer, 0, j))],
        out_specs=pl.BlockSpec((tm, tn), lambda i, j: (i, j)),
        out_shape=jax.ShapeDtypeStruct((M, D_MODEL), f32),
        compiler_params=_cparams(("parallel", "arbitrary")),
        name="mm_down",
    )(act, x, w)


def _silu(x):
    return x * jax.nn.sigmoid(x)


def _mm_up_prompt_kernel(x_ref, n2_ref, wg_ref, wa_ref, cwg_ref, cwa_ref, cbg_ref, cba_ref,
                         act_ref, cs_ref, h_ref, carry_ref, zb_ref, *, tm, nj, n_steps, tiles_per_seq):
    s = pl.program_id(0)
    sa = jnp.minimum(s, n_steps - 1)
    ia = sa // nj
    ja = sa % nj
    tf = act_ref.shape[1]

    @pl.when(s == 0)
    def _():
        zb_ref[...] = jnp.zeros(zb_ref.shape, f32)
        carry_ref[...] = jnp.zeros(carry_ref.shape, f32)

    @pl.when(ja == 0)
    def _():
        h_ref[...] = _rms(x_ref[...], n2_ref[...]).astype(bf16)

    chunks = [slice(c0, c0 + UP_CHUNK) for c0 in range(0, tf, UP_CHUNK)]

    rblocks = [(r0, min(UP_ROWS, tm - r0)) for r0 in range(0, tm, UP_ROWS)]

    def conv(idx, cw_ref, cb_ref, cs, r0, nr):
        return (cb_ref[:, cs]
                + zb_ref[idx, r0 + SUBLANES - 2:r0 + SUBLANES - 2 + nr, cs] * cw_ref[0:1, cs]
                + zb_ref[idx, r0 + SUBLANES - 1:r0 + SUBLANES - 1 + nr, cs] * cw_ref[1:2, cs]
                + zb_ref[idx, r0 + SUBLANES:r0 + SUBLANES + nr, cs] * cw_ref[2:3, cs])

    for cs in chunks:
        for r0, nr in rblocks:
            g = conv(0, cwg_ref, cbg_ref, cs, r0, nr)
            a = conv(1, cwa_ref, cba_ref, cs, r0, nr)
            act_ref[r0:r0 + nr, cs] = (_silu(g) * a).astype(bf16)

    seq_start = (ia % tiles_per_seq) == 0
    for cs in chunks:
        for r0, nr in rblocks:
            for idx, w_ref in ((0, wg_ref), (1, wa_ref)):
                z = jnp.dot(h_ref[r0:r0 + nr, :], w_ref[:, cs], preferred_element_type=f32)
                if r0 == 0:
                    zb_ref[idx, 0:SUBLANES, cs] = jnp.where(seq_start, 0.0, carry_ref[idx, ja, :, cs])
                zb_ref[idx, SUBLANES + r0:SUBLANES + r0 + nr, cs] = z
                if r0 + nr == tm:
                    tail = z[nr - SUBLANES:nr, :]
                    carry_ref[idx, ja, :, cs] = tail
                    cs_ref[0, idx, :, cs] = tail


def _mm_up_prompt(x, norm2, w_up, conv_w, conv_b, layer, batch, tm, tf):
    M = x.shape[0]
    seq = M // batch
    tiles_per_seq = seq // tm
    nj = D_FF // tf
    n_steps = (M // tm) * nj
    kern = functools.partial(_mm_up_prompt_kernel, tm=tm, nj=nj, n_steps=n_steps, tiles_per_seq=tiles_per_seq)
    ia = lambda s: jnp.minimum(s, n_steps - 1) // nj
    ja = lambda s: jnp.minimum(s, n_steps - 1) % nj
    ib = lambda s: jnp.maximum(s - 1, 0) // nj
    jb = lambda s: jnp.maximum(s - 1, 0) % nj
    return pl.pallas_call(
        kern,
        grid=(n_steps + 1,),
        in_specs=[pl.BlockSpec((tm, D_MODEL), lambda s: (ia(s), 0)),
                  pl.BlockSpec((None, 1, D_MODEL), lambda s: (layer, 0, 0)),
                  pl.BlockSpec((None, D_MODEL, tf), lambda s: (layer, 0, ja(s))),
                  pl.BlockSpec((None, D_MODEL, tf), lambda s: (layer, 0, nj + ja(s))),
                  pl.BlockSpec((None, CONV_W, tf), lambda s: (layer, 0, jb(s))),
                  pl.BlockSpec((None, CONV_W, tf), lambda s: (layer, 0, nj + jb(s))),
                  pl.BlockSpec((None, 1, tf), lambda s: (layer, 0, jb(s))),
                  pl.BlockSpec((None, 1, tf), lambda s: (layer, 0, nj + jb(s)))],
        out_specs=(pl.BlockSpec((tm, tf), lambda s: (ib(s), jb(s))),
                   pl.BlockSpec((1, 2, SUBLANES, tf), lambda s: (ia(s), 0, 0, ja(s)))),
        out_shape=(jax.ShapeDtypeStruct((M, D_FF), bf16),
                   jax.ShapeDtypeStruct((M // tm, 2, SUBLANES, D_FF), f32)),
        scratch_shapes=[pltpu.VMEM((tm, D_MODEL), bf16),
                        pltpu.VMEM((2, nj, SUBLANES, tf), f32),
                        pltpu.VMEM((2, tm + SUBLANES, tf), f32)],
        compiler_params=_cparams(("arbitrary",)),
        name="mm_up_prompt",
    )(x, norm2, w_up, w_up, conv_w, conv_w, conv_b, conv_b)


def _mm_up_sample_kernel(x_ref, n2_ref, wg_ref, wa_ref, cwg_ref, cwa_ref, cbg_ref, cba_ref,
                         b0g_ref, b1g_ref, b0a_ref, b1a_ref, act_ref, zg_ref, za_ref, h_ref):
    @pl.when(pl.program_id(0) == 0)
    def _():
        h_ref[...] = _rms(x_ref[...], n2_ref[...]).astype(bf16)

    def conv_half(w_ref, cw_ref, cb_ref, b0_ref, b1_ref, z_ref):
        z = jnp.dot(h_ref[...], w_ref[...], preferred_element_type=f32)
        z_ref[...] = z
        return (cb_ref[...] + b0_ref[...] * cw_ref[0:1, :] + b1_ref[...] * cw_ref[1:2, :]
                + z * cw_ref[2:3, :])

    g = conv_half(wg_ref, cwg_ref, cbg_ref, b0g_ref, b1g_ref, zg_ref)
    a = conv_half(wa_ref, cwa_ref, cba_ref, b0a_ref, b1a_ref, za_ref)
    act_ref[...] = (_silu(g) * a).astype(bf16)


def _mm_up_sample(x, norm2, w_up, conv_w, conv_b, buf, layer, tf):
    M = x.shape[0]
    nj = D_FF // tf
    wspec = lambda off: pl.BlockSpec((None, D_MODEL, tf), lambda j: (layer, 0, off + j))
    cwspec = lambda off: pl.BlockSpec((None, CONV_W, tf), lambda j: (layer, 0, off + j))
    cbspec = lambda off: pl.BlockSpec((None, 1, tf), lambda j: (layer, 0, off + j))
    bufspec = lambda row, off: pl.BlockSpec((None, None, M, tf), lambda j: (layer, row, 0, off + j))
    return pl.pallas_call(
        _mm_up_sample_kernel,
        grid=(nj,),
        in_specs=[pl.BlockSpec((M, D_MODEL), lambda j: (0, 0)),
                  pl.BlockSpec((None, 1, D_MODEL), lambda j: (layer, 0, 0)),
                  wspec(0), wspec(nj), cwspec(0), cwspec(nj), cbspec(0), cbspec(nj),
                  bufspec(0, 0), bufspec(1, 0), bufspec(0, nj), bufspec(1, nj)],
        out_specs=(pl.BlockSpec((M, tf), lambda j: (0, j)),
                   pl.BlockSpec((M, tf), lambda j: (0, j)),
                   pl.BlockSpec((M, tf), lambda j: (0, j))),
        out_shape=(jax.ShapeDtypeStruct((M, D_FF), bf16),
                   jax.ShapeDtypeStruct((M, D_FF), f32),
                   jax.ShapeDtypeStruct((M, D_FF), f32)),
        scratch_shapes=[pltpu.VMEM((M, D_MODEL), bf16)],
        compiler_params=_cparams(("arbitrary",)),
        name="mm_up_sample",
    )(x, norm2, w_up, w_up, conv_w, conv_w, conv_b, conv_b, buf, buf, buf, buf)


def _mix_a_prompt_kernel(u_ref, v_ref, vg_ref, ws_ref, bs_ref, o_ref):
    row = lax.broadcasted_iota(jnp.int32, (CHUNK, CHUNK), 0)
    col = lax.broadcasted_iota(jnp.int32, (CHUNK, CHUNK), 1)
    tri = col <= row
    for g in range(A_GROUPS):
        sl = slice(g * A_DIM, (g + 1) * A_DIM)
        ws = jnp.where(tri, ws_ref[g], 0.0).astype(bf16)
        for r0 in range(0, u_ref.shape[0], CHUNK):
            rs = slice(r0, r0 + CHUNK)
            vn = _rms(_gelu(v_ref[rs, sl]), vg_ref[:, sl])
            mixv = jnp.dot(ws, vn.astype(bf16), preferred_element_type=f32) + bs_ref[:, sl]
            o_ref[rs, sl] = (_gelu(u_ref[rs, sl]) * mixv).astype(bf16)


def _mix_a_prompt(z, v_gain, w_s, b_s_full, layer, rows):
    M = z.shape[0]
    return pl.pallas_call(
        _mix_a_prompt_kernel,
        grid=(M // rows,),
        in_specs=[pl.BlockSpec((rows, A_WIDTH), lambda r: (r, Z_AU // A_WIDTH)),
                  pl.BlockSpec((rows, A_WIDTH), lambda r: (r, Z_AV // A_WIDTH)),
                  pl.BlockSpec((None, 1, A_WIDTH), lambda r: (layer, 0, 0)),
                  pl.BlockSpec((None, A_GROUPS, CHUNK, CHUNK), lambda r: (layer, 0, 0, 0)),
                  pl.BlockSpec((None, CHUNK, A_WIDTH), lambda r: (layer, 0, 0))],
        out_specs=pl.BlockSpec((rows, A_WIDTH), lambda r: (r, 0)),
        out_shape=jax.ShapeDtypeStruct((M, A_WIDTH), bf16),
        compiler_params=_cparams(("parallel",)),
        name="mix_a_prompt",
    )(z, z, v_gain, w_s, b_s_full)


def _mix_a_sample_kernel(u_ref, v_ref, vg_ref, ws_ref, bs_ref, o_ref, vn_ref):
    for g in range(A_GROUPS):
        sl = slice(g * A_DIM, (g + 1) * A_DIM)
        vn = _rms(_gelu(v_ref[:, sl]), vg_ref[:, sl])
        vn_ref[:, sl] = vn
        mixv = ws_ref[:, sl] * vn + bs_ref[:, sl]
        o_ref[:, sl] = (_gelu(u_ref[:, sl]) * mixv).astype(bf16)


def _mix_a_sample(z, v_gain, ws0, bs0, layer):
    M = z.shape[0]
    vec = pl.BlockSpec((None, 1, A_WIDTH), lambda r: (layer, 0, 0))
    return pl.pallas_call(
        _mix_a_sample_kernel,
        grid=(1,),
        in_specs=[pl.BlockSpec((M, A_WIDTH), lambda r: (0, Z_AU // A_WIDTH)),
                  pl.BlockSpec((M, A_WIDTH), lambda r: (0, Z_AV // A_WIDTH)),
                  vec, vec, vec],
        out_specs=(pl.BlockSpec((M, A_WIDTH), lambda r: (0, 0)),
                   pl.BlockSpec((M, A_WIDTH), lambda r: (0, 0))),
        out_shape=(jax.ShapeDtypeStruct((M, A_WIDTH), bf16),
                   jax.ShapeDtypeStruct((M, A_WIDTH), f32)),
        compiler_params=_cparams(("arbitrary",)),
        name="mix_a_sample",
    )(z, z, v_gain, ws0, bs0)


def _mix_b_prompt_kernel(sink_ref, q_ref, k_ref, v_ref, qg_ref, kg_ref, bias_ref,
                         o_ref, klast_ref, vlast_ref, kcat_ref, vcat_ref):
    i = pl.program_id(1)
    W = WINDOW
    kn = _norm_head_pairs(k_ref[...], kg_ref[...])
    v = v_ref[...]
    klast_ref[0] = kn
    vlast_ref[0] = v

    @pl.when(i == 0)
    def _():
        kcat_ref[0:W, :] = jnp.zeros((W, LANES), f32)
        vcat_ref[0:W, :] = jnp.zeros((W, LANES), f32)

    @pl.when(i > 0)
    def _():
        kcat_ref[0:W, :] = kcat_ref[W:2 * W, :]
        vcat_ref[0:W, :] = vcat_ref[W:2 * W, :]

    kcat_ref[W:2 * W, :] = kn
    vcat_ref[W:2 * W, :] = v
    kc = kcat_ref[...]
    vc = vcat_ref[...]
    kc_sw = pltpu.roll(kc, B_HEAD_DIM, 1)
    vc_sw = pltpu.roll(vc, B_HEAD_DIM, 1)
    lo = lax.broadcasted_iota(jnp.int32, (2 * W, LANES), 1) < B_HEAD_DIM

    qi = lax.broadcasted_iota(jnp.int32, (W, 2 * W), 0)
    kj = lax.broadcasted_iota(jnp.int32, (W, 2 * W), 1)
    dist = W + qi - kj
    valid = (dist >= 0) & (dist < W) & ((kj >= W) | (i > 0))

    tiles = GQA // 2
    for kvh in range(B_KV_HEADS):
        k_src, k_alt = (kc, kc_sw) if kvh == 0 else (kc_sw, kc)
        v_src, v_alt = (vc, vc_sw) if kvh == 0 else (vc_sw, vc)
        k_even = jnp.where(lo, k_src, 0.0).astype(bf16)
        k_odd = jnp.where(lo, 0.0, k_alt).astype(bf16)
        v_even = jnp.where(lo, v_src, 0.0).astype(bf16)
        v_odd = jnp.where(lo, 0.0, v_alt).astype(bf16)
        qs = jnp.concatenate(
            [_norm_head_pairs(q_ref[:, (kvh * tiles + t) * LANES:(kvh * tiles + t + 1) * LANES], qg_ref[...])
             for t in range(tiles)], axis=0).astype(bf16)
        dn = (((1,), (1,)), ((), ()))
        s_par = [lax.dot_general(qs, k_even, dn, preferred_element_type=f32),
                 lax.dot_general(qs, k_odd, dn, preferred_element_type=f32)]
        p_par = []
        for par in range(2):
            blocks = []
            for t in range(tiles):
                h = kvh * GQA + 2 * t + par
                s = s_par[par][t * W:(t + 1) * W, :] * (B_HEAD_DIM ** -0.5) + bias_ref[h]
                s = jnp.where(valid, s, NEG_INF)
                sk = sink_ref[h]
                mx = jnp.maximum(jnp.max(s, axis=-1, keepdims=True), sk)
                p = jnp.exp(s - mx)
                den = jnp.sum(p, axis=-1, keepdims=True) + jnp.exp(sk - mx)
                blocks.append((p * (1.0 / den)).astype(bf16))
            p_par.append(jnp.concatenate(blocks, axis=0))
        o = (jnp.dot(p_par[0], v_even, preferred_element_type=f32)
             + jnp.dot(p_par[1], v_odd, preferred_element_type=f32))
        for t in range(tiles):
            c0 = (kvh * tiles + t) * LANES
            o_ref[:, c0:c0 + LANES] = o[t * W:(t + 1) * W, :].astype(bf16)


def _mix_b_prompt(z, sinks, q_gain2, k_gain2, bias_p, layer, batch):
    M = z.shape[0]
    nb = M // batch // WINDOW
    return pl.pallas_call(
        _mix_b_prompt_kernel,
        grid=(batch, nb),
        in_specs=[pl.BlockSpec(memory_space=pltpu.SMEM),
                  pl.BlockSpec((WINDOW, B_WIDTH), lambda b, i: (b * nb + i, Z_BQ // B_WIDTH)),
                  pl.BlockSpec((WINDOW, B_KV_WIDTH), lambda b, i: (b * nb + i, Z_BK // B_KV_WIDTH)),
                  pl.BlockSpec((WINDOW, B_KV_WIDTH), lambda b, i: (b * nb + i, Z_BV // B_KV_WIDTH)),
                  pl.BlockSpec((None, 1, LANES), lambda b, i: (layer, 0, 0)),
                  pl.BlockSpec((None, 1, LANES), lambda b, i: (layer, 0, 0)),
                  pl.BlockSpec((B_HEADS, WINDOW, 2 * WINDOW), lambda b, i: (0, 0, 0))],
        out_specs=(pl.BlockSpec((WINDOW, B_WIDTH), lambda b, i: (b * nb + i, 0)),
                   pl.BlockSpec((1, WINDOW, B_KV_WIDTH), lambda b, i: (b, 0, 0)),
                   pl.BlockSpec((1, WINDOW, B_KV_WIDTH), lambda b, i: (b, 0, 0))),
        out_shape=(jax.ShapeDtypeStruct((M, B_WIDTH), bf16),
                   jax.ShapeDtypeStruct((batch, WINDOW, B_KV_WIDTH), f32),
                   jax.ShapeDtypeStruct((batch, WINDOW, B_KV_WIDTH), f32)),
        scratch_shapes=[pltpu.VMEM((2 * WINDOW, LANES), f32),
                        pltpu.VMEM((2 * WINDOW, LANES), f32)],
        compiler_params=_cparams(("arbitrary", "arbitrary")),
        name="mix_b_prompt",
    )(sinks, z, z, z, q_gain2, k_gain2, bias_p)


def _mix_c_prompt_kernel(q_ref, k_ref, v0_ref, v1_ref, o0_ref, o1_ref, g_ref, gb_ref, hg_ref,
                         out_ref, c_out, n_out, m_out, c_s, n_s, m_s):
    L = MLSTM_L
    c = pl.program_id(1)

    @pl.when(c == 0)
    def _():
        c_s[...] = jnp.zeros(c_s.shape, f32)
        n_s[...] = jnp.zeros(n_s.shape, f32)
        m_s[...] = jnp.zeros(m_s.shape, f32)

    row = lax.broadcasted_iota(jnp.int32, (L, L), 0)
    col = lax.broadcasted_iota(jnp.int32, (L, L), 1)
    tri = col <= row
    tri_f = tri.astype(f32)
    for bi in range(q_ref.shape[0]):
        g = g_ref[bi] + gb_ref[...]
        lf = jax.nn.log_sigmoid(g)
        b = jnp.dot(tri_f, lf, precision=lax.Precision.HIGHEST, preferred_element_type=f32)
        bT = b.T
        gT = g.T
        for h in range(C_HEADS):
            b_col = b[:, C_HEADS + h:C_HEADS + h + 1]
            i_col = g[:, h:h + 1]
            b_row = bT[C_HEADS + h:C_HEADS + h + 1, :]
            i_row = gT[h:h + 1, :]
            dlog = jnp.where(tri, b_col - b_row + i_row, -jnp.inf)
            m_prev = m_s[bi, h][0:1, 0:1]
            inter = b_col + m_prev
            mt = jnp.maximum(inter, jnp.max(dlog, axis=-1, keepdims=True))
            v_ref, o_ref = (v0_ref, o0_ref) if h < 2 else (v1_ref, o1_ref)
            vsl = slice((h % 2) * C_V_DIM, (h % 2 + 1) * C_V_DIM)
            qh = q_ref[bi, :, h * C_QK_DIM:(h + 1) * C_QK_DIM]
            kh = k_ref[bi, :, h * C_QK_DIM:(h + 1) * C_QK_DIM] * (C_QK_DIM ** -0.5)
            vh = v_ref[bi, :, vsl]
            qb, kb, vb = qh.astype(bf16), kh.astype(bf16), vh.astype(bf16)
            s = lax.dot_general(qb, kb, (((1,), (1,)), ((), ())), preferred_element_type=f32) * jnp.exp(dlog - mt)
            iw = jnp.exp(inter - mt)
            c_prev = c_s[bi, h]
            n_prev = n_s[bi, h]
            num = (iw * jnp.dot(qb, c_prev.astype(bf16), preferred_element_type=f32)
                   + jnp.dot(s.astype(bf16), vb, preferred_element_type=f32))
            qn = iw * jnp.sum(qh * n_prev, axis=-1, keepdims=True) + jnp.sum(s, axis=-1, keepdims=True)
            hh = num / jnp.maximum(jnp.abs(qn), jnp.exp(-mt))
            m_new = mt[L - 1:L, :]
            b_last = b_col[L - 1:L, :]
            wc = jnp.exp(b_last - b_col + i_col - m_new)
            decay = jnp.exp(b_last + m_prev - m_new)
            kw = kh * wc
            c_s[bi, h] = decay * c_prev + lax.dot_general(kw.astype(bf16), vb, (((0,), (0,)), ((), ())),
                                                          preferred_element_type=f32)
            n_s[bi, h] = decay * n_prev + jnp.sum(kw, axis=0, keepdims=True)
            m_s[bi, h] = jnp.broadcast_to(m_new, (SUBLANES, LANES))
            osl = slice(h * C_V_DIM, (h + 1) * C_V_DIM)
            out_ref[bi, :, osl] = (_rms(hh, hg_ref[:, osl]) * jax.nn.sigmoid(o_ref[bi, :, vsl])).astype(bf16)

    @pl.when(c == pl.num_programs(1) - 1)
    def _():
        c_out[...] = c_s[...]
        n_out[...] = n_s[...]
        m_out[...] = m_s[...]


def _mix_c_prompt(z, gate_bias, h_gain, layer, batch):
    M = z.shape[0]
    L = MLSTM_L
    T = M // batch
    z3 = z.reshape(batch, T, Z_WIDTH)
    ns = C_SEQS if batch % C_SEQS == 0 else 1
    col = lambda width, off: pl.BlockSpec((ns, L, width), lambda b, c: (b, c, off // width))
    state = lambda *shape: pl.BlockSpec((ns,) + shape, lambda b, c: (b,) + (0,) * len(shape))
    out, c_st, n_st, m_st = pl.pallas_call(
        _mix_c_prompt_kernel,
        grid=(batch // ns, T // L),
        in_specs=[col(C_QK_WIDTH, Z_CQ), col(C_QK_WIDTH, Z_CK),
                  col(C_PAIR, Z_CV), col(C_PAIR, Z_CV + C_PAIR),
                  col(C_PAIR, Z_CO), col(C_PAIR, Z_CO + C_PAIR),
                  col(LANES, Z_GATE),
                  pl.BlockSpec((None, 1, LANES), lambda b, c: (layer, 0, 0)),
                  pl.BlockSpec((None, 1, C_WIDTH), lambda b, c: (layer, 0, 0))],
        out_specs=(pl.BlockSpec((ns, L, C_WIDTH), lambda b, c: (b, c, 0)),
                   state(C_HEADS, C_QK_DIM, C_V_DIM),
                   state(C_HEADS, 1, C_QK_DIM),
                   state(C_HEADS, SUBLANES, LANES)),
        out_shape=(jax.ShapeDtypeStruct((batch, T, C_WIDTH), bf16),
                   jax.ShapeDtypeStruct((batch, C_HEADS, C_QK_DIM, C_V_DIM), f32),
                   jax.ShapeDtypeStruct((batch, C_HEADS, 1, C_QK_DIM), f32),
                   jax.ShapeDtypeStruct((batch, C_HEADS, SUBLANES, LANES), f32)),
        scratch_shapes=[pltpu.VMEM((ns, C_HEADS, C_QK_DIM, C_V_DIM), f32),
                        pltpu.VMEM((ns, C_HEADS, 1, C_QK_DIM), f32),
                        pltpu.VMEM((ns, C_HEADS, SUBLANES, LANES), f32)],
        compiler_params=_cparams(("arbitrary", "arbitrary")),
        name="mix_c_prompt",
    )(z3, z3, z3, z3, z3, z3, z3, gate_bias, h_gain)
    return out.reshape(M, C_WIDTH), c_st, n_st, m_st


def _mix_bc_sample_kernel(zr_ref, q_ref, kp_ref, vp_ref, qg_ref, kg_ref, sink_ref, bias_ref,
                          cq_ref, ck_ref, cv_ref, co_ref, gb_ref, hg_ref, c0_ref, n0_ref, m0_ref,
                          bo_ref, kn_ref, co_out, c_out, n_out, m_out):
    W = WINDOW
    zr = zr_ref[0]

    qn = _rms(q_ref[0], qg_ref[...])
    hrow = lax.broadcasted_iota(jnp.int32, (B_HEADS, LANES), 0)
    lane = lax.broadcasted_iota(jnp.int32, (B_HEADS, LANES), 1)
    own = (hrow < GQA) == (lane < B_HEAD_DIM)
    q2 = jnp.where(own, jnp.concatenate([qn, qn], axis=-1), 0.0)
    k_new = _norm_head_pairs(zr[:, Z_BK:Z_BK + B_KV_WIDTH], kg_ref[...])
    v_new = zr[:, Z_BV:Z_BV + B_KV_WIDTH]
    kn_ref[0] = k_new
    scale = B_HEAD_DIM ** -0.5
    s_past = lax.dot_general(q2.astype(bf16), kp_ref[0].astype(bf16), (((1,), (1,)), ((), ())),
                             preferred_element_type=f32) * scale + bias_ref[:, 0:W]
    s_past = jnp.where(lane >= 1, s_past, NEG_INF)
    s_new = jnp.sum(q2 * k_new, axis=-1, keepdims=True) * scale + bias_ref[:, W:W + 1]
    sk = sink_ref[...]
    mx = jnp.maximum(jnp.maximum(jnp.max(s_past, axis=-1, keepdims=True), s_new), sk)
    p_past = jnp.exp(s_past - mx)
    p_new = jnp.exp(s_new - mx)
    den = jnp.sum(p_past, axis=-1, keepdims=True) + p_new + jnp.exp(sk - mx)
    inv = 1.0 / den
    o2 = (jnp.dot((p_past * inv).astype(bf16), vp_ref[0].astype(bf16), preferred_element_type=f32)
          + (p_new * inv) * v_new)
    o2_sw = pltpu.roll(o2, B_HEAD_DIM, 1)
    bo_ref[0] = jnp.where(hrow < GQA, o2, o2_sw)[:, 0:B_HEAD_DIM].astype(bf16)

    g = zr[:, Z_GATE:Z_GATE + LANES] + gb_ref[...]
    lf = jax.nn.log_sigmoid(g)
    m0 = m0_ref[0]
    lane1 = lax.broadcasted_iota(jnp.int32, (1, LANES), 1)
    m_row = jnp.zeros((1, LANES), f32)
    row8 = lax.broadcasted_iota(jnp.int32, (SUBLANES, C_QK_DIM), 0)
    c_rows = []
    for h in range(C_HEADS):
        ig = g[:, h:h + 1]
        b = lf[:, C_HEADS + h:C_HEADS + h + 1]
        m_prev = m0[:, h:h + 1]
        inter = b + m_prev
        mt = jnp.maximum(inter, ig)
        qh = cq_ref[0, h:h + 1, :]
        kh = ck_ref[0, h:h + 1, :] * (C_QK_DIM ** -0.5)
        vh = cv_ref[0, h:h + 1, :]
        c_prev = c0_ref[0, h]
        n_prev = n0_ref[0, h:h + 1, :]
        s = jnp.sum(qh * kh, axis=-1, keepdims=True) * jnp.exp(ig - mt)
        iw = jnp.exp(inter - mt)
        q8 = jnp.broadcast_to(qh, (SUBLANES, C_QK_DIM)).astype(bf16)
        qc = jnp.dot(q8, c_prev.astype(bf16), preferred_element_type=f32)[0:1, :]
        num = iw * qc + s * vh
        qn_ = iw * jnp.sum(qh * n_prev, axis=-1, keepdims=True) + s
        hh = num / jnp.maximum(jnp.abs(qn_), jnp.exp(-mt))
        wc = jnp.exp(ig - mt)
        decay = jnp.exp(inter - mt)
        kw = kh * wc
        kw8 = jnp.where(row8 == 0, jnp.broadcast_to(kw, (SUBLANES, C_QK_DIM)), 0.0).astype(bf16)
        v8 = jnp.broadcast_to(vh, (SUBLANES, C_V_DIM)).astype(bf16)
        c_out[0, h] = decay * c_prev + lax.dot_general(kw8, v8, (((0,), (0,)), ((), ())),
                                                      preferred_element_type=f32)
        n_out[0, h:h + 1, :] = decay * n_prev + kw
        m_row = jnp.where(lane1 == h, mt, m_row)
        c_rows.append(_rms(hh, hg_ref[h:h + 1, :]) * jax.nn.sigmoid(co_ref[0, h:h + 1, :]))
    co_out[0] = jnp.concatenate(c_rows, axis=0).astype(bf16)
    m_out[0] = m_row


def _mix_bc_sample(z, k_past, v_past, q_gain, k_gain2, sinks, bias_s, gate_bias, h_gain4,
                   c0, n0, m0, layer):
    R = z.shape[0]
    zr = z.reshape(R, 1, Z_WIDTH)
    q = z[:, Z_BQ:Z_BQ + B_WIDTH].reshape(R, B_HEADS, B_HEAD_DIM)
    cq = z[:, Z_CQ:Z_CQ + C_QK_WIDTH].reshape(R, C_HEADS, C_QK_DIM)
    ck = z[:, Z_CK:Z_CK + C_QK_WIDTH].reshape(R, C_HEADS, C_QK_DIM)
    cv = z[:, Z_CV:Z_CV + C_WIDTH].reshape(R, C_HEADS, C_V_DIM)
    co = z[:, Z_CO:Z_CO + C_WIDTH].reshape(R, C_HEADS, C_V_DIM)
    row3 = lambda n: pl.BlockSpec((1, 1, n), lambda r: (r, 0, 0))
    return pl.pallas_call(
        _mix_bc_sample_kernel,
        grid=(R,),
        in_specs=[row3(Z_WIDTH),
                  pl.BlockSpec((1, B_HEADS, B_HEAD_DIM), lambda r: (r, 0, 0)),
                  pl.BlockSpec((None, 1, WINDOW, B_KV_WIDTH), lambda r: (layer, r, 0, 0)),
                  pl.BlockSpec((None, 1, WINDOW, B_KV_WIDTH), lambda r: (layer, r, 0, 0)),
                  pl.BlockSpec((None, 1, B_HEAD_DIM), lambda r: (layer, 0, 0)),
                  pl.BlockSpec((None, 1, LANES), lambda r: (layer, 0, 0)),
                  pl.BlockSpec((None, B_HEADS, 1), lambda r: (layer, 0, 0)),
                  pl.BlockSpec((B_HEADS, 2 * WINDOW), lambda r: (0, 0)),
                  pl.BlockSpec((1, C_HEADS, C_QK_DIM), lambda r: (r, 0, 0)),
                  pl.BlockSpec((1, C_HEADS, C_QK_DIM), lambda r: (r, 0, 0)),
                  pl.BlockSpec((1, C_HEADS, C_V_DIM), lambda r: (r, 0, 0)),
                  pl.BlockSpec((1, C_HEADS, C_V_DIM), lambda r: (r, 0, 0)),
                  pl.BlockSpec((None, 1, LANES), lambda r: (layer, 0, 0)),
                  pl.BlockSpec((None, C_HEADS, C_V_DIM), lambda r: (layer, 0, 0)),
                  pl.BlockSpec((None, 1, C_HEADS, C_QK_DIM, C_V_DIM), lambda r: (layer, r, 0, 0, 0)),
                  pl.BlockSpec((None, 1, C_HEADS, C_QK_DIM), lambda r: (layer, r, 0, 0)),
                  pl.BlockSpec((None, 1, 1, C_HEADS), lambda r: (layer, r, 0, 0))],
        out_specs=(pl.BlockSpec((1, B_HEADS, B_HEAD_DIM), lambda r: (r, 0, 0)),
                   row3(B_KV_WIDTH),
                   pl.BlockSpec((1, C_HEADS, C_V_DIM), lambda r: (r, 0, 0)),
                   pl.BlockSpec((1, C_HEADS, C_QK_DIM, C_V_DIM), lambda r: (r, 0, 0, 0)),
                   pl.BlockSpec((1, C_HEADS, C_QK_DIM), lambda r: (r, 0, 0)),
                   row3(LANES)),
        out_shape=(jax.ShapeDtypeStruct((R, B_HEADS, B_HEAD_DIM), bf16),
                   jax.ShapeDtypeStruct((R, 1, B_KV_WIDTH), f32),
                   jax.ShapeDtypeStruct((R, C_HEADS, C_V_DIM), bf16),
                   jax.ShapeDtypeStruct((R, C_HEADS, C_QK_DIM, C_V_DIM), f32),
                   jax.ShapeDtypeStruct((R, C_HEADS, C_QK_DIM), f32),
                   jax.ShapeDtypeStruct((R, 1, LANES), f32)),
        compiler_params=_cparams(("parallel",)),
        name="mix_bc_sample",
    )(zr, q, k_past, v_past, q_gain, k_gain2, sinks, bias_s, cq, ck, cv, co, gate_bias, h_gain4,
      c0, n0, m0)


def _prep_w_in(w_in):
    return jnp.pad(w_in.astype(bf16), ((0, 0), (0, 0), (0, Z_WIDTH - w_in.shape[-1])))


def _row_tile(m, pref):
    return pref if m % pref == 0 else m


def kernel(x_prompt, x_sample, cache_swa_k, cache_swa_v, state_mlstm_C, state_mlstm_n, state_mlstm_m, state_ffn_conv, rel_bias, norm1, w_in, a_v_gain, a_spatial_w, a_spatial_b, b_q_gain, b_k_gain, b_sinks, c_gate_bias, c_h_gain, w_out, norm2, w_up, ffn_conv_w, ffn_conv_b, w_down):
    depth = w_in.shape[0]
    Bp, T, _ = x_prompt.shape
    R = x_sample.shape[0]
    assert x_sample.shape[1] == 1 and T % CHUNK == 0 and T % MLSTM_L == 0

    w_in_p = _prep_w_in(w_in)
    w_out_b = w_out.astype(bf16)
    w_up_b = w_up.astype(bf16)
    w_down_b = w_down.astype(bf16)
    norm1_3 = norm1.reshape(depth, 1, D_MODEL)
    norm2_3 = norm2.reshape(depth, 1, D_MODEL)
    v_gain3 = a_v_gain.reshape(depth, 1, A_WIDTH)
    bs_full = jnp.repeat(jnp.swapaxes(a_spatial_b, 1, 2), A_DIM, axis=-1)
    ws0 = jnp.repeat(a_spatial_w[:, :, 0, 0], A_DIM, axis=-1).reshape(depth, 1, A_WIDTH)
    bs0 = jnp.repeat(a_spatial_b[:, :, 0], A_DIM, axis=-1).reshape(depth, 1, A_WIDTH)
    q_gain3 = b_q_gain.reshape(depth, 1, B_HEAD_DIM)
    q_gain2 = jnp.tile(b_q_gain, (1, 2)).reshape(depth, 1, LANES)
    k_gain2 = jnp.tile(b_k_gain, (1, 2)).reshape(depth, 1, LANES)
    sinks3 = b_sinks.reshape(depth, B_HEADS, 1)
    gate_b = jnp.pad(c_gate_bias, ((0, 0), (0, LANES - 2 * C_HEADS))).reshape(depth, 1, LANES)
    h_gain3 = c_h_gain.reshape(depth, 1, C_WIDTH)
    h_gain4 = c_h_gain.reshape(depth, C_HEADS, C_V_DIM)
    conv_b3 = ffn_conv_b.reshape(depth, 1, 2 * D_FF)
    k_cache = cache_swa_k.reshape(depth, R, WINDOW, B_KV_WIDTH)
    v_cache = cache_swa_v.reshape(depth, R, WINDOW, B_KV_WIDTH)
    m_state = state_mlstm_m.reshape(depth, R, 1, C_HEADS)
    conv_hist = jnp.swapaxes(state_ffn_conv, 1, 2)

    bias_p, bias_s = _bias_tables(rel_bias)

    Mp = Bp * T
    tm = _row_tile(T, 1024)
    a_rows = _row_tile(T, 512)
    xp = x_prompt.reshape(Mp, D_MODEL)
    xs = x_sample.reshape(R, D_MODEL)
    P = [[] for _ in range(6)]
    S = [[] for _ in range(7)]
    for l in range(depth):
        z = _mm_in(xp, norm1_3, w_in_p, l, tm, 1024)
        a_o = _mix_a_prompt(z, v_gain3, a_spatial_w, bs_full, l, a_rows)
        b_o, k_last, v_last = _mix_b_prompt(z, b_sinks[l], q_gain2, k_gain2, bias_p, l, Bp)
        c_o, c_st, n_st, m_st = _mix_c_prompt(z, gate_b, h_gain3, l, Bp)
        x1 = _mm_out(a_o, b_o, c_o, xp, w_out_b, l, tm, 1024)
        act, cs = _mm_up_prompt(x1, norm2_3, w_up_b, ffn_conv_w, conv_b3, l, Bp, tm, 512)
        xp = _mm_down(act, x1, w_down_b, l, tm, 512)
        P[0].append(k_last.reshape(Bp, WINDOW, B_KV_HEADS, B_HEAD_DIM))
        P[1].append(v_last.reshape(Bp, WINDOW, B_KV_HEADS, B_HEAD_DIM))
        P[2].append(c_st)
        P[3].append(n_st.reshape(Bp, C_HEADS, C_QK_DIM))
        P[4].append(m_st[:, :, 0, 0])
        seq_tiles = T // tm
        tail = cs[seq_tiles - 1::seq_tiles, :, SUBLANES - (CONV_W - 1):, :]
        P[5].append(jnp.swapaxes(tail, 1, 2).reshape(Bp, CONV_W - 1, 2 * D_FF))

        zs = _mm_in(xs, norm1_3, w_in_p, l, R, 1024)
        a_s, vn_s = _mix_a_sample(zs, v_gain3, ws0, bs0, l)
        b_s, kn_s, c_s, c_new, n_new, m_new = _mix_bc_sample(
            zs, k_cache, v_cache, q_gain3, k_gain2, sinks3, bias_s, gate_b, h_gain4,
            state_mlstm_C, state_mlstm_n, m_state, l)
        x1s = _mm_out(a_s, b_s.reshape(R, B_WIDTH), c_s.reshape(R, C_WIDTH), xs, w_out_b, l, R, 512)
        act_s, zg_s, za_s = _mm_up_sample(x1s, norm2_3, w_up_b, ffn_conv_w, conv_b3, conv_hist, l, 512)
        xs = _mm_down(act_s, x1s, w_down_b, l, R, 512)
        S[0].append(vn_s.reshape(R, 1, A_WIDTH))
        S[1].append(kn_s.reshape(R, 1, B_KV_HEADS, B_HEAD_DIM))
        S[2].append(zs[:, Z_BV:Z_BV + B_KV_WIDTH].reshape(R, 1, B_KV_HEADS, B_HEAD_DIM))
        S[3].append(c_new)
        S[4].append(n_new)
        S[5].append(m_new[:, 0, 0:C_HEADS])
        z_new = jnp.concatenate([zg_s, za_s], axis=-1)
        S[6].append(jnp.stack([state_ffn_conv[l][:, CONV_W - 2], z_new], axis=1))

    st = lambda lst: jnp.stack(lst, axis=0)
    return (xp.reshape(Bp, T, D_MODEL), xs.reshape(R, 1, D_MODEL),
            st(P[0]), st(P[1]), st(S[1]), st(S[2]),
            st(P[2]), st(P[3]), st(P[4]),
            st(S[3]), st(S[4]), st(S[5]),
            st(P[5]), st(S[6]),
            st(S[0]))
```

```python
import functools
import math

import numpy as np
import jax
import jax.numpy as jnp
from jax import lax
from jax.experimental import pallas as pl
from jax.experimental.pallas import tpu as pltpu

f32 = jnp.float32
bf16 = jnp.bfloat16

D_MODEL = 2048
EPS = 1e-6
NEG_INF = -1e30
SQRT_HALF = 0.7071067811865476

A_GROUPS = 4
A_DIM = 128
A_WIDTH = 512
CHUNK = 128
B_HEADS = 16
B_KV_HEADS = 2
B_HEAD_DIM = 64
GQA = 8
B_WIDTH = 1024
B_KV_WIDTH = 128
WINDOW = 128
N_BUCKETS = 32
MAX_DISTANCE = 128
C_HEADS = 4
C_QK_DIM = 64
C_V_DIM = 128
C_QK_WIDTH = 256
C_WIDTH = 512
IN_SPLITS = (A_WIDTH, A_WIDTH, B_WIDTH, B_KV_WIDTH, B_KV_WIDTH,
             C_QK_WIDTH, C_QK_WIDTH, C_WIDTH, C_WIDTH, C_HEADS, C_HEADS)
IN_OFFSETS = tuple(int(o) for o in np.cumsum(IN_SPLITS)[:-1])
D_FF = 5632
CONV_W = 3

Z_WIDTH = 4096
Z_AU, Z_AV, Z_BQ, Z_BK, Z_BV, Z_CQ, Z_CK, Z_CV, Z_CO, Z_GATE = (0,) + IN_OFFSETS[:9]
C_PAIR = 2 * C_V_DIM

MLSTM_L = 128
UP_CHUNK = 256
UP_ROWS = 256
C_SEQS = 1
LANES = 128
SUBLANES = 8
VMEM_LIMIT = 52 * 1024 * 1024


def _cparams(sem, flags=None):
    return pltpu.CompilerParams(dimension_semantics=sem, vmem_limit_bytes=VMEM_LIMIT, flags=flags)


def _gelu(x):
    return 0.5 * x * (1.0 + lax.erf(x * SQRT_HALF))


def _rms(x, gain):
    return x * lax.rsqrt(jnp.mean(x * x, axis=-1, keepdims=True) + EPS) * gain


def _norm_head_pairs(x, gain2):
    lo = lax.broadcasted_iota(jnp.int32, x.shape, 1) < B_HEAD_DIM
    x2 = x * x
    s_lo = jnp.sum(jnp.where(lo, x2, 0.0), axis=-1, keepdims=True)
    s_hi = jnp.sum(jnp.where(lo, 0.0, x2), axis=-1, keepdims=True)
    ms = jnp.where(lo, s_lo, s_hi) * (1.0 / B_HEAD_DIM)
    return x * lax.rsqrt(ms + EPS) * gain2


def _t5_bucket_np(dist):
    n = np.maximum(dist, 0)
    max_exact = N_BUCKETS // 2
    nf = np.maximum(n, 1).astype(np.float32)
    large = max_exact + (np.log(nf / np.float32(max_exact)) / np.float32(math.log(MAX_DISTANCE / max_exact))
                         * np.float32(N_BUCKETS - max_exact)).astype(np.int32)
    return np.where(n < max_exact, n, np.minimum(large, N_BUCKETS - 1)).astype(np.int32)


def _bias_kernel(rb_ref, bkp_ref, bks_ref, op_ref, os_ref):
    bkp = bkp_ref[...]
    bks = bks_ref[...]
    for h in range(B_HEADS):
        accp = jnp.zeros(bkp.shape, f32)
        accs = jnp.zeros(bks.shape, f32)
        for b in range(N_BUCKETS):
            val = rb_ref[b, h]
            accp = jnp.where(bkp == b, val, accp)
            accs = jnp.where(bks == b, val, accs)
        op_ref[h] = accp
        os_ref[h:h + 1, :] = accs[0:1, :]


def _bias_tables(rel_bias):
    qi = np.arange(WINDOW)[:, None]
    kj = np.arange(2 * WINDOW)[None, :]
    bkp = _t5_bucket_np(WINDOW + qi - kj)
    j = np.arange(2 * WINDOW)
    dist_s = np.where(j < WINDOW, WINDOW - j, 0)
    bks = np.broadcast_to(_t5_bucket_np(dist_s)[None, :], (SUBLANES, 2 * WINDOW)).copy()
    return pl.pallas_call(
        _bias_kernel,
        out_shape=(jax.ShapeDtypeStruct((B_HEADS, WINDOW, 2 * WINDOW), f32),
                   jax.ShapeDtypeStruct((B_HEADS, 2 * WINDOW), f32)),
        in_specs=[pl.BlockSpec(memory_space=pltpu.SMEM),
                  pl.BlockSpec(memory_space=pltpu.VMEM),
                  pl.BlockSpec(memory_space=pltpu.VMEM)],
        out_specs=(pl.BlockSpec(memory_space=pltpu.VMEM), pl.BlockSpec(memory_space=pltpu.VMEM)),
        name="bias_tables",
    )(rel_bias, jnp.asarray(bkp), jnp.asarray(bks))


def _mm_in_kernel(x_ref, g_ref, w_ref, z_ref, h_ref):
    @pl.when(pl.program_id(1) == 0)
    def _():
        h_ref[...] = _rms(x_ref[...], g_ref[...]).astype(bf16)

    z_ref[...] = jnp.dot(h_ref[...], w_ref[...], preferred_element_type=f32)


def _mm_in(x, gain, w, layer, tm):
    M = x.shape[0]
    tn = w.shape[-1]
    return pl.pallas_call(
        _mm_in_kernel,
        grid=(M // tm, Z_WIDTH // tn),
        in_specs=[pl.BlockSpec((tm, D_MODEL), lambda i, j: (i, 0)),
                  pl.BlockSpec((None, 1, D_MODEL), lambda i, j: (layer, 0, 0)),
                  pl.BlockSpec((None, None, D_MODEL, tn), lambda i, j: (layer, j, 0, 0))],
        out_specs=pl.BlockSpec((tm, tn), lambda i, j: (i, j)),
        out_shape=jax.ShapeDtypeStruct((M, Z_WIDTH), f32),
        scratch_shapes=[pltpu.VMEM((tm, D_MODEL), bf16)],
        compiler_params=_cparams(("parallel", "arbitrary")),
        name="mm_in",
    )(x, gain, w)


def _mm_out_kernel(a_ref, b_ref, c_ref, x_ref, w_ref, o_ref, lhs_ref):
    @pl.when(pl.program_id(1) == 0)
    def _():
        lhs_ref[:, 0:A_WIDTH] = a_ref[...]
        lhs_ref[:, A_WIDTH:A_WIDTH + B_WIDTH] = b_ref[...]
        lhs_ref[:, A_WIDTH + B_WIDTH:D_MODEL] = c_ref[...]

    o_ref[...] = x_ref[...] + jnp.dot(lhs_ref[...], w_ref[...], preferred_element_type=f32)


def _mm_out(a, b, c, x, w, layer, tm):
    M = x.shape[0]
    tn = w.shape[-1]
    return pl.pallas_call(
        _mm_out_kernel,
        grid=(M // tm, D_MODEL // tn),
        in_specs=[pl.BlockSpec((tm, A_WIDTH), lambda i, j: (i, 0)),
                  pl.BlockSpec((tm, B_WIDTH), lambda i, j: (i, 0)),
                  pl.BlockSpec((tm, C_WIDTH), lambda i, j: (i, 0)),
                  pl.BlockSpec((tm, tn), lambda i, j: (i, j)),
                  pl.BlockSpec((None, None, D_MODEL, tn), lambda i, j: (layer, j, 0, 0))],
        out_specs=pl.BlockSpec((tm, tn), lambda i, j: (i, j)),
        out_shape=jax.ShapeDtypeStruct((M, D_MODEL), f32),
        scratch_shapes=[pltpu.VMEM((tm, D_MODEL), bf16)],
        compiler_params=_cparams(("parallel", "arbitrary")),
        name="mm_out",
    )(a, b, c, x, w)


def _mm_down_kernel(a_ref, x_ref, w_ref, o_ref):
    o_ref[...] = x_ref[...] + jnp.dot(a_ref[...], w_ref[...], preferred_element_type=f32)


def _mm_down(act, x, w, layer, tm):
    M = x.shape[0]
    tn = w.shape[-1]
    return pl.pallas_call(
        _mm_down_kernel,
        grid=(M // tm, D_MODEL // tn),
        in_specs=[pl.BlockSpec((tm, D_FF), lambda i, j: (i, 0)),
                  pl.BlockSpec((tm, tn), lambda i, j: (i, j)),
                  pl.BlockSpec((None, None, D_FF, tn), lambda i, j: (layer, j, 0, 0))],
        out_specs=pl.BlockSpec((tm, tn), lambda i, j: (i, j)),
        out_shape=jax.ShapeDtypeStruct((M, D_MODEL), f32),
        compiler_params=_cparams(("parallel", "arbitrary")),
        name="mm_down",
    )(act, x, w)


def _silu(x):
    return x * jax.nn.sigmoid(x)


def _mm_up_prompt_kernel(x_ref, n2_ref, wg_ref, wa_ref, cwg_ref, cwa_ref, cbg_ref, cba_ref,
                         act_ref, cs_ref, h_ref, carry_ref, zb_ref, *, tm, nj, n_steps, tiles_per_seq):
    s = pl.program_id(0)
    sa = jnp.minimum(s, n_steps - 1)
    ia = sa // nj
    ja = sa % nj
    tf = act_ref.shape[1]

    @pl.when(s == 0)
    def _():
        zb_ref[...] = jnp.zeros(zb_ref.shape, f32)
        carry_ref[...] = jnp.zeros(carry_ref.shape, f32)

    @pl.when(ja == 0)
    def _():
        h_ref[...] = _rms(x_ref[...], n2_ref[...]).astype(bf16)

    chunks = [slice(c0, c0 + UP_CHUNK) for c0 in range(0, tf, UP_CHUNK)]

    rblocks = [(r0, min(UP_ROWS, tm - r0)) for r0 in range(0, tm, UP_ROWS)]

    def conv(idx, cw_ref, cb_ref, cs, r0, nr):
        return (cb_ref[:, cs]
                + zb_ref[idx, r0 + SUBLANES - 2:r0 + SUBLANES - 2 + nr, cs] * cw_ref[0:1, cs]
                + zb_ref[idx, r0 + SUBLANES - 1:r0 + SUBLANES - 1 + nr, cs] * cw_ref[1:2, cs]
                + zb_ref[idx, r0 + SUBLANES:r0 + SUBLANES + nr, cs] * cw_ref[2:3, cs])

    for cs in chunks:
        for r0, nr in rblocks:
            g = conv(0, cwg_ref, cbg_ref, cs, r0, nr)
            a = conv(1, cwa_ref, cba_ref, cs, r0, nr)
            act_ref[r0:r0 + nr, cs] = (_silu(g) * a).astype(bf16)

    seq_start = (ia % tiles_per_seq) == 0
    for cs in chunks:
        for r0, nr in rblocks:
            for idx, w_ref in ((0, wg_ref), (1, wa_ref)):
                z = jnp.dot(h_ref[r0:r0 + nr, :], w_ref[:, cs], preferred_element_type=f32)
                if r0 == 0:
                    zb_ref[idx, 0:SUBLANES, cs] = jnp.where(seq_start, 0.0, carry_ref[idx, ja, :, cs])
                zb_ref[idx, SUBLANES + r0:SUBLANES + r0 + nr, cs] = z
                if r0 + nr == tm:
                    tail = z[nr - SUBLANES:nr, :]
                    carry_ref[idx, ja, :, cs] = tail
                    cs_ref[0, idx, :, cs] = tail


def _mm_up_prompt(x, norm2, w_up, conv_w, conv_b, layer, batch, tm):
    M = x.shape[0]
    seq = M // batch
    tiles_per_seq = seq // tm
    tf = w_up.shape[-1]
    nj = D_FF // tf
    n_steps = (M // tm) * nj
    kern = functools.partial(_mm_up_prompt_kernel, tm=tm, nj=nj, n_steps=n_steps, tiles_per_seq=tiles_per_seq)
    ia = lambda s: jnp.minimum(s, n_steps - 1) // nj
    ja = lambda s: jnp.minimum(s, n_steps - 1) % nj
    ib = lambda s: jnp.maximum(s - 1, 0) // nj
    jb = lambda s: jnp.maximum(s - 1, 0) % nj
    return pl.pallas_call(
        kern,
        grid=(n_steps + 1,),
        in_specs=[pl.BlockSpec((tm, D_MODEL), lambda s: (ia(s), 0)),
                  pl.BlockSpec((None, 1, D_MODEL), lambda s: (layer, 0, 0)),
                  pl.BlockSpec((None, None, D_MODEL, tf), lambda s: (layer, ja(s), 0, 0)),
                  pl.BlockSpec((None, None, D_MODEL, tf), lambda s: (layer, nj + ja(s), 0, 0)),
                  pl.BlockSpec((None, CONV_W, tf), lambda s: (layer, 0, jb(s))),
                  pl.BlockSpec((None, CONV_W, tf), lambda s: (layer, 0, nj + jb(s))),
                  pl.BlockSpec((None, 1, tf), lambda s: (layer, 0, jb(s))),
                  pl.BlockSpec((None, 1, tf), lambda s: (layer, 0, nj + jb(s)))],
        out_specs=(pl.BlockSpec((tm, tf), lambda s: (ib(s), jb(s))),
                   pl.BlockSpec((1, 2, SUBLANES, tf), lambda s: (ia(s), 0, 0, ja(s)))),
        out_shape=(jax.ShapeDtypeStruct((M, D_FF), bf16),
                   jax.ShapeDtypeStruct((M // tm, 2, SUBLANES, D_FF), f32)),
        scratch_shapes=[pltpu.VMEM((tm, D_MODEL), bf16),
                        pltpu.VMEM((2, nj, SUBLANES, tf), f32),
                        pltpu.VMEM((2, tm + SUBLANES, tf), f32)],
        compiler_params=_cparams(("arbitrary",)),
        name="mm_up_prompt",
    )(x, norm2, w_up, w_up, conv_w, conv_w, conv_b, conv_b)


def _mm_up_sample_kernel(x_ref, n2_ref, wg_ref, wa_ref, cwg_ref, cwa_ref, cbg_ref, cba_ref,
                         b0g_ref, b1g_ref, b0a_ref, b1a_ref, act_ref, zg_ref, za_ref, h_ref):
    @pl.when(pl.program_id(0) == 0)
    def _():
        h_ref[...] = _rms(x_ref[...], n2_ref[...]).astype(bf16)

    def conv_half(w_ref, cw_ref, cb_ref, b0_ref, b1_ref, z_ref):
        z = jnp.dot(h_ref[...], w_ref[...], preferred_element_type=f32)
        z_ref[...] = z
        return (cb_ref[...] + b0_ref[...] * cw_ref[0:1, :] + b1_ref[...] * cw_ref[1:2, :]
                + z * cw_ref[2:3, :])

    g = conv_half(wg_ref, cwg_ref, cbg_ref, b0g_ref, b1g_ref, zg_ref)
    a = conv_half(wa_ref, cwa_ref, cba_ref, b0a_ref, b1a_ref, za_ref)
    act_ref[...] = (_silu(g) * a).astype(bf16)


def _mm_up_sample(x, norm2, w_up, conv_w, conv_b, buf, layer):
    M = x.shape[0]
    tf = w_up.shape[-1]
    nj = D_FF // tf
    wspec = lambda off: pl.BlockSpec((None, None, D_MODEL, tf), lambda j: (layer, off + j, 0, 0))
    cwspec = lambda off: pl.BlockSpec((None, CONV_W, tf), lambda j: (layer, 0, off + j))
    cbspec = lambda off: pl.BlockSpec((None, 1, tf), lambda j: (layer, 0, off + j))
    bufspec = lambda row, off: pl.BlockSpec((None, None, M, tf), lambda j: (layer, row, 0, off + j))
    return pl.pallas_call(
        _mm_up_sample_kernel,
        grid=(nj,),
        in_specs=[pl.BlockSpec((M, D_MODEL), lambda j: (0, 0)),
                  pl.BlockSpec((None, 1, D_MODEL), lambda j: (layer, 0, 0)),
                  wspec(0), wspec(nj), cwspec(0), cwspec(nj), cbspec(0), cbspec(nj),
                  bufspec(0, 0), bufspec(1, 0), bufspec(0, nj), bufspec(1, nj)],
        out_specs=(pl.BlockSpec((M, tf), lambda j: (0, j)),
                   pl.BlockSpec((M, tf), lambda j: (0, j)),
                   pl.BlockSpec((M, tf), lambda j: (0, j))),
        out_shape=(jax.ShapeDtypeStruct((M, D_FF), bf16),
                   jax.ShapeDtypeStruct((M, D_FF), f32),
                   jax.ShapeDtypeStruct((M, D_FF), f32)),
        scratch_shapes=[pltpu.VMEM((M, D_MODEL), bf16)],
        compiler_params=_cparams(("arbitrary",)),
        name="mm_up_sample",
    )(x, norm2, w_up, w_up, conv_w, conv_w, conv_b, conv_b, buf, buf, buf, buf)


def _mix_a_prompt_kernel(u_ref, v_ref, vg_ref, ws_ref, bs_ref, o_ref):
    row = lax.broadcasted_iota(jnp.int32, (CHUNK, CHUNK), 0)
    col = lax.broadcasted_iota(jnp.int32, (CHUNK, CHUNK), 1)
    tri = col <= row
    for g in range(A_GROUPS):
        sl = slice(g * A_DIM, (g + 1) * A_DIM)
        ws = jnp.where(tri, ws_ref[g], 0.0).astype(bf16)
        for r0 in range(0, u_ref.shape[0], CHUNK):
            rs = slice(r0, r0 + CHUNK)
            vn = _rms(_gelu(v_ref[rs, sl]), vg_ref[:, sl])
            mixv = jnp.dot(ws, vn.astype(bf16), preferred_element_type=f32) + bs_ref[:, sl]
            o_ref[rs, sl] = (_gelu(u_ref[rs, sl]) * mixv).astype(bf16)


def _mix_a_prompt(z, v_gain, w_s, b_s_full, layer, rows):
    M = z.shape[0]
    return pl.pallas_call(
        _mix_a_prompt_kernel,
        grid=(M // rows,),
        in_specs=[pl.BlockSpec((rows, A_WIDTH), lambda r: (r, Z_AU // A_WIDTH)),
                  pl.BlockSpec((rows, A_WIDTH), lambda r: (r, Z_AV // A_WIDTH)),
                  pl.BlockSpec((None, 1, A_WIDTH), lambda r: (layer, 0, 0)),
                  pl.BlockSpec((None, A_GROUPS, CHUNK, CHUNK), lambda r: (layer, 0, 0, 0)),
                  pl.BlockSpec((None, CHUNK, A_WIDTH), lambda r: (layer, 0, 0))],
        out_specs=pl.BlockSpec((rows, A_WIDTH), lambda r: (r, 0)),
        out_shape=jax.ShapeDtypeStruct((M, A_WIDTH), bf16),
        compiler_params=_cparams(("parallel",)),
        name="mix_a_prompt",
    )(z, z, v_gain, w_s, b_s_full)


def _mix_a_sample_kernel(u_ref, v_ref, vg_ref, ws_ref, bs_ref, o_ref, vn_ref):
    for g in range(A_GROUPS):
        sl = slice(g * A_DIM, (g + 1) * A_DIM)
        vn = _rms(_gelu(v_ref[:, sl]), vg_ref[:, sl])
        vn_ref[:, sl] = vn
        mixv = ws_ref[:, sl] * vn + bs_ref[:, sl]
        o_ref[:, sl] = (_gelu(u_ref[:, sl]) * mixv).astype(bf16)


def _mix_a_sample(z, v_gain, ws0, bs0, layer):
    M = z.shape[0]
    vec = pl.BlockSpec((None, 1, A_WIDTH), lambda r: (layer, 0, 0))
    return pl.pallas_call(
        _mix_a_sample_kernel,
        grid=(1,),
        in_specs=[pl.BlockSpec((M, A_WIDTH), lambda r: (0, Z_AU // A_WIDTH)),
                  pl.BlockSpec((M, A_WIDTH), lambda r: (0, Z_AV // A_WIDTH)),
                  vec, vec, vec],
        out_specs=(pl.BlockSpec((M, A_WIDTH), lambda r: (0, 0)),
                   pl.BlockSpec((M, A_WIDTH), lambda r: (0, 0))),
        out_shape=(jax.ShapeDtypeStruct((M, A_WIDTH), bf16),
                   jax.ShapeDtypeStruct((M, A_WIDTH), f32)),
        compiler_params=_cparams(("arbitrary",)),
        name="mix_a_sample",
    )(z, z, v_gain, ws0, bs0)


def _mix_b_prompt_kernel(sink_ref, q_ref, k_ref, v_ref, qg_ref, kg_ref, bias_ref,
                         o_ref, klast_ref, vlast_ref, kcat_ref, vcat_ref):
    i = pl.program_id(1)
    W = WINDOW
    kn = _norm_head_pairs(k_ref[...], kg_ref[...])
    v = v_ref[...]
    klast_ref[0] = kn
    vlast_ref[0] = v

    @pl.when(i == 0)
    def _():
        kcat_ref[0:W, :] = jnp.zeros((W, LANES), f32)
        vcat_ref[0:W, :] = jnp.zeros((W, LANES), f32)

    @pl.when(i > 0)
    def _():
        kcat_ref[0:W, :] = kcat_ref[W:2 * W, :]
        vcat_ref[0:W, :] = vcat_ref[W:2 * W, :]

    kcat_ref[W:2 * W, :] = kn
    vcat_ref[W:2 * W, :] = v
    kc = kcat_ref[...]
    vc = vcat_ref[...]
    kc_sw = pltpu.roll(kc, B_HEAD_DIM, 1)
    vc_sw = pltpu.roll(vc, B_HEAD_DIM, 1)
    lo = lax.broadcasted_iota(jnp.int32, (2 * W, LANES), 1) < B_HEAD_DIM

    qi = lax.broadcasted_iota(jnp.int32, (W, 2 * W), 0)
    kj = lax.broadcasted_iota(jnp.int32, (W, 2 * W), 1)
    dist = W + qi - kj
    valid = (dist >= 0) & (dist < W) & ((kj >= W) | (i > 0))

    tiles = GQA // 2
    for kvh in range(B_KV_HEADS):
        k_src, k_alt = (kc, kc_sw) if kvh == 0 else (kc_sw, kc)
        v_src, v_alt = (vc, vc_sw) if kvh == 0 else (vc_sw, vc)
        k_even = jnp.where(lo, k_src, 0.0).astype(bf16)
        k_odd = jnp.where(lo, 0.0, k_alt).astype(bf16)
        v_even = jnp.where(lo, v_src, 0.0).astype(bf16)
        v_odd = jnp.where(lo, 0.0, v_alt).astype(bf16)
        qs = jnp.concatenate(
            [_norm_head_pairs(q_ref[:, (kvh * tiles + t) * LANES:(kvh * tiles + t + 1) * LANES], qg_ref[...])
             for t in range(tiles)], axis=0).astype(bf16)
        dn = (((1,), (1,)), ((), ()))
        s_par = [lax.dot_general(qs, k_even, dn, preferred_element_type=f32),
                 lax.dot_general(qs, k_odd, dn, preferred_element_type=f32)]
        p_par = []
        for par in range(2):
            blocks = []
            for t in range(tiles):
                h = kvh * GQA + 2 * t + par
                s = s_par[par][t * W:(t + 1) * W, :] * (B_HEAD_DIM ** -0.5) + bias_ref[h]
                s = jnp.where(valid, s, NEG_INF)
                sk = sink_ref[h]
                mx = jnp.maximum(jnp.max(s, axis=-1, keepdims=True), sk)
                p = jnp.exp(s - mx)
                den = jnp.sum(p, axis=-1, keepdims=True) + jnp.exp(sk - mx)
                blocks.append((p * (1.0 / den)).astype(bf16))
            p_par.append(jnp.concatenate(blocks, axis=0))
        o = (jnp.dot(p_par[0], v_even, preferred_element_type=f32)
             + jnp.dot(p_par[1], v_odd, preferred_element_type=f32))
        for t in range(tiles):
            c0 = (kvh * tiles + t) * LANES
            o_ref[:, c0:c0 + LANES] = o[t * W:(t + 1) * W, :].astype(bf16)


def _mix_b_prompt(z, sinks, q_gain2, k_gain2, bias_p, layer, batch):
    M = z.shape[0]
    nb = M // batch // WINDOW
    return pl.pallas_call(
        _mix_b_prompt_kernel,
        grid=(batch, nb),
        in_specs=[pl.BlockSpec(memory_space=pltpu.SMEM),
                  pl.BlockSpec((WINDOW, B_WIDTH), lambda b, i: (b * nb + i, Z_BQ // B_WIDTH)),
                  pl.BlockSpec((WINDOW, B_KV_WIDTH), lambda b, i: (b * nb + i, Z_BK // B_KV_WIDTH)),
                  pl.BlockSpec((WINDOW, B_KV_WIDTH), lambda b, i: (b * nb + i, Z_BV // B_KV_WIDTH)),
                  pl.BlockSpec((None, 1, LANES), lambda b, i: (layer, 0, 0)),
                  pl.BlockSpec((None, 1, LANES), lambda b, i: (layer, 0, 0)),
                  pl.BlockSpec((B_HEADS, WINDOW, 2 * WINDOW), lambda b, i: (0, 0, 0))],
        out_specs=(pl.BlockSpec((WINDOW, B_WIDTH), lambda b, i: (b * nb + i, 0)),
                   pl.BlockSpec((1, WINDOW, B_KV_WIDTH), lambda b, i: (b, 0, 0)),
                   pl.BlockSpec((1, WINDOW, B_KV_WIDTH), lambda b, i: (b, 0, 0))),
        out_shape=(jax.ShapeDtypeStruct((M, B_WIDTH), bf16),
                   jax.ShapeDtypeStruct((batch, WINDOW, B_KV_WIDTH), f32),
                   jax.ShapeDtypeStruct((batch, WINDOW, B_KV_WIDTH), f32)),
        scratch_shapes=[pltpu.VMEM((2 * WINDOW, LANES), f32),
                        pltpu.VMEM((2 * WINDOW, LANES), f32)],
        compiler_params=_cparams(("arbitrary", "arbitrary")),
        name="mix_b_prompt",
    )(sinks, z, z, z, q_gain2, k_gain2, bias_p)


def _mix_c_prompt_kernel(q_ref, k_ref, v0_ref, v1_ref, o0_ref, o1_ref, g_ref, gb_ref, hg_ref,
                         out_ref, c_out, n_out, m_out, c_s, n_s, m_s):
    L = MLSTM_L
    c = pl.program_id(1)

    @pl.when(c == 0)
    def _():
        c_s[...] = jnp.zeros(c_s.shape, f32)
        n_s[...] = jnp.zeros(n_s.shape, f32)
        m_s[...] = jnp.zeros(m_s.shape, f32)

    row = lax.broadcasted_iota(jnp.int32, (L, L), 0)
    col = lax.broadcasted_iota(jnp.int32, (L, L), 1)
    tri = col <= row
    tri_f = tri.astype(f32)
    for bi in range(q_ref.shape[0]):
        g = g_ref[bi] + gb_ref[...]
        lf = jax.nn.log_sigmoid(g)
        b = jnp.dot(tri_f, lf, precision=lax.Precision.HIGHEST, preferred_element_type=f32)
        bT = b.T
        gT = g.T
        for h in range(C_HEADS):
            b_col = b[:, C_HEADS + h:C_HEADS + h + 1]
            i_col = g[:, h:h + 1]
            b_row = bT[C_HEADS + h:C_HEADS + h + 1, :]
            i_row = gT[h:h + 1, :]
            dlog = jnp.where(tri, b_col - b_row + i_row, -jnp.inf)
            m_prev = m_s[bi, h][0:1, 0:1]
            inter = b_col + m_prev
            mt = jnp.maximum(inter, jnp.max(dlog, axis=-1, keepdims=True))
            v_ref, o_ref = (v0_ref, o0_ref) if h < 2 else (v1_ref, o1_ref)
            vsl = slice((h % 2) * C_V_DIM, (h % 2 + 1) * C_V_DIM)
            qh = q_ref[bi, :, h * C_QK_DIM:(h + 1) * C_QK_DIM]
            kh = k_ref[bi, :, h * C_QK_DIM:(h + 1) * C_QK_DIM] * (C_QK_DIM ** -0.5)
            vh = v_ref[bi, :, vsl]
            qb, kb, vb = qh.astype(bf16), kh.astype(bf16), vh.astype(bf16)
            s = lax.dot_general(qb, kb, (((1,), (1,)), ((), ())), preferred_element_type=f32) * jnp.exp(dlog - mt)
            iw = jnp.exp(inter - mt)
            c_prev = c_s[bi, h]
            n_prev = n_s[bi, h]
            num = (iw * jnp.dot(qb, c_prev.astype(bf16), preferred_element_type=f32)
                   + jnp.dot(s.astype(bf16), vb, preferred_element_type=f32))
            qn = iw * jnp.sum(qh * n_prev, axis=-1, keepdims=True) + jnp.sum(s, axis=-1, keepdims=True)
            hh = num / jnp.maximum(jnp.abs(qn), jnp.exp(-mt))
            m_new = mt[L - 1:L, :]
            b_last = b_col[L - 1:L, :]
            wc = jnp.exp(b_last - b_col + i_col - m_new)
            decay = jnp.exp(b_last + m_prev - m_new)
            kw = kh * wc
            c_s[bi, h] = decay * c_prev + lax.dot_general(kw.astype(bf16), vb, (((0,), (0,)), ((), ())),
                                                          preferred_element_type=f32)
            n_s[bi, h] = decay * n_prev + jnp.sum(kw, axis=0, keepdims=True)
            m_s[bi, h] = jnp.broadcast_to(m_new, (SUBLANES, LANES))
            osl = slice(h * C_V_DIM, (h + 1) * C_V_DIM)
            out_ref[bi, :, osl] = (_rms(hh, hg_ref[:, osl]) * jax.nn.sigmoid(o_ref[bi, :, vsl])).astype(bf16)

    @pl.when(c == pl.num_programs(1) - 1)
    def _():
        c_out[...] = c_s[...]
        n_out[...] = n_s[...]
        m_out[...] = m_s[...]


def _mix_c_prompt(z, gate_bias, h_gain, layer, batch):
    M = z.shape[0]
    L = MLSTM_L
    T = M // batch
    z3 = z.reshape(batch, T, Z_WIDTH)
    ns = C_SEQS if batch % C_SEQS == 0 else 1
    col = lambda width, off: pl.BlockSpec((ns, L, width), lambda b, c: (b, c, off // width))
    state = lambda *shape: pl.BlockSpec((ns,) + shape, lambda b, c: (b,) + (0,) * len(shape))
    out, c_st, n_st, m_st = pl.pallas_call(
        _mix_c_prompt_kernel,
        grid=(batch // ns, T // L),
        in_specs=[col(C_QK_WIDTH, Z_CQ), col(C_QK_WIDTH, Z_CK),
                  col(C_PAIR, Z_CV), col(C_PAIR, Z_CV + C_PAIR),
                  col(C_PAIR, Z_CO), col(C_PAIR, Z_CO + C_PAIR),
                  col(LANES, Z_GATE),
                  pl.BlockSpec((None, 1, LANES), lambda b, c: (layer, 0, 0)),
                  pl.BlockSpec((None, 1, C_WIDTH), lambda b, c: (layer, 0, 0))],
        out_specs=(pl.BlockSpec((ns, L, C_WIDTH), lambda b, c: (b, c, 0)),
                   state(C_HEADS, C_QK_DIM, C_V_DIM),
                   state(C_HEADS, 1, C_QK_DIM),
                   state(C_HEADS, SUBLANES, LANES)),
        out_shape=(jax.ShapeDtypeStruct((batch, T, C_WIDTH), bf16),
                   jax.ShapeDtypeStruct((batch, C_HEADS, C_QK_DIM, C_V_DIM), f32),
                   jax.ShapeDtypeStruct((batch, C_HEADS, 1, C_QK_DIM), f32),
                   jax.ShapeDtypeStruct((batch, C_HEADS, SUBLANES, LANES), f32)),
        scratch_shapes=[pltpu.VMEM((ns, C_HEADS, C_QK_DIM, C_V_DIM), f32),
                        pltpu.VMEM((ns, C_HEADS, 1, C_QK_DIM), f32),
                        pltpu.VMEM((ns, C_HEADS, SUBLANES, LANES), f32)],
        compiler_params=_cparams(("arbitrary", "arbitrary")),
        name="mix_c_prompt",
    )(z3, z3, z3, z3, z3, z3, z3, gate_bias, h_gain)
    return out.reshape(M, C_WIDTH), c_st, n_st, m_st


def _mix_bc_sample_kernel(zr_ref, q_ref, kp_ref, vp_ref, qg_ref, kg_ref, sink_ref, bias_ref,
                          cq_ref, ck_ref, cv_ref, co_ref, gb_ref, hg_ref, c0_ref, n0_ref, m0_ref,
                          bo_ref, kn_ref, co_out, c_out, n_out, m_out):
    W = WINDOW
    zr = zr_ref[0]

    qn = _rms(q_ref[0], qg_ref[...])
    hrow = lax.broadcasted_iota(jnp.int32, (B_HEADS, LANES), 0)
    lane = lax.broadcasted_iota(jnp.int32, (B_HEADS, LANES), 1)
    own = (hrow < GQA) == (lane < B_HEAD_DIM)
    q2 = jnp.where(own, jnp.concatenate([qn, qn], axis=-1), 0.0)
    k_new = _norm_head_pairs(zr[:, Z_BK:Z_BK + B_KV_WIDTH], kg_ref[...])
    v_new = zr[:, Z_BV:Z_BV + B_KV_WIDTH]
    kn_ref[0] = k_new
    scale = B_HEAD_DIM ** -0.5
    s_past = lax.dot_general(q2.astype(bf16), kp_ref[0].astype(bf16), (((1,), (1,)), ((), ())),
                             preferred_element_type=f32) * scale + bias_ref[:, 0:W]
    s_past = jnp.where(lane >= 1, s_past, NEG_INF)
    s_new = jnp.sum(q2 * k_new, axis=-1, keepdims=True) * scale + bias_ref[:, W:W + 1]
    sk = sink_ref[...]
    mx = jnp.maximum(jnp.maximum(jnp.max(s_past, axis=-1, keepdims=True), s_new), sk)
    p_past = jnp.exp(s_past - mx)
    p_new = jnp.exp(s_new - mx)
    den = jnp.sum(p_past, axis=-1, keepdims=True) + p_new + jnp.exp(sk - mx)
    inv = 1.0 / den
    o2 = (jnp.dot((p_past * inv).astype(bf16), vp_ref[0].astype(bf16), preferred_element_type=f32)
          + (p_new * inv) * v_new)
    o2_sw = pltpu.roll(o2, B_HEAD_DIM, 1)
    bo_ref[0] = jnp.where(hrow < GQA, o2, o2_sw)[:, 0:B_HEAD_DIM].astype(bf16)

    g = zr[:, Z_GATE:Z_GATE + LANES] + gb_ref[...]
    lf = jax.nn.log_sigmoid(g)
    m0 = m0_ref[0]
    lane1 = lax.broadcasted_iota(jnp.int32, (1, LANES), 1)
    m_row = jnp.zeros((1, LANES), f32)
    row8 = lax.broadcasted_iota(jnp.int32, (SUBLANES, C_QK_DIM), 0)
    c_rows = []
    for h in range(C_HEADS):
        ig = g[:, h:h + 1]
        b = lf[:, C_HEADS + h:C_HEADS + h + 1]
        m_prev = m0[:, h:h + 1]
        inter = b + m_prev
        mt = jnp.maximum(inter, ig)
        qh = cq_ref[0, h:h + 1, :]
        kh = ck_ref[0, h:h + 1, :] * (C_QK_DIM ** -0.5)
        vh = cv_ref[0, h:h + 1, :]
        c_prev = c0_ref[0, h]
        n_prev = n0_ref[0, h:h + 1, :]
        s = jnp.sum(qh * kh, axis=-1, keepdims=True) * jnp.exp(ig - mt)
        iw = jnp.exp(inter - mt)
        q8 = jnp.broadcast_to(qh, (SUBLANES, C_QK_DIM)).astype(bf16)
        qc = jnp.dot(q8, c_prev.astype(bf16), preferred_element_type=f32)[0:1, :]
        num = iw * qc + s * vh
        qn_ = iw * jnp.sum(qh * n_prev, axis=-1, keepdims=True) + s
        hh = num / jnp.maximum(jnp.abs(qn_), jnp.exp(-mt))
        wc = jnp.exp(ig - mt)
        decay = jnp.exp(inter - mt)
        kw = kh * wc
        kw8 = jnp.where(row8 == 0, jnp.broadcast_to(kw, (SUBLANES, C_QK_DIM)), 0.0).astype(bf16)
        v8 = jnp.broadcast_to(vh, (SUBLANES, C_V_DIM)).astype(bf16)
        c_out[0, h] = decay * c_prev + lax.dot_general(kw8, v8, (((0,), (0,)), ((), ())),
                                                      preferred_element_type=f32)
        n_out[0, h:h + 1, :] = decay * n_prev + kw
        m_row = jnp.where(lane1 == h, mt, m_row)
        c_rows.append(_rms(hh, hg_ref[h:h + 1, :]) * jax.nn.sigmoid(co_ref[0, h:h + 1, :]))
    co_out[0] = jnp.concatenate(c_rows, axis=0).astype(bf16)
    m_out[0] = m_row


def _mix_bc_sample(z, k_past, v_past, q_gain, k_gain2, sinks, bias_s, gate_bias, h_gain4,
                   c0, n0, m0, layer):
    R = z.shape[0]
    zr = z.reshape(R, 1, Z_WIDTH)
    q = z[:, Z_BQ:Z_BQ + B_WIDTH].reshape(R, B_HEADS, B_HEAD_DIM)
    cq = z[:, Z_CQ:Z_CQ + C_QK_WIDTH].reshape(R, C_HEADS, C_QK_DIM)
    ck = z[:, Z_CK:Z_CK + C_QK_WIDTH].reshape(R, C_HEADS, C_QK_DIM)
    cv = z[:, Z_CV:Z_CV + C_WIDTH].reshape(R, C_HEADS, C_V_DIM)
    co = z[:, Z_CO:Z_CO + C_WIDTH].reshape(R, C_HEADS, C_V_DIM)
    row3 = lambda n: pl.BlockSpec((1, 1, n), lambda r: (r, 0, 0))
    return pl.pallas_call(
        _mix_bc_sample_kernel,
        grid=(R,),
        in_specs=[row3(Z_WIDTH),
                  pl.BlockSpec((1, B_HEADS, B_HEAD_DIM), lambda r: (r, 0, 0)),
                  pl.BlockSpec((None, 1, WINDOW, B_KV_WIDTH), lambda r: (layer, r, 0, 0)),
                  pl.BlockSpec((None, 1, WINDOW, B_KV_WIDTH), lambda r: (layer, r, 0, 0)),
                  pl.BlockSpec((None, 1, B_HEAD_DIM), lambda r: (layer, 0, 0)),
                  pl.BlockSpec((None, 1, LANES), lambda r: (layer, 0, 0)),
                  pl.BlockSpec((None, B_HEADS, 1), lambda r: (layer, 0, 0)),
                  pl.BlockSpec((B_HEADS, 2 * WINDOW), lambda r: (0, 0)),
                  pl.BlockSpec((1, C_HEADS, C_QK_DIM), lambda r: (r, 0, 0)),
                  pl.BlockSpec((1, C_HEADS, C_QK_DIM), lambda r: (r, 0, 0)),
                  pl.BlockSpec((1, C_HEADS, C_V_DIM), lambda r: (r, 0, 0)),
                  pl.BlockSpec((1, C_HEADS, C_V_DIM), lambda r: (r, 0, 0)),
                  pl.BlockSpec((None, 1, LANES), lambda r: (layer, 0, 0)),
                  pl.BlockSpec((None, C_HEADS, C_V_DIM), lambda r: (layer, 0, 0)),
                  pl.BlockSpec((None, 1, C_HEADS, C_QK_DIM, C_V_DIM), lambda r: (layer, r, 0, 0, 0)),
                  pl.BlockSpec((None, 1, C_HEADS, C_QK_DIM), lambda r: (layer, r, 0, 0)),
                  pl.BlockSpec((None, 1, 1, C_HEADS), lambda r: (layer, r, 0, 0))],
        out_specs=(pl.BlockSpec((1, B_HEADS, B_HEAD_DIM), lambda r: (r, 0, 0)),
                   row3(B_KV_WIDTH),
                   pl.BlockSpec((1, C_HEADS, C_V_DIM), lambda r: (r, 0, 0)),
                   pl.BlockSpec((1, C_HEADS, C_QK_DIM, C_V_DIM), lambda r: (r, 0, 0, 0)),
                   pl.BlockSpec((1, C_HEADS, C_QK_DIM), lambda r: (r, 0, 0)),
                   row3(LANES)),
        out_shape=(jax.ShapeDtypeStruct((R, B_HEADS, B_HEAD_DIM), bf16),
                   jax.ShapeDtypeStruct((R, 1, B_KV_WIDTH), f32),
                   jax.ShapeDtypeStruct((R, C_HEADS, C_V_DIM), bf16),
                   jax.ShapeDtypeStruct((R, C_HEADS, C_QK_DIM, C_V_DIM), f32),
                   jax.ShapeDtypeStruct((R, C_HEADS, C_QK_DIM), f32),
                   jax.ShapeDtypeStruct((R, 1, LANES), f32)),
        compiler_params=_cparams(("parallel",)),
        name="mix_bc_sample",
    )(zr, q, k_past, v_past, q_gain, k_gain2, sinks, bias_s, cq, ck, cv, co, gate_bias, h_gain4,
      c0, n0, m0)


def _prep_w_in(w_in):
    return jnp.pad(w_in.astype(bf16), ((0, 0), (0, 0), (0, Z_WIDTH - w_in.shape[-1])))


def _col_tiles(w, tn):
    depth, k, n = w.shape
    return jnp.swapaxes(w.reshape(depth, k, n // tn, tn), 1, 2)


def _row_tile(m, pref):
    return pref if m % pref == 0 else m


def kernel(x_prompt, x_sample, cache_swa_k, cache_swa_v, state_mlstm_C, state_mlstm_n, state_mlstm_m, state_ffn_conv, rel_bias, norm1, w_in, a_v_gain, a_spatial_w, a_spatial_b, b_q_gain, b_k_gain, b_sinks, c_gate_bias, c_h_gain, w_out, norm2, w_up, ffn_conv_w, ffn_conv_b, w_down):
    depth = w_in.shape[0]
    Bp, T, _ = x_prompt.shape
    R = x_sample.shape[0]
    assert x_sample.shape[1] == 1 and T % CHUNK == 0 and T % MLSTM_L == 0

    w_in_p = _col_tiles(_prep_w_in(w_in), 1024)
    w_out_b = _col_tiles(w_out.astype(bf16), 1024)
    w_up_b = _col_tiles(w_up.astype(bf16), 512)
    w_down_b = _col_tiles(w_down.astype(bf16), 512)
    norm1_3 = norm1.reshape(depth, 1, D_MODEL)
    norm2_3 = norm2.reshape(depth, 1, D_MODEL)
    v_gain3 = a_v_gain.reshape(depth, 1, A_WIDTH)
    bs_full = jnp.repeat(jnp.swapaxes(a_spatial_b, 1, 2), A_DIM, axis=-1)
    ws0 = jnp.repeat(a_spatial_w[:, :, 0, 0], A_DIM, axis=-1).reshape(depth, 1, A_WIDTH)
    bs0 = jnp.repeat(a_spatial_b[:, :, 0], A_DIM, axis=-1).reshape(depth, 1, A_WIDTH)
    q_gain3 = b_q_gain.reshape(depth, 1, B_HEAD_DIM)
    q_gain2 = jnp.tile(b_q_gain, (1, 2)).reshape(depth, 1, LANES)
    k_gain2 = jnp.tile(b_k_gain, (1, 2)).reshape(depth, 1, LANES)
    sinks3 = b_sinks.reshape(depth, B_HEADS, 1)
    gate_b = jnp.pad(c_gate_bias, ((0, 0), (0, LANES - 2 * C_HEADS))).reshape(depth, 1, LANES)
    h_gain3 = c_h_gain.reshape(depth, 1, C_WIDTH)
    h_gain4 = c_h_gain.reshape(depth, C_HEADS, C_V_DIM)
    conv_b3 = ffn_conv_b.reshape(depth, 1, 2 * D_FF)
    k_cache = cache_swa_k.reshape(depth, R, WINDOW, B_KV_WIDTH)
    v_cache = cache_swa_v.reshape(depth, R, WINDOW, B_KV_WIDTH)
    m_state = state_mlstm_m.reshape(depth, R, 1, C_HEADS)
    conv_hist = jnp.swapaxes(state_ffn_conv, 1, 2)

    bias_p, bias_s = _bias_tables(rel_bias)

    Mp = Bp * T
    tm = _row_tile(T, 1024)
    a_rows = _row_tile(T, 512)
    xp = x_prompt.reshape(Mp, D_MODEL)
    xs = x_sample.reshape(R, D_MODEL)
    P = [[] for _ in range(6)]
    S = [[] for _ in range(7)]
    for l in range(depth):
        z = _mm_in(xp, norm1_3, w_in_p, l, tm)
        a_o = _mix_a_prompt(z, v_gain3, a_spatial_w, bs_full, l, a_rows)
        b_o, k_last, v_last = _mix_b_prompt(z, b_sinks[l], q_gain2, k_gain2, bias_p, l, Bp)
        c_o, c_st, n_st, m_st = _mix_c_prompt(z, gate_b, h_gain3, l, Bp)
        x1 = _mm_out(a_o, b_o, c_o, xp, w_out_b, l, tm)
        act, cs = _mm_up_prompt(x1, norm2_3, w_up_b, ffn_conv_w, conv_b3, l, Bp, tm)
        xp = _mm_down(act, x1, w_down_b, l, tm)
        P[0].append(k_last.reshape(Bp, WINDOW, B_KV_HEADS, B_HEAD_DIM))
        P[1].append(v_last.reshape(Bp, WINDOW, B_KV_HEADS, B_HEAD_DIM))
        P[2].append(c_st)
        P[3].append(n_st.reshape(Bp, C_HEADS, C_QK_DIM))
        P[4].append(m_st[:, :, 0, 0])
        seq_tiles = T // tm
        tail = cs[seq_tiles - 1::seq_tiles, :, SUBLANES - (CONV_W - 1):, :]
        P[5].append(jnp.swapaxes(tail, 1, 2).reshape(Bp, CONV_W - 1, 2 * D_FF))

        zs = _mm_in(xs, norm1_3, w_in_p, l, R)
        a_s, vn_s = _mix_a_sample(zs, v_gain3, ws0, bs0, l)
        b_s, kn_s, c_s, c_new, n_new, m_new = _mix_bc_sample(
            zs, k_cache, v_cache, q_gain3, k_gain2, sinks3, bias_s, gate_b, h_gain4,
            state_mlstm_C, state_mlstm_n, m_state, l)
        x1s = _mm_out(a_s, b_s.reshape(R, B_WIDTH), c_s.reshape(R, C_WIDTH), xs, w_out_b, l, R)
        act_s, zg_s, za_s = _mm_up_sample(x1s, norm2_3, w_up_b, ffn_conv_w, conv_b3, conv_hist, l)
        xs = _mm_down(act_s, x1s, w_down_b, l, R)
        S[0].append(vn_s.reshape(R, 1, A_WIDTH))
        S[1].append(kn_s.reshape(R, 1, B_KV_HEADS, B_HEAD_DIM))
        S[2].append(zs[:, Z_BV:Z_BV + B_KV_WIDTH].reshape(R, 1, B_KV_HEADS, B_HEAD_DIM))
        S[3].append(c_new)
        S[4].append(n_new)
        S[5].append(m_new[:, 0, 0:C_HEADS])
        z_new = jnp.concatenate([zg_s, za_s], axis=-1)
        S[6].append(jnp.stack([state_ffn_conv[l][:, CONV_W - 2], z_new], axis=1))

    st = lambda lst: jnp.stack(lst, axis=0)
    return (xp.reshape(Bp, T, D_MODEL), xs.reshape(R, 1, D_MODEL),
            st(P[0]), st(P[1]), st(S[1]), st(S[2]),
            st(P[2]), st(P[3]), st(P[4]),
            st(S[3]), st(S[4]), st(S[5]),
            st(P[5]), st(S[6]),
            st(S[0]))
```

```python
import functools
import math

import numpy as np
import jax
import jax.numpy as jnp
from jax import lax
from jax.experimental import pallas as pl
from jax.experimental.pallas import tpu as pltpu

f32 = jnp.float32
bf16 = jnp.bfloat16

D_MODEL = 2048
EPS = 1e-6
NEG_INF = -1e30
SQRT_HALF = 0.7071067811865476

A_GROUPS = 4
A_DIM = 128
A_WIDTH = 512
CHUNK = 128
B_HEADS = 16
B_KV_HEADS = 2
B_HEAD_DIM = 64
GQA = 8
B_WIDTH = 1024
B_KV_WIDTH = 128
WINDOW = 128
N_BUCKETS = 32
MAX_DISTANCE = 128
C_HEADS = 4
C_QK_DIM = 64
C_V_DIM = 128
C_QK_WIDTH = 256
C_WIDTH = 512
IN_SPLITS = (A_WIDTH, A_WIDTH, B_WIDTH, B_KV_WIDTH, B_KV_WIDTH,
             C_QK_WIDTH, C_QK_WIDTH, C_WIDTH, C_WIDTH, C_HEADS, C_HEADS)
IN_OFFSETS = tuple(int(o) for o in np.cumsum(IN_SPLITS)[:-1])
D_FF = 5632
CONV_W = 3

Z_WIDTH = 4096
Z_AU, Z_AV, Z_BQ, Z_BK, Z_BV, Z_CQ, Z_CK, Z_CV, Z_CO, Z_GATE = (0,) + IN_OFFSETS[:9]
C_PAIR = 2 * C_V_DIM

MLSTM_L = 128
UP_CHUNK = 256
UP_ROWS = 256
C_SEQS = 2
DEC_ROWS = 1
MM_TILE_N = {(D_MODEL, Z_WIDTH): 1024, (D_MODEL, D_MODEL): 1024, (D_MODEL, 2 * D_FF): 512, (D_FF, D_MODEL): 512}
LANES = 128
SUBLANES = 8
VMEM_LIMIT = 52 * 1024 * 1024


def _cparams(sem, flags=None):
    return pltpu.CompilerParams(dimension_semantics=sem, vmem_limit_bytes=VMEM_LIMIT, flags=flags)


def _gelu(x):
    return 0.5 * x * (1.0 + lax.erf(x * SQRT_HALF))


def _rms(x, gain):
    return x * lax.rsqrt(jnp.mean(x * x, axis=-1, keepdims=True) + EPS) * gain


def _norm_head_pairs(x, gain2):
    lo = lax.broadcasted_iota(jnp.int32, x.shape, 1) < B_HEAD_DIM
    x2 = x * x
    s_lo = jnp.sum(jnp.where(lo, x2, 0.0), axis=-1, keepdims=True)
    s_hi = jnp.sum(jnp.where(lo, 0.0, x2), axis=-1, keepdims=True)
    ms = jnp.where(lo, s_lo, s_hi) * (1.0 / B_HEAD_DIM)
    return x * lax.rsqrt(ms + EPS) * gain2


def _t5_bucket_np(dist):
    n = np.maximum(dist, 0)
    max_exact = N_BUCKETS // 2
    nf = np.maximum(n, 1).astype(np.float32)
    large = max_exact + (np.log(nf / np.float32(max_exact)) / np.float32(math.log(MAX_DISTANCE / max_exact))
                         * np.float32(N_BUCKETS - max_exact)).astype(np.int32)
    return np.where(n < max_exact, n, np.minimum(large, N_BUCKETS - 1)).astype(np.int32)


def _bias_kernel(rb_ref, bkp_ref, bks_ref, op_ref, os_ref):
    bkp = bkp_ref[...]
    bks = bks_ref[...]
    for h in range(B_HEADS):
        accp = jnp.zeros(bkp.shape, f32)
        accs = jnp.zeros(bks.shape, f32)
        for b in range(N_BUCKETS):
            val = rb_ref[b, h]
            accp = jnp.where(bkp == b, val, accp)
            accs = jnp.where(bks == b, val, accs)
        op_ref[h] = accp
        os_ref[h:h + 1, :] = accs[0:1, :]


def _bias_tables(rel_bias):
    qi = np.arange(WINDOW)[:, None]
    cj = np.arange(WINDOW)[None, :]
    bkp = _t5_bucket_np((qi - cj) % WINDOW)
    j = np.arange(2 * WINDOW)
    dist_s = np.where(j < WINDOW, WINDOW - j, 0)
    bks = np.broadcast_to(_t5_bucket_np(dist_s)[None, :], (SUBLANES, 2 * WINDOW)).copy()
    return pl.pallas_call(
        _bias_kernel,
        out_shape=(jax.ShapeDtypeStruct((B_HEADS, WINDOW, WINDOW), f32),
                   jax.ShapeDtypeStruct((B_HEADS, 2 * WINDOW), f32)),
        in_specs=[pl.BlockSpec(memory_space=pltpu.SMEM),
                  pl.BlockSpec(memory_space=pltpu.VMEM),
                  pl.BlockSpec(memory_space=pltpu.VMEM)],
        out_specs=(pl.BlockSpec(memory_space=pltpu.VMEM), pl.BlockSpec(memory_space=pltpu.VMEM)),
        name="bias_tables",
    )(rel_bias, jnp.asarray(bkp), jnp.asarray(bks))


def _mm_in_kernel(x_ref, g_ref, w_ref, z_ref, h_ref):
    @pl.when(pl.program_id(1) == 0)
    def _():
        h_ref[...] = _rms(x_ref[...], g_ref[...]).astype(bf16)

    z_ref[...] = jnp.dot(h_ref[...], w_ref[...], preferred_element_type=f32)


def _mm_in(x, gain, w, layer, tm):
    M = x.shape[0]
    tn = MM_TILE_N[w.shape[-2:]]
    return pl.pallas_call(
        _mm_in_kernel,
        grid=(M // tm, Z_WIDTH // tn),
        in_specs=[pl.BlockSpec((tm, D_MODEL), lambda i, j: (i, 0)),
                  pl.BlockSpec((None, 1, D_MODEL), lambda i, j: (layer, 0, 0)),
                  pl.BlockSpec((None, D_MODEL, tn), lambda i, j: (layer, 0, j))],
        out_specs=pl.BlockSpec((tm, tn), lambda i, j: (i, j)),
        out_shape=jax.ShapeDtypeStruct((M, Z_WIDTH), f32),
        scratch_shapes=[pltpu.VMEM((tm, D_MODEL), bf16)],
        compiler_params=_cparams(("parallel", "arbitrary")),
        name="mm_in",
    )(x, gain, w)


def _mm_out_kernel(a_ref, b_ref, c_ref, x_ref, w_ref, o_ref, lhs_ref):
    @pl.when(pl.program_id(1) == 0)
    def _():
        lhs_ref[:, 0:A_WIDTH] = a_ref[...]
        lhs_ref[:, A_WIDTH:A_WIDTH + B_WIDTH] = b_ref[...]
        lhs_ref[:, A_WIDTH + B_WIDTH:D_MODEL] = c_ref[...]

    o_ref[...] = x_ref[...] + jnp.dot(lhs_ref[...], w_ref[...], preferred_element_type=f32)


def _mm_out(a, b, c, x, w, layer, tm):
    M = x.shape[0]
    tn = MM_TILE_N[w.shape[-2:]]
    return pl.pallas_call(
        _mm_out_kernel,
        grid=(M // tm, D_MODEL // tn),
        in_specs=[pl.BlockSpec((tm, A_WIDTH), lambda i, j: (i, 0)),
                  pl.BlockSpec((tm, B_WIDTH), lambda i, j: (i, 0)),
                  pl.BlockSpec((tm, C_WIDTH), lambda i, j: (i, 0)),
                  pl.BlockSpec((tm, tn), lambda i, j: (i, j)),
                  pl.BlockSpec((None, D_MODEL, tn), lambda i, j: (layer, 0, j))],
        out_specs=pl.BlockSpec((tm, tn), lambda i, j: (i, j)),
        out_shape=jax.ShapeDtypeStruct((M, D_MODEL), f32),
        scratch_shapes=[pltpu.VMEM((tm, D_MODEL), bf16)],
        compiler_params=_cparams(("parallel", "arbitrary")),
        name="mm_out",
    )(a, b, c, x, w)


def _mm_down_kernel(a_ref, x_ref, w_ref, o_ref):
    o_ref[...] = x_ref[...] + jnp.dot(a_ref[...], w_ref[...], preferred_element_type=f32)


def _mm_down(act, x, w, layer, tm):
    M = x.shape[0]
    tn = MM_TILE_N[w.shape[-2:]]
    return pl.pallas_call(
        _mm_down_kernel,
        grid=(M // tm, D_MODEL // tn),
        in_specs=[pl.BlockSpec((tm, D_FF), lambda i, j: (i, 0)),
                  pl.BlockSpec((tm, tn), lambda i, j: (i, j)),
                  pl.BlockSpec((None, D_FF, tn), lambda i, j: (layer, 0, j))],
        out_specs=pl.BlockSpec((tm, tn), lambda i, j: (i, j)),
        out_shape=jax.ShapeDtypeStruct((M, D_MODEL), f32),
        compiler_params=_cparams(("parallel", "arbitrary")),
        name="mm_down",
    )(act, x, w)


def _silu(x):
    return x * jax.nn.sigmoid(x)


def _mm_up_prompt_kernel(x_ref, n2_ref, wg_ref, wa_ref, cwg_ref, cwa_ref, cbg_ref, cba_ref,
                         act_ref, cs_ref, h_ref, carry_ref, zb_ref, *, tm, nj, n_steps, tiles_per_seq):
    s = pl.program_id(0)
    sa = jnp.minimum(s, n_steps - 1)
    ia = sa // nj
    ja = sa % nj
    tf = act_ref.shape[1]

    @pl.when(s == 0)
    def _():
        zb_ref[...] = jnp.zeros(zb_ref.shape, f32)
        carry_ref[...] = jnp.zeros(carry_ref.shape, f32)

    @pl.when(ja == 0)
    def _():
        h_ref[...] = _rms(x_ref[...], n2_ref[...]).astype(bf16)

    chunks = [slice(c0, c0 + UP_CHUNK) for c0 in range(0, tf, UP_CHUNK)]

    rblocks = [(r0, min(UP_ROWS, tm - r0)) for r0 in range(0, tm, UP_ROWS)]

    def conv(idx, cw_ref, cb_ref, cs, r0, nr):
        return (cb_ref[:, cs]
                + zb_ref[idx, r0 + SUBLANES - 2:r0 + SUBLANES - 2 + nr, cs] * cw_ref[0:1, cs]
                + zb_ref[idx, r0 + SUBLANES - 1:r0 + SUBLANES - 1 + nr, cs] * cw_ref[1:2, cs]
                + zb_ref[idx, r0 + SUBLANES:r0 + SUBLANES + nr, cs] * cw_ref[2:3, cs])

    for cs in chunks:
        for r0, nr in rblocks:
            g = conv(0, cwg_ref, cbg_ref, cs, r0, nr)
            a = conv(1, cwa_ref, cba_ref, cs, r0, nr)
            act_ref[r0:r0 + nr, cs] = (_silu(g) * a).astype(bf16)

    seq_start = (ia % tiles_per_seq) == 0
    for cs in chunks:
        for r0, nr in rblocks:
            for idx, w_ref in ((0, wg_ref), (1, wa_ref)):
                z = jnp.dot(h_ref[r0:r0 + nr, :], w_ref[:, cs], preferred_element_type=f32)
                if r0 == 0:
                    zb_ref[idx, 0:SUBLANES, cs] = jnp.where(seq_start, 0.0, carry_ref[idx, ja, :, cs])
                zb_ref[idx, SUBLANES + r0:SUBLANES + r0 + nr, cs] = z
                if r0 + nr == tm:
                    tail = z[nr - SUBLANES:nr, :]
                    carry_ref[idx, ja, :, cs] = tail
                    cs_ref[0, idx, :, cs] = tail


def _mm_up_prompt(x, norm2, w_up, conv_w, conv_b, layer, batch, tm):
    M = x.shape[0]
    seq = M // batch
    tiles_per_seq = seq // tm
    tf = MM_TILE_N[w_up.shape[-2:]]
    nj = D_FF // tf
    n_steps = (M // tm) * nj
    kern = functools.partial(_mm_up_prompt_kernel, tm=tm, nj=nj, n_steps=n_steps, tiles_per_seq=tiles_per_seq)
    ia = lambda s: jnp.minimum(s, n_steps - 1) // nj
    ja = lambda s: jnp.minimum(s, n_steps - 1) % nj
    ib = lambda s: jnp.maximum(s - 1, 0) // nj
    jb = lambda s: jnp.maximum(s - 1, 0) % nj
    return pl.pallas_call(
        kern,
        grid=(n_steps + 1,),
        in_specs=[pl.BlockSpec((tm, D_MODEL), lambda s: (ia(s), 0)),
                  pl.BlockSpec((None, 1, D_MODEL), lambda s: (layer, 0, 0)),
                  pl.BlockSpec((None, D_MODEL, tf), lambda s: (layer, 0, ja(s))),
                  pl.BlockSpec((None, D_MODEL, tf), lambda s: (layer, 0, nj + ja(s))),
                  pl.BlockSpec((None, CONV_W, tf), lambda s: (layer, 0, jb(s))),
                  pl.BlockSpec((None, CONV_W, tf), lambda s: (layer, 0, nj + jb(s))),
                  pl.BlockSpec((None, 1, tf), lambda s: (layer, 0, jb(s))),
                  pl.BlockSpec((None, 1, tf), lambda s: (layer, 0, nj + jb(s)))],
        out_specs=(pl.BlockSpec((tm, tf), lambda s: (ib(s), jb(s))),
                   pl.BlockSpec((1, 2, SUBLANES, tf), lambda s: (ia(s), 0, 0, ja(s)))),
        out_shape=(jax.ShapeDtypeStruct((M, D_FF), bf16),
                   jax.ShapeDtypeStruct((M // tm, 2, SUBLANES, D_FF), f32)),
        scratch_shapes=[pltpu.VMEM((tm, D_MODEL), bf16),
                        pltpu.VMEM((2, nj, SUBLANES, tf), f32),
                        pltpu.VMEM((2, tm + SUBLANES, tf), f32)],
        compiler_params=_cparams(("arbitrary",)),
        name="mm_up_prompt",
    )(x, norm2, w_up, w_up, conv_w, conv_w, conv_b, conv_b)


def _mm_up_sample_kernel(x_ref, n2_ref, wg_ref, wa_ref, cwg_ref, cwa_ref, cbg_ref, cba_ref,
                         b0g_ref, b1g_ref, b0a_ref, b1a_ref, act_ref, zg_ref, za_ref, h_ref):
    @pl.when(pl.program_id(0) == 0)
    def _():
        h_ref[...] = _rms(x_ref[...], n2_ref[...]).astype(bf16)

    def conv_half(w_ref, cw_ref, cb_ref, b0_ref, b1_ref, z_ref):
        z = jnp.dot(h_ref[...], w_ref[...], preferred_element_type=f32)
        z_ref[...] = z
        return (cb_ref[...] + b0_ref[...] * cw_ref[0:1, :] + b1_ref[...] * cw_ref[1:2, :]
                + z * cw_ref[2:3, :])

    g = conv_half(wg_ref, cwg_ref, cbg_ref, b0g_ref, b1g_ref, zg_ref)
    a = conv_half(wa_ref, cwa_ref, cba_ref, b0a_ref, b1a_ref, za_ref)
    act_ref[...] = (_silu(g) * a).astype(bf16)


def _mm_up_sample(x, norm2, w_up, conv_w, conv_b, buf, layer):
    M = x.shape[0]
    tf = MM_TILE_N[w_up.shape[-2:]]
    nj = D_FF // tf
    wspec = lambda off: pl.BlockSpec((None, D_MODEL, tf), lambda j: (layer, 0, off + j))
    cwspec = lambda off: pl.BlockSpec((None, CONV_W, tf), lambda j: (layer, 0, off + j))
    cbspec = lambda off: pl.BlockSpec((None, 1, tf), lambda j: (layer, 0, off + j))
    bufspec = lambda row, off: pl.BlockSpec((None, None, M, tf), lambda j: (layer, row, 0, off + j))
    return pl.pallas_call(
        _mm_up_sample_kernel,
        grid=(nj,),
        in_specs=[pl.BlockSpec((M, D_MODEL), lambda j: (0, 0)),
                  pl.BlockSpec((None, 1, D_MODEL), lambda j: (layer, 0, 0)),
                  wspec(0), wspec(nj), cwspec(0), cwspec(nj), cbspec(0), cbspec(nj),
                  bufspec(0, 0), bufspec(1, 0), bufspec(0, nj), bufspec(1, nj)],
        out_specs=(pl.BlockSpec((M, tf), lambda j: (0, j)),
                   pl.BlockSpec((M, tf), lambda j: (0, j)),
                   pl.BlockSpec((M, tf), lambda j: (0, j))),
        out_shape=(jax.ShapeDtypeStruct((M, D_FF), bf16),
                   jax.ShapeDtypeStruct((M, D_FF), f32),
                   jax.ShapeDtypeStruct((M, D_FF), f32)),
        scratch_shapes=[pltpu.VMEM((M, D_MODEL), bf16)],
        compiler_params=_cparams(("arbitrary",)),
        name="mm_up_sample",
    )(x, norm2, w_up, w_up, conv_w, conv_w, conv_b, conv_b, buf, buf, buf, buf)


def _mix_a_prompt_kernel(u_ref, v_ref, vg_ref, ws_ref, bs_ref, o_ref):
    row = lax.broadcasted_iota(jnp.int32, (CHUNK, CHUNK), 0)
    col = lax.broadcasted_iota(jnp.int32, (CHUNK, CHUNK), 1)
    tri = col <= row
    for g in range(A_GROUPS):
        sl = slice(g * A_DIM, (g + 1) * A_DIM)
        ws = jnp.where(tri, ws_ref[g], 0.0).astype(bf16)
        for r0 in range(0, u_ref.shape[0], CHUNK):
            rs = slice(r0, r0 + CHUNK)
            vn = _rms(_gelu(v_ref[rs, sl]), vg_ref[:, sl])
            mixv = jnp.dot(ws, vn.astype(bf16), preferred_element_type=f32) + bs_ref[:, sl]
            o_ref[rs, sl] = (_gelu(u_ref[rs, sl]) * mixv).astype(bf16)


def _mix_a_prompt(z, v_gain, w_s, b_s_full, layer, rows):
    M = z.shape[0]
    return pl.pallas_call(
        _mix_a_prompt_kernel,
        grid=(M // rows,),
        in_specs=[pl.BlockSpec((rows, A_WIDTH), lambda r: (r, Z_AU // A_WIDTH)),
                  pl.BlockSpec((rows, A_WIDTH), lambda r: (r, Z_AV // A_WIDTH)),
                  pl.BlockSpec((None, 1, A_WIDTH), lambda r: (layer, 0, 0)),
                  pl.BlockSpec((None, A_GROUPS, CHUNK, CHUNK), lambda r: (layer, 0, 0, 0)),
                  pl.BlockSpec((None, CHUNK, A_WIDTH), lambda r: (layer, 0, 0))],
        out_specs=pl.BlockSpec((rows, A_WIDTH), lambda r: (r, 0)),
        out_shape=jax.ShapeDtypeStruct((M, A_WIDTH), bf16),
        compiler_params=_cparams(("parallel",)),
        name="mix_a_prompt",
    )(z, z, v_gain, w_s, b_s_full)


def _mix_a_sample_kernel(u_ref, v_ref, vg_ref, ws_ref, bs_ref, o_ref, vn_ref):
    for g in range(A_GROUPS):
        sl = slice(g * A_DIM, (g + 1) * A_DIM)
        vn = _rms(_gelu(v_ref[:, sl]), vg_ref[:, sl])
        vn_ref[:, sl] = vn
        mixv = ws_ref[:, sl] * vn + bs_ref[:, sl]
        o_ref[:, sl] = (_gelu(u_ref[:, sl]) * mixv).astype(bf16)


def _mix_a_sample(z, v_gain, ws0, bs0, layer):
    M = z.shape[0]
    vec = pl.BlockSpec((None, 1, A_WIDTH), lambda r: (layer, 0, 0))
    return pl.pallas_call(
        _mix_a_sample_kernel,
        grid=(1,),
        in_specs=[pl.BlockSpec((M, A_WIDTH), lambda r: (0, Z_AU // A_WIDTH)),
                  pl.BlockSpec((M, A_WIDTH), lambda r: (0, Z_AV // A_WIDTH)),
                  vec, vec, vec],
        out_specs=(pl.BlockSpec((M, A_WIDTH), lambda r: (0, 0)),
                   pl.BlockSpec((M, A_WIDTH), lambda r: (0, 0))),
        out_shape=(jax.ShapeDtypeStruct((M, A_WIDTH), bf16),
                   jax.ShapeDtypeStruct((M, A_WIDTH), f32)),
        compiler_params=_cparams(("arbitrary",)),
        name="mix_a_sample",
    )(z, z, v_gain, ws0, bs0)


def _mix_b_prompt_kernel(sink_ref, q_ref, k_ref, v_ref, qg_ref, kg_ref, bias_ref,
                         o_ref, klast_ref, vlast_ref, kcat_ref, vcat_ref):
    i = pl.program_id(1)
    W = WINDOW
    kn = _norm_head_pairs(k_ref[...], kg_ref[...])
    v = v_ref[...]
    klast_ref[0] = kn
    vlast_ref[0] = v

    @pl.when(i == 0)
    def _():
        kcat_ref[0:W, :] = jnp.zeros((W, LANES), f32)
        vcat_ref[0:W, :] = jnp.zeros((W, LANES), f32)

    @pl.when(i > 0)
    def _():
        kcat_ref[0:W, :] = kcat_ref[W:2 * W, :]
        vcat_ref[0:W, :] = vcat_ref[W:2 * W, :]

    kcat_ref[W:2 * W, :] = kn
    vcat_ref[W:2 * W, :] = v
    kc = kcat_ref[...]
    vc = vcat_ref[...]
    kc_sw = pltpu.roll(kc, B_HEAD_DIM, 1)
    vc_sw = pltpu.roll(vc, B_HEAD_DIM, 1)
    lo = lax.broadcasted_iota(jnp.int32, (2 * W, LANES), 1) < B_HEAD_DIM

    qi = lax.broadcasted_iota(jnp.int32, (W, W), 0)
    cj = lax.broadcasted_iota(jnp.int32, (W, W), 1)
    own = cj <= qi
    keep = own | (i > 0)
    q_gain = qg_ref[...] * (B_HEAD_DIM ** -0.5)

    tiles = GQA // 2
    dn = (((1,), (1,)), ((), ()))
    scores, values = [], []
    for kvh in range(B_KV_HEADS):
        k_src, k_alt = (kc, kc_sw) if kvh == 0 else (kc_sw, kc)
        v_src, v_alt = (vc, vc_sw) if kvh == 0 else (vc_sw, vc)
        k_even = jnp.where(lo, k_src, 0.0).astype(bf16)
        k_odd = jnp.where(lo, 0.0, k_alt).astype(bf16)
        values.append((jnp.where(lo, v_src, 0.0).astype(bf16), jnp.where(lo, 0.0, v_alt).astype(bf16)))
        qs = jnp.concatenate(
            [_norm_head_pairs(q_ref[:, (kvh * tiles + t) * LANES:(kvh * tiles + t + 1) * LANES], q_gain)
             for t in range(tiles)], axis=0).astype(bf16)
        scores.append((lax.dot_general(qs, k_even, dn, preferred_element_type=f32),
                       lax.dot_general(qs, k_odd, dn, preferred_element_type=f32)))
    for kvh in range(B_KV_HEADS):
        p_par = []
        for par in range(2):
            blocks = []
            for t in range(tiles):
                h = kvh * GQA + 2 * t + par
                sb = scores[kvh][par][t * W:(t + 1) * W, :]
                s = jnp.where(own, sb[:, W:2 * W], sb[:, 0:W]) + bias_ref[h]
                s = jnp.where(keep, s, NEG_INF)
                sk = sink_ref[h]
                mx = jnp.maximum(jnp.max(s, axis=-1, keepdims=True), sk)
                p = jnp.exp(s - mx)
                den = jnp.sum(p, axis=-1, keepdims=True) + jnp.exp(sk - mx)
                p = p * (1.0 / den)
                blocks.append(jnp.concatenate([jnp.where(own, 0.0, p), jnp.where(own, p, 0.0)],
                                              axis=1).astype(bf16))
            p_par.append(jnp.concatenate(blocks, axis=0))
        o = (jnp.dot(p_par[0], values[kvh][0], preferred_element_type=f32)
             + jnp.dot(p_par[1], values[kvh][1], preferred_element_type=f32))
        for t in range(tiles):
            c0 = (kvh * tiles + t) * LANES
            o_ref[:, c0:c0 + LANES] = o[t * W:(t + 1) * W, :].astype(bf16)


def _mix_b_prompt(z, sinks, q_gain2, k_gain2, bias_p, layer, batch):
    M = z.shape[0]
    nb = M // batch // WINDOW
    return pl.pallas_call(
        _mix_b_prompt_kernel,
        grid=(batch, nb),
        in_specs=[pl.BlockSpec(memory_space=pltpu.SMEM),
                  pl.BlockSpec((WINDOW, B_WIDTH), lambda b, i: (b * nb + i, Z_BQ // B_WIDTH)),
                  pl.BlockSpec((WINDOW, B_KV_WIDTH), lambda b, i: (b * nb + i, Z_BK // B_KV_WIDTH)),
                  pl.BlockSpec((WINDOW, B_KV_WIDTH), lambda b, i: (b * nb + i, Z_BV // B_KV_WIDTH)),
                  pl.BlockSpec((None, 1, LANES), lambda b, i: (layer, 0, 0)),
                  pl.BlockSpec((None, 1, LANES), lambda b, i: (layer, 0, 0)),
                  pl.BlockSpec((B_HEADS, WINDOW, WINDOW), lambda b, i: (0, 0, 0))],
        out_specs=(pl.BlockSpec((WINDOW, B_WIDTH), lambda b, i: (b * nb + i, 0)),
                   pl.BlockSpec((1, WINDOW, B_KV_WIDTH), lambda b, i: (b, 0, 0)),
                   pl.BlockSpec((1, WINDOW, B_KV_WIDTH), lambda b, i: (b, 0, 0))),
        out_shape=(jax.ShapeDtypeStruct((M, B_WIDTH), bf16),
                   jax.ShapeDtypeStruct((batch, WINDOW, B_KV_WIDTH), f32),
                   jax.ShapeDtypeStruct((batch, WINDOW, B_KV_WIDTH), f32)),
        scratch_shapes=[pltpu.VMEM((2 * WINDOW, LANES), f32),
                        pltpu.VMEM((2 * WINDOW, LANES), f32)],
        compiler_params=_cparams(("arbitrary", "arbitrary")),
        name="mix_b_prompt",
    )(sinks, z, z, z, q_gain2, k_gain2, bias_p)


def _mix_c_prompt_kernel(q_ref, k_ref, v0_ref, v1_ref, o0_ref, o1_ref, g_ref, gb_ref, hg_ref,
                         out_ref, c_out, n_out, m_out, c_s, n_s, m_s):
    L = MLSTM_L
    c = pl.program_id(1)

    @pl.when(c == 0)
    def _():
        c_s[...] = jnp.zeros(c_s.shape, f32)
        n_s[...] = jnp.zeros(n_s.shape, f32)
        m_s[...] = jnp.zeros(m_s.shape, f32)

    row = lax.broadcasted_iota(jnp.int32, (L, L), 0)
    col = lax.broadcasted_iota(jnp.int32, (L, L), 1)
    tri = col <= row
    tri_f = tri.astype(f32)
    seqs = range(q_ref.shape[0])
    units = [(bi, h) for bi in seqs for h in range(C_HEADS)]
    g, b = {}, {}
    for bi in seqs:
        g[bi] = g_ref[bi] + gb_ref[...]
        b[bi] = jnp.dot(tri_f, jax.nn.log_sigmoid(g[bi]), precision=lax.Precision.HIGHEST,
                        preferred_element_type=f32)
    bT = {bi: b[bi].T for bi in seqs}
    gT = {bi: g[bi].T for bi in seqs}
    qh, kh, qb, vb, c_prev, n_prev, qk, qc = {}, {}, {}, {}, {}, {}, {}, {}
    for u in units:
        bi, h = u
        v_ref = v0_ref if h < 2 else v1_ref
        vsl = slice((h % 2) * C_V_DIM, (h % 2 + 1) * C_V_DIM)
        qh[u] = q_ref[bi, :, h * C_QK_DIM:(h + 1) * C_QK_DIM]
        kh[u] = k_ref[bi, :, h * C_QK_DIM:(h + 1) * C_QK_DIM] * (C_QK_DIM ** -0.5)
        qb[u] = qh[u].astype(bf16)
        vb[u] = v_ref[bi, :, vsl].astype(bf16)
        c_prev[u] = c_s[bi, h]
        n_prev[u] = n_s[bi, h]
        qk[u] = lax.dot_general(qb[u], kh[u].astype(bf16), (((1,), (1,)), ((), ())), preferred_element_type=f32)
        qc[u] = jnp.dot(qb[u], c_prev[u].astype(bf16), preferred_element_type=f32)
    s, iw, mt, kw, decay, m_new = {}, {}, {}, {}, {}, {}
    for u in units:
        bi, h = u
        b_col = b[bi][:, C_HEADS + h:C_HEADS + h + 1]
        i_col = g[bi][:, h:h + 1]
        b_row = bT[bi][C_HEADS + h:C_HEADS + h + 1, :]
        i_row = gT[bi][h:h + 1, :]
        dlog = jnp.where(tri, b_col - b_row + i_row, -jnp.inf)
        m_prev = m_s[bi, h][0:1, 0:1]
        inter = b_col + m_prev
        mt[u] = jnp.maximum(inter, jnp.max(dlog, axis=-1, keepdims=True))
        s[u] = qk[u] * jnp.exp(dlog - mt[u])
        iw[u] = jnp.exp(inter - mt[u])
        m_new[u] = mt[u][L - 1:L, :]
        b_last = b_col[L - 1:L, :]
        wc = jnp.exp(b_last - b_col + i_col - m_new[u])
        decay[u] = jnp.exp(b_last + m_prev - m_new[u])
        kw[u] = kh[u] * wc
    sv = {u: jnp.dot(s[u].astype(bf16), vb[u], preferred_element_type=f32) for u in units}
    kv = {u: lax.dot_general(kw[u].astype(bf16), vb[u], (((0,), (0,)), ((), ())), preferred_element_type=f32)
          for u in units}
    for u in units:
        bi, h = u
        o_ref = o0_ref if h < 2 else o1_ref
        vsl = slice((h % 2) * C_V_DIM, (h % 2 + 1) * C_V_DIM)
        num = iw[u] * qc[u] + sv[u]
        qn = iw[u] * jnp.sum(qh[u] * n_prev[u], axis=-1, keepdims=True) + jnp.sum(s[u], axis=-1, keepdims=True)
        hh = num / jnp.maximum(jnp.abs(qn), jnp.exp(-mt[u]))
        c_s[bi, h] = decay[u] * c_prev[u] + kv[u]
        n_s[bi, h] = decay[u] * n_prev[u] + jnp.sum(kw[u], axis=0, keepdims=True)
        m_s[bi, h] = jnp.broadcast_to(m_new[u], (SUBLANES, LANES))
        osl = slice(h * C_V_DIM, (h + 1) * C_V_DIM)
        out_ref[bi, :, osl] = (_rms(hh, hg_ref[:, osl]) * jax.nn.sigmoid(o_ref[bi, :, vsl])).astype(bf16)

    @pl.when(c == pl.num_programs(1) - 1)
    def _():
        c_out[...] = c_s[...]
        n_out[...] = n_s[...]
        m_out[...] = m_s[...]


def _mix_c_prompt(z, gate_bias, h_gain, layer, batch):
    M = z.shape[0]
    L = MLSTM_L
    T = M // batch
    z3 = z.reshape(batch, T, Z_WIDTH)
    ns = C_SEQS if batch % C_SEQS == 0 else 1
    col = lambda width, off: pl.BlockSpec((ns, L, width), lambda b, c: (b, c, off // width))
    state = lambda *shape: pl.BlockSpec((ns,) + shape, lambda b, c: (b,) + (0,) * len(shape))
    out, c_st, n_st, m_st = pl.pallas_call(
        _mix_c_prompt_kernel,
        grid=(batch // ns, T // L),
        in_specs=[col(C_QK_WIDTH, Z_CQ), col(C_QK_WIDTH, Z_CK),
                  col(C_PAIR, Z_CV), col(C_PAIR, Z_CV + C_PAIR),
                  col(C_PAIR, Z_CO), col(C_PAIR, Z_CO + C_PAIR),
                  col(LANES, Z_GATE),
                  pl.BlockSpec((None, 1, LANES), lambda b, c: (layer, 0, 0)),
                  pl.BlockSpec((None, 1, C_WIDTH), lambda b, c: (layer, 0, 0))],
        out_specs=(pl.BlockSpec((ns, L, C_WIDTH), lambda b, c: (b, c, 0)),
                   state(C_HEADS, C_QK_DIM, C_V_DIM),
                   state(C_HEADS, 1, C_QK_DIM),
                   state(C_HEADS, SUBLANES, LANES)),
        out_shape=(jax.ShapeDtypeStruct((batch, T, C_WIDTH), bf16),
                   jax.ShapeDtypeStruct((batch, C_HEADS, C_QK_DIM, C_V_DIM), f32),
                   jax.ShapeDtypeStruct((batch, C_HEADS, 1, C_QK_DIM), f32),
                   jax.ShapeDtypeStruct((batch, C_HEADS, SUBLANES, LANES), f32)),
        scratch_shapes=[pltpu.VMEM((ns, C_HEADS, C_QK_DIM, C_V_DIM), f32),
                        pltpu.VMEM((ns, C_HEADS, 1, C_QK_DIM), f32),
                        pltpu.VMEM((ns, C_HEADS, SUBLANES, LANES), f32)],
        compiler_params=_cparams(("arbitrary", "arbitrary")),
        name="mix_c_prompt",
    )(z3, z3, z3, z3, z3, z3, z3, gate_bias, h_gain)
    return out.reshape(M, C_WIDTH), c_st, n_st, m_st


_BC_SHARED = (4, 5, 6, 7, 12, 13)


def _mix_bc_sample_kernel(*refs):
    for r in range(refs[0].shape[0]):
        _mix_bc_sample_row(*[ref if k in _BC_SHARED else ref.at[pl.ds(r, 1)] for k, ref in enumerate(refs)])


def _mix_bc_sample_row(zr_ref, q_ref, kp_ref, vp_ref, qg_ref, kg_ref, sink_ref, bias_ref,
                       cq_ref, ck_ref, cv_ref, co_ref, gb_ref, hg_ref, c0_ref, n0_ref, m0_ref,
                       bo_ref, kn_ref, co_out, c_out, n_out, m_out):
    W = WINDOW
    zr = zr_ref[0]

    qn = _rms(q_ref[0], qg_ref[...])
    hrow = lax.broadcasted_iota(jnp.int32, (B_HEADS, LANES), 0)
    lane = lax.broadcasted_iota(jnp.int32, (B_HEADS, LANES), 1)
    own = (hrow < GQA) == (lane < B_HEAD_DIM)
    q2 = jnp.where(own, jnp.concatenate([qn, qn], axis=-1), 0.0)
    k_new = _norm_head_pairs(zr[:, Z_BK:Z_BK + B_KV_WIDTH], kg_ref[...])
    v_new = zr[:, Z_BV:Z_BV + B_KV_WIDTH]
    kn_ref[0] = k_new
    scale = B_HEAD_DIM ** -0.5
    s_past = lax.dot_general(q2.astype(bf16), kp_ref[0].astype(bf16), (((1,), (1,)), ((), ())),
                             preferred_element_type=f32) * scale + bias_ref[:, 0:W]
    s_past = jnp.where(lane >= 1, s_past, NEG_INF)
    s_new = jnp.sum(q2 * k_new, axis=-1, keepdims=True) * scale + bias_ref[:, W:W + 1]
    sk = sink_ref[...]
    mx = jnp.maximum(jnp.maximum(jnp.max(s_past, axis=-1, keepdims=True), s_new), sk)
    p_past = jnp.exp(s_past - mx)
    p_new = jnp.exp(s_new - mx)
    den = jnp.sum(p_past, axis=-1, keepdims=True) + p_new + jnp.exp(sk - mx)
    inv = 1.0 / den
    o2 = (jnp.dot((p_past * inv).astype(bf16), vp_ref[0].astype(bf16), preferred_element_type=f32)
          + (p_new * inv) * v_new)
    o2_sw = pltpu.roll(o2, B_HEAD_DIM, 1)
    bo_ref[0] = jnp.where(hrow < GQA, o2, o2_sw)[:, 0:B_HEAD_DIM].astype(bf16)

    g = zr[:, Z_GATE:Z_GATE + LANES] + gb_ref[...]
    lf = jax.nn.log_sigmoid(g)
    m0 = m0_ref[0]
    lane1 = lax.broadcasted_iota(jnp.int32, (1, LANES), 1)
    m_row = jnp.zeros((1, LANES), f32)
    row8 = lax.broadcasted_iota(jnp.int32, (SUBLANES, C_QK_DIM), 0)
    c_rows = []
    for h in range(C_HEADS):
        ig = g[:, h:h + 1]
        b = lf[:, C_HEADS + h:C_HEADS + h + 1]
        m_prev = m0[:, h:h + 1]
        inter = b + m_prev
        mt = jnp.maximum(inter, ig)
        qh = cq_ref[0, h:h + 1, :]
        kh = ck_ref[0, h:h + 1, :] * (C_QK_DIM ** -0.5)
        vh = cv_ref[0, h:h + 1, :]
        c_prev = c0_ref[0, h]
        n_prev = n0_ref[0, h:h + 1, :]
        s = jnp.sum(qh * kh, axis=-1, keepdims=True) * jnp.exp(ig - mt)
        iw = jnp.exp(inter - mt)
        q8 = jnp.broadcast_to(qh, (SUBLANES, C_QK_DIM)).astype(bf16)
        qc = jnp.dot(q8, c_prev.astype(bf16), preferred_element_type=f32)[0:1, :]
        num = iw * qc + s * vh
        qn_ = iw * jnp.sum(qh * n_prev, axis=-1, keepdims=True) + s
        hh = num / jnp.maximum(jnp.abs(qn_), jnp.exp(-mt))
        wc = jnp.exp(ig - mt)
        decay = jnp.exp(inter - mt)
        kw = kh * wc
        kw8 = jnp.where(row8 == 0, jnp.broadcast_to(kw, (SUBLANES, C_QK_DIM)), 0.0).astype(bf16)
        v8 = jnp.broadcast_to(vh, (SUBLANES, C_V_DIM)).astype(bf16)
        c_out[0, h] = decay * c_prev + lax.dot_general(kw8, v8, (((0,), (0,)), ((), ())),
                                                      preferred_element_type=f32)
        n_out[0, h:h + 1, :] = decay * n_prev + kw
        m_row = jnp.where(lane1 == h, mt, m_row)
        c_rows.append(_rms(hh, hg_ref[h:h + 1, :]) * jax.nn.sigmoid(co_ref[0, h:h + 1, :]))
    co_out[0] = jnp.concatenate(c_rows, axis=0).astype(bf16)
    m_out[0] = m_row


def _mix_bc_sample(z, k_past, v_past, q_gain, k_gain2, sinks, bias_s, gate_bias, h_gain4,
                   c0, n0, m0, layer):
    R = z.shape[0]
    zr = z.reshape(R, 1, Z_WIDTH)
    q = z[:, Z_BQ:Z_BQ + B_WIDTH].reshape(R, B_HEADS, B_HEAD_DIM)
    cq = z[:, Z_CQ:Z_CQ + C_QK_WIDTH].reshape(R, C_HEADS, C_QK_DIM)
    ck = z[:, Z_CK:Z_CK + C_QK_WIDTH].reshape(R, C_HEADS, C_QK_DIM)
    cv = z[:, Z_CV:Z_CV + C_WIDTH].reshape(R, C_HEADS, C_V_DIM)
    co = z[:, Z_CO:Z_CO + C_WIDTH].reshape(R, C_HEADS, C_V_DIM)
    rb = DEC_ROWS if R % DEC_ROWS == 0 else 1
    row3 = lambda n: pl.BlockSpec((rb, 1, n), lambda r: (r, 0, 0))
    return pl.pallas_call(
        _mix_bc_sample_kernel,
        grid=(R // rb,),
        in_specs=[row3(Z_WIDTH),
                  pl.BlockSpec((rb, B_HEADS, B_HEAD_DIM), lambda r: (r, 0, 0)),
                  pl.BlockSpec((None, rb, WINDOW, B_KV_WIDTH), lambda r: (layer, r, 0, 0)),
                  pl.BlockSpec((None, rb, WINDOW, B_KV_WIDTH), lambda r: (layer, r, 0, 0)),
                  pl.BlockSpec((None, 1, B_HEAD_DIM), lambda r: (layer, 0, 0)),
                  pl.BlockSpec((None, 1, LANES), lambda r: (layer, 0, 0)),
                  pl.BlockSpec((None, B_HEADS, 1), lambda r: (layer, 0, 0)),
                  pl.BlockSpec((B_HEADS, 2 * WINDOW), lambda r: (0, 0)),
                  pl.BlockSpec((rb, C_HEADS, C_QK_DIM), lambda r: (r, 0, 0)),
                  pl.BlockSpec((rb, C_HEADS, C_QK_DIM), lambda r: (r, 0, 0)),
                  pl.BlockSpec((rb, C_HEADS, C_V_DIM), lambda r: (r, 0, 0)),
                  pl.BlockSpec((rb, C_HEADS, C_V_DIM), lambda r: (r, 0, 0)),
                  pl.BlockSpec((None, 1, LANES), lambda r: (layer, 0, 0)),
                  pl.BlockSpec((None, C_HEADS, C_V_DIM), lambda r: (layer, 0, 0)),
                  pl.BlockSpec((None, rb, C_HEADS, C_QK_DIM, C_V_DIM), lambda r: (layer, r, 0, 0, 0)),
                  pl.BlockSpec((None, rb, C_HEADS, C_QK_DIM), lambda r: (layer, r, 0, 0)),
                  pl.BlockSpec((None, rb, 1, C_HEADS), lambda r: (layer, r, 0, 0))],
        out_specs=(pl.BlockSpec((rb, B_HEADS, B_HEAD_DIM), lambda r: (r, 0, 0)),
                   row3(B_KV_WIDTH),
                   pl.BlockSpec((rb, C_HEADS, C_V_DIM), lambda r: (r, 0, 0)),
                   pl.BlockSpec((rb, C_HEADS, C_QK_DIM, C_V_DIM), lambda r: (r, 0, 0, 0)),
                   pl.BlockSpec((rb, C_HEADS, C_QK_DIM), lambda r: (r, 0, 0)),
                   row3(LANES)),
        out_shape=(jax.ShapeDtypeStruct((R, B_HEADS, B_HEAD_DIM), bf16),
                   jax.ShapeDtypeStruct((R, 1, B_KV_WIDTH), f32),
                   jax.ShapeDtypeStruct((R, C_HEADS, C_V_DIM), bf16),
                   jax.ShapeDtypeStruct((R, C_HEADS, C_QK_DIM, C_V_DIM), f32),
                   jax.ShapeDtypeStruct((R, C_HEADS, C_QK_DIM), f32),
                   jax.ShapeDtypeStruct((R, 1, LANES), f32)),
        compiler_params=_cparams(("parallel",)),
        name="mix_bc_sample",
    )(zr, q, k_past, v_past, q_gain, k_gain2, sinks, bias_s, cq, ck, cv, co, gate_bias, h_gain4,
      c0, n0, m0)


def _prep_w_in(w_in):
    return jnp.pad(w_in.astype(bf16), ((0, 0), (0, 0), (0, Z_WIDTH - w_in.shape[-1])))


def _row_tile(m, pref):
    return pref if m % pref == 0 else m


def kernel(x_prompt, x_sample, cache_swa_k, cache_swa_v, state_mlstm_C, state_mlstm_n, state_mlstm_m, state_ffn_conv, rel_bias, norm1, w_in, a_v_gain, a_spatial_w, a_spatial_b, b_q_gain, b_k_gain, b_sinks, c_gate_bias, c_h_gain, w_out, norm2, w_up, ffn_conv_w, ffn_conv_b, w_down):
    depth = w_in.shape[0]
    Bp, T, _ = x_prompt.shape
    R = x_sample.shape[0]
    assert x_sample.shape[1] == 1 and T % CHUNK == 0 and T % MLSTM_L == 0

    w_in_p = _prep_w_in(w_in)
    w_out_b = w_out.astype(bf16)
    w_up_b = w_up.astype(bf16)
    w_down_b = w_down.astype(bf16)
    norm1_3 = norm1.reshape(depth, 1, D_MODEL)
    norm2_3 = norm2.reshape(depth, 1, D_MODEL)
    v_gain3 = a_v_gain.reshape(depth, 1, A_WIDTH)
    bs_full = jnp.repeat(jnp.swapaxes(a_spatial_b, 1, 2), A_DIM, axis=-1)
    ws0 = jnp.repeat(a_spatial_w[:, :, 0, 0], A_DIM, axis=-1).reshape(depth, 1, A_WIDTH)
    bs0 = jnp.repeat(a_spatial_b[:, :, 0], A_DIM, axis=-1).reshape(depth, 1, A_WIDTH)
    q_gain3 = b_q_gain.reshape(depth, 1, B_HEAD_DIM)
    q_gain2 = jnp.tile(b_q_gain, (1, 2)).reshape(depth, 1, LANES)
    k_gain2 = jnp.tile(b_k_gain, (1, 2)).reshape(depth, 1, LANES)
    sinks3 = b_sinks.reshape(depth, B_HEADS, 1)
    gate_b = jnp.pad(c_gate_bias, ((0, 0), (0, LANES - 2 * C_HEADS))).reshape(depth, 1, LANES)
    h_gain3 = c_h_gain.reshape(depth, 1, C_WIDTH)
    h_gain4 = c_h_gain.reshape(depth, C_HEADS, C_V_DIM)
    conv_b3 = ffn_conv_b.reshape(depth, 1, 2 * D_FF)
    k_cache = cache_swa_k.reshape(depth, R, WINDOW, B_KV_WIDTH)
    v_cache = cache_swa_v.reshape(depth, R, WINDOW, B_KV_WIDTH)
    m_state = state_mlstm_m.reshape(depth, R, 1, C_HEADS)
    conv_hist = jnp.swapaxes(state_ffn_conv, 1, 2)

    bias_p, bias_s = _bias_tables(rel_bias)

    Mp = Bp * T
    tm = _row_tile(T, 1024)
    a_rows = _row_tile(T, 512)
    xp = x_prompt.reshape(Mp, D_MODEL)
    xs = x_sample.reshape(R, D_MODEL)
    P = [[] for _ in range(6)]
    S = [[] for _ in range(7)]
    for l in range(depth):
        z = _mm_in(xp, norm1_3, w_in_p, l, tm)
        a_o = _mix_a_prompt(z, v_gain3, a_spatial_w, bs_full, l, a_rows)
        b_o, k_last, v_last = _mix_b_prompt(z, b_sinks[l], q_gain2, k_gain2, bias_p, l, Bp)
        c_o, c_st, n_st, m_st = _mix_c_prompt(z, gate_b, h_gain3, l, Bp)
        x1 = _mm_out(a_o, b_o, c_o, xp, w_out_b, l, tm)
        act, cs = _mm_up_prompt(x1, norm2_3, w_up_b, ffn_conv_w, conv_b3, l, Bp, tm)
        xp = _mm_down(act, x1, w_down_b, l, tm)
        P[0].append(k_last.reshape(Bp, WINDOW, B_KV_HEADS, B_HEAD_DIM))
        P[1].append(v_last.reshape(Bp, WINDOW, B_KV_HEADS, B_HEAD_DIM))
        P[2].append(c_st)
        P[3].append(n_st.reshape(Bp, C_HEADS, C_QK_DIM))
        P[4].append(m_st[:, :, 0, 0])
        seq_tiles = T // tm
        tail = cs[seq_tiles - 1::seq_tiles, :, SUBLANES - (CONV_W - 1):, :]
        P[5].append(jnp.swapaxes(tail, 1, 2).reshape(Bp, CONV_W - 1, 2 * D_FF))

        zs = _mm_in(xs, norm1_3, w_in_p, l, R)
        a_s, vn_s = _mix_a_sample(zs, v_gain3, ws0, bs0, l)
        b_s, kn_s, c_s, c_new, n_new, m_new = _mix_bc_sample(
            zs, k_cache, v_cache, q_gain3, k_gain2, sinks3, bias_s, gate_b, h_gain4,
            state_mlstm_C, state_mlstm_n, m_state, l)
        x1s = _mm_out(a_s, b_s.reshape(R, B_WIDTH), c_s.reshape(R, C_WIDTH), xs, w_out_b, l, R)
        act_s, zg_s, za_s = _mm_up_sample(x1s, norm2_3, w_up_b, ffn_conv_w, conv_b3, conv_hist, l)
        xs = _mm_down(act_s, x1s, w_down_b, l, R)
        S[0].append(vn_s.reshape(R, 1, A_WIDTH))
        S[1].append(kn_s.reshape(R, 1, B_KV_HEADS, B_HEAD_DIM))
        S[2].append(zs[:, Z_BV:Z_BV + B_KV_WIDTH].reshape(R, 1, B_KV_HEADS, B_HEAD_DIM))
        S[3].append(c_new)
        S[4].append(n_new)
        S[5].append(m_new[:, 0, 0:C_HEADS])
        z_new = jnp.concatenate([zg_s, za_s], axis=-1)
        S[6].append(jnp.stack([state_ffn_conv[l][:, CONV_W - 2], z_new], axis=1))

    st = lambda lst: jnp.stack(lst, axis=0)
    return (xp.reshape(Bp, T, D_MODEL), xs.reshape(R, 1, D_MODEL),
            st(P[0]), st(P[1]), st(S[1]), st(S[2]),
            st(P[2]), st(P[3]), st(P[4]),
            st(S[3]), st(S[4]), st(S[5]),
            st(P[5]), st(S[6]),
            st(S[0]))
```

```python
import functools
import math

import numpy as np
import jax
import jax.numpy as jnp
from jax import lax
from jax.experimental import pallas as pl
from jax.experimental.pallas import tpu as pltpu

f32 = jnp.float32
bf16 = jnp.bfloat16

D_MODEL = 2048
EPS = 1e-6
NEG_INF = -1e30
SQRT_HALF = 0.7071067811865476

A_GROUPS = 4
A_DIM = 128
A_WIDTH = 512
CHUNK = 128
B_HEADS = 16
B_KV_HEADS = 2
B_HEAD_DIM = 64
GQA = 8
B_WIDTH = 1024
B_KV_WIDTH = 128
WINDOW = 128
N_BUCKETS = 32
MAX_DISTANCE = 128
C_HEADS = 4
C_QK_DIM = 64
C_V_DIM = 128
C_QK_WIDTH = 256
C_WIDTH = 512
IN_SPLITS = (A_WIDTH, A_WIDTH, B_WIDTH, B_KV_WIDTH, B_KV_WIDTH,
             C_QK_WIDTH, C_QK_WIDTH, C_WIDTH, C_WIDTH, C_HEADS, C_HEADS)
IN_OFFSETS = tuple(int(o) for o in np.cumsum(IN_SPLITS)[:-1])
D_FF = 5632
CONV_W = 3

Z_WIDTH = 4096
Z_AU, Z_AV, Z_BQ, Z_BK, Z_BV, Z_CQ, Z_CK, Z_CV, Z_CO, Z_GATE = (0,) + IN_OFFSETS[:9]
C_PAIR = 2 * C_V_DIM

MLSTM_L = 128
UP_CHUNK = 256
UP_ROWS = 256
C_SEQS = 2
DEC_ROWS = 8
MM_TILE_N = {(D_MODEL, Z_WIDTH): 1024, (D_MODEL, D_MODEL): 1024, (D_MODEL, 2 * D_FF): 512, (D_FF, D_MODEL): 512}
LANES = 128
SUBLANES = 8
VMEM_LIMIT = 52 * 1024 * 1024


def _cparams(sem, flags=None):
    return pltpu.CompilerParams(dimension_semantics=sem, vmem_limit_bytes=VMEM_LIMIT, flags=flags)


def _gelu(x):
    return 0.5 * x * (1.0 + lax.erf(x * SQRT_HALF))


def _rms(x, gain):
    return x * lax.rsqrt(jnp.mean(x * x, axis=-1, keepdims=True) + EPS) * gain


def _norm_head_pairs(x, gain2):
    lo = lax.broadcasted_iota(jnp.int32, x.shape, 1) < B_HEAD_DIM
    x2 = x * x
    s_lo = jnp.sum(jnp.where(lo, x2, 0.0), axis=-1, keepdims=True)
    s_hi = jnp.sum(jnp.where(lo, 0.0, x2), axis=-1, keepdims=True)
    ms = jnp.where(lo, s_lo, s_hi) * (1.0 / B_HEAD_DIM)
    return x * lax.rsqrt(ms + EPS) * gain2


def _t5_bucket_np(dist):
    n = np.maximum(dist, 0)
    max_exact = N_BUCKETS // 2
    nf = np.maximum(n, 1).astype(np.float32)
    large = max_exact + (np.log(nf / np.float32(max_exact)) / np.float32(math.log(MAX_DISTANCE / max_exact))
                         * np.float32(N_BUCKETS - max_exact)).astype(np.int32)
    return np.where(n < max_exact, n, np.minimum(large, N_BUCKETS - 1)).astype(np.int32)


def _bias_kernel(rb_ref, bkp_ref, bks_ref, op_ref, os_ref):
    bkp = bkp_ref[...]
    bks = bks_ref[...]
    for h in range(B_HEADS):
        accp = jnp.zeros(bkp.shape, f32)
        accs = jnp.zeros(bks.shape, f32)
        for b in range(N_BUCKETS):
            val = rb_ref[b, h]
            accp = jnp.where(bkp == b, val, accp)
            accs = jnp.where(bks == b, val, accs)
        op_ref[h] = accp
        os_ref[h:h + 1, :] = accs[0:1, :]


def _bias_tables(rel_bias):
    qi = np.arange(WINDOW)[:, None]
    cj = np.arange(WINDOW)[None, :]
    bkp = _t5_bucket_np((qi - cj) % WINDOW)
    j = np.arange(2 * WINDOW)
    dist_s = np.where(j < WINDOW, WINDOW - j, 0)
    bks = np.broadcast_to(_t5_bucket_np(dist_s)[None, :], (SUBLANES, 2 * WINDOW)).copy()
    return pl.pallas_call(
        _bias_kernel,
        out_shape=(jax.ShapeDtypeStruct((B_HEADS, WINDOW, WINDOW), f32),
                   jax.ShapeDtypeStruct((B_HEADS, 2 * WINDOW), f32)),
        in_specs=[pl.BlockSpec(memory_space=pltpu.SMEM),
                  pl.BlockSpec(memory_space=pltpu.VMEM),
                  pl.BlockSpec(memory_space=pltpu.VMEM)],
        out_specs=(pl.BlockSpec(memory_space=pltpu.VMEM), pl.BlockSpec(memory_space=pltpu.VMEM)),
        name="bias_tables",
    )(rel_bias, jnp.asarray(bkp), jnp.asarray(bks))


def _mm_in_kernel(x_ref, g_ref, w_ref, z_ref, h_ref):
    @pl.when(pl.program_id(1) == 0)
    def _():
        h_ref[...] = _rms(x_ref[...], g_ref[...]).astype(bf16)

    z_ref[...] = jnp.dot(h_ref[...], w_ref[...], preferred_element_type=f32)


def _mm_in(x, gain, w, layer, tm):
    M = x.shape[0]
    tn = MM_TILE_N[w.shape[-2:]]
    return pl.pallas_call(
        _mm_in_kernel,
        grid=(M // tm, Z_WIDTH // tn),
        in_specs=[pl.BlockSpec((tm, D_MODEL), lambda i, j: (i, 0)),
                  pl.BlockSpec((None, 1, D_MODEL), lambda i, j: (layer, 0, 0)),
                  pl.BlockSpec((None, D_MODEL, tn), lambda i, j: (layer, 0, j))],
        out_specs=pl.BlockSpec((tm, tn), lambda i, j: (i, j)),
        out_shape=jax.ShapeDtypeStruct((M, Z_WIDTH), f32),
        scratch_shapes=[pltpu.VMEM((tm, D_MODEL), bf16)],
        compiler_params=_cparams(("parallel", "arbitrary")),
        name="mm_in",
    )(x, gain, w)


def _mm_out_kernel(a_ref, b_ref, c_ref, x_ref, w_ref, o_ref, lhs_ref):
    @pl.when(pl.program_id(1) == 0)
    def _():
        lhs_ref[:, 0:A_WIDTH] = a_ref[...]
        lhs_ref[:, A_WIDTH:A_WIDTH + B_WIDTH] = b_ref[...]
        lhs_ref[:, A_WIDTH + B_WIDTH:D_MODEL] = c_ref[...]

    o_ref[...] = x_ref[...] + jnp.dot(lhs_ref[...], w_ref[...], preferred_element_type=f32)


def _mm_out(a, b, c, x, w, layer, tm):
    M = x.shape[0]
    tn = MM_TILE_N[w.shape[-2:]]
    return pl.pallas_call(
        _mm_out_kernel,
        grid=(M // tm, D_MODEL // tn),
        in_specs=[pl.BlockSpec((tm, A_WIDTH), lambda i, j: (i, 0)),
                  pl.BlockSpec((tm, B_WIDTH), lambda i, j: (i, 0)),
                  pl.BlockSpec((tm, C_WIDTH), lambda i, j: (i, 0)),
                  pl.BlockSpec((tm, tn), lambda i, j: (i, j)),
                  pl.BlockSpec((None, D_MODEL, tn), lambda i, j: (layer, 0, j))],
        out_specs=pl.BlockSpec((tm, tn), lambda i, j: (i, j)),
        out_shape=jax.ShapeDtypeStruct((M, D_MODEL), f32),
        scratch_shapes=[pltpu.VMEM((tm, D_MODEL), bf16)],
        compiler_params=_cparams(("parallel", "arbitrary")),
        name="mm_out",
    )(a, b, c, x, w)


def _mm_down_kernel(a_ref, x_ref, w_ref, o_ref):
    o_ref[...] = x_ref[...] + jnp.dot(a_ref[...], w_ref[...], preferred_element_type=f32)


def _mm_down(act, x, w, layer, tm):
    M = x.shape[0]
    tn = MM_TILE_N[w.shape[-2:]]
    return pl.pallas_call(
        _mm_down_kernel,
        grid=(M // tm, D_MODEL // tn),
        in_specs=[pl.BlockSpec((tm, D_FF), lambda i, j: (i, 0)),
                  pl.BlockSpec((tm, tn), lambda i, j: (i, j)),
                  pl.BlockSpec((None, D_FF, tn), lambda i, j: (layer, 0, j))],
        out_specs=pl.BlockSpec((tm, tn), lambda i, j: (i, j)),
        out_shape=jax.ShapeDtypeStruct((M, D_MODEL), f32),
        compiler_params=_cparams(("parallel", "arbitrary")),
        name="mm_down",
    )(act, x, w)


def _silu(x):
    return x * jax.nn.sigmoid(x)


def _mm_up_prompt_kernel(x_ref, n2_ref, wg_ref, wa_ref, cwg_ref, cwa_ref, cbg_ref, cba_ref,
                         act_ref, cs_ref, h_ref, carry_ref, zb_ref, *, tm, nj, n_steps, tiles_per_seq):
    s = pl.program_id(0)
    sa = jnp.minimum(s, n_steps - 1)
    ia = sa // nj
    ja = sa % nj
    tf = act_ref.shape[1]

    @pl.when(s == 0)
    def _():
        zb_ref[...] = jnp.zeros(zb_ref.shape, f32)
        carry_ref[...] = jnp.zeros(carry_ref.shape, f32)

    @pl.when(ja == 0)
    def _():
        h_ref[...] = _rms(x_ref[...], n2_ref[...]).astype(bf16)

    chunks = [slice(c0, c0 + UP_CHUNK) for c0 in range(0, tf, UP_CHUNK)]

    rblocks = [(r0, min(UP_ROWS, tm - r0)) for r0 in range(0, tm, UP_ROWS)]

    def conv(idx, cw_ref, cb_ref, cs, r0, nr):
        zz = zb_ref[idx, r0:r0 + SUBLANES + nr, cs]
        z1 = pltpu.roll(zz, 1, 0)[SUBLANES:]
        z2 = pltpu.roll(zz, 2, 0)[SUBLANES:]
        return (cb_ref[:, cs] + z2 * cw_ref[0:1, cs] + z1 * cw_ref[1:2, cs] + zz[SUBLANES:] * cw_ref[2:3, cs])

    for cs in chunks:
        for r0, nr in rblocks:
            g = conv(0, cwg_ref, cbg_ref, cs, r0, nr)
            a = conv(1, cwa_ref, cba_ref, cs, r0, nr)
            act_ref[r0:r0 + nr, cs] = (_silu(g) * a).astype(bf16)

    seq_start = (ia % tiles_per_seq) == 0
    for cs in chunks:
        for r0, nr in rblocks:
            for idx, w_ref in ((0, wg_ref), (1, wa_ref)):
                z = jnp.dot(h_ref[r0:r0 + nr, :], w_ref[:, cs], preferred_element_type=f32)
                if r0 == 0:
                    zb_ref[idx, 0:SUBLANES, cs] = jnp.where(seq_start, 0.0, carry_ref[idx, ja, :, cs])
                zb_ref[idx, SUBLANES + r0:SUBLANES + r0 + nr, cs] = z
                if r0 + nr == tm:
                    tail = z[nr - SUBLANES:nr, :]
                    carry_ref[idx, ja, :, cs] = tail
                    cs_ref[0, idx, :, cs] = tail


def _mm_up_prompt(x, norm2, w_up, conv_w, conv_b, layer, batch, tm):
    M = x.shape[0]
    seq = M // batch
    tiles_per_seq = seq // tm
    tf = MM_TILE_N[w_up.shape[-2:]]
    nj = D_FF // tf
    n_steps = (M // tm) * nj
    kern = functools.partial(_mm_up_prompt_kernel, tm=tm, nj=nj, n_steps=n_steps, tiles_per_seq=tiles_per_seq)
    ia = lambda s: jnp.minimum(s, n_steps - 1) // nj
    ja = lambda s: jnp.minimum(s, n_steps - 1) % nj
    ib = lambda s: jnp.maximum(s - 1, 0) // nj
    jb = lambda s: jnp.maximum(s - 1, 0) % nj
    return pl.pallas_call(
        kern,
        grid=(n_steps + 1,),
        in_specs=[pl.BlockSpec((tm, D_MODEL), lambda s: (ia(s), 0)),
                  pl.BlockSpec((None, 1, D_MODEL), lambda s: (layer, 0, 0)),
                  pl.BlockSpec((None, D_MODEL, tf), lambda s: (layer, 0, ja(s))),
                  pl.BlockSpec((None, D_MODEL, tf), lambda s: (layer, 0, nj + ja(s))),
                  pl.BlockSpec((None, CONV_W, tf), lambda s: (layer, 0, jb(s))),
                  pl.BlockSpec((None, CONV_W, tf), lambda s: (layer, 0, nj + jb(s))),
                  pl.BlockSpec((None, 1, tf), lambda s: (layer, 0, jb(s))),
                  pl.BlockSpec((None, 1, tf), lambda s: (layer, 0, nj + jb(s)))],
        out_specs=(pl.BlockSpec((tm, tf), lambda s: (ib(s), jb(s))),
                   pl.BlockSpec((1, 2, SUBLANES, tf), lambda s: (ia(s), 0, 0, ja(s)))),
        out_shape=(jax.ShapeDtypeStruct((M, D_FF), bf16),
                   jax.ShapeDtypeStruct((M // tm, 2, SUBLANES, D_FF), f32)),
        scratch_shapes=[pltpu.VMEM((tm, D_MODEL), bf16),
                        pltpu.VMEM((2, nj, SUBLANES, tf), f32),
                        pltpu.VMEM((2, tm + SUBLANES, tf), f32)],
        compiler_params=_cparams(("arbitrary",)),
        name="mm_up_prompt",
    )(x, norm2, w_up, w_up, conv_w, conv_w, conv_b, conv_b)


def _mm_up_sample_kernel(x_ref, n2_ref, wg_ref, wa_ref, cwg_ref, cwa_ref, cbg_ref, cba_ref,
                         b0g_ref, b1g_ref, b0a_ref, b1a_ref, act_ref, zg_ref, za_ref, h_ref):
    @pl.when(pl.program_id(0) == 0)
    def _():
        h_ref[...] = _rms(x_ref[...], n2_ref[...]).astype(bf16)

    def conv_half(w_ref, cw_ref, cb_ref, b0_ref, b1_ref, z_ref):
        z = jnp.dot(h_ref[...], w_ref[...], preferred_element_type=f32)
        z_ref[...] = z
        return (cb_ref[...] + b0_ref[...] * cw_ref[0:1, :] + b1_ref[...] * cw_ref[1:2, :]
                + z * cw_ref[2:3, :])

    g = conv_half(wg_ref, cwg_ref, cbg_ref, b0g_ref, b1g_ref, zg_ref)
    a = conv_half(wa_ref, cwa_ref, cba_ref, b0a_ref, b1a_ref, za_ref)
    act_ref[...] = (_silu(g) * a).astype(bf16)


def _mm_up_sample(x, norm2, w_up, conv_w, conv_b, buf, layer):
    M = x.shape[0]
    tf = MM_TILE_N[w_up.shape[-2:]]
    nj = D_FF // tf
    wspec = lambda off: pl.BlockSpec((None, D_MODEL, tf), lambda j: (layer, 0, off + j))
    cwspec = lambda off: pl.BlockSpec((None, CONV_W, tf), lambda j: (layer, 0, off + j))
    cbspec = lambda off: pl.BlockSpec((None, 1, tf), lambda j: (layer, 0, off + j))
    bufspec = lambda row, off: pl.BlockSpec((None, None, M, tf), lambda j: (layer, row, 0, off + j))
    return pl.pallas_call(
        _mm_up_sample_kernel,
        grid=(nj,),
        in_specs=[pl.BlockSpec((M, D_MODEL), lambda j: (0, 0)),
                  pl.BlockSpec((None, 1, D_MODEL), lambda j: (layer, 0, 0)),
                  wspec(0), wspec(nj), cwspec(0), cwspec(nj), cbspec(0), cbspec(nj),
                  bufspec(0, 0), bufspec(1, 0), bufspec(0, nj), bufspec(1, nj)],
        out_specs=(pl.BlockSpec((M, tf), lambda j: (0, j)),
                   pl.BlockSpec((M, tf), lambda j: (0, j)),
                   pl.BlockSpec((M, tf), lambda j: (0, j))),
        out_shape=(jax.ShapeDtypeStruct((M, D_FF), bf16),
                   jax.ShapeDtypeStruct((M, D_FF), f32),
                   jax.ShapeDtypeStruct((M, D_FF), f32)),
        scratch_shapes=[pltpu.VMEM((M, D_MODEL), bf16)],
        compiler_params=_cparams(("arbitrary",)),
        name="mm_up_sample",
    )(x, norm2, w_up, w_up, conv_w, conv_w, conv_b, conv_b, buf, buf, buf, buf)


def _mix_a_prompt_kernel(u_ref, v_ref, vg_ref, ws_ref, bs_ref, o_ref):
    row = lax.broadcasted_iota(jnp.int32, (CHUNK, CHUNK), 0)
    col = lax.broadcasted_iota(jnp.int32, (CHUNK, CHUNK), 1)
    tri = col <= row
    for g in range(A_GROUPS):
        sl = slice(g * A_DIM, (g + 1) * A_DIM)
        ws = jnp.where(tri, ws_ref[g], 0.0).astype(bf16)
        for r0 in range(0, u_ref.shape[0], CHUNK):
            rs = slice(r0, r0 + CHUNK)
            vn = _rms(_gelu(v_ref[rs, sl]), vg_ref[:, sl])
            mixv = jnp.dot(ws, vn.astype(bf16), preferred_element_type=f32) + bs_ref[:, sl]
            o_ref[rs, sl] = (_gelu(u_ref[rs, sl]) * mixv).astype(bf16)


def _mix_a_prompt(z, v_gain, w_s, b_s_full, layer, rows):
    M = z.shape[0]
    return pl.pallas_call(
        _mix_a_prompt_kernel,
        grid=(M // rows,),
        in_specs=[pl.BlockSpec((rows, A_WIDTH), lambda r: (r, Z_AU // A_WIDTH)),
                  pl.BlockSpec((rows, A_WIDTH), lambda r: (r, Z_AV // A_WIDTH)),
                  pl.BlockSpec((None, 1, A_WIDTH), lambda r: (layer, 0, 0)),
                  pl.BlockSpec((None, A_GROUPS, CHUNK, CHUNK), lambda r: (layer, 0, 0, 0)),
                  pl.BlockSpec((None, CHUNK, A_WIDTH), lambda r: (layer, 0, 0))],
        out_specs=pl.BlockSpec((rows, A_WIDTH), lambda r: (r, 0)),
        out_shape=jax.ShapeDtypeStruct((M, A_WIDTH), bf16),
        compiler_params=_cparams(("parallel",)),
        name="mix_a_prompt",
    )(z, z, v_gain, w_s, b_s_full)


def _mix_a_sample_kernel(u_ref, v_ref, vg_ref, ws_ref, bs_ref, o_ref, vn_ref):
    for g in range(A_GROUPS):
        sl = slice(g * A_DIM, (g + 1) * A_DIM)
        vn = _rms(_gelu(v_ref[:, sl]), vg_ref[:, sl])
        vn_ref[:, sl] = vn
        mixv = ws_ref[:, sl] * vn + bs_ref[:, sl]
        o_ref[:, sl] = (_gelu(u_ref[:, sl]) * mixv).astype(bf16)


def _mix_a_sample(z, v_gain, ws0, bs0, layer):
    M = z.shape[0]
    vec = pl.BlockSpec((None, 1, A_WIDTH), lambda r: (layer, 0, 0))
    return pl.pallas_call(
        _mix_a_sample_kernel,
        grid=(1,),
        in_specs=[pl.BlockSpec((M, A_WIDTH), lambda r: (0, Z_AU // A_WIDTH)),
                  pl.BlockSpec((M, A_WIDTH), lambda r: (0, Z_AV // A_WIDTH)),
                  vec, vec, vec],
        out_specs=(pl.BlockSpec((M, A_WIDTH), lambda r: (0, 0)),
                   pl.BlockSpec((M, A_WIDTH), lambda r: (0, 0))),
        out_shape=(jax.ShapeDtypeStruct((M, A_WIDTH), bf16),
                   jax.ShapeDtypeStruct((M, A_WIDTH), f32)),
        compiler_params=_cparams(("arbitrary",)),
        name="mix_a_sample",
    )(z, z, v_gain, ws0, bs0)


def _mix_b_prompt_kernel(sink_ref, q_ref, k_ref, v_ref, qg_ref, kg_ref, bias_ref,
                         o_ref, klast_ref, vlast_ref, kcat_ref, vcat_ref):
    i = pl.program_id(1)
    W = WINDOW
    kn = _norm_head_pairs(k_ref[...], kg_ref[...])
    v = v_ref[...]
    klast_ref[0] = kn
    vlast_ref[0] = v

    @pl.when(i == 0)
    def _():
        kcat_ref[0:W, :] = jnp.zeros((W, LANES), f32)
        vcat_ref[0:W, :] = jnp.zeros((W, LANES), f32)

    @pl.when(i > 0)
    def _():
        kcat_ref[0:W, :] = kcat_ref[W:2 * W, :]
        vcat_ref[0:W, :] = vcat_ref[W:2 * W, :]

    kcat_ref[W:2 * W, :] = kn
    vcat_ref[W:2 * W, :] = v
    kc = kcat_ref[...]
    vc = vcat_ref[...]
    kc_sw = pltpu.roll(kc, B_HEAD_DIM, 1)
    vc_sw = pltpu.roll(vc, B_HEAD_DIM, 1)
    lo = lax.broadcasted_iota(jnp.int32, (2 * W, LANES), 1) < B_HEAD_DIM

    qi = lax.broadcasted_iota(jnp.int32, (W, W), 0)
    cj = lax.broadcasted_iota(jnp.int32, (W, W), 1)
    own = cj <= qi
    keep = own | (i > 0)
    q_gain = qg_ref[...] * (B_HEAD_DIM ** -0.5)

    tiles = GQA // 2
    dn = (((1,), (1,)), ((), ()))
    scores, values = [], []
    for kvh in range(B_KV_HEADS):
        k_src, k_alt = (kc, kc_sw) if kvh == 0 else (kc_sw, kc)
        v_src, v_alt = (vc, vc_sw) if kvh == 0 else (vc_sw, vc)
        k_even = jnp.where(lo, k_src, 0.0).astype(bf16)
        k_odd = jnp.where(lo, 0.0, k_alt).astype(bf16)
        values.append((jnp.where(lo, v_src, 0.0).astype(bf16), jnp.where(lo, 0.0, v_alt).astype(bf16)))
        qs = jnp.concatenate(
            [_norm_head_pairs(q_ref[:, (kvh * tiles + t) * LANES:(kvh * tiles + t + 1) * LANES], q_gain)
             for t in range(tiles)], axis=0).astype(bf16)
        scores.append((lax.dot_general(qs, k_even, dn, preferred_element_type=f32),
                       lax.dot_general(qs, k_odd, dn, preferred_element_type=f32)))
    for kvh in range(B_KV_HEADS):
        p_par = []
        for par in range(2):
            blocks = []
            for t in range(tiles):
                h = kvh * GQA + 2 * t + par
                sb = scores[kvh][par][t * W:(t + 1) * W, :]
                s = jnp.where(own, sb[:, W:2 * W], sb[:, 0:W]) + bias_ref[h]
                s = jnp.where(keep, s, NEG_INF)
                sk = sink_ref[h]
                mx = jnp.maximum(jnp.max(s, axis=-1, keepdims=True), sk)
                p = jnp.exp(s - mx)
                den = jnp.sum(p, axis=-1, keepdims=True) + jnp.exp(sk - mx)
                p = p * (1.0 / den)
                blocks.append(jnp.concatenate([jnp.where(own, 0.0, p), jnp.where(own, p, 0.0)],
                                              axis=1).astype(bf16))
            p_par.append(jnp.concatenate(blocks, axis=0))
        o = (jnp.dot(p_par[0], values[kvh][0], preferred_element_type=f32)
             + jnp.dot(p_par[1], values[kvh][1], preferred_element_type=f32))
        for t in range(tiles):
            c0 = (kvh * tiles + t) * LANES
            o_ref[:, c0:c0 + LANES] = o[t * W:(t + 1) * W, :].astype(bf16)


def _mix_b_prompt(z, sinks, q_gain2, k_gain2, bias_p, layer, batch):
    M = z.shape[0]
    nb = M // batch // WINDOW
    return pl.pallas_call(
        _mix_b_prompt_kernel,
        grid=(batch, nb),
        in_specs=[pl.BlockSpec(memory_space=pltpu.SMEM),
                  pl.BlockSpec((WINDOW, B_WIDTH), lambda b, i: (b * nb + i, Z_BQ // B_WIDTH)),
                  pl.BlockSpec((WINDOW, B_KV_WIDTH), lambda b, i: (b * nb + i, Z_BK // B_KV_WIDTH)),
                  pl.BlockSpec((WINDOW, B_KV_WIDTH), lambda b, i: (b * nb + i, Z_BV // B_KV_WIDTH)),
                  pl.BlockSpec((None, 1, LANES), lambda b, i: (layer, 0, 0)),
                  pl.BlockSpec((None, 1, LANES), lambda b, i: (layer, 0, 0)),
                  pl.BlockSpec((B_HEADS, WINDOW, WINDOW), lambda b, i: (0, 0, 0))],
        out_specs=(pl.BlockSpec((WINDOW, B_WIDTH), lambda b, i: (b * nb + i, 0)),
                   pl.BlockSpec((1, WINDOW, B_KV_WIDTH), lambda b, i: (b, 0, 0)),
                   pl.BlockSpec((1, WINDOW, B_KV_WIDTH), lambda b, i: (b, 0, 0))),
        out_shape=(jax.ShapeDtypeStruct((M, B_WIDTH), bf16),
                   jax.ShapeDtypeStruct((batch, WINDOW, B_KV_WIDTH), f32),
                   jax.ShapeDtypeStruct((batch, WINDOW, B_KV_WIDTH), f32)),
        scratch_shapes=[pltpu.VMEM((2 * WINDOW, LANES), f32),
                        pltpu.VMEM((2 * WINDOW, LANES), f32)],
        compiler_params=_cparams(("arbitrary", "arbitrary")),
        name="mix_b_prompt",
    )(sinks, z, z, z, q_gain2, k_gain2, bias_p)


def _mix_c_prompt_kernel(q_ref, k_ref, v0_ref, v1_ref, o0_ref, o1_ref, g_ref, gb_ref, hg_ref,
                         out_ref, c_out, n_out, m_out, c_s, n_s, m_s):
    L = MLSTM_L
    c = pl.program_id(1)

    @pl.when(c == 0)
    def _():
        c_s[...] = jnp.zeros(c_s.shape, f32)
        n_s[...] = jnp.zeros(n_s.shape, f32)
        m_s[...] = jnp.zeros(m_s.shape, f32)

    row = lax.broadcasted_iota(jnp.int32, (L, L), 0)
    col = lax.broadcasted_iota(jnp.int32, (L, L), 1)
    tri = col <= row
    tri_f = tri.astype(f32)
    seqs = range(q_ref.shape[0])
    units = [(bi, h) for bi in seqs for h in range(C_HEADS)]
    g, b = {}, {}
    for bi in seqs:
        g[bi] = g_ref[bi] + gb_ref[...]
        b[bi] = jnp.dot(tri_f, jax.nn.log_sigmoid(g[bi]), precision=lax.Precision.HIGHEST,
                        preferred_element_type=f32)
    bT = {bi: b[bi].T for bi in seqs}
    gT = {bi: g[bi].T for bi in seqs}
    qh, kh, qb, vb, c_prev, n_prev, qk, qc = {}, {}, {}, {}, {}, {}, {}, {}
    for u in units:
        bi, h = u
        v_ref = v0_ref if h < 2 else v1_ref
        vsl = slice((h % 2) * C_V_DIM, (h % 2 + 1) * C_V_DIM)
        qh[u] = q_ref[bi, :, h * C_QK_DIM:(h + 1) * C_QK_DIM]
        kh[u] = k_ref[bi, :, h * C_QK_DIM:(h + 1) * C_QK_DIM] * (C_QK_DIM ** -0.5)
        qb[u] = qh[u].astype(bf16)
        vb[u] = v_ref[bi, :, vsl].astype(bf16)
        c_prev[u] = c_s[bi, h]
        n_prev[u] = n_s[bi, h]
        qk[u] = lax.dot_general(qb[u], kh[u].astype(bf16), (((1,), (1,)), ((), ())), preferred_element_type=f32)
        qc[u] = jnp.dot(qb[u], c_prev[u].astype(bf16), preferred_element_type=f32)
    s, iw, mt, kw, decay, m_new = {}, {}, {}, {}, {}, {}
    for u in units:
        bi, h = u
        b_col = b[bi][:, C_HEADS + h:C_HEADS + h + 1]
        i_col = g[bi][:, h:h + 1]
        b_row = bT[bi][C_HEADS + h:C_HEADS + h + 1, :]
        i_row = gT[bi][h:h + 1, :]
        dlog = jnp.where(tri, b_col - b_row + i_row, -jnp.inf)
        m_prev = m_s[bi, h][0:1, 0:1]
        inter = b_col + m_prev
        mt[u] = jnp.maximum(inter, jnp.max(dlog, axis=-1, keepdims=True))
        s[u] = qk[u] * jnp.exp(dlog - mt[u])
        iw[u] = jnp.exp(inter - mt[u])
        m_new[u] = mt[u][L - 1:L, :]
        b_last = b_col[L - 1:L, :]
        wc = jnp.exp(b_last - b_col + i_col - m_new[u])
        decay[u] = jnp.exp(b_last + m_prev - m_new[u])
        kw[u] = kh[u] * wc
    sv = {u: jnp.dot(s[u].astype(bf16), vb[u], preferred_element_type=f32) for u in units}
    kv = {u: lax.dot_general(kw[u].astype(bf16), vb[u], (((0,), (0,)), ((), ())), preferred_element_type=f32)
          for u in units}
    for u in units:
        bi, h = u
        o_ref = o0_ref if h < 2 else o1_ref
        vsl = slice((h % 2) * C_V_DIM, (h % 2 + 1) * C_V_DIM)
        num = iw[u] * qc[u] + sv[u]
        qn = iw[u] * jnp.sum(qh[u] * n_prev[u], axis=-1, keepdims=True) + jnp.sum(s[u], axis=-1, keepdims=True)
        hh = num / jnp.maximum(jnp.abs(qn), jnp.exp(-mt[u]))
        c_s[bi, h] = decay[u] * c_prev[u] + kv[u]
        n_s[bi, h] = decay[u] * n_prev[u] + jnp.sum(kw[u], axis=0, keepdims=True)
        m_s[bi, h] = jnp.broadcast_to(m_new[u], (SUBLANES, LANES))
        osl = slice(h * C_V_DIM, (h + 1) * C_V_DIM)
        out_ref[bi, :, osl] = (_rms(hh, hg_ref[:, osl]) * jax.nn.sigmoid(o_ref[bi, :, vsl])).astype(bf16)

    @pl.when(c == pl.num_programs(1) - 1)
    def _():
        c_out[...] = c_s[...]
        n_out[...] = n_s[...]
        m_out[...] = m_s[...]


def _mix_c_prompt(z, gate_bias, h_gain, layer, batch):
    M = z.shape[0]
    L = MLSTM_L
    T = M // batch
    z3 = z.reshape(batch, T, Z_WIDTH)
    ns = C_SEQS if batch % C_SEQS == 0 else 1
    col = lambda width, off: pl.BlockSpec((ns, L, width), lambda b, c: (b, c, off // width))
    state = lambda *shape: pl.BlockSpec((ns,) + shape, lambda b, c: (b,) + (0,) * len(shape))
    out, c_st, n_st, m_st = pl.pallas_call(
        _mix_c_prompt_kernel,
        grid=(batch // ns, T // L),
        in_specs=[col(C_QK_WIDTH, Z_CQ), col(C_QK_WIDTH, Z_CK),
                  col(C_PAIR, Z_CV), col(C_PAIR, Z_CV + C_PAIR),
                  col(C_PAIR, Z_CO), col(C_PAIR, Z_CO + C_PAIR),
                  col(LANES, Z_GATE),
                  pl.BlockSpec((None, 1, LANES), lambda b, c: (layer, 0, 0)),
                  pl.BlockSpec((None, 1, C_WIDTH), lambda b, c: (layer, 0, 0))],
        out_specs=(pl.BlockSpec((ns, L, C_WIDTH), lambda b, c: (b, c, 0)),
                   state(C_HEADS, C_QK_DIM, C_V_DIM),
                   state(C_HEADS, 1, C_QK_DIM),
                   state(C_HEADS, SUBLANES, LANES)),
        out_shape=(jax.ShapeDtypeStruct((batch, T, C_WIDTH), bf16),
                   jax.ShapeDtypeStruct((batch, C_HEADS, C_QK_DIM, C_V_DIM), f32),
                   jax.ShapeDtypeStruct((batch, C_HEADS, 1, C_QK_DIM), f32),
                   jax.ShapeDtypeStruct((batch, C_HEADS, SUBLANES, LANES), f32)),
        scratch_shapes=[pltpu.VMEM((ns, C_HEADS, C_QK_DIM, C_V_DIM), f32),
                        pltpu.VMEM((ns, C_HEADS, 1, C_QK_DIM), f32),
                        pltpu.VMEM((ns, C_HEADS, SUBLANES, LANES), f32)],
        compiler_params=_cparams(("arbitrary", "arbitrary")),
        name="mix_c_prompt",
    )(z3, z3, z3, z3, z3, z3, z3, gate_bias, h_gain)
    return out.reshape(M, C_WIDTH), c_st, n_st, m_st


_BC_SHARED = (4, 5, 6, 7, 12, 13)


def _mix_bc_sample_kernel(*refs):
    rows = [_mix_bc_sample_row(*[ref if k in _BC_SHARED else ref.at[pl.ds(r, 1)] for k, ref in enumerate(refs)])
            for r in range(refs[0].shape[0])]
    while rows:
        rows = [row for row in rows if next(row, True) is None]


def _mix_bc_sample_row(zr_ref, q_ref, kp_ref, vp_ref, qg_ref, kg_ref, sink_ref, bias_ref,
                       cq_ref, ck_ref, cv_ref, co_ref, gb_ref, hg_ref, c0_ref, n0_ref, m0_ref,
                       bo_ref, kn_ref, co_out, c_out, n_out, m_out):
    W = WINDOW
    zr = zr_ref[0]

    qn = _rms(q_ref[0], qg_ref[...])
    hrow = lax.broadcasted_iota(jnp.int32, (B_HEADS, LANES), 0)
    lane = lax.broadcasted_iota(jnp.int32, (B_HEADS, LANES), 1)
    own = (hrow < GQA) == (lane < B_HEAD_DIM)
    q2 = jnp.where(own, jnp.concatenate([qn, qn], axis=-1), 0.0)
    k_new = _norm_head_pairs(zr[:, Z_BK:Z_BK + B_KV_WIDTH], kg_ref[...])
    v_new = zr[:, Z_BV:Z_BV + B_KV_WIDTH]
    kn_ref[0] = k_new
    scale = B_HEAD_DIM ** -0.5
    s_past = lax.dot_general(q2.astype(bf16), kp_ref[0].astype(bf16), (((1,), (1,)), ((), ())),
                             preferred_element_type=f32) * scale + bias_ref[:, 0:W]
    yield
    s_past = jnp.where(lane >= 1, s_past, NEG_INF)
    s_new = jnp.sum(q2 * k_new, axis=-1, keepdims=True) * scale + bias_ref[:, W:W + 1]
    sk = sink_ref[...]
    mx = jnp.maximum(jnp.maximum(jnp.max(s_past, axis=-1, keepdims=True), s_new), sk)
    p_past = jnp.exp(s_past - mx)
    p_new = jnp.exp(s_new - mx)
    den = jnp.sum(p_past, axis=-1, keepdims=True) + p_new + jnp.exp(sk - mx)
    inv = 1.0 / den
    o2 = (jnp.dot((p_past * inv).astype(bf16), vp_ref[0].astype(bf16), preferred_element_type=f32)
          + (p_new * inv) * v_new)
    o2_sw = pltpu.roll(o2, B_HEAD_DIM, 1)
    bo_ref[0] = jnp.where(hrow < GQA, o2, o2_sw)[:, 0:B_HEAD_DIM].astype(bf16)

    g = zr[:, Z_GATE:Z_GATE + LANES] + gb_ref[...]
    lf = jax.nn.log_sigmoid(g)
    m0 = m0_ref[0]
    lane1 = lax.broadcasted_iota(jnp.int32, (1, LANES), 1)
    m_row = jnp.zeros((1, LANES), f32)
    row8 = lax.broadcasted_iota(jnp.int32, (SUBLANES, C_QK_DIM), 0)
    c_rows = []
    staged = []
    for h in range(C_HEADS):
        ig = g[:, h:h + 1]
        b = lf[:, C_HEADS + h:C_HEADS + h + 1]
        m_prev = m0[:, h:h + 1]
        inter = b + m_prev
        mt = jnp.maximum(inter, ig)
        qh = cq_ref[0, h:h + 1, :]
        kh = ck_ref[0, h:h + 1, :] * (C_QK_DIM ** -0.5)
        vh = cv_ref[0, h:h + 1, :]
        c_prev = c0_ref[0, h]
        n_prev = n0_ref[0, h:h + 1, :]
        s = jnp.sum(qh * kh, axis=-1, keepdims=True) * jnp.exp(ig - mt)
        iw = jnp.exp(inter - mt)
        q8 = jnp.broadcast_to(qh, (SUBLANES, C_QK_DIM)).astype(bf16)
        qc = jnp.dot(q8, c_prev.astype(bf16), preferred_element_type=f32)[0:1, :]
        staged.append((ig, inter, mt, qh, kh, vh, c_prev, n_prev, s, iw, qc))
    yield
    for h, (ig, inter, mt, qh, kh, vh, c_prev, n_prev, s, iw, qc) in enumerate(staged):
        num = iw * qc + s * vh
        qn_ = iw * jnp.sum(qh * n_prev, axis=-1, keepdims=True) + s
        hh = num / jnp.maximum(jnp.abs(qn_), jnp.exp(-mt))
        wc = jnp.exp(ig - mt)
        decay = jnp.exp(inter - mt)
        kw = kh * wc
        kw8 = jnp.where(row8 == 0, jnp.broadcast_to(kw, (SUBLANES, C_QK_DIM)), 0.0).astype(bf16)
        v8 = jnp.broadcast_to(vh, (SUBLANES, C_V_DIM)).astype(bf16)
        c_out[0, h] = decay * c_prev + lax.dot_general(kw8, v8, (((0,), (0,)), ((), ())),
                                                      preferred_element_type=f32)
        n_out[0, h:h + 1, :] = decay * n_prev + kw
        m_row = jnp.where(lane1 == h, mt, m_row)
        c_rows.append(_rms(hh, hg_ref[h:h + 1, :]) * jax.nn.sigmoid(co_ref[0, h:h + 1, :]))
    co_out[0] = jnp.concatenate(c_rows, axis=0).astype(bf16)
    m_out[0] = m_row


def _mix_bc_sample(z, k_past, v_past, q_gain, k_gain2, sinks, bias_s, gate_bias, h_gain4,
                   c0, n0, m0, layer):
    R = z.shape[0]
    zr = z.reshape(R, 1, Z_WIDTH)
    q = z[:, Z_BQ:Z_BQ + B_WIDTH].reshape(R, B_HEADS, B_HEAD_DIM)
    cq = z[:, Z_CQ:Z_CQ + C_QK_WIDTH].reshape(R, C_HEADS, C_QK_DIM)
    ck = z[:, Z_CK:Z_CK + C_QK_WIDTH].reshape(R, C_HEADS, C_QK_DIM)
    cv = z[:, Z_CV:Z_CV + C_WIDTH].reshape(R, C_HEADS, C_V_DIM)
    co = z[:, Z_CO:Z_CO + C_WIDTH].reshape(R, C_HEADS, C_V_DIM)
    rb = DEC_ROWS if R % DEC_ROWS == 0 else 1
    row3 = lambda n: pl.BlockSpec((rb, 1, n), lambda r: (r, 0, 0))
    return pl.pallas_call(
        _mix_bc_sample_kernel,
        grid=(R // rb,),
        in_specs=[row3(Z_WIDTH),
                  pl.BlockSpec((rb, B_HEADS, B_HEAD_DIM), lambda r: (r, 0, 0)),
                  pl.BlockSpec((None, rb, WINDOW, B_KV_WIDTH), lambda r: (layer, r, 0, 0)),
                  pl.BlockSpec((None, rb, WINDOW, B_KV_WIDTH), lambda r: (layer, r, 0, 0)),
                  pl.BlockSpec((None, 1, B_HEAD_DIM), lambda r: (layer, 0, 0)),
                  pl.BlockSpec((None, 1, LANES), lambda r: (layer, 0, 0)),
                  pl.BlockSpec((None, B_HEADS, 1), lambda r: (layer, 0, 0)),
                  pl.BlockSpec((B_HEADS, 2 * WINDOW), lambda r: (0, 0)),
                  pl.BlockSpec((rb, C_HEADS, C_QK_DIM), lambda r: (r, 0, 0)),
                  pl.BlockSpec((rb, C_HEADS, C_QK_DIM), lambda r: (r, 0, 0)),
                  pl.BlockSpec((rb, C_HEADS, C_V_DIM), lambda r: (r, 0, 0)),
                  pl.BlockSpec((rb, C_HEADS, C_V_DIM), lambda r: (r, 0, 0)),
                  pl.BlockSpec((None, 1, LANES), lambda r: (layer, 0, 0)),
                  pl.BlockSpec((None, C_HEADS, C_V_DIM), lambda r: (layer, 0, 0)),
                  pl.BlockSpec((None, rb, C_HEADS, C_QK_DIM, C_V_DIM), lambda r: (layer, r, 0, 0, 0)),
                  pl.BlockSpec((None, rb, C_HEADS, C_QK_DIM), lambda r: (layer, r, 0, 0)),
                  pl.BlockSpec((None, rb, 1, C_HEADS), lambda r: (layer, r, 0, 0))],
        out_specs=(pl.BlockSpec((rb, B_HEADS, B_HEAD_DIM), lambda r: (r, 0, 0)),
                   row3(B_KV_WIDTH),
                   pl.BlockSpec((rb, C_HEADS, C_V_DIM), lambda r: (r, 0, 0)),
                   pl.BlockSpec((rb, C_HEADS, C_QK_DIM, C_V_DIM), lambda r: (r, 0, 0, 0)),
                   pl.BlockSpec((rb, C_HEADS, C_QK_DIM), lambda r: (r, 0, 0)),
                   row3(LANES)),
        out_shape=(jax.ShapeDtypeStruct((R, B_HEADS, B_HEAD_DIM), bf16),
                   jax.ShapeDtypeStruct((R, 1, B_KV_WIDTH), f32),
                   jax.ShapeDtypeStruct((R, C_HEADS, C_V_DIM), bf16),
                   jax.ShapeDtypeStruct((R, C_HEADS, C_QK_DIM, C_V_DIM), f32),
                   jax.ShapeDtypeStruct((R, C_HEADS, C_QK_DIM), f32),
                   jax.ShapeDtypeStruct((R, 1, LANES), f32)),
        compiler_params=_cparams(("parallel",)),
        name="mix_bc_sample",
    )(zr, q, k_past, v_past, q_gain, k_gain2, sinks, bias_s, cq, ck, cv, co, gate_bias, h_gain4,
      c0, n0, m0)


def _prep_w_in(w_in):
    return jnp.pad(w_in.astype(bf16), ((0, 0), (0, 0), (0, Z_WIDTH - w_in.shape[-1])))


def _row_tile(m, pref):
    return pref if m % pref == 0 else m


def kernel(x_prompt, x_sample, cache_swa_k, cache_swa_v, state_mlstm_C, state_mlstm_n, state_mlstm_m, state_ffn_conv, rel_bias, norm1, w_in, a_v_gain, a_spatial_w, a_spatial_b, b_q_gain, b_k_gain, b_sinks, c_gate_bias, c_h_gain, w_out, norm2, w_up, ffn_conv_w, ffn_conv_b, w_down):
    depth = w_in.shape[0]
    Bp, T, _ = x_prompt.shape
    R = x_sample.shape[0]
    assert x_sample.shape[1] == 1 and T % CHUNK == 0 and T % MLSTM_L == 0

    w_in_p = _prep_w_in(w_in)
    w_out_b = w_out.astype(bf16)
    w_up_b = w_up.astype(bf16)
    w_down_b = w_down.astype(bf16)
    norm1_3 = norm1.reshape(depth, 1, D_MODEL)
    norm2_3 = norm2.reshape(depth, 1, D_MODEL)
    v_gain3 = a_v_gain.reshape(depth, 1, A_WIDTH)
    bs_full = jnp.repeat(jnp.swapaxes(a_spatial_b, 1, 2), A_DIM, axis=-1)
    ws0 = jnp.repeat(a_spatial_w[:, :, 0, 0], A_DIM, axis=-1).reshape(depth, 1, A_WIDTH)
    bs0 = jnp.repeat(a_spatial_b[:, :, 0], A_DIM, axis=-1).reshape(depth, 1, A_WIDTH)
    q_gain3 = b_q_gain.reshape(depth, 1, B_HEAD_DIM)
    q_gain2 = jnp.tile(b_q_gain, (1, 2)).reshape(depth, 1, LANES)
    k_gain2 = jnp.tile(b_k_gain, (1, 2)).reshape(depth, 1, LANES)
    sinks3 = b_sinks.reshape(depth, B_HEADS, 1)
    gate_b = jnp.pad(c_gate_bias, ((0, 0), (0, LANES - 2 * C_HEADS))).reshape(depth, 1, LANES)
    h_gain3 = c_h_gain.reshape(depth, 1, C_WIDTH)
    h_gain4 = c_h_gain.reshape(depth, C_HEADS, C_V_DIM)
    conv_b3 = ffn_conv_b.reshape(depth, 1, 2 * D_FF)
    k_cache = cache_swa_k.reshape(depth, R, WINDOW, B_KV_WIDTH)
    v_cache = cache_swa_v.reshape(depth, R, WINDOW, B_KV_WIDTH)
    m_state = state_mlstm_m.reshape(depth, R, 1, C_HEADS)
    conv_hist = jnp.swapaxes(state_ffn_conv, 1, 2)

    bias_p, bias_s = _bias_tables(rel_bias)

    Mp = Bp * T
    tm = _row_tile(T, 1024)
    a_rows = _row_tile(T, 512)
    xp = x_prompt.reshape(Mp, D_MODEL)
    xs = x_sample.reshape(R, D_MODEL)
    P = [[] for _ in range(6)]
    S = [[] for _ in range(7)]
    for l in range(depth):
        z = _mm_in(xp, norm1_3, w_in_p, l, tm)
        a_o = _mix_a_prompt(z, v_gain3, a_spatial_w, bs_full, l, a_rows)
        b_o, k_last, v_last = _mix_b_prompt(z, b_sinks[l], q_gain2, k_gain2, bias_p, l, Bp)
        c_o, c_st, n_st, m_st = _mix_c_prompt(z, gate_b, h_gain3, l, Bp)
        x1 = _mm_out(a_o, b_o, c_o, xp, w_out_b, l, tm)
        act, cs = _mm_up_prompt(x1, norm2_3, w_up_b, ffn_conv_w, conv_b3, l, Bp, tm)
        xp = _mm_down(act, x1, w_down_b, l, tm)
        P[0].append(k_last.reshape(Bp, WINDOW, B_KV_HEADS, B_HEAD_DIM))
        P[1].append(v_last.reshape(Bp, WINDOW, B_KV_HEADS, B_HEAD_DIM))
        P[2].append(c_st)
        P[3].append(n_st.reshape(Bp, C_HEADS, C_QK_DIM))
        P[4].append(m_st[:, :, 0, 0])
        seq_tiles = T // tm
        tail = cs[seq_tiles - 1::seq_tiles, :, SUBLANES - (CONV_W - 1):, :]
        P[5].append(jnp.swapaxes(tail, 1, 2).reshape(Bp, CONV_W - 1, 2 * D_FF))

        zs = _mm_in(xs, norm1_3, w_in_p, l, R)
        a_s, vn_s = _mix_a_sample(zs, v_gain3, ws0, bs0, l)
        b_s, kn_s, c_s, c_new, n_new, m_new = _mix_bc_sample(
            zs, k_cache, v_cache, q_gain3, k_gain2, sinks3, bias_s, gate_b, h_gain4,
            state_mlstm_C, state_mlstm_n, m_state, l)
        x1s = _mm_out(a_s, b_s.reshape(R, B_WIDTH), c_s.reshape(R, C_WIDTH), xs, w_out_b, l, R)
        act_s, zg_s, za_s = _mm_up_sample(x1s, norm2_3, w_up_b, ffn_conv_w, conv_b3, conv_hist, l)
        xs = _mm_down(act_s, x1s, w_down_b, l, R)
        S[0].append(vn_s.reshape(R, 1, A_WIDTH))
        S[1].append(kn_s.reshape(R, 1, B_KV_HEADS, B_HEAD_DIM))
        S[2].append(zs[:, Z_BV:Z_BV + B_KV_WIDTH].reshape(R, 1, B_KV_HEADS, B_HEAD_DIM))
        S[3].append(c_new)
        S[4].append(n_new)
        S[5].append(m_new[:, 0, 0:C_HEADS])
        z_new = jnp.concatenate([zg_s, za_s], axis=-1)
        S[6].append(jnp.stack([state_ffn_conv[l][:, CONV_W - 2], z_new], axis=1))

    st = lambda lst: jnp.stack(lst, axis=0)
    return (xp.reshape(Bp, T, D_MODEL), xs.reshape(R, 1, D_MODEL),
            st(P[0]), st(P[1]), st(S[1]), st(S[2]),
            st(P[2]), st(P[3]), st(P[4]),
            st(S[3]), st(S[4]), st(S[5]),
            st(P[5]), st(S[6]),
            st(S[0]))
```

```python
import functools
import math

import numpy as np
import jax
import jax.numpy as jnp
from jax import lax
from jax.experimental import pallas as pl
from jax.experimental.pallas import tpu as pltpu

f32 = jnp.float32
bf16 = jnp.bfloat16

D_MODEL = 2048
EPS = 1e-6
NEG_INF = -1e30
SQRT_HALF = 0.7071067811865476

A_GROUPS = 4
A_DIM = 128
A_WIDTH = 512
CHUNK = 128
B_HEADS = 16
B_KV_HEADS = 2
B_HEAD_DIM = 64
GQA = 8
B_WIDTH = 1024
B_KV_WIDTH = 128
WINDOW = 128
N_BUCKETS = 32
MAX_DISTANCE = 128
C_HEADS = 4
C_QK_DIM = 64
C_V_DIM = 128
C_QK_WIDTH = 256
C_WIDTH = 512
IN_SPLITS = (A_WIDTH, A_WIDTH, B_WIDTH, B_KV_WIDTH, B_KV_WIDTH,
             C_QK_WIDTH, C_QK_WIDTH, C_WIDTH, C_WIDTH, C_HEADS, C_HEADS)
IN_OFFSETS = tuple(int(o) for o in np.cumsum(IN_SPLITS)[:-1])
D_FF = 5632
CONV_W = 3

Z_WIDTH = 4096
Z_AU, Z_AV, Z_BQ, Z_BK, Z_BV, Z_CQ, Z_CK, Z_CV, Z_CO, Z_GATE = (0,) + IN_OFFSETS[:9]
C_PAIR = 2 * C_V_DIM

MLSTM_L = 128
UP_CHUNK = 256
UP_ROWS = 256
C_SEQS = 2
DEC_ROWS = 8
MM_TILE_N = {(D_MODEL, Z_WIDTH): 1024, (D_MODEL, D_MODEL): 1024, (D_MODEL, 2 * D_FF): 512, (D_FF, D_MODEL): 512}
LANES = 128
SUBLANES = 8
VMEM_LIMIT = 52 * 1024 * 1024


def _cparams(sem, flags=None):
    return pltpu.CompilerParams(dimension_semantics=sem, vmem_limit_bytes=VMEM_LIMIT, flags=flags)


def _gelu(x):
    return 0.5 * x * (1.0 + lax.erf(x * SQRT_HALF))


def _rms(x, gain):
    return x * lax.rsqrt(jnp.mean(x * x, axis=-1, keepdims=True) + EPS) * gain


def _norm_head_pairs(x, gain2):
    lo = lax.broadcasted_iota(jnp.int32, x.shape, 1) < B_HEAD_DIM
    x2 = x * x
    s_lo = jnp.sum(jnp.where(lo, x2, 0.0), axis=-1, keepdims=True)
    s_hi = jnp.sum(jnp.where(lo, 0.0, x2), axis=-1, keepdims=True)
    ms = jnp.where(lo, s_lo, s_hi) * (1.0 / B_HEAD_DIM)
    return x * lax.rsqrt(ms + EPS) * gain2


def _t5_bucket_np(dist):
    n = np.maximum(dist, 0)
    max_exact = N_BUCKETS // 2
    nf = np.maximum(n, 1).astype(np.float32)
    large = max_exact + (np.log(nf / np.float32(max_exact)) / np.float32(math.log(MAX_DISTANCE / max_exact))
                         * np.float32(N_BUCKETS - max_exact)).astype(np.int32)
    return np.where(n < max_exact, n, np.minimum(large, N_BUCKETS - 1)).astype(np.int32)


def _bias_kernel(rb_ref, bkp_ref, bks_ref, op_ref, os_ref):
    bkp = bkp_ref[...]
    bks = bks_ref[...]
    for h in range(B_HEADS):
        accp = jnp.zeros(bkp.shape, f32)
        accs = jnp.zeros(bks.shape, f32)
        for b in range(N_BUCKETS):
            val = rb_ref[b, h]
            accp = jnp.where(bkp == b, val, accp)
            accs = jnp.where(bks == b, val, accs)
        op_ref[h] = accp
        os_ref[h:h + 1, :] = accs[0:1, :]


def _bias_tables(rel_bias):
    qi = np.arange(WINDOW)[:, None]
    cj = np.arange(WINDOW)[None, :]
    bkp = _t5_bucket_np((qi - cj) % WINDOW)
    j = np.arange(2 * WINDOW)
    dist_s = np.where(j < WINDOW, WINDOW - j, 0)
    bks = np.broadcast_to(_t5_bucket_np(dist_s)[None, :], (SUBLANES, 2 * WINDOW)).copy()
    return pl.pallas_call(
        _bias_kernel,
        out_shape=(jax.ShapeDtypeStruct((B_HEADS, WINDOW, WINDOW), f32),
                   jax.ShapeDtypeStruct((B_HEADS, 2 * WINDOW), f32)),
        in_specs=[pl.BlockSpec(memory_space=pltpu.SMEM),
                  pl.BlockSpec(memory_space=pltpu.VMEM),
                  pl.BlockSpec(memory_space=pltpu.VMEM)],
        out_specs=(pl.BlockSpec(memory_space=pltpu.VMEM), pl.BlockSpec(memory_space=pltpu.VMEM)),
        name="bias_tables",
    )(rel_bias, jnp.asarray(bkp), jnp.asarray(bks))


def _mm_in_kernel(x_ref, g_ref, w_ref, z_ref, h_ref):
    @pl.when(pl.program_id(1) == 0)
    def _():
        h_ref[...] = _rms(x_ref[...], g_ref[...]).astype(bf16)

    z_ref[...] = jnp.dot(h_ref[...], w_ref[...], preferred_element_type=f32)


def _mm_in(x, gain, w, layer, tm):
    M = x.shape[0]
    tn = MM_TILE_N[w.shape[-2:]]
    return pl.pallas_call(
        _mm_in_kernel,
        grid=(M // tm, Z_WIDTH // tn),
        in_specs=[pl.BlockSpec((tm, D_MODEL), lambda i, j: (i, 0)),
                  pl.BlockSpec((None, 1, D_MODEL), lambda i, j: (layer, 0, 0)),
                  pl.BlockSpec((None, D_MODEL, tn), lambda i, j: (layer, 0, j))],
        out_specs=pl.BlockSpec((tm, tn), lambda i, j: (i, j)),
        out_shape=jax.ShapeDtypeStruct((M, Z_WIDTH), f32),
        scratch_shapes=[pltpu.VMEM((tm, D_MODEL), bf16)],
        compiler_params=_cparams(("parallel", "arbitrary")),
        name="mm_in",
    )(x, gain, w)


def _mm_out_kernel(a_ref, b_ref, c_ref, x_ref, w_ref, o_ref, lhs_ref):
    @pl.when(pl.program_id(1) == 0)
    def _():
        lhs_ref[:, 0:A_WIDTH] = a_ref[...]
        lhs_ref[:, A_WIDTH:A_WIDTH + B_WIDTH] = b_ref[...]
        lhs_ref[:, A_WIDTH + B_WIDTH:D_MODEL] = c_ref[...]

    o_ref[...] = x_ref[...] + jnp.dot(lhs_ref[...], w_ref[...], preferred_element_type=f32)


def _mm_out(a, b, c, x, w, layer, tm):
    M = x.shape[0]
    tn = MM_TILE_N[w.shape[-2:]]
    return pl.pallas_call(
        _mm_out_kernel,
        grid=(M // tm, D_MODEL // tn),
        in_specs=[pl.BlockSpec((tm, A_WIDTH), lambda i, j: (i, 0)),
                  pl.BlockSpec((tm, B_WIDTH), lambda i, j: (i, 0)),
                  pl.BlockSpec((tm, C_WIDTH), lambda i, j: (i, 0)),
                  pl.BlockSpec((tm, tn), lambda i, j: (i, j)),
                  pl.BlockSpec((None, D_MODEL, tn), lambda i, j: (layer, 0, j))],
        out_specs=pl.BlockSpec((tm, tn), lambda i, j: (i, j)),
        out_shape=jax.ShapeDtypeStruct((M, D_MODEL), f32),
        scratch_shapes=[pltpu.VMEM((tm, D_MODEL), bf16)],
        compiler_params=_cparams(("parallel", "arbitrary")),
        name="mm_out",
    )(a, b, c, x, w)


def _mm_down_kernel(a_ref, x_ref, w_ref, o_ref):
    o_ref[...] = x_ref[...] + jnp.dot(a_ref[...], w_ref[...], preferred_element_type=f32)


def _mm_down(act, x, w, layer, tm):
    M = x.shape[0]
    tn = MM_TILE_N[w.shape[-2:]]
    return pl.pallas_call(
        _mm_down_kernel,
        grid=(M // tm, D_MODEL // tn),
        in_specs=[pl.BlockSpec((tm, D_FF), lambda i, j: (i, 0)),
                  pl.BlockSpec((tm, tn), lambda i, j: (i, j)),
                  pl.BlockSpec((None, D_FF, tn), lambda i, j: (layer, 0, j))],
        out_specs=pl.BlockSpec((tm, tn), lambda i, j: (i, j)),
        out_shape=jax.ShapeDtypeStruct((M, D_MODEL), f32),
        compiler_params=_cparams(("parallel", "arbitrary")),
        name="mm_down",
    )(act, x, w)


def _silu(x):
    return x * jax.nn.sigmoid(x)


def _mm_up_prompt_kernel(x_ref, n2_ref, wg_ref, wa_ref, cwg_ref, cwa_ref, cbg_ref, cba_ref,
                         act_ref, cs_ref, h_ref, carry_ref, zb_ref, *, tm, nj, n_steps, tiles_per_seq):
    s = pl.program_id(0)
    sa = jnp.minimum(s, n_steps - 1)
    ia = sa // nj
    ja = sa % nj
    tf = act_ref.shape[1]

    @pl.when(s == 0)
    def _():
        zb_ref[...] = jnp.zeros(zb_ref.shape, f32)
        carry_ref[...] = jnp.zeros(carry_ref.shape, f32)

    @pl.when(ja == 0)
    def _():
        h_ref[...] = _rms(x_ref[...], n2_ref[...]).astype(bf16)

    chunks = [slice(c0, c0 + UP_CHUNK) for c0 in range(0, tf, UP_CHUNK)]

    rblocks = [(r0, min(UP_ROWS, tm - r0)) for r0 in range(0, tm, UP_ROWS)]

    def conv(idx, cw_ref, cb_ref, cs, r0, nr):
        zz = zb_ref[idx, r0:r0 + SUBLANES + nr, cs]
        z1 = pltpu.roll(zz, 1, 0)[SUBLANES:]
        z2 = pltpu.roll(zz, 2, 0)[SUBLANES:]
        return (cb_ref[:, cs] + z2 * cw_ref[0:1, cs] + z1 * cw_ref[1:2, cs] + zz[SUBLANES:] * cw_ref[2:3, cs])

    for cs in chunks:
        for r0, nr in rblocks:
            g = conv(0, cwg_ref, cbg_ref, cs, r0, nr)
            a = conv(1, cwa_ref, cba_ref, cs, r0, nr)
            act_ref[r0:r0 + nr, cs] = (_silu(g) * a).astype(bf16)

    seq_start = (ia % tiles_per_seq) == 0
    for cs in chunks:
        for r0, nr in rblocks:
            for idx, w_ref in ((0, wg_ref), (1, wa_ref)):
                z = jnp.dot(h_ref[r0:r0 + nr, :], w_ref[:, cs], preferred_element_type=f32)
                if r0 == 0:
                    zb_ref[idx, 0:SUBLANES, cs] = jnp.where(seq_start, 0.0, carry_ref[idx, ja, :, cs])
                zb_ref[idx, SUBLANES + r0:SUBLANES + r0 + nr, cs] = z
                if r0 + nr == tm:
                    tail = z[nr - SUBLANES:nr, :]
                    carry_ref[idx, ja, :, cs] = tail
                    cs_ref[0, idx, :, cs] = tail


def _mm_up_prompt(x, norm2, w_up, conv_w, conv_b, layer, batch, tm):
    M = x.shape[0]
    seq = M // batch
    tiles_per_seq = seq // tm
    tf = MM_TILE_N[w_up.shape[-2:]]
    nj = D_FF // tf
    n_steps = (M // tm) * nj
    kern = functools.partial(_mm_up_prompt_kernel, tm=tm, nj=nj, n_steps=n_steps, tiles_per_seq=tiles_per_seq)
    ia = lambda s: jnp.minimum(s, n_steps - 1) // nj
    ja = lambda s: jnp.minimum(s, n_steps - 1) % nj
    ib = lambda s: jnp.maximum(s - 1, 0) // nj
    jb = lambda s: jnp.maximum(s - 1, 0) % nj
    return pl.pallas_call(
        kern,
        grid=(n_steps + 1,),
        in_specs=[pl.BlockSpec((tm, D_MODEL), lambda s: (ia(s), 0)),
                  pl.BlockSpec((None, 1, D_MODEL), lambda s: (layer, 0, 0)),
                  pl.BlockSpec((None, D_MODEL, tf), lambda s: (layer, 0, ja(s))),
                  pl.BlockSpec((None, D_MODEL, tf), lambda s: (layer, 0, nj + ja(s))),
                  pl.BlockSpec((None, CONV_W, tf), lambda s: (layer, 0, jb(s))),
                  pl.BlockSpec((None, CONV_W, tf), lambda s: (layer, 0, nj + jb(s))),
                  pl.BlockSpec((None, 1, tf), lambda s: (layer, 0, jb(s))),
                  pl.BlockSpec((None, 1, tf), lambda s: (layer, 0, nj + jb(s)))],
        out_specs=(pl.BlockSpec((tm, tf), lambda s: (ib(s), jb(s))),
                   pl.BlockSpec((1, 2, SUBLANES, tf), lambda s: (ia(s), 0, 0, ja(s)))),
        out_shape=(jax.ShapeDtypeStruct((M, D_FF), bf16),
                   jax.ShapeDtypeStruct((M // tm, 2, SUBLANES, D_FF), f32)),
        scratch_shapes=[pltpu.VMEM((tm, D_MODEL), bf16),
                        pltpu.VMEM((2, nj, SUBLANES, tf), f32),
                        pltpu.VMEM((2, tm + SUBLANES, tf), f32)],
        compiler_params=_cparams(("arbitrary",)),
        name="mm_up_prompt",
    )(x, norm2, w_up, w_up, conv_w, conv_w, conv_b, conv_b)


def _mm_up_sample_kernel(x_ref, n2_ref, wg_ref, wa_ref, cwg_ref, cwa_ref, cbg_ref, cba_ref,
                         b0g_ref, b1g_ref, b0a_ref, b1a_ref, act_ref, zg_ref, za_ref, h_ref):
    @pl.when(pl.program_id(0) == 0)
    def _():
        h_ref[...] = _rms(x_ref[...], n2_ref[...]).astype(bf16)

    def conv_half(w_ref, cw_ref, cb_ref, b0_ref, b1_ref, z_ref):
        z = jnp.dot(h_ref[...], w_ref[...], preferred_element_type=f32)
        z_ref[...] = z
        return (cb_ref[...] + b0_ref[...] * cw_ref[0:1, :] + b1_ref[...] * cw_ref[1:2, :]
                + z * cw_ref[2:3, :])

    g = conv_half(wg_ref, cwg_ref, cbg_ref, b0g_ref, b1g_ref, zg_ref)
    a = conv_half(wa_ref, cwa_ref, cba_ref, b0a_ref, b1a_ref, za_ref)
    act_ref[...] = (_silu(g) * a).astype(bf16)


def _mm_up_sample(x, norm2, w_up, conv_w, conv_b, buf, layer):
    M = x.shape[0]
    tf = MM_TILE_N[w_up.shape[-2:]]
    nj = D_FF // tf
    wspec = lambda off: pl.BlockSpec((None, D_MODEL, tf), lambda j: (layer, 0, off + j))
    cwspec = lambda off: pl.BlockSpec((None, CONV_W, tf), lambda j: (layer, 0, off + j))
    cbspec = lambda off: pl.BlockSpec((None, 1, tf), lambda j: (layer, 0, off + j))
    bufspec = lambda row, off: pl.BlockSpec((None, None, M, tf), lambda j: (layer, row, 0, off + j))
    return pl.pallas_call(
        _mm_up_sample_kernel,
        grid=(nj,),
        in_specs=[pl.BlockSpec((M, D_MODEL), lambda j: (0, 0)),
                  pl.BlockSpec((None, 1, D_MODEL), lambda j: (layer, 0, 0)),
                  wspec(0), wspec(nj), cwspec(0), cwspec(nj), cbspec(0), cbspec(nj),
                  bufspec(0, 0), bufspec(1, 0), bufspec(0, nj), bufspec(1, nj)],
        out_specs=(pl.BlockSpec((M, tf), lambda j: (0, j)),
                   pl.BlockSpec((M, tf), lambda j: (0, j)),
                   pl.BlockSpec((M, tf), lambda j: (0, j))),
        out_shape=(jax.ShapeDtypeStruct((M, D_FF), bf16),
                   jax.ShapeDtypeStruct((M, D_FF), f32),
                   jax.ShapeDtypeStruct((M, D_FF), f32)),
        scratch_shapes=[pltpu.VMEM((M, D_MODEL), bf16)],
        compiler_params=_cparams(("arbitrary",)),
        name="mm_up_sample",
    )(x, norm2, w_up, w_up, conv_w, conv_w, conv_b, conv_b, buf, buf, buf, buf)


def _mix_a_prompt_kernel(u_ref, v_ref, vg_ref, ws_ref, bs_ref, o_ref):
    row = lax.broadcasted_iota(jnp.int32, (CHUNK, CHUNK), 0)
    col = lax.broadcasted_iota(jnp.int32, (CHUNK, CHUNK), 1)
    tri = col <= row
    for g in range(A_GROUPS):
        sl = slice(g * A_DIM, (g + 1) * A_DIM)
        ws = jnp.where(tri, ws_ref[g], 0.0).astype(bf16)
        for r0 in range(0, u_ref.shape[0], CHUNK):
            rs = slice(r0, r0 + CHUNK)
            vn = _rms(_gelu(v_ref[rs, sl]), vg_ref[:, sl])
            mixv = jnp.dot(ws, vn.astype(bf16), preferred_element_type=f32) + bs_ref[:, sl]
            o_ref[rs, sl] = (_gelu(u_ref[rs, sl]) * mixv).astype(bf16)


def _mix_a_prompt(z, v_gain, w_s, b_s_full, layer, rows):
    M = z.shape[0]
    return pl.pallas_call(
        _mix_a_prompt_kernel,
        grid=(M // rows,),
        in_specs=[pl.BlockSpec((rows, A_WIDTH), lambda r: (r, Z_AU // A_WIDTH)),
                  pl.BlockSpec((rows, A_WIDTH), lambda r: (r, Z_AV // A_WIDTH)),
                  pl.BlockSpec((None, 1, A_WIDTH), lambda r: (layer, 0, 0)),
                  pl.BlockSpec((None, A_GROUPS, CHUNK, CHUNK), lambda r: (layer, 0, 0, 0)),
                  pl.BlockSpec((None, CHUNK, A_WIDTH), lambda r: (layer, 0, 0))],
        out_specs=pl.BlockSpec((rows, A_WIDTH), lambda r: (r, 0)),
        out_shape=jax.ShapeDtypeStruct((M, A_WIDTH), bf16),
        compiler_params=_cparams(("parallel",)),
        name="mix_a_prompt",
    )(z, z, v_gain, w_s, b_s_full)


def _mix_a_sample_kernel(u_ref, v_ref, vg_ref, ws_ref, bs_ref, o_ref, vn_ref):
    for g in range(A_GROUPS):
        sl = slice(g * A_DIM, (g + 1) * A_DIM)
        vn = _rms(_gelu(v_ref[:, sl]), vg_ref[:, sl])
        vn_ref[:, sl] = vn
        mixv = ws_ref[:, sl] * vn + bs_ref[:, sl]
        o_ref[:, sl] = (_gelu(u_ref[:, sl]) * mixv).astype(bf16)


def _mix_a_sample(z, v_gain, ws0, bs0, layer):
    M = z.shape[0]
    vec = pl.BlockSpec((None, 1, A_WIDTH), lambda r: (layer, 0, 0))
    return pl.pallas_call(
        _mix_a_sample_kernel,
        grid=(1,),
        in_specs=[pl.BlockSpec((M, A_WIDTH), lambda r: (0, Z_AU // A_WIDTH)),
                  pl.BlockSpec((M, A_WIDTH), lambda r: (0, Z_AV // A_WIDTH)),
                  vec, vec, vec],
        out_specs=(pl.BlockSpec((M, A_WIDTH), lambda r: (0, 0)),
                   pl.BlockSpec((M, A_WIDTH), lambda r: (0, 0))),
        out_shape=(jax.ShapeDtypeStruct((M, A_WIDTH), bf16),
                   jax.ShapeDtypeStruct((M, A_WIDTH), f32)),
        compiler_params=_cparams(("arbitrary",)),
        name="mix_a_sample",
    )(z, z, v_gain, ws0, bs0)


def _mix_b_prompt_kernel(sink_ref, q_ref, k_ref, v_ref, qg_ref, kg_ref, bias_ref,
                         o_ref, klast_ref, vlast_ref, kcat_ref, vcat_ref):
    i = pl.program_id(1)
    W = WINDOW
    kn = _norm_head_pairs(k_ref[...], kg_ref[...])
    v = v_ref[...]
    klast_ref[0] = kn
    vlast_ref[0] = v

    @pl.when(i == 0)
    def _():
        kcat_ref[0:W, :] = jnp.zeros((W, LANES), f32)
        vcat_ref[0:W, :] = jnp.zeros((W, LANES), f32)

    @pl.when(i > 0)
    def _():
        kcat_ref[0:W, :] = kcat_ref[W:2 * W, :]
        vcat_ref[0:W, :] = vcat_ref[W:2 * W, :]

    kcat_ref[W:2 * W, :] = kn
    vcat_ref[W:2 * W, :] = v
    kc = kcat_ref[...]
    vc = vcat_ref[...]
    kc_sw = pltpu.roll(kc, B_HEAD_DIM, 1)
    vc_sw = pltpu.roll(vc, B_HEAD_DIM, 1)
    lo = lax.broadcasted_iota(jnp.int32, (2 * W, LANES), 1) < B_HEAD_DIM

    qi = lax.broadcasted_iota(jnp.int32, (W, W), 0)
    cj = lax.broadcasted_iota(jnp.int32, (W, W), 1)
    own = cj <= qi
    keep = own | (i > 0)
    q_gain = qg_ref[...] * (B_HEAD_DIM ** -0.5)

    tiles = GQA // 2
    dn = (((1,), (1,)), ((), ()))
    scores, values = [], []
    for kvh in range(B_KV_HEADS):
        k_src, k_alt = (kc, kc_sw) if kvh == 0 else (kc_sw, kc)
        v_src, v_alt = (vc, vc_sw) if kvh == 0 else (vc_sw, vc)
        k_even = jnp.where(lo, k_src, 0.0).astype(bf16)
        k_odd = jnp.where(lo, 0.0, k_alt).astype(bf16)
        values.append((jnp.where(lo, v_src, 0.0).astype(bf16), jnp.where(lo, 0.0, v_alt).astype(bf16)))
        qs = jnp.concatenate(
            [_norm_head_pairs(q_ref[:, (kvh * tiles + t) * LANES:(kvh * tiles + t + 1) * LANES], q_gain)
             for t in range(tiles)], axis=0).astype(bf16)
        scores.append((lax.dot_general(qs, k_even, dn, preferred_element_type=f32),
                       lax.dot_general(qs, k_odd, dn, preferred_element_type=f32)))
    for kvh in range(B_KV_HEADS):
        p_par = []
        for par in range(2):
            blocks = []
            for t in range(tiles):
                h = kvh * GQA + 2 * t + par
                sb = scores[kvh][par][t * W:(t + 1) * W, :]
                s = jnp.where(own, sb[:, W:2 * W], sb[:, 0:W]) + bias_ref[h]
                s = jnp.where(keep, s, NEG_INF)
                sk = sink_ref[h]
                mx = jnp.maximum(jnp.max(s, axis=-1, keepdims=True), sk)
                p = jnp.exp(s - mx)
                den = jnp.sum(p, axis=-1, keepdims=True) + jnp.exp(sk - mx)
                p = p * (1.0 / den)
                blocks.append(jnp.concatenate([jnp.where(own, 0.0, p), jnp.where(own, p, 0.0)],
                                              axis=1).astype(bf16))
            p_par.append(jnp.concatenate(blocks, axis=0))
        o = (jnp.dot(p_par[0], values[kvh][0], preferred_element_type=f32)
             + jnp.dot(p_par[1], values[kvh][1], preferred_element_type=f32))
        for t in range(tiles):
            c0 = (kvh * tiles + t) * LANES
            o_ref[:, c0:c0 + LANES] = o[t * W:(t + 1) * W, :].astype(bf16)


def _mix_b_prompt(z, sinks, q_gain2, k_gain2, bias_p, layer, batch):
    M = z.shape[0]
    nb = M // batch // WINDOW
    return pl.pallas_call(
        _mix_b_prompt_kernel,
        grid=(batch, nb),
        in_specs=[pl.BlockSpec(memory_space=pltpu.SMEM),
                  pl.BlockSpec((WINDOW, B_WIDTH), lambda b, i: (b * nb + i, Z_BQ // B_WIDTH)),
                  pl.BlockSpec((WINDOW, B_KV_WIDTH), lambda b, i: (b * nb + i, Z_BK // B_KV_WIDTH)),
                  pl.BlockSpec((WINDOW, B_KV_WIDTH), lambda b, i: (b * nb + i, Z_BV // B_KV_WIDTH)),
                  pl.BlockSpec((None, 1, LANES), lambda b, i: (layer, 0, 0)),
                  pl.BlockSpec((None, 1, LANES), lambda b, i: (layer, 0, 0)),
                  pl.BlockSpec((B_HEADS, WINDOW, WINDOW), lambda b, i: (0, 0, 0))],
        out_specs=(pl.BlockSpec((WINDOW, B_WIDTH), lambda b, i: (b * nb + i, 0)),
                   pl.BlockSpec((1, WINDOW, B_KV_WIDTH), lambda b, i: (b, 0, 0)),
                   pl.BlockSpec((1, WINDOW, B_KV_WIDTH), lambda b, i: (b, 0, 0))),
        out_shape=(jax.ShapeDtypeStruct((M, B_WIDTH), bf16),
                   jax.ShapeDtypeStruct((batch, WINDOW, B_KV_WIDTH), f32),
                   jax.ShapeDtypeStruct((batch, WINDOW, B_KV_WIDTH), f32)),
        scratch_shapes=[pltpu.VMEM((2 * WINDOW, LANES), f32),
                        pltpu.VMEM((2 * WINDOW, LANES), f32)],
        compiler_params=_cparams(("arbitrary", "arbitrary")),
        name="mix_b_prompt",
    )(sinks, z, z, z, q_gain2, k_gain2, bias_p)


def _mix_c_prompt_kernel(q_ref, k_ref, v0_ref, v1_ref, o0_ref, o1_ref, g_ref, gb_ref, hgt_ref,
                         out_ref, c_out, n_out, m_out, c_s, n_s, m_s):
    L = MLSTM_L
    c = pl.program_id(1)

    @pl.when(c == 0)
    def _():
        c_s[...] = jnp.zeros(c_s.shape, f32)
        n_s[...] = jnp.zeros(n_s.shape, f32)
        m_s[...] = jnp.zeros(m_s.shape, f32)

    row = lax.broadcasted_iota(jnp.int32, (L, L), 0)
    col = lax.broadcasted_iota(jnp.int32, (L, L), 1)
    tri = col <= row
    tri_f = tri.astype(f32)
    seqs = range(q_ref.shape[0])
    units = [(bi, h) for bi in seqs for h in range(C_HEADS)]
    g, b = {}, {}
    for bi in seqs:
        g[bi] = g_ref[bi] + gb_ref[...]
        b[bi] = jnp.dot(tri_f, jax.nn.log_sigmoid(g[bi]), precision=lax.Precision.HIGHEST,
                        preferred_element_type=f32)
    bT = {bi: b[bi].T for bi in seqs}
    vis = row <= col
    nt = (((1,), (1,)), ((), ()))
    tn = (((0,), (0,)), ((), ()))
    qb, kh, vb, ct_prev, n_prev, st, qct, qn_lin = {}, {}, {}, {}, {}, {}, {}, {}
    for u in units:
        bi, h = u
        v_ref = v0_ref if h < 2 else v1_ref
        vsl = slice((h % 2) * C_V_DIM, (h % 2 + 1) * C_V_DIM)
        qb[u] = q_ref[bi, :, h * C_QK_DIM:(h + 1) * C_QK_DIM].astype(bf16)
        kh[u] = k_ref[bi, :, h * C_QK_DIM:(h + 1) * C_QK_DIM] * (C_QK_DIM ** -0.5)
        vb[u] = v_ref[bi, :, vsl].astype(bf16)
        ct_prev[u] = c_s[bi, h]
        n_prev[u] = n_s[bi, h]
        st[u] = lax.dot_general(kh[u].astype(bf16), qb[u], nt, preferred_element_type=f32)
        qct[u] = lax.dot_general(ct_prev[u].astype(bf16), qb[u], nt, preferred_element_type=f32)
        n8 = jnp.broadcast_to(n_prev[u], (SUBLANES, C_QK_DIM)).astype(bf16)
        qn_lin[u] = lax.dot_general(n8, qb[u], nt, preferred_element_type=f32)[0:1, :]
    pt, iw, inv, kw, decay, m_new = {}, {}, {}, {}, {}, {}
    for u in units:
        bi, h = u
        b_row = bT[bi][C_HEADS + h:C_HEADS + h + 1, :]
        src = g[bi][:, h:h + 1] - b[bi][:, C_HEADS + h:C_HEADS + h + 1]
        dlog = jnp.where(vis, b_row + src, -jnp.inf)
        m_prev = m_s[bi, h][0:1, 0:1]
        inter = b_row + m_prev
        mt = jnp.maximum(inter, jnp.max(dlog, axis=0, keepdims=True))
        dt = jnp.exp(dlog - mt)
        pt[u] = st[u] * dt
        iw[u] = jnp.exp(inter - mt)
        qn = iw[u] * qn_lin[u] + jnp.sum(pt[u], axis=0, keepdims=True)
        inv[u] = 1.0 / jnp.maximum(jnp.abs(qn), jnp.exp(-mt))
        m_new[u] = mt[:, L - 1:L]
        decay[u] = jnp.exp(b_row[:, L - 1:L] + m_prev - m_new[u])
        kw[u] = kh[u] * dt[:, L - 1:L]
    svt = {u: lax.dot_general(vb[u], pt[u].astype(bf16), tn, preferred_element_type=f32) for u in units}
    kvt = {u: lax.dot_general(vb[u], kw[u].astype(bf16), tn, preferred_element_type=f32) for u in units}
    for u in units:
        bi, h = u
        o_ref = o0_ref if h < 2 else o1_ref
        vsl = slice((h % 2) * C_V_DIM, (h % 2 + 1) * C_V_DIM)
        ht = (iw[u] * qct[u] + svt[u]) * inv[u]
        ms = jnp.mean(ht * ht, axis=0, keepdims=True)
        hn = (ht * lax.rsqrt(ms + EPS) * hgt_ref[h]).T
        c_s[bi, h] = decay[u] * ct_prev[u] + kvt[u]
        n_s[bi, h] = decay[u] * n_prev[u] + jnp.sum(kw[u], axis=0, keepdims=True)
        m_s[bi, h] = jnp.broadcast_to(m_new[u], (SUBLANES, LANES))
        osl = slice(h * C_V_DIM, (h + 1) * C_V_DIM)
        out_ref[bi, :, osl] = (hn * jax.nn.sigmoid(o_ref[bi, :, vsl])).astype(bf16)

    @pl.when(c == pl.num_programs(1) - 1)
    def _():
        eye = (row == col).astype(f32)
        for bi, h in units:
            c_out[bi, h] = lax.dot_general(c_s[bi, h], eye, tn, precision=lax.Precision.HIGHEST,
                                           preferred_element_type=f32)
        n_out[...] = n_s[...]
        m_out[...] = m_s[...]


def _mix_c_prompt(z, gate_bias, h_gain, layer, batch):
    M = z.shape[0]
    L = MLSTM_L
    T = M // batch
    z3 = z.reshape(batch, T, Z_WIDTH)
    ns = C_SEQS if batch % C_SEQS == 0 else 1
    col = lambda width, off: pl.BlockSpec((ns, L, width), lambda b, c: (b, c, off // width))
    state = lambda *shape: pl.BlockSpec((ns,) + shape, lambda b, c: (b,) + (0,) * len(shape))
    out, c_st, n_st, m_st = pl.pallas_call(
        _mix_c_prompt_kernel,
        grid=(batch // ns, T // L),
        in_specs=[col(C_QK_WIDTH, Z_CQ), col(C_QK_WIDTH, Z_CK),
                  col(C_PAIR, Z_CV), col(C_PAIR, Z_CV + C_PAIR),
                  col(C_PAIR, Z_CO), col(C_PAIR, Z_CO + C_PAIR),
                  col(LANES, Z_GATE),
                  pl.BlockSpec((None, 1, LANES), lambda b, c: (layer, 0, 0)),
                  pl.BlockSpec((None, C_HEADS, C_V_DIM, L), lambda b, c: (layer, 0, 0, 0))],
        out_specs=(pl.BlockSpec((ns, L, C_WIDTH), lambda b, c: (b, c, 0)),
                   state(C_HEADS, C_QK_DIM, C_V_DIM),
                   state(C_HEADS, 1, C_QK_DIM),
                   state(C_HEADS, SUBLANES, LANES)),
        out_shape=(jax.ShapeDtypeStruct((batch, T, C_WIDTH), bf16),
                   jax.ShapeDtypeStruct((batch, C_HEADS, C_QK_DIM, C_V_DIM), f32),
                   jax.ShapeDtypeStruct((batch, C_HEADS, 1, C_QK_DIM), f32),
                   jax.ShapeDtypeStruct((batch, C_HEADS, SUBLANES, LANES), f32)),
        scratch_shapes=[pltpu.VMEM((ns, C_HEADS, C_V_DIM, C_QK_DIM), f32),
                        pltpu.VMEM((ns, C_HEADS, 1, C_QK_DIM), f32),
                        pltpu.VMEM((ns, C_HEADS, SUBLANES, LANES), f32)],
        compiler_params=_cparams(("arbitrary", "arbitrary")),
        name="mix_c_prompt",
    )(z3, z3, z3, z3, z3, z3, z3, gate_bias, h_gain)
    return out.reshape(M, C_WIDTH), c_st, n_st, m_st


_BC_SHARED = (4, 5, 6, 7, 12, 13)


def _mix_bc_sample_kernel(*refs):
    rows = [_mix_bc_sample_row(*[ref if k in _BC_SHARED else ref.at[pl.ds(r, 1)] for k, ref in enumerate(refs)])
            for r in range(refs[0].shape[0])]
    while rows:
        rows = [row for row in rows if next(row, True) is None]


def _mix_bc_sample_row(zr_ref, q_ref, kp_ref, vp_ref, qg_ref, kg_ref, sink_ref, bias_ref,
                       cq_ref, ck_ref, cv_ref, co_ref, gb_ref, hg_ref, c0_ref, n0_ref, m0_ref,
                       bo_ref, kn_ref, co_out, c_out, n_out, m_out):
    W = WINDOW
    zr = zr_ref[0]

    qn = _rms(q_ref[0], qg_ref[...])
    hrow = lax.broadcasted_iota(jnp.int32, (B_HEADS, LANES), 0)
    lane = lax.broadcasted_iota(jnp.int32, (B_HEADS, LANES), 1)
    own = (hrow < GQA) == (lane < B_HEAD_DIM)
    q2 = jnp.where(own, jnp.concatenate([qn, qn], axis=-1), 0.0)
    k_new = _norm_head_pairs(zr[:, Z_BK:Z_BK + B_KV_WIDTH], kg_ref[...])
    v_new = zr[:, Z_BV:Z_BV + B_KV_WIDTH]
    kn_ref[0] = k_new
    scale = B_HEAD_DIM ** -0.5
    s_past = lax.dot_general(q2.astype(bf16), kp_ref[0].astype(bf16), (((1,), (1,)), ((), ())),
                             preferred_element_type=f32) * scale + bias_ref[:, 0:W]
    yield
    s_past = jnp.where(lane >= 1, s_past, NEG_INF)
    s_new = jnp.sum(q2 * k_new, axis=-1, keepdims=True) * scale + bias_ref[:, W:W + 1]
    sk = sink_ref[...]
    mx = jnp.maximum(jnp.maximum(jnp.max(s_past, axis=-1, keepdims=True), s_new), sk)
    p_past = jnp.exp(s_past - mx)
    p_new = jnp.exp(s_new - mx)
    den = jnp.sum(p_past, axis=-1, keepdims=True) + p_new + jnp.exp(sk - mx)
    inv = 1.0 / den
    o2 = (jnp.dot((p_past * inv).astype(bf16), vp_ref[0].astype(bf16), preferred_element_type=f32)
          + (p_new * inv) * v_new)
    o2_sw = pltpu.roll(o2, B_HEAD_DIM, 1)
    bo_ref[0] = jnp.where(hrow < GQA, o2, o2_sw)[:, 0:B_HEAD_DIM].astype(bf16)

    g = zr[:, Z_GATE:Z_GATE + LANES] + gb_ref[...]
    lf = jax.nn.log_sigmoid(g)
    m0 = m0_ref[0]
    lane1 = lax.broadcasted_iota(jnp.int32, (1, LANES), 1)
    m_row = jnp.zeros((1, LANES), f32)
    row8 = lax.broadcasted_iota(jnp.int32, (SUBLANES, C_QK_DIM), 0)
    c_rows = []
    staged = []
    for h in range(C_HEADS):
        ig = g[:, h:h + 1]
        b = lf[:, C_HEADS + h:C_HEADS + h + 1]
        m_prev = m0[:, h:h + 1]
        inter = b + m_prev
        mt = jnp.maximum(inter, ig)
        qh = cq_ref[0, h:h + 1, :]
        kh = ck_ref[0, h:h + 1, :] * (C_QK_DIM ** -0.5)
        vh = cv_ref[0, h:h + 1, :]
        c_prev = c0_ref[0, h]
        n_prev = n0_ref[0, h:h + 1, :]
        s = jnp.sum(qh * kh, axis=-1, keepdims=True) * jnp.exp(ig - mt)
        iw = jnp.exp(inter - mt)
        q8 = jnp.broadcast_to(qh, (SUBLANES, C_QK_DIM)).astype(bf16)
        qc = jnp.dot(q8, c_prev.astype(bf16), preferred_element_type=f32)[0:1, :]
        staged.append((ig, inter, mt, qh, kh, vh, c_prev, n_prev, s, iw, qc))
    yield
    for h, (ig, inter, mt, qh, kh, vh, c_prev, n_prev, s, iw, qc) in enumerate(staged):
        num = iw * qc + s * vh
        qn_ = iw * jnp.sum(qh * n_prev, axis=-1, keepdims=True) + s
        hh = num / jnp.maximum(jnp.abs(qn_), jnp.exp(-mt))
        wc = jnp.exp(ig - mt)
        decay = jnp.exp(inter - mt)
        kw = kh * wc
        kw8 = jnp.where(row8 == 0, jnp.broadcast_to(kw, (SUBLANES, C_QK_DIM)), 0.0).astype(bf16)
        v8 = jnp.broadcast_to(vh, (SUBLANES, C_V_DIM)).astype(bf16)
        c_out[0, h] = decay * c_prev + lax.dot_general(kw8, v8, (((0,), (0,)), ((), ())),
                                                      preferred_element_type=f32)
        n_out[0, h:h + 1, :] = decay * n_prev + kw
        m_row = jnp.where(lane1 == h, mt, m_row)
        c_rows.append(_rms(hh, hg_ref[h:h + 1, :]) * jax.nn.sigmoid(co_ref[0, h:h + 1, :]))
    co_out[0] = jnp.concatenate(c_rows, axis=0).astype(bf16)
    m_out[0] = m_row


def _mix_bc_sample(z, k_past, v_past, q_gain, k_gain2, sinks, bias_s, gate_bias, h_gain4,
                   c0, n0, m0, layer):
    R = z.shape[0]
    zr = z.reshape(R, 1, Z_WIDTH)
    q = z[:, Z_BQ:Z_BQ + B_WIDTH].reshape(R, B_HEADS, B_HEAD_DIM)
    cq = z[:, Z_CQ:Z_CQ + C_QK_WIDTH].reshape(R, C_HEADS, C_QK_DIM)
    ck = z[:, Z_CK:Z_CK + C_QK_WIDTH].reshape(R, C_HEADS, C_QK_DIM)
    cv = z[:, Z_CV:Z_CV + C_WIDTH].reshape(R, C_HEADS, C_V_DIM)
    co = z[:, Z_CO:Z_CO + C_WIDTH].reshape(R, C_HEADS, C_V_DIM)
    rb = DEC_ROWS if R % DEC_ROWS == 0 else 1
    row3 = lambda n: pl.BlockSpec((rb, 1, n), lambda r: (r, 0, 0))
    return pl.pallas_call(
        _mix_bc_sample_kernel,
        grid=(R // rb,),
        in_specs=[row3(Z_WIDTH),
                  pl.BlockSpec((rb, B_HEADS, B_HEAD_DIM), lambda r: (r, 0, 0)),
                  pl.BlockSpec((None, rb, WINDOW, B_KV_WIDTH), lambda r: (layer, r, 0, 0)),
                  pl.BlockSpec((None, rb, WINDOW, B_KV_WIDTH), lambda r: (layer, r, 0, 0)),
                  pl.BlockSpec((None, 1, B_HEAD_DIM), lambda r: (layer, 0, 0)),
                  pl.BlockSpec((None, 1, LANES), lambda r: (layer, 0, 0)),
                  pl.BlockSpec((None, B_HEADS, 1), lambda r: (layer, 0, 0)),
                  pl.BlockSpec((B_HEADS, 2 * WINDOW), lambda r: (0, 0)),
                  pl.BlockSpec((rb, C_HEADS, C_QK_DIM), lambda r: (r, 0, 0)),
                  pl.BlockSpec((rb, C_HEADS, C_QK_DIM), lambda r: (r, 0, 0)),
                  pl.BlockSpec((rb, C_HEADS, C_V_DIM), lambda r: (r, 0, 0)),
                  pl.BlockSpec((rb, C_HEADS, C_V_DIM), lambda r: (r, 0, 0)),
                  pl.BlockSpec((None, 1, LANES), lambda r: (layer, 0, 0)),
                  pl.BlockSpec((None, C_HEADS, C_V_DIM), lambda r: (layer, 0, 0)),
                  pl.BlockSpec((None, rb, C_HEADS, C_QK_DIM, C_V_DIM), lambda r: (layer, r, 0, 0, 0)),
                  pl.BlockSpec((None, rb, C_HEADS, C_QK_DIM), lambda r: (layer, r, 0, 0)),
                  pl.BlockSpec((None, rb, 1, C_HEADS), lambda r: (layer, r, 0, 0))],
        out_specs=(pl.BlockSpec((rb, B_HEADS, B_HEAD_DIM), lambda r: (r, 0, 0)),
                   row3(B_KV_WIDTH),
                   pl.BlockSpec((rb, C_HEADS, C_V_DIM), lambda r: (r, 0, 0)),
                   pl.BlockSpec((rb, C_HEADS, C_QK_DIM, C_V_DIM), lambda r: (r, 0, 0, 0)),
                   pl.BlockSpec((rb, C_HEADS, C_QK_DIM), lambda r: (r, 0, 0)),
                   row3(LANES)),
        out_shape=(jax.ShapeDtypeStruct((R, B_HEADS, B_HEAD_DIM), bf16),
                   jax.ShapeDtypeStruct((R, 1, B_KV_WIDTH), f32),
                   jax.ShapeDtypeStruct((R, C_HEADS, C_V_DIM), bf16),
                   jax.ShapeDtypeStruct((R, C_HEADS, C_QK_DIM, C_V_DIM), f32),
                   jax.ShapeDtypeStruct((R, C_HEADS, C_QK_DIM), f32),
                   jax.ShapeDtypeStruct((R, 1, LANES), f32)),
        compiler_params=_cparams(("parallel",)),
        name="mix_bc_sample",
    )(zr, q, k_past, v_past, q_gain, k_gain2, sinks, bias_s, cq, ck, cv, co, gate_bias, h_gain4,
      c0, n0, m0)


def _prep_w_in(w_in):
    pad = jnp.zeros(w_in.shape[:-1] + (Z_WIDTH - w_in.shape[-1],), bf16)
    return jnp.concatenate([w_in.astype(bf16), pad], axis=-1)


def _row_tile(m, pref):
    return pref if m % pref == 0 else m


def kernel(x_prompt, x_sample, cache_swa_k, cache_swa_v, state_mlstm_C, state_mlstm_n, state_mlstm_m, state_ffn_conv, rel_bias, norm1, w_in, a_v_gain, a_spatial_w, a_spatial_b, b_q_gain, b_k_gain, b_sinks, c_gate_bias, c_h_gain, w_out, norm2, w_up, ffn_conv_w, ffn_conv_b, w_down):
    depth = w_in.shape[0]
    Bp, T, _ = x_prompt.shape
    R = x_sample.shape[0]
    assert x_sample.shape[1] == 1 and T % CHUNK == 0 and T % MLSTM_L == 0

    w_in_p = _prep_w_in(w_in)
    w_out_b = w_out.astype(bf16)
    w_up_b = w_up.astype(bf16)
    w_down_b = w_down.astype(bf16)
    norm1_3 = norm1.reshape(depth, 1, D_MODEL)
    norm2_3 = norm2.reshape(depth, 1, D_MODEL)
    v_gain3 = a_v_gain.reshape(depth, 1, A_WIDTH)
    bs_full = jnp.repeat(jnp.swapaxes(a_spatial_b, 1, 2), A_DIM, axis=-1)
    ws0 = jnp.repeat(a_spatial_w[:, :, 0, 0], A_DIM, axis=-1).reshape(depth, 1, A_WIDTH)
    bs0 = jnp.repeat(a_spatial_b[:, :, 0], A_DIM, axis=-1).reshape(depth, 1, A_WIDTH)
    q_gain3 = b_q_gain.reshape(depth, 1, B_HEAD_DIM)
    q_gain2 = jnp.tile(b_q_gain, (1, 2)).reshape(depth, 1, LANES)
    k_gain2 = jnp.tile(b_k_gain, (1, 2)).reshape(depth, 1, LANES)
    sinks3 = b_sinks.reshape(depth, B_HEADS, 1)
    gate_b = jnp.pad(c_gate_bias, ((0, 0), (0, LANES - 2 * C_HEADS))).reshape(depth, 1, LANES)
    h_gain_t = jnp.broadcast_to(c_h_gain.reshape(depth, C_HEADS, C_V_DIM, 1), (depth, C_HEADS, C_V_DIM, MLSTM_L))
    h_gain4 = c_h_gain.reshape(depth, C_HEADS, C_V_DIM)
    conv_b3 = ffn_conv_b.reshape(depth, 1, 2 * D_FF)
    k_cache = cache_swa_k.reshape(depth, R, WINDOW, B_KV_WIDTH)
    v_cache = cache_swa_v.reshape(depth, R, WINDOW, B_KV_WIDTH)
    m_state = state_mlstm_m.reshape(depth, R, 1, C_HEADS)
    conv_hist = jnp.swapaxes(state_ffn_conv, 1, 2)

    bias_p, bias_s = _bias_tables(rel_bias)

    Mp = Bp * T
    tm = _row_tile(T, 1024)
    a_rows = _row_tile(T, 512)
    xp = x_prompt.reshape(Mp, D_MODEL)
    xs = x_sample.reshape(R, D_MODEL)
    P = [[] for _ in range(6)]
    S = [[] for _ in range(7)]
    for l in range(depth):
        z = _mm_in(xp, norm1_3, w_in_p, l, tm)
        a_o = _mix_a_prompt(z, v_gain3, a_spatial_w, bs_full, l, a_rows)
        b_o, k_last, v_last = _mix_b_prompt(z, b_sinks[l], q_gain2, k_gain2, bias_p, l, Bp)
        c_o, c_st, n_st, m_st = _mix_c_prompt(z, gate_b, h_gain_t, l, Bp)
        x1 = _mm_out(a_o, b_o, c_o, xp, w_out_b, l, tm)
        act, cs = _mm_up_prompt(x1, norm2_3, w_up_b, ffn_conv_w, conv_b3, l, Bp, tm)
        xp = _mm_down(act, x1, w_down_b, l, tm)
        P[0].append(k_last.reshape(Bp, WINDOW, B_KV_HEADS, B_HEAD_DIM))
        P[1].append(v_last.reshape(Bp, WINDOW, B_KV_HEADS, B_HEAD_DIM))
        P[2].append(c_st)
        P[3].append(n_st.reshape(Bp, C_HEADS, C_QK_DIM))
        P[4].append(m_st[:, :, 0, 0])
        seq_tiles = T // tm
        tail = cs[seq_tiles - 1::seq_tiles, :, SUBLANES - (CONV_W - 1):, :]
        P[5].append(jnp.swapaxes(tail, 1, 2).reshape(Bp, CONV_W - 1, 2 * D_FF))

        zs = _mm_in(xs, norm1_3, w_in_p, l, R)
        a_s, vn_s = _mix_a_sample(zs, v_gain3, ws0, bs0, l)
        b_s, kn_s, c_s, c_new, n_new, m_new = _mix_bc_sample(
            zs, k_cache, v_cache, q_gain3, k_gain2, sinks3, bias_s, gate_b, h_gain4,
            state_mlstm_C, state_mlstm_n, m_state, l)
        x1s = _mm_out(a_s, b_s.reshape(R, B_WIDTH), c_s.reshape(R, C_WIDTH), xs, w_out_b, l, R)
        act_s, zg_s, za_s = _mm_up_sample(x1s, norm2_3, w_up_b, ffn_conv_w, conv_b3, conv_hist, l)
        xs = _mm_down(act_s, x1s, w_down_b, l, R)
        S[0].append(vn_s.reshape(R, 1, A_WIDTH))
        S[1].append(kn_s.reshape(R, 1, B_KV_HEADS, B_HEAD_DIM))
        S[2].append(zs[:, Z_BV:Z_BV + B_KV_WIDTH].reshape(R, 1, B_KV_HEADS, B_HEAD_DIM))
        S[3].append(c_new)
        S[4].append(n_new)
        S[5].append(m_new[:, 0, 0:C_HEADS])
        z_new = jnp.concatenate([zg_s, za_s], axis=-1)
        S[6].append(jnp.stack([state_ffn_conv[l][:, CONV_W - 2], z_new], axis=1))

    st = lambda lst: jnp.stack(lst, axis=0)
    return (xp.reshape(Bp, T, D_MODEL), xs.reshape(R, 1, D_MODEL),
            st(P[0]), st(P[1]), st(S[1]), st(S[2]),
            st(P[2]), st(P[3]), st(P[4]),
            st(S[3]), st(S[4]), st(S[5]),
            st(P[5]), st(S[6]),
            st(S[0]))
```

```python
import functools
import math

import numpy as np
import jax
import jax.numpy as jnp
from jax import lax
from jax.experimental import pallas as pl
from jax.experimental.pallas import tpu as pltpu

f32 = jnp.float32
bf16 = jnp.bfloat16

D_MODEL = 2048
EPS = 1e-6
NEG_INF = -1e30
SQRT_HALF = 0.7071067811865476

A_GROUPS = 4
A_DIM = 128
A_WIDTH = 512
CHUNK = 128
B_HEADS = 16
B_KV_HEADS = 2
B_HEAD_DIM = 64
GQA = 8
B_WIDTH = 1024
B_KV_WIDTH = 128
WINDOW = 128
N_BUCKETS = 32
MAX_DISTANCE = 128
C_HEADS = 4
C_QK_DIM = 64
C_V_DIM = 128
C_QK_WIDTH = 256
C_WIDTH = 512
IN_SPLITS = (A_WIDTH, A_WIDTH, B_WIDTH, B_KV_WIDTH, B_KV_WIDTH,
             C_QK_WIDTH, C_QK_WIDTH, C_WIDTH, C_WIDTH, C_HEADS, C_HEADS)
IN_OFFSETS = tuple(int(o) for o in np.cumsum(IN_SPLITS)[:-1])
D_FF = 5632
CONV_W = 3

Z_WIDTH = 4096
Z_AU, Z_AV, Z_BQ, Z_BK, Z_BV, Z_CQ, Z_CK, Z_CV, Z_CO, Z_GATE = (0,) + IN_OFFSETS[:9]
C_PAIR = 2 * C_V_DIM

MLSTM_L = 128
UP_CHUNK = 256
UP_ROWS = 256
C_SEQS = 2
DEC_ROWS = 8
MM_TILE_N = {(D_MODEL, Z_WIDTH): 1024, (D_MODEL, D_MODEL): 1024, (D_MODEL, 2 * D_FF): 512, (D_FF, D_MODEL): 512}
LANES = 128
SUBLANES = 8
VMEM_LIMIT = 52 * 1024 * 1024


def _cparams(sem, flags=None):
    return pltpu.CompilerParams(dimension_semantics=sem, vmem_limit_bytes=VMEM_LIMIT, flags=flags)


def _gelu(x):
    return 0.5 * x * (1.0 + lax.erf(x * SQRT_HALF))


def _rms(x, gain):
    return x * lax.rsqrt(jnp.mean(x * x, axis=-1, keepdims=True) + EPS) * gain


def _norm_head_pairs(x, gain2):
    lo = lax.broadcasted_iota(jnp.int32, x.shape, 1) < B_HEAD_DIM
    x2 = x * x
    s_lo = jnp.sum(jnp.where(lo, x2, 0.0), axis=-1, keepdims=True)
    s_hi = jnp.sum(jnp.where(lo, 0.0, x2), axis=-1, keepdims=True)
    ms = jnp.where(lo, s_lo, s_hi) * (1.0 / B_HEAD_DIM)
    return x * lax.rsqrt(ms + EPS) * gain2


def _t5_bucket_np(dist):
    n = np.maximum(dist, 0)
    max_exact = N_BUCKETS // 2
    nf = np.maximum(n, 1).astype(np.float32)
    large = max_exact + (np.log(nf / np.float32(max_exact)) / np.float32(math.log(MAX_DISTANCE / max_exact))
                         * np.float32(N_BUCKETS - max_exact)).astype(np.int32)
    return np.where(n < max_exact, n, np.minimum(large, N_BUCKETS - 1)).astype(np.int32)


def _bias_kernel(rb_ref, bkp_ref, bks_ref, op_ref, os_ref):
    bkp = bkp_ref[...]
    bks = bks_ref[...]
    for h in range(B_HEADS):
        accp = jnp.zeros(bkp.shape, f32)
        accs = jnp.zeros(bks.shape, f32)
        for b in range(N_BUCKETS):
            val = rb_ref[b, h]
            accp = jnp.where(bkp == b, val, accp)
            accs = jnp.where(bks == b, val, accs)
        op_ref[h] = accp
        os_ref[h:h + 1, :] = accs[0:1, :]


def _bias_tables(rel_bias):
    cj = np.arange(WINDOW)[:, None]
    qi = np.arange(WINDOW)[None, :]
    bkp = _t5_bucket_np((qi - cj) % WINDOW)
    j = np.arange(2 * WINDOW)
    dist_s = np.where(j < WINDOW, WINDOW - j, 0)
    bks = np.broadcast_to(_t5_bucket_np(dist_s)[None, :], (SUBLANES, 2 * WINDOW)).copy()
    return pl.pallas_call(
        _bias_kernel,
        out_shape=(jax.ShapeDtypeStruct((B_HEADS, WINDOW, WINDOW), f32),
                   jax.ShapeDtypeStruct((B_HEADS, 2 * WINDOW), f32)),
        in_specs=[pl.BlockSpec(memory_space=pltpu.SMEM),
                  pl.BlockSpec(memory_space=pltpu.VMEM),
                  pl.BlockSpec(memory_space=pltpu.VMEM)],
        out_specs=(pl.BlockSpec(memory_space=pltpu.VMEM), pl.BlockSpec(memory_space=pltpu.VMEM)),
        name="bias_tables",
    )(rel_bias, jnp.asarray(bkp), jnp.asarray(bks))


def _mm_in_kernel(x_ref, g_ref, w_ref, z_ref, h_ref):
    @pl.when(pl.program_id(1) == 0)
    def _():
        h_ref[...] = _rms(x_ref[...], g_ref[...]).astype(bf16)

    z_ref[...] = jnp.dot(h_ref[...], w_ref[...], preferred_element_type=f32)


def _mm_in(x, gain, w, layer, tm):
    M = x.shape[0]
    tn = MM_TILE_N[w.shape[-2:]]
    return pl.pallas_call(
        _mm_in_kernel,
        grid=(M // tm, Z_WIDTH // tn),
        in_specs=[pl.BlockSpec((tm, D_MODEL), lambda i, j: (i, 0)),
                  pl.BlockSpec((None, 1, D_MODEL), lambda i, j: (layer, 0, 0)),
                  pl.BlockSpec((None, D_MODEL, tn), lambda i, j: (layer, 0, j))],
        out_specs=pl.BlockSpec((tm, tn), lambda i, j: (i, j)),
        out_shape=jax.ShapeDtypeStruct((M, Z_WIDTH), f32),
        scratch_shapes=[pltpu.VMEM((tm, D_MODEL), bf16)],
        compiler_params=_cparams(("parallel", "arbitrary")),
        name="mm_in",
    )(x, gain, w)


def _mm_out_kernel(a_ref, b_ref, c_ref, x_ref, w_ref, o_ref, lhs_ref):
    @pl.when(pl.program_id(1) == 0)
    def _():
        lhs_ref[:, 0:A_WIDTH] = a_ref[...]
        lhs_ref[:, A_WIDTH:A_WIDTH + B_WIDTH] = b_ref[...]
        lhs_ref[:, A_WIDTH + B_WIDTH:D_MODEL] = c_ref[...]

    o_ref[...] = x_ref[...] + jnp.dot(lhs_ref[...], w_ref[...], preferred_element_type=f32)


def _mm_out(a, b, c, x, w, layer, tm):
    M = x.shape[0]
    tn = MM_TILE_N[w.shape[-2:]]
    return pl.pallas_call(
        _mm_out_kernel,
        grid=(M // tm, D_MODEL // tn),
        in_specs=[pl.BlockSpec((tm, A_WIDTH), lambda i, j: (i, 0)),
                  pl.BlockSpec((tm, B_WIDTH), lambda i, j: (i, 0)),
                  pl.BlockSpec((tm, C_WIDTH), lambda i, j: (i, 0)),
                  pl.BlockSpec((tm, tn), lambda i, j: (i, j)),
                  pl.BlockSpec((None, D_MODEL, tn), lambda i, j: (layer, 0, j))],
        out_specs=pl.BlockSpec((tm, tn), lambda i, j: (i, j)),
        out_shape=jax.ShapeDtypeStruct((M, D_MODEL), f32),
        scratch_shapes=[pltpu.VMEM((tm, D_MODEL), bf16)],
        compiler_params=_cparams(("parallel", "arbitrary")),
        name="mm_out",
    )(a, b, c, x, w)


def _mm_down_kernel(a_ref, x_ref, w_ref, o_ref):
    o_ref[...] = x_ref[...] + jnp.dot(a_ref[...], w_ref[...], preferred_element_type=f32)


def _mm_down(act, x, w, layer, tm):
    M = x.shape[0]
    tn = MM_TILE_N[w.shape[-2:]]
    return pl.pallas_call(
        _mm_down_kernel,
        grid=(M // tm, D_MODEL // tn),
        in_specs=[pl.BlockSpec((tm, D_FF), lambda i, j: (i, 0)),
                  pl.BlockSpec((tm, tn), lambda i, j: (i, j)),
                  pl.BlockSpec((None, D_FF, tn), lambda i, j: (layer, 0, j))],
        out_specs=pl.BlockSpec((tm, tn), lambda i, j: (i, j)),
        out_shape=jax.ShapeDtypeStruct((M, D_MODEL), f32),
        compiler_params=_cparams(("parallel", "arbitrary")),
        name="mm_down",
    )(act, x, w)


def _silu(x):
    return x * jax.nn.sigmoid(x)


def _mm_up_prompt_kernel(x_ref, n2_ref, wg_ref, wa_ref, cwg_ref, cwa_ref, cbg_ref, cba_ref,
                         act_ref, cs_ref, h_ref, carry_ref, zb_ref, *, tm, nj, n_steps, tiles_per_seq):
    s = pl.program_id(0)
    sa = jnp.minimum(s, n_steps - 1)
    ia = sa // nj
    ja = sa % nj
    tf = act_ref.shape[1]

    @pl.when(s == 0)
    def _():
        zb_ref[...] = jnp.zeros(zb_ref.shape, f32)
        carry_ref[...] = jnp.zeros(carry_ref.shape, f32)

    @pl.when(ja == 0)
    def _():
        h_ref[...] = _rms(x_ref[...], n2_ref[...]).astype(bf16)

    chunks = [slice(c0, c0 + UP_CHUNK) for c0 in range(0, tf, UP_CHUNK)]

    rblocks = [(r0, min(UP_ROWS, tm - r0)) for r0 in range(0, tm, UP_ROWS)]

    def conv(idx, cw_ref, cb_ref, cs, r0, nr):
        zz = zb_ref[idx, r0:r0 + SUBLANES + nr, cs]
        z1 = pltpu.roll(zz, 1, 0)[SUBLANES:]
        z2 = pltpu.roll(zz, 2, 0)[SUBLANES:]
        return (cb_ref[:, cs] + z2 * cw_ref[0:1, cs] + z1 * cw_ref[1:2, cs] + zz[SUBLANES:] * cw_ref[2:3, cs])

    for cs in chunks:
        for r0, nr in rblocks:
            g = conv(0, cwg_ref, cbg_ref, cs, r0, nr)
            a = conv(1, cwa_ref, cba_ref, cs, r0, nr)
            act_ref[r0:r0 + nr, cs] = (_silu(g) * a).astype(bf16)

    seq_start = (ia % tiles_per_seq) == 0
    for cs in chunks:
        for r0, nr in rblocks:
            for idx, w_ref in ((0, wg_ref), (1, wa_ref)):
                z = jnp.dot(h_ref[r0:r0 + nr, :], w_ref[:, cs], preferred_element_type=f32)
                if r0 == 0:
                    zb_ref[idx, 0:SUBLANES, cs] = jnp.where(seq_start, 0.0, carry_ref[idx, ja, :, cs])
                zb_ref[idx, SUBLANES + r0:SUBLANES + r0 + nr, cs] = z
                if r0 + nr == tm:
                    tail = z[nr - SUBLANES:nr, :]
                    carry_ref[idx, ja, :, cs] = tail
                    cs_ref[0, idx, :, cs] = tail


def _mm_up_prompt(x, norm2, w_up, conv_w, conv_b, layer, batch, tm):
    M = x.shape[0]
    seq = M // batch
    tiles_per_seq = seq // tm
    tf = MM_TILE_N[w_up.shape[-2:]]
    nj = D_FF // tf
    n_steps = (M // tm) * nj
    kern = functools.partial(_mm_up_prompt_kernel, tm=tm, nj=nj, n_steps=n_steps, tiles_per_seq=tiles_per_seq)
    ia = lambda s: jnp.minimum(s, n_steps - 1) // nj
    ja = lambda s: jnp.minimum(s, n_steps - 1) % nj
    ib = lambda s: jnp.maximum(s - 1, 0) // nj
    jb = lambda s: jnp.maximum(s - 1, 0) % nj
    return pl.pallas_call(
        kern,
        grid=(n_steps + 1,),
        in_specs=[pl.BlockSpec((tm, D_MODEL), lambda s: (ia(s), 0)),
                  pl.BlockSpec((None, 1, D_MODEL), lambda s: (layer, 0, 0)),
                  pl.BlockSpec((None, D_MODEL, tf), lambda s: (layer, 0, ja(s))),
                  pl.BlockSpec((None, D_MODEL, tf), lambda s: (layer, 0, nj + ja(s))),
                  pl.BlockSpec((None, CONV_W, tf), lambda s: (layer, 0, jb(s))),
                  pl.BlockSpec((None, CONV_W, tf), lambda s: (layer, 0, nj + jb(s))),
                  pl.BlockSpec((None, 1, tf), lambda s: (layer, 0, jb(s))),
                  pl.BlockSpec((None, 1, tf), lambda s: (layer, 0, nj + jb(s)))],
        out_specs=(pl.BlockSpec((tm, tf), lambda s: (ib(s), jb(s))),
                   pl.BlockSpec((1, 2, SUBLANES, tf), lambda s: (ia(s), 0, 0, ja(s)))),
        out_shape=(jax.ShapeDtypeStruct((M, D_FF), bf16),
                   jax.ShapeDtypeStruct((M // tm, 2, SUBLANES, D_FF), f32)),
        scratch_shapes=[pltpu.VMEM((tm, D_MODEL), bf16),
                        pltpu.VMEM((2, nj, SUBLANES, tf), f32),
                        pltpu.VMEM((2, tm + SUBLANES, tf), f32)],
        compiler_params=_cparams(("arbitrary",)),
        name="mm_up_prompt",
    )(x, norm2, w_up, w_up, conv_w, conv_w, conv_b, conv_b)


def _mm_up_sample_kernel(x_ref, n2_ref, wg_ref, wa_ref, cwg_ref, cwa_ref, cbg_ref, cba_ref,
                         b0g_ref, b1g_ref, b0a_ref, b1a_ref, act_ref, zg_ref, za_ref, h_ref):
    @pl.when(pl.program_id(0) == 0)
    def _():
        h_ref[...] = _rms(x_ref[...], n2_ref[...]).astype(bf16)

    def conv_half(w_ref, cw_ref, cb_ref, b0_ref, b1_ref, z_ref):
        z = jnp.dot(h_ref[...], w_ref[...], preferred_element_type=f32)
        z_ref[...] = z
        return (cb_ref[...] + b0_ref[...] * cw_ref[0:1, :] + b1_ref[...] * cw_ref[1:2, :]
                + z * cw_ref[2:3, :])

    g = conv_half(wg_ref, cwg_ref, cbg_ref, b0g_ref, b1g_ref, zg_ref)
    a = conv_half(wa_ref, cwa_ref, cba_ref, b0a_ref, b1a_ref, za_ref)
    act_ref[...] = (_silu(g) * a).astype(bf16)


def _mm_up_sample(x, norm2, w_up, conv_w, conv_b, buf, layer):
    M = x.shape[0]
    tf = MM_TILE_N[w_up.shape[-2:]]
    nj = D_FF // tf
    wspec = lambda off: pl.BlockSpec((None, D_MODEL, tf), lambda j: (layer, 0, off + j))
    cwspec = lambda off: pl.BlockSpec((None, CONV_W, tf), lambda j: (layer, 0, off + j))
    cbspec = lambda off: pl.BlockSpec((None, 1, tf), lambda j: (layer, 0, off + j))
    bufspec = lambda row, off: pl.BlockSpec((None, None, M, tf), lambda j: (layer, row, 0, off + j))
    return pl.pallas_call(
        _mm_up_sample_kernel,
        grid=(nj,),
        in_specs=[pl.BlockSpec((M, D_MODEL), lambda j: (0, 0)),
                  pl.BlockSpec((None, 1, D_MODEL), lambda j: (layer, 0, 0)),
                  wspec(0), wspec(nj), cwspec(0), cwspec(nj), cbspec(0), cbspec(nj),
                  bufspec(0, 0), bufspec(1, 0), bufspec(0, nj), bufspec(1, nj)],
        out_specs=(pl.BlockSpec((M, tf), lambda j: (0, j)),
                   pl.BlockSpec((M, tf), lambda j: (0, j)),
                   pl.BlockSpec((M, tf), lambda j: (0, j))),
        out_shape=(jax.ShapeDtypeStruct((M, D_FF), bf16),
                   jax.ShapeDtypeStruct((M, D_FF), f32),
                   jax.ShapeDtypeStruct((M, D_FF), f32)),
        scratch_shapes=[pltpu.VMEM((M, D_MODEL), bf16)],
        compiler_params=_cparams(("arbitrary",)),
        name="mm_up_sample",
    )(x, norm2, w_up, w_up, conv_w, conv_w, conv_b, conv_b, buf, buf, buf, buf)


def _mix_a_prompt_kernel(u_ref, v_ref, vg_ref, ws_ref, bs_ref, o_ref):
    row = lax.broadcasted_iota(jnp.int32, (CHUNK, CHUNK), 0)
    col = lax.broadcasted_iota(jnp.int32, (CHUNK, CHUNK), 1)
    tri = col <= row
    for g in range(A_GROUPS):
        sl = slice(g * A_DIM, (g + 1) * A_DIM)
        ws = jnp.where(tri, ws_ref[g], 0.0).astype(bf16)
        for r0 in range(0, u_ref.shape[0], CHUNK):
            rs = slice(r0, r0 + CHUNK)
            vn = _rms(_gelu(v_ref[rs, sl]), vg_ref[:, sl])
            mixv = jnp.dot(ws, vn.astype(bf16), preferred_element_type=f32) + bs_ref[:, sl]
            o_ref[rs, sl] = (_gelu(u_ref[rs, sl]) * mixv).astype(bf16)


def _mix_a_prompt(z, v_gain, w_s, b_s_full, layer, rows):
    M = z.shape[0]
    return pl.pallas_call(
        _mix_a_prompt_kernel,
        grid=(M // rows,),
        in_specs=[pl.BlockSpec((rows, A_WIDTH), lambda r: (r, Z_AU // A_WIDTH)),
                  pl.BlockSpec((rows, A_WIDTH), lambda r: (r, Z_AV // A_WIDTH)),
                  pl.BlockSpec((None, 1, A_WIDTH), lambda r: (layer, 0, 0)),
                  pl.BlockSpec((None, A_GROUPS, CHUNK, CHUNK), lambda r: (layer, 0, 0, 0)),
                  pl.BlockSpec((None, CHUNK, A_WIDTH), lambda r: (layer, 0, 0))],
        out_specs=pl.BlockSpec((rows, A_WIDTH), lambda r: (r, 0)),
        out_shape=jax.ShapeDtypeStruct((M, A_WIDTH), bf16),
        compiler_params=_cparams(("parallel",)),
        name="mix_a_prompt",
    )(z, z, v_gain, w_s, b_s_full)


def _mix_a_sample_kernel(u_ref, v_ref, vg_ref, ws_ref, bs_ref, o_ref, vn_ref):
    for g in range(A_GROUPS):
        sl = slice(g * A_DIM, (g + 1) * A_DIM)
        vn = _rms(_gelu(v_ref[:, sl]), vg_ref[:, sl])
        vn_ref[:, sl] = vn
        mixv = ws_ref[:, sl] * vn + bs_ref[:, sl]
        o_ref[:, sl] = (_gelu(u_ref[:, sl]) * mixv).astype(bf16)


def _mix_a_sample(z, v_gain, ws0, bs0, layer):
    M = z.shape[0]
    vec = pl.BlockSpec((None, 1, A_WIDTH), lambda r: (layer, 0, 0))
    return pl.pallas_call(
        _mix_a_sample_kernel,
        grid=(1,),
        in_specs=[pl.BlockSpec((M, A_WIDTH), lambda r: (0, Z_AU // A_WIDTH)),
                  pl.BlockSpec((M, A_WIDTH), lambda r: (0, Z_AV // A_WIDTH)),
                  vec, vec, vec],
        out_specs=(pl.BlockSpec((M, A_WIDTH), lambda r: (0, 0)),
                   pl.BlockSpec((M, A_WIDTH), lambda r: (0, 0))),
        out_shape=(jax.ShapeDtypeStruct((M, A_WIDTH), bf16),
                   jax.ShapeDtypeStruct((M, A_WIDTH), f32)),
        compiler_params=_cparams(("arbitrary",)),
        name="mix_a_sample",
    )(z, z, v_gain, ws0, bs0)


def _mix_b_prompt_kernel(sink_ref, q_ref, k_ref, v_ref, qg_ref, kg_ref, bias_ref,
                         o_ref, klast_ref, vlast_ref, kcat_ref, vcat_ref):
    i = pl.program_id(1)
    W = WINDOW
    kn = _norm_head_pairs(k_ref[...], kg_ref[...])
    v = v_ref[...]
    klast_ref[0] = kn
    vlast_ref[0] = v

    @pl.when(i == 0)
    def _():
        kcat_ref[0:W, :] = jnp.zeros((W, LANES), f32)
        vcat_ref[0:W, :] = jnp.zeros((W, LANES), f32)

    @pl.when(i > 0)
    def _():
        kcat_ref[0:W, :] = kcat_ref[W:2 * W, :]
        vcat_ref[0:W, :] = vcat_ref[W:2 * W, :]

    kcat_ref[W:2 * W, :] = kn
    vcat_ref[W:2 * W, :] = v
    kc = kcat_ref[...]
    vc = vcat_ref[...]
    kc_sw = pltpu.roll(kc, B_HEAD_DIM, 1)
    vc_sw = pltpu.roll(vc, B_HEAD_DIM, 1)
    lo = lax.broadcasted_iota(jnp.int32, (2 * W, LANES), 1) < B_HEAD_DIM

    cj = lax.broadcasted_iota(jnp.int32, (W, W), 0)
    qi = lax.broadcasted_iota(jnp.int32, (W, W), 1)
    own = cj <= qi
    keep = own | (i > 0)
    q_gain = qg_ref[...] * (B_HEAD_DIM ** -0.5)

    tiles = GQA // 2
    nt = (((1,), (1,)), ((), ()))
    tn = (((0,), (0,)), ((), ()))
    scores, values = [], []
    for kvh in range(B_KV_HEADS):
        k_src, k_alt = (kc, kc_sw) if kvh == 0 else (kc_sw, kc)
        v_src, v_alt = (vc, vc_sw) if kvh == 0 else (vc_sw, vc)
        k_even = jnp.where(lo, k_src, 0.0).astype(bf16)
        k_odd = jnp.where(lo, 0.0, k_alt).astype(bf16)
        values.append((jnp.where(lo, v_src, 0.0).astype(bf16), jnp.where(lo, 0.0, v_alt).astype(bf16)))
        qts = []
        for t in range(tiles):
            c0 = (kvh * tiles + t) * LANES
            qt = q_ref[:, c0:c0 + LANES].T
            q2 = qt * qt
            halves = []
            for r0 in (0, B_HEAD_DIM):
                ms = jnp.mean(q2[r0:r0 + B_HEAD_DIM, :], axis=0, keepdims=True)
                halves.append(qt[r0:r0 + B_HEAD_DIM, :] * lax.rsqrt(ms + EPS))
            qts.append(jnp.concatenate(halves, axis=0) * q_gain)
        qst = jnp.concatenate(qts, axis=1).astype(bf16)
        scores.append((jnp.dot(k_even, qst, preferred_element_type=f32),
                       jnp.dot(k_odd, qst, preferred_element_type=f32)))
    for kvh in range(B_KV_HEADS):
        p_par = []
        for par in range(2):
            blocks = []
            for t in range(tiles):
                h = kvh * GQA + 2 * t + par
                sb = scores[kvh][par][:, t * W:(t + 1) * W]
                s = jnp.where(own, sb[W:2 * W, :], sb[0:W, :]) + bias_ref[h]
                s = jnp.where(keep, s, NEG_INF)
                sk = sink_ref[h]
                mx = jnp.maximum(jnp.max(s, axis=0, keepdims=True), sk)
                p = jnp.exp(s - mx)
                den = jnp.sum(p, axis=0, keepdims=True) + jnp.exp(sk - mx)
                p = p * (1.0 / den)
                blocks.append(jnp.concatenate([jnp.where(own, 0.0, p), jnp.where(own, p, 0.0)],
                                              axis=0).astype(bf16))
            p_par.append(jnp.concatenate(blocks, axis=1))
        ot = (lax.dot_general(values[kvh][0], p_par[0], tn, preferred_element_type=f32)
              + lax.dot_general(values[kvh][1], p_par[1], tn, preferred_element_type=f32))
        for t in range(tiles):
            c0 = (kvh * tiles + t) * LANES
            o_ref[:, c0:c0 + LANES] = ot[:, t * W:(t + 1) * W].T.astype(bf16)


def _mix_b_prompt(z, sinks, q_gain_t, k_gain2, bias_p, layer, batch):
    M = z.shape[0]
    nb = M // batch // WINDOW
    return pl.pallas_call(
        _mix_b_prompt_kernel,
        grid=(batch, nb),
        in_specs=[pl.BlockSpec(memory_space=pltpu.SMEM),
                  pl.BlockSpec((WINDOW, B_WIDTH), lambda b, i: (b * nb + i, Z_BQ // B_WIDTH)),
                  pl.BlockSpec((WINDOW, B_KV_WIDTH), lambda b, i: (b * nb + i, Z_BK // B_KV_WIDTH)),
                  pl.BlockSpec((WINDOW, B_KV_WIDTH), lambda b, i: (b * nb + i, Z_BV // B_KV_WIDTH)),
                  pl.BlockSpec((None, LANES, WINDOW), lambda b, i: (layer, 0, 0)),
                  pl.BlockSpec((None, 1, LANES), lambda b, i: (layer, 0, 0)),
                  pl.BlockSpec((B_HEADS, WINDOW, WINDOW), lambda b, i: (0, 0, 0))],
        out_specs=(pl.BlockSpec((WINDOW, B_WIDTH), lambda b, i: (b * nb + i, 0)),
                   pl.BlockSpec((1, WINDOW, B_KV_WIDTH), lambda b, i: (b, 0, 0)),
                   pl.BlockSpec((1, WINDOW, B_KV_WIDTH), lambda b, i: (b, 0, 0))),
        out_shape=(jax.ShapeDtypeStruct((M, B_WIDTH), bf16),
                   jax.ShapeDtypeStruct((batch, WINDOW, B_KV_WIDTH), f32),
                   jax.ShapeDtypeStruct((batch, WINDOW, B_KV_WIDTH), f32)),
        scratch_shapes=[pltpu.VMEM((2 * WINDOW, LANES), f32),
                        pltpu.VMEM((2 * WINDOW, LANES), f32)],
        compiler_params=_cparams(("arbitrary", "arbitrary")),
        name="mix_b_prompt",
    )(sinks, z, z, z, q_gain_t, k_gain2, bias_p)


def _mix_c_prompt_kernel(q_ref, k_ref, v0_ref, v1_ref, o0_ref, o1_ref, g_ref, gb_ref, hgt_ref,
                         out_ref, c_out, n_out, m_out, c_s, n_s, m_s):
    L = MLSTM_L
    c = pl.program_id(1)

    @pl.when(c == 0)
    def _():
        c_s[...] = jnp.zeros(c_s.shape, f32)
        n_s[...] = jnp.zeros(n_s.shape, f32)
        m_s[...] = jnp.zeros(m_s.shape, f32)

    row = lax.broadcasted_iota(jnp.int32, (L, L), 0)
    col = lax.broadcasted_iota(jnp.int32, (L, L), 1)
    tri = col <= row
    tri_f = tri.astype(f32)
    seqs = range(q_ref.shape[0])
    units = [(bi, h) for bi in seqs for h in range(C_HEADS)]
    g, b = {}, {}
    for bi in seqs:
        g[bi] = g_ref[bi] + gb_ref[...]
        b[bi] = jnp.dot(tri_f, jax.nn.log_sigmoid(g[bi]), precision=lax.Precision.HIGHEST,
                        preferred_element_type=f32)
    bT = {bi: b[bi].T for bi in seqs}
    vis = row <= col
    nt = (((1,), (1,)), ((), ()))
    tn = (((0,), (0,)), ((), ()))
    qb, kh, vb, ct_prev, n_prev, st, qct, qn_lin = {}, {}, {}, {}, {}, {}, {}, {}
    for u in units:
        bi, h = u
        v_ref = v0_ref if h < 2 else v1_ref
        vsl = slice((h % 2) * C_V_DIM, (h % 2 + 1) * C_V_DIM)
        qb[u] = q_ref[bi, :, h * C_QK_DIM:(h + 1) * C_QK_DIM].astype(bf16)
        kh[u] = k_ref[bi, :, h * C_QK_DIM:(h + 1) * C_QK_DIM] * (C_QK_DIM ** -0.5)
        vb[u] = v_ref[bi, :, vsl].astype(bf16)
        ct_prev[u] = c_s[bi, h]
        n_prev[u] = n_s[bi, h]
        st[u] = lax.dot_general(kh[u].astype(bf16), qb[u], nt, preferred_element_type=f32)
        qct[u] = lax.dot_general(ct_prev[u].astype(bf16), qb[u], nt, preferred_element_type=f32)
        n8 = jnp.broadcast_to(n_prev[u], (SUBLANES, C_QK_DIM)).astype(bf16)
        qn_lin[u] = lax.dot_general(n8, qb[u], nt, preferred_element_type=f32)[0:1, :]
    pt, iw, inv, kw, decay, m_new = {}, {}, {}, {}, {}, {}
    for u in units:
        bi, h = u
        b_row = bT[bi][C_HEADS + h:C_HEADS + h + 1, :]
        src = g[bi][:, h:h + 1] - b[bi][:, C_HEADS + h:C_HEADS + h + 1]
        dlog = jnp.where(vis, b_row + src, -jnp.inf)
        m_prev = m_s[bi, h][0:1, 0:1]
        inter = b_row + m_prev
        mt = jnp.maximum(inter, jnp.max(dlog, axis=0, keepdims=True))
        dt = jnp.exp(dlog - mt)
        pt[u] = st[u] * dt
        iw[u] = jnp.exp(inter - mt)
        qn = iw[u] * qn_lin[u] + jnp.sum(pt[u], axis=0, keepdims=True)
        inv[u] = 1.0 / jnp.maximum(jnp.abs(qn), jnp.exp(-mt))
        m_new[u] = mt[:, L - 1:L]
        decay[u] = jnp.exp(b_row[:, L - 1:L] + m_prev - m_new[u])
        kw[u] = kh[u] * dt[:, L - 1:L]
    svt = {u: lax.dot_general(vb[u], pt[u].astype(bf16), tn, preferred_element_type=f32) for u in units}
    kvt = {u: lax.dot_general(vb[u], kw[u].astype(bf16), tn, preferred_element_type=f32) for u in units}
    for u in units:
        bi, h = u
        o_ref = o0_ref if h < 2 else o1_ref
        vsl = slice((h % 2) * C_V_DIM, (h % 2 + 1) * C_V_DIM)
        ht = (iw[u] * qct[u] + svt[u]) * inv[u]
        ms = jnp.mean(ht * ht, axis=0, keepdims=True)
        hn = (ht * lax.rsqrt(ms + EPS) * hgt_ref[h]).T
        c_s[bi, h] = decay[u] * ct_prev[u] + kvt[u]
        n_s[bi, h] = decay[u] * n_prev[u] + jnp.sum(kw[u], axis=0, keepdims=True)
        m_s[bi, h] = jnp.broadcast_to(m_new[u], (SUBLANES, LANES))
        osl = slice(h * C_V_DIM, (h + 1) * C_V_DIM)
        out_ref[bi, :, osl] = (hn * jax.nn.sigmoid(o_ref[bi, :, vsl])).astype(bf16)

    @pl.when(c == pl.num_programs(1) - 1)
    def _():
        eye = (row == col).astype(f32)
        for bi, h in units:
            c_out[bi, h] = lax.dot_general(c_s[bi, h], eye, tn, precision=lax.Precision.HIGHEST,
                                           preferred_element_type=f32)
        n_out[...] = n_s[...]
        m_out[...] = m_s[...]


def _mix_c_prompt(z, gate_bias, h_gain, layer, batch):
    M = z.shape[0]
    L = MLSTM_L
    T = M // batch
    z3 = z.reshape(batch, T, Z_WIDTH)
    ns = C_SEQS if batch % C_SEQS == 0 else 1
    col = lambda width, off: pl.BlockSpec((ns, L, width), lambda b, c: (b, c, off // width))
    state = lambda *shape: pl.BlockSpec((ns,) + shape, lambda b, c: (b,) + (0,) * len(shape))
    out, c_st, n_st, m_st = pl.pallas_call(
        _mix_c_prompt_kernel,
        grid=(batch // ns, T // L),
        in_specs=[col(C_QK_WIDTH, Z_CQ), col(C_QK_WIDTH, Z_CK),
                  col(C_PAIR, Z_CV), col(C_PAIR, Z_CV + C_PAIR),
                  col(C_PAIR, Z_CO), col(C_PAIR, Z_CO + C_PAIR),
                  col(LANES, Z_GATE),
                  pl.BlockSpec((None, 1, LANES), lambda b, c: (layer, 0, 0)),
                  pl.BlockSpec((None, C_HEADS, C_V_DIM, L), lambda b, c: (layer, 0, 0, 0))],
        out_specs=(pl.BlockSpec((ns, L, C_WIDTH), lambda b, c: (b, c, 0)),
                   state(C_HEADS, C_QK_DIM, C_V_DIM),
                   state(C_HEADS, 1, C_QK_DIM),
                   state(C_HEADS, SUBLANES, LANES)),
        out_shape=(jax.ShapeDtypeStruct((batch, T, C_WIDTH), bf16),
                   jax.ShapeDtypeStruct((batch, C_HEADS, C_QK_DIM, C_V_DIM), f32),
                   jax.ShapeDtypeStruct((batch, C_HEADS, 1, C_QK_DIM), f32),
                   jax.ShapeDtypeStruct((batch, C_HEADS, SUBLANES, LANES), f32)),
        scratch_shapes=[pltpu.VMEM((ns, C_HEADS, C_V_DIM, C_QK_DIM), f32),
                        pltpu.VMEM((ns, C_HEADS, 1, C_QK_DIM), f32),
                        pltpu.VMEM((ns, C_HEADS, SUBLANES, LANES), f32)],
        compiler_params=_cparams(("arbitrary", "arbitrary")),
        name="mix_c_prompt",
    )(z3, z3, z3, z3, z3, z3, z3, gate_bias, h_gain)
    return out.reshape(M, C_WIDTH), c_st, n_st, m_st


_BC_SHARED = (4, 5, 6, 7, 12, 13)


def _mix_bc_sample_kernel(*refs):
    rows = [_mix_bc_sample_row(*[ref if k in _BC_SHARED else ref.at[pl.ds(r, 1)] for k, ref in enumerate(refs)])
            for r in range(refs[0].shape[0])]
    while rows:
        rows = [row for row in rows if next(row, True) is None]


def _mix_bc_sample_row(zr_ref, q_ref, kp_ref, vp_ref, qg_ref, kg_ref, sink_ref, bias_ref,
                       cq_ref, ck_ref, cv_ref, co_ref, gb_ref, hg_ref, c0_ref, n0_ref, m0_ref,
                       bo_ref, kn_ref, co_out, c_out, n_out, m_out):
    W = WINDOW
    zr = zr_ref[0]

    qn = _rms(q_ref[0], qg_ref[...])
    hrow = lax.broadcasted_iota(jnp.int32, (B_HEADS, LANES), 0)
    lane = lax.broadcasted_iota(jnp.int32, (B_HEADS, LANES), 1)
    own = (hrow < GQA) == (lane < B_HEAD_DIM)
    q2 = jnp.where(own, jnp.concatenate([qn, qn], axis=-1), 0.0)
    k_new = _norm_head_pairs(zr[:, Z_BK:Z_BK + B_KV_WIDTH], kg_ref[...])
    v_new = zr[:, Z_BV:Z_BV + B_KV_WIDTH]
    kn_ref[0] = k_new
    scale = B_HEAD_DIM ** -0.5
    s_past = lax.dot_general(q2.astype(bf16), kp_ref[0].astype(bf16), (((1,), (1,)), ((), ())),
                             preferred_element_type=f32) * scale + bias_ref[:, 0:W]
    yield
    s_past = jnp.where(lane >= 1, s_past, NEG_INF)
    s_new = jnp.sum(q2 * k_new, axis=-1, keepdims=True) * scale + bias_ref[:, W:W + 1]
    sk = sink_ref[...]
    mx = jnp.maximum(jnp.maximum(jnp.max(s_past, axis=-1, keepdims=True), s_new), sk)
    p_past = jnp.exp(s_past - mx)
    p_new = jnp.exp(s_new - mx)
    den = jnp.sum(p_past, axis=-1, keepdims=True) + p_new + jnp.exp(sk - mx)
    inv = 1.0 / den
    o2 = (jnp.dot((p_past * inv).astype(bf16), vp_ref[0].astype(bf16), preferred_element_type=f32)
          + (p_new * inv) * v_new)
    o2_sw = pltpu.roll(o2, B_HEAD_DIM, 1)
    bo_ref[0] = jnp.where(hrow < GQA, o2, o2_sw)[:, 0:B_HEAD_DIM].astype(bf16)

    g = zr[:, Z_GATE:Z_GATE + LANES] + gb_ref[...]
    lf = jax.nn.log_sigmoid(g)
    m0 = m0_ref[0]
    lane1 = lax.broadcasted_iota(jnp.int32, (1, LANES), 1)
    m_row = jnp.zeros((1, LANES), f32)
    row8 = lax.broadcasted_iota(jnp.int32, (SUBLANES, C_QK_DIM), 0)
    c_rows = []
    staged = []
    for h in range(C_HEADS):
        ig = g[:, h:h + 1]
        b = lf[:, C_HEADS + h:C_HEADS + h + 1]
        m_prev = m0[:, h:h + 1]
        inter = b + m_prev
        mt = jnp.maximum(inter, ig)
        qh = cq_ref[0, h:h + 1, :]
        kh = ck_ref[0, h:h + 1, :] * (C_QK_DIM ** -0.5)
        vh = cv_ref[0, h:h + 1, :]
        c_prev = c0_ref[0, h]
        n_prev = n0_ref[0, h:h + 1, :]
        s = jnp.sum(qh * kh, axis=-1, keepdims=True) * jnp.exp(ig - mt)
        iw = jnp.exp(inter - mt)
        q8 = jnp.broadcast_to(qh, (SUBLANES, C_QK_DIM)).astype(bf16)
        qc = jnp.dot(q8, c_prev.astype(bf16), preferred_element_type=f32)[0:1, :]
        staged.append((ig, inter, mt, qh, kh, vh, c_prev, n_prev, s, iw, qc))
    yield
    for h, (ig, inter, mt, qh, kh, vh, c_prev, n_prev, s, iw, qc) in enumerate(staged):
        num = iw * qc + s * vh
        qn_ = iw * jnp.sum(qh * n_prev, axis=-1, keepdims=True) + s
        hh = num / jnp.maximum(jnp.abs(qn_), jnp.exp(-mt))
        wc = jnp.exp(ig - mt)
        decay = jnp.exp(inter - mt)
        kw = kh * wc
        kw8 = jnp.where(row8 == 0, jnp.broadcast_to(kw, (SUBLANES, C_QK_DIM)), 0.0).astype(bf16)
        v8 = jnp.broadcast_to(vh, (SUBLANES, C_V_DIM)).astype(bf16)
        c_out[0, h] = decay * c_prev + lax.dot_general(kw8, v8, (((0,), (0,)), ((), ())),
                                                      preferred_element_type=f32)
        n_out[0, h:h + 1, :] = decay * n_prev + kw
        m_row = jnp.where(lane1 == h, mt, m_row)
        c_rows.append(_rms(hh, hg_ref[h:h + 1, :]) * jax.nn.sigmoid(co_ref[0, h:h + 1, :]))
    co_out[0] = jnp.concatenate(c_rows, axis=0).astype(bf16)
    m_out[0] = m_row


def _mix_bc_sample(z, k_past, v_past, q_gain, k_gain2, sinks, bias_s, gate_bias, h_gain4,
                   c0, n0, m0, layer):
    R = z.shape[0]
    zr = z.reshape(R, 1, Z_WIDTH)
    q = z[:, Z_BQ:Z_BQ + B_WIDTH].reshape(R, B_HEADS, B_HEAD_DIM)
    cq = z[:, Z_CQ:Z_CQ + C_QK_WIDTH].reshape(R, C_HEADS, C_QK_DIM)
    ck = z[:, Z_CK:Z_CK + C_QK_WIDTH].reshape(R, C_HEADS, C_QK_DIM)
    cv = z[:, Z_CV:Z_CV + C_WIDTH].reshape(R, C_HEADS, C_V_DIM)
    co = z[:, Z_CO:Z_CO + C_WIDTH].reshape(R, C_HEADS, C_V_DIM)
    rb = DEC_ROWS if R % DEC_ROWS == 0 else 1
    row3 = lambda n: pl.BlockSpec((rb, 1, n), lambda r: (r, 0, 0))
    return pl.pallas_call(
        _mix_bc_sample_kernel,
        grid=(R // rb,),
        in_specs=[row3(Z_WIDTH),
                  pl.BlockSpec((rb, B_HEADS, B_HEAD_DIM), lambda r: (r, 0, 0)),
                  pl.BlockSpec((None, rb, WINDOW, B_KV_WIDTH), lambda r: (layer, r, 0, 0)),
                  pl.BlockSpec((None, rb, WINDOW, B_KV_WIDTH), lambda r: (layer, r, 0, 0)),
                  pl.BlockSpec((None, 1, B_HEAD_DIM), lambda r: (layer, 0, 0)),
                  pl.BlockSpec((None, 1, LANES), lambda r: (layer, 0, 0)),
                  pl.BlockSpec((None, B_HEADS, 1), lambda r: (layer, 0, 0)),
                  pl.BlockSpec((B_HEADS, 2 * WINDOW), lambda r: (0, 0)),
                  pl.BlockSpec((rb, C_HEADS, C_QK_DIM), lambda r: (r, 0, 0)),
                  pl.BlockSpec((rb, C_HEADS, C_QK_DIM), lambda r: (r, 0, 0)),
                  pl.BlockSpec((rb, C_HEADS, C_V_DIM), lambda r: (r, 0, 0)),
                  pl.BlockSpec((rb, C_HEADS, C_V_DIM), lambda r: (r, 0, 0)),
                  pl.BlockSpec((None, 1, LANES), lambda r: (layer, 0, 0)),
                  pl.BlockSpec((None, C_HEADS, C_V_DIM), lambda r: (layer, 0, 0)),
                  pl.BlockSpec((None, rb, C_HEADS, C_QK_DIM, C_V_DIM), lambda r: (layer, r, 0, 0, 0)),
                  pl.BlockSpec((None, rb, C_HEADS, C_QK_DIM), lambda r: (layer, r, 0, 0)),
                  pl.BlockSpec((None, rb, 1, C_HEADS), lambda r: (layer, r, 0, 0))],
        out_specs=(pl.BlockSpec((rb, B_HEADS, B_HEAD_DIM), lambda r: (r, 0, 0)),
                   row3(B_KV_WIDTH),
                   pl.BlockSpec((rb, C_HEADS, C_V_DIM), lambda r: (r, 0, 0)),
                   pl.BlockSpec((rb, C_HEADS, C_QK_DIM, C_V_DIM), lambda r: (r, 0, 0, 0)),
                   pl.BlockSpec((rb, C_HEADS, C_QK_DIM), lambda r: (r, 0, 0)),
                   row3(LANES)),
        out_shape=(jax.ShapeDtypeStruct((R, B_HEADS, B_HEAD_DIM), bf16),
                   jax.ShapeDtypeStruct((R, 1, B_KV_WIDTH), f32),
                   jax.ShapeDtypeStruct((R, C_HEADS, C_V_DIM), bf16),
                   jax.ShapeDtypeStruct((R, C_HEADS, C_QK_DIM, C_V_DIM), f32),
                   jax.ShapeDtypeStruct((R, C_HEADS, C_QK_DIM), f32),
                   jax.ShapeDtypeStruct((R, 1, LANES), f32)),
        compiler_params=_cparams(("parallel",)),
        name="mix_bc_sample",
    )(zr, q, k_past, v_past, q_gain, k_gain2, sinks, bias_s, cq, ck, cv, co, gate_bias, h_gain4,
      c0, n0, m0)


def _prep_w_in(w_in):
    pad = jnp.zeros(w_in.shape[:-1] + (Z_WIDTH - w_in.shape[-1],), bf16)
    return jnp.concatenate([w_in.astype(bf16), pad], axis=-1)


def _row_tile(m, pref):
    return pref if m % pref == 0 else m


def kernel(x_prompt, x_sample, cache_swa_k, cache_swa_v, state_mlstm_C, state_mlstm_n, state_mlstm_m, state_ffn_conv, rel_bias, norm1, w_in, a_v_gain, a_spatial_w, a_spatial_b, b_q_gain, b_k_gain, b_sinks, c_gate_bias, c_h_gain, w_out, norm2, w_up, ffn_conv_w, ffn_conv_b, w_down):
    depth = w_in.shape[0]
    Bp, T, _ = x_prompt.shape
    R = x_sample.shape[0]
    assert x_sample.shape[1] == 1 and T % CHUNK == 0 and T % MLSTM_L == 0

    w_in_p = _prep_w_in(w_in)
    w_out_b = w_out.astype(bf16)
    w_up_b = w_up.astype(bf16)
    w_down_b = w_down.astype(bf16)
    norm1_3 = norm1.reshape(depth, 1, D_MODEL)
    norm2_3 = norm2.reshape(depth, 1, D_MODEL)
    v_gain3 = a_v_gain.reshape(depth, 1, A_WIDTH)
    bs_full = jnp.repeat(jnp.swapaxes(a_spatial_b, 1, 2), A_DIM, axis=-1)
    ws0 = jnp.repeat(a_spatial_w[:, :, 0, 0], A_DIM, axis=-1).reshape(depth, 1, A_WIDTH)
    bs0 = jnp.repeat(a_spatial_b[:, :, 0], A_DIM, axis=-1).reshape(depth, 1, A_WIDTH)
    q_gain3 = b_q_gain.reshape(depth, 1, B_HEAD_DIM)
    q_gain_t = jnp.broadcast_to(jnp.tile(b_q_gain, (1, 2)).reshape(depth, LANES, 1), (depth, LANES, WINDOW))
    k_gain2 = jnp.tile(b_k_gain, (1, 2)).reshape(depth, 1, LANES)
    sinks3 = b_sinks.reshape(depth, B_HEADS, 1)
    gate_b = jnp.pad(c_gate_bias, ((0, 0), (0, LANES - 2 * C_HEADS))).reshape(depth, 1, LANES)
    h_gain_t = jnp.broadcast_to(c_h_gain.reshape(depth, C_HEADS, C_V_DIM, 1), (depth, C_HEADS, C_V_DIM, MLSTM_L))
    h_gain4 = c_h_gain.reshape(depth, C_HEADS, C_V_DIM)
    conv_b3 = ffn_conv_b.reshape(depth, 1, 2 * D_FF)
    k_cache = cache_swa_k.reshape(depth, R, WINDOW, B_KV_WIDTH)
    v_cache = cache_swa_v.reshape(depth, R, WINDOW, B_KV_WIDTH)
    m_state = state_mlstm_m.reshape(depth, R, 1, C_HEADS)
    conv_hist = jnp.swapaxes(state_ffn_conv, 1, 2)

    bias_p, bias_s = _bias_tables(rel_bias)

    Mp = Bp * T
    tm = _row_tile(T, 1024)
    a_rows = _row_tile(T, 512)
    xp = x_prompt.reshape(Mp, D_MODEL)
    xs = x_sample.reshape(R, D_MODEL)
    P = [[] for _ in range(6)]
    S = [[] for _ in range(7)]
    for l in range(depth):
        z = _mm_in(xp, norm1_3, w_in_p, l, tm)
        a_o = _mix_a_prompt(z, v_gain3, a_spatial_w, bs_full, l, a_rows)
        b_o, k_last, v_last = _mix_b_prompt(z, b_sinks[l], q_gain_t, k_gain2, bias_p, l, Bp)
        c_o, c_st, n_st, m_st = _mix_c_prompt(z, gate_b, h_gain_t, l, Bp)
        x1 = _mm_out(a_o, b_o, c_o, xp, w_out_b, l, tm)
        act, cs = _mm_up_prompt(x1, norm2_3, w_up_b, ffn_conv_w, conv_b3, l, Bp, tm)
        xp = _mm_down(act, x1, w_down_b, l, tm)
        P[0].append(k_last.reshape(Bp, WINDOW, B_KV_HEADS, B_HEAD_DIM))
        P[1].append(v_last.reshape(Bp, WINDOW, B_KV_HEADS, B_HEAD_DIM))
        P[2].append(c_st)
        P[3].append(n_st.reshape(Bp, C_HEADS, C_QK_DIM))
        P[4].append(m_st[:, :, 0, 0])
        seq_tiles = T // tm
        tail = cs[seq_tiles - 1::seq_tiles, :, SUBLANES - (CONV_W - 1):, :]
        P[5].append(jnp.swapaxes(tail, 1, 2).reshape(Bp, CONV_W - 1, 2 * D_FF))

        zs = _mm_in(xs, norm1_3, w_in_p, l, R)
        a_s, vn_s = _mix_a_sample(zs, v_gain3, ws0, bs0, l)
        b_s, kn_s, c_s, c_new, n_new, m_new = _mix_bc_sample(
            zs, k_cache, v_cache, q_gain3, k_gain2, sinks3, bias_s, gate_b, h_gain4,
            state_mlstm_C, state_mlstm_n, m_state, l)
        x1s = _mm_out(a_s, b_s.reshape(R, B_WIDTH), c_s.reshape(R, C_WIDTH), xs, w_out_b, l, R)
        act_s, zg_s, za_s = _mm_up_sample(x1s, norm2_3, w_up_b, ffn_conv_w, conv_b3, conv_hist, l)
        xs = _mm_down(act_s, x1s, w_down_b, l, R)
        S[0].append(vn_s.reshape(R, 1, A_WIDTH))
        S[1].append(kn_s.reshape(R, 1, B_KV_HEADS, B_HEAD_DIM))
        S[2].append(zs[:, Z_BV:Z_BV + B_KV_WIDTH].reshape(R, 1, B_KV_HEADS, B_HEAD_DIM))
        S[3].append(c_new)
        S[4].append(n_new)
        S[5].append(m_new[:, 0, 0:C_HEADS])
        z_new = jnp.concatenate([zg_s, za_s], axis=-1)
        S[6].append(jnp.stack([state_ffn_conv[l][:, CONV_W - 2], z_new], axis=1))

    st = lambda lst: jnp.stack(lst, axis=0)
    return (xp.reshape(Bp, T, D_MODEL), xs.reshape(R, 1, D_MODEL),
            st(P[0]), st(P[1]), st(S[1]), st(S[2]),
            st(P[2]), st(P[3]), st(P[4]),
            st(S[3]), st(S[4]), st(S[5]),
            st(P[5]), st(S[6]),
            st(S[0]))
```

```python
import functools
import math

import numpy as np
import jax
import jax.numpy as jnp
from jax import lax
from jax.experimental import pallas as pl
from jax.experimental.pallas import tpu as pltpu

f32 = jnp.float32
bf16 = jnp.bfloat16

D_MODEL = 2048
EPS = 1e-6
NEG_INF = -1e30
SQRT_HALF = 0.7071067811865476

A_GROUPS = 4
A_DIM = 128
A_WIDTH = 512
CHUNK = 128
B_HEADS = 16
B_KV_HEADS = 2
B_HEAD_DIM = 64
GQA = 8
B_WIDTH = 1024
B_KV_WIDTH = 128
WINDOW = 128
N_BUCKETS = 32
MAX_DISTANCE = 128
C_HEADS = 4
C_QK_DIM = 64
C_V_DIM = 128
C_QK_WIDTH = 256
C_WIDTH = 512
IN_SPLITS = (A_WIDTH, A_WIDTH, B_WIDTH, B_KV_WIDTH, B_KV_WIDTH,
             C_QK_WIDTH, C_QK_WIDTH, C_WIDTH, C_WIDTH, C_HEADS, C_HEADS)
IN_OFFSETS = tuple(int(o) for o in np.cumsum(IN_SPLITS)[:-1])
D_FF = 5632
CONV_W = 3

Z_WIDTH = 4096
Z_AU, Z_AV, Z_BQ, Z_BK, Z_BV, Z_CQ, Z_CK, Z_CV, Z_CO, Z_GATE = (0,) + IN_OFFSETS[:9]
C_PAIR = 2 * C_V_DIM

MLSTM_L = 128
UP_CHUNK = 256
UP_ROWS = 256
C_SEQS = 2
DEC_ROWS = 8
MM_TILE_N = {(D_MODEL, Z_WIDTH): 1024, (D_MODEL, D_MODEL): 1024, (D_MODEL, 2 * D_FF): 512, (D_FF, D_MODEL): 512}
LANES = 128
SUBLANES = 8
VMEM_LIMIT = 52 * 1024 * 1024


def _cparams(sem, flags=None):
    return pltpu.CompilerParams(dimension_semantics=sem, vmem_limit_bytes=VMEM_LIMIT, flags=flags)


def _gelu(x):
    return 0.5 * x * (1.0 + lax.erf(x * SQRT_HALF))


def _rms(x, gain):
    return x * lax.rsqrt(jnp.mean(x * x, axis=-1, keepdims=True) + EPS) * gain


def _norm_head_pairs(x, gain2):
    lo = lax.broadcasted_iota(jnp.int32, x.shape, 1) < B_HEAD_DIM
    x2 = x * x
    s_lo = jnp.sum(jnp.where(lo, x2, 0.0), axis=-1, keepdims=True)
    s_hi = jnp.sum(jnp.where(lo, 0.0, x2), axis=-1, keepdims=True)
    ms = jnp.where(lo, s_lo, s_hi) * (1.0 / B_HEAD_DIM)
    return x * lax.rsqrt(ms + EPS) * gain2


def _t5_bucket_np(dist):
    n = np.maximum(dist, 0)
    max_exact = N_BUCKETS // 2
    nf = np.maximum(n, 1).astype(np.float32)
    large = max_exact + (np.log(nf / np.float32(max_exact)) / np.float32(math.log(MAX_DISTANCE / max_exact))
                         * np.float32(N_BUCKETS - max_exact)).astype(np.int32)
    return np.where(n < max_exact, n, np.minimum(large, N_BUCKETS - 1)).astype(np.int32)


def _bias_kernel(rb_ref, bkp_ref, bks_ref, op_ref, os_ref):
    bkp = bkp_ref[...]
    bks = bks_ref[...]
    for h in range(B_HEADS):
        accp = jnp.zeros(bkp.shape, f32)
        accs = jnp.zeros(bks.shape, f32)
        for b in range(N_BUCKETS):
            val = rb_ref[b, h]
            accp = jnp.where(bkp == b, val, accp)
            accs = jnp.where(bks == b, val, accs)
        op_ref[h] = accp
        os_ref[h:h + 1, :] = accs[0:1, :]


def _bias_tables(rel_bias):
    cj = np.arange(WINDOW)[:, None]
    qi = np.arange(WINDOW)[None, :]
    bkp = _t5_bucket_np((qi - cj) % WINDOW)
    j = np.arange(2 * WINDOW)
    dist_s = np.where(j < WINDOW, WINDOW - j, 0)
    bks = np.broadcast_to(_t5_bucket_np(dist_s)[None, :], (SUBLANES, 2 * WINDOW)).copy()
    return pl.pallas_call(
        _bias_kernel,
        out_shape=(jax.ShapeDtypeStruct((B_HEADS, WINDOW, WINDOW), f32),
                   jax.ShapeDtypeStruct((B_HEADS, 2 * WINDOW), f32)),
        in_specs=[pl.BlockSpec(memory_space=pltpu.SMEM),
                  pl.BlockSpec(memory_space=pltpu.VMEM),
                  pl.BlockSpec(memory_space=pltpu.VMEM)],
        out_specs=(pl.BlockSpec(memory_space=pltpu.VMEM), pl.BlockSpec(memory_space=pltpu.VMEM)),
        name="bias_tables",
    )(rel_bias, jnp.asarray(bkp), jnp.asarray(bks))


def _cast_job(src, layer, n_chunks, chunk_of_step, width=None):
    _, k, n = src.shape
    rows = k // n_chunks
    width = width or n
    return (pl.BlockSpec((None, rows, n), lambda *g: (layer, chunk_of_step(*g), 0)),
            pl.BlockSpec((rows, width), lambda *g: (chunk_of_step(*g), 0)),
            jax.ShapeDtypeStruct((k, width), bf16))


def _cast_chunk(src_ref, dst_ref):
    n = src_ref.shape[1]
    dst_ref[:, 0:n] = src_ref[...].astype(bf16)
    if dst_ref.shape[1] > n:
        dst_ref[:, n:] = jnp.zeros((dst_ref.shape[0], dst_ref.shape[1] - n), bf16)


def _mm_in_kernel(*refs, cast):
    if cast:
        x_ref, g_ref, w_ref, src_ref, z_ref, dst_ref, h_ref = refs
        _cast_chunk(src_ref, dst_ref)
    else:
        x_ref, g_ref, w_ref, z_ref, h_ref = refs

    @pl.when(pl.program_id(1) == 0)
    def _():
        h_ref[...] = _rms(x_ref[...], g_ref[...]).astype(bf16)

    z_ref[...] = jnp.dot(h_ref[...], w_ref[...], preferred_element_type=f32)


def _mm_in(x, gain, w, layer, tm, cast=None):
    M = x.shape[0]
    tn = MM_TILE_N[w.shape]
    nj = Z_WIDTH // tn
    in_specs = [pl.BlockSpec((tm, D_MODEL), lambda i, j: (i, 0)),
                pl.BlockSpec((None, 1, D_MODEL), lambda i, j: (layer, 0, 0)),
                pl.BlockSpec((D_MODEL, tn), lambda i, j: (0, j))]
    out_specs = [pl.BlockSpec((tm, tn), lambda i, j: (i, j))]
    out_shape = [jax.ShapeDtypeStruct((M, Z_WIDTH), f32)]
    args = [x, gain, w]
    if cast:
        job = _cast_job(cast[0], cast[1], (M // tm) * nj, lambda i, j: i * nj + j)
        in_specs.append(job[0]); out_specs.append(job[1]); out_shape.append(job[2]); args.append(cast[0])
    out = pl.pallas_call(
        functools.partial(_mm_in_kernel, cast=bool(cast)),
        grid=(M // tm, nj),
        in_specs=in_specs, out_specs=out_specs, out_shape=out_shape,
        scratch_shapes=[pltpu.VMEM((tm, D_MODEL), bf16)],
        compiler_params=_cparams(("arbitrary", "arbitrary")),
        name="mm_in",
    )(*args)
    return out if cast else out[0]


def _mm_out_kernel(a_ref, b_ref, c_ref, x_ref, w_ref, o_ref, lhs_ref):
    @pl.when(pl.program_id(1) == 0)
    def _():
        lhs_ref[:, 0:A_WIDTH] = a_ref[...]
        lhs_ref[:, A_WIDTH:A_WIDTH + B_WIDTH] = b_ref[...]
        lhs_ref[:, A_WIDTH + B_WIDTH:D_MODEL] = c_ref[...]

    o_ref[...] = x_ref[...] + jnp.dot(lhs_ref[...], w_ref[...], preferred_element_type=f32)


def _mm_out(a, b, c, x, w, tm):
    M = x.shape[0]
    tn = MM_TILE_N[w.shape]
    return pl.pallas_call(
        _mm_out_kernel,
        grid=(M // tm, D_MODEL // tn),
        in_specs=[pl.BlockSpec((tm, A_WIDTH), lambda i, j: (i, 0)),
                  pl.BlockSpec((tm, B_WIDTH), lambda i, j: (i, 0)),
                  pl.BlockSpec((tm, C_WIDTH), lambda i, j: (i, 0)),
                  pl.BlockSpec((tm, tn), lambda i, j: (i, j)),
                  pl.BlockSpec((D_MODEL, tn), lambda i, j: (0, j))],
        out_specs=pl.BlockSpec((tm, tn), lambda i, j: (i, j)),
        out_shape=jax.ShapeDtypeStruct((M, D_MODEL), f32),
        scratch_shapes=[pltpu.VMEM((tm, D_MODEL), bf16)],
        compiler_params=_cparams(("parallel", "arbitrary")),
        name="mm_out",
    )(a, b, c, x, w)


def _mm_down_kernel(*refs, cast):
    if cast:
        a_ref, x_ref, w_ref, src_ref, o_ref, dst_ref = refs
        _cast_chunk(src_ref, dst_ref)
    else:
        a_ref, x_ref, w_ref, o_ref = refs
    o_ref[...] = x_ref[...] + jnp.dot(a_ref[...], w_ref[...], preferred_element_type=f32)


def _mm_down(act, x, w, tm, cast=None, cast_width=None):
    M = x.shape[0]
    tn = MM_TILE_N[w.shape]
    nj = D_MODEL // tn
    in_specs = [pl.BlockSpec((tm, D_FF), lambda i, j: (i, 0)),
                pl.BlockSpec((tm, tn), lambda i, j: (i, j)),
                pl.BlockSpec((D_FF, tn), lambda i, j: (0, j))]
    out_specs = [pl.BlockSpec((tm, tn), lambda i, j: (i, j))]
    out_shape = [jax.ShapeDtypeStruct((M, D_MODEL), f32)]
    args = [act, x, w]
    if cast:
        job = _cast_job(cast[0], cast[1], (M // tm) * nj, lambda i, j: i * nj + j, cast_width)
        in_specs.append(job[0]); out_specs.append(job[1]); out_shape.append(job[2]); args.append(cast[0])
    out = pl.pallas_call(
        functools.partial(_mm_down_kernel, cast=bool(cast)),
        grid=(M // tm, nj),
        in_specs=in_specs, out_specs=out_specs, out_shape=out_shape,
        compiler_params=_cparams(("arbitrary", "arbitrary")),
        name="mm_down",
    )(*args)
    return out if cast else out[0]


def _silu(x):
    return x * jax.nn.sigmoid(x)


def _mm_up_prompt_kernel(x_ref, n2_ref, wg_ref, wa_ref, cwg_ref, cwa_ref, cbg_ref, cba_ref, src_ref,
                         act_ref, cs_ref, dst_ref, h_ref, carry_ref, zb_ref,
                         *, tm, nj, n_steps, tiles_per_seq):
    _cast_chunk(src_ref, dst_ref)
    s = pl.program_id(0)
    sa = jnp.minimum(s, n_steps - 1)
    ia = sa // nj
    ja = sa % nj
    tf = act_ref.shape[1]

    @pl.when(s == 0)
    def _():
        zb_ref[...] = jnp.zeros(zb_ref.shape, f32)
        carry_ref[...] = jnp.zeros(carry_ref.shape, f32)

    @pl.when(ja == 0)
    def _():
        h_ref[...] = _rms(x_ref[...], n2_ref[...]).astype(bf16)

    chunks = [slice(c0, c0 + UP_CHUNK) for c0 in range(0, tf, UP_CHUNK)]

    rblocks = [(r0, min(UP_ROWS, tm - r0)) for r0 in range(0, tm, UP_ROWS)]

    def conv(idx, cw_ref, cb_ref, cs, r0, nr):
        zz = zb_ref[idx, r0:r0 + SUBLANES + nr, cs]
        z1 = pltpu.roll(zz, 1, 0)[SUBLANES:]
        z2 = pltpu.roll(zz, 2, 0)[SUBLANES:]
        return (cb_ref[:, cs] + z2 * cw_ref[0:1, cs] + z1 * cw_ref[1:2, cs] + zz[SUBLANES:] * cw_ref[2:3, cs])

    for cs in chunks:
        for r0, nr in rblocks:
            g = conv(0, cwg_ref, cbg_ref, cs, r0, nr)
            a = conv(1, cwa_ref, cba_ref, cs, r0, nr)
            act_ref[r0:r0 + nr, cs] = (_silu(g) * a).astype(bf16)

    seq_start = (ia % tiles_per_seq) == 0
    for cs in chunks:
        for r0, nr in rblocks:
            for idx, w_ref in ((0, wg_ref), (1, wa_ref)):
                z = jnp.dot(h_ref[r0:r0 + nr, :], w_ref[:, cs], preferred_element_type=f32)
                if r0 == 0:
                    zb_ref[idx, 0:SUBLANES, cs] = jnp.where(seq_start, 0.0, carry_ref[idx, ja, :, cs])
                zb_ref[idx, SUBLANES + r0:SUBLANES + r0 + nr, cs] = z
                if r0 + nr == tm:
                    tail = z[nr - SUBLANES:nr, :]
                    carry_ref[idx, ja, :, cs] = tail
                    cs_ref[0, idx, :, cs] = tail


def _mm_up_prompt(x, norm2, w_up, conv_w, conv_b, layer, batch, tm, cast):
    M = x.shape[0]
    seq = M // batch
    tiles_per_seq = seq // tm
    tf = MM_TILE_N[w_up.shape]
    nj = D_FF // tf
    n_steps = (M // tm) * nj
    job = _cast_job(cast[0], cast[1], n_steps, lambda s: jnp.minimum(s, n_steps - 1))
    kern = functools.partial(_mm_up_prompt_kernel, tm=tm, nj=nj, n_steps=n_steps, tiles_per_seq=tiles_per_seq)
    ia = lambda s: jnp.minimum(s, n_steps - 1) // nj
    ja = lambda s: jnp.minimum(s, n_steps - 1) % nj
    ib = lambda s: jnp.maximum(s - 1, 0) // nj
    jb = lambda s: jnp.maximum(s - 1, 0) % nj
    return pl.pallas_call(
        kern,
        grid=(n_steps + 1,),
        in_specs=[pl.BlockSpec((tm, D_MODEL), lambda s: (ia(s), 0)),
                  pl.BlockSpec((None, 1, D_MODEL), lambda s: (layer, 0, 0)),
                  pl.BlockSpec((D_MODEL, tf), lambda s: (0, ja(s))),
                  pl.BlockSpec((D_MODEL, tf), lambda s: (0, nj + ja(s))),
                  pl.BlockSpec((None, CONV_W, tf), lambda s: (layer, 0, jb(s))),
                  pl.BlockSpec((None, CONV_W, tf), lambda s: (layer, 0, nj + jb(s))),
                  pl.BlockSpec((None, 1, tf), lambda s: (layer, 0, jb(s))),
                  pl.BlockSpec((None, 1, tf), lambda s: (layer, 0, nj + jb(s))),
                  job[0]],
        out_specs=(pl.BlockSpec((tm, tf), lambda s: (ib(s), jb(s))),
                   pl.BlockSpec((1, 2, SUBLANES, tf), lambda s: (ia(s), 0, 0, ja(s))),
                   job[1]),
        out_shape=(jax.ShapeDtypeStruct((M, D_FF), bf16),
                   jax.ShapeDtypeStruct((M // tm, 2, SUBLANES, D_FF), f32),
                   job[2]),
        scratch_shapes=[pltpu.VMEM((tm, D_MODEL), bf16),
                        pltpu.VMEM((2, nj, SUBLANES, tf), f32),
                        pltpu.VMEM((2, tm + SUBLANES, tf), f32)],
        compiler_params=_cparams(("arbitrary",)),
        name="mm_up_prompt",
    )(x, norm2, w_up, w_up, conv_w, conv_w, conv_b, conv_b, cast[0])


def _mm_up_sample_kernel(x_ref, n2_ref, wg_ref, wa_ref, cwg_ref, cwa_ref, cbg_ref, cba_ref,
                         b0g_ref, b1g_ref, b0a_ref, b1a_ref, act_ref, zg_ref, za_ref, h_ref):
    @pl.when(pl.program_id(0) == 0)
    def _():
        h_ref[...] = _rms(x_ref[...], n2_ref[...]).astype(bf16)

    def conv_half(w_ref, cw_ref, cb_ref, b0_ref, b1_ref, z_ref):
        z = jnp.dot(h_ref[...], w_ref[...], preferred_element_type=f32)
        z_ref[...] = z
        return (cb_ref[...] + b0_ref[...] * cw_ref[0:1, :] + b1_ref[...] * cw_ref[1:2, :]
                + z * cw_ref[2:3, :])

    g = conv_half(wg_ref, cwg_ref, cbg_ref, b0g_ref, b1g_ref, zg_ref)
    a = conv_half(wa_ref, cwa_ref, cba_ref, b0a_ref, b1a_ref, za_ref)
    act_ref[...] = (_silu(g) * a).astype(bf16)


def _mm_up_sample(x, norm2, w_up, conv_w, conv_b, buf, layer):
    M = x.shape[0]
    tf = MM_TILE_N[w_up.shape]
    nj = D_FF // tf
    wspec = lambda off: pl.BlockSpec((D_MODEL, tf), lambda j: (0, off + j))
    cwspec = lambda off: pl.BlockSpec((None, CONV_W, tf), lambda j: (layer, 0, off + j))
    cbspec = lambda off: pl.BlockSpec((None, 1, tf), lambda j: (layer, 0, off + j))
    bufspec = lambda row, off: pl.BlockSpec((None, None, M, tf), lambda j: (layer, row, 0, off + j))
    return pl.pallas_call(
        _mm_up_sample_kernel,
        grid=(nj,),
        in_specs=[pl.BlockSpec((M, D_MODEL), lambda j: (0, 0)),
                  pl.BlockSpec((None, 1, D_MODEL), lambda j: (layer, 0, 0)),
                  wspec(0), wspec(nj), cwspec(0), cwspec(nj), cbspec(0), cbspec(nj),
                  bufspec(0, 0), bufspec(1, 0), bufspec(0, nj), bufspec(1, nj)],
        out_specs=(pl.BlockSpec((M, tf), lambda j: (0, j)),
                   pl.BlockSpec((M, tf), lambda j: (0, j)),
                   pl.BlockSpec((M, tf), lambda j: (0, j))),
        out_shape=(jax.ShapeDtypeStruct((M, D_FF), bf16),
                   jax.ShapeDtypeStruct((M, D_FF), f32),
                   jax.ShapeDtypeStruct((M, D_FF), f32)),
        scratch_shapes=[pltpu.VMEM((M, D_MODEL), bf16)],
        compiler_params=_cparams(("arbitrary",)),
        name="mm_up_sample",
    )(x, norm2, w_up, w_up, conv_w, conv_w, conv_b, conv_b, buf, buf, buf, buf)


def _mix_a_prompt_kernel(u_ref, v_ref, vg_ref, ws_ref, bs_ref, o_ref):
    row = lax.broadcasted_iota(jnp.int32, (CHUNK, CHUNK), 0)
    col = lax.broadcasted_iota(jnp.int32, (CHUNK, CHUNK), 1)
    tri = col <= row
    for g in range(A_GROUPS):
        sl = slice(g * A_DIM, (g + 1) * A_DIM)
        ws = jnp.where(tri, ws_ref[g], 0.0).astype(bf16)
        for r0 in range(0, u_ref.shape[0], CHUNK):
            rs = slice(r0, r0 + CHUNK)
            vn = _rms(_gelu(v_ref[rs, sl]), vg_ref[:, sl])
            mixv = jnp.dot(ws, vn.astype(bf16), preferred_element_type=f32) + bs_ref[:, sl]
            o_ref[rs, sl] = (_gelu(u_ref[rs, sl]) * mixv).astype(bf16)


def _mix_a_prompt(z, v_gain, w_s, b_s_full, layer, rows):
    M = z.shape[0]
    return pl.pallas_call(
        _mix_a_prompt_kernel,
        grid=(M // rows,),
        in_specs=[pl.BlockSpec((rows, A_WIDTH), lambda r: (r, Z_AU // A_WIDTH)),
                  pl.BlockSpec((rows, A_WIDTH), lambda r: (r, Z_AV // A_WIDTH)),
                  pl.BlockSpec((None, 1, A_WIDTH), lambda r: (layer, 0, 0)),
                  pl.BlockSpec((None, A_GROUPS, CHUNK, CHUNK), lambda r: (layer, 0, 0, 0)),
                  pl.BlockSpec((None, CHUNK, A_WIDTH), lambda r: (layer, 0, 0))],
        out_specs=pl.BlockSpec((rows, A_WIDTH), lambda r: (r, 0)),
        out_shape=jax.ShapeDtypeStruct((M, A_WIDTH), bf16),
        compiler_params=_cparams(("parallel",)),
        name="mix_a_prompt",
    )(z, z, v_gain, w_s, b_s_full)


def _mix_a_sample_kernel(u_ref, v_ref, vg_ref, ws_ref, bs_ref, o_ref, vn_ref):
    for g in range(A_GROUPS):
        sl = slice(g * A_DIM, (g + 1) * A_DIM)
        vn = _rms(_gelu(v_ref[:, sl]), vg_ref[:, sl])
        vn_ref[:, sl] = vn
        mixv = ws_ref[:, sl] * vn + bs_ref[:, sl]
        o_ref[:, sl] = (_gelu(u_ref[:, sl]) * mixv).astype(bf16)


def _mix_a_sample(z, v_gain, ws0, bs0, layer):
    M = z.shape[0]
    vec = pl.BlockSpec((None, 1, A_WIDTH), lambda r: (layer, 0, 0))
    return pl.pallas_call(
        _mix_a_sample_kernel,
        grid=(1,),
        in_specs=[pl.BlockSpec((M, A_WIDTH), lambda r: (0, Z_AU // A_WIDTH)),
                  pl.BlockSpec((M, A_WIDTH), lambda r: (0, Z_AV // A_WIDTH)),
                  vec, vec, vec],
        out_specs=(pl.BlockSpec((M, A_WIDTH), lambda r: (0, 0)),
                   pl.BlockSpec((M, A_WIDTH), lambda r: (0, 0))),
        out_shape=(jax.ShapeDtypeStruct((M, A_WIDTH), bf16),
                   jax.ShapeDtypeStruct((M, A_WIDTH), f32)),
        compiler_params=_cparams(("arbitrary",)),
        name="mix_a_sample",
    )(z, z, v_gain, ws0, bs0)


def _mix_b_prompt_kernel(sink_ref, q_ref, k_ref, v_ref, qg_ref, kg_ref, bias_ref, src_ref,
                         o_ref, klast_ref, vlast_ref, dst_ref, kcat_ref, vcat_ref):
    _cast_chunk(src_ref, dst_ref)
    i = pl.program_id(1)
    W = WINDOW
    kn = _norm_head_pairs(k_ref[...], kg_ref[...])
    v = v_ref[...]
    klast_ref[0] = kn
    vlast_ref[0] = v

    @pl.when(i == 0)
    def _():
        kcat_ref[0:W, :] = jnp.zeros((W, LANES), f32)
        vcat_ref[0:W, :] = jnp.zeros((W, LANES), f32)

    @pl.when(i > 0)
    def _():
        kcat_ref[0:W, :] = kcat_ref[W:2 * W, :]
        vcat_ref[0:W, :] = vcat_ref[W:2 * W, :]

    kcat_ref[W:2 * W, :] = kn
    vcat_ref[W:2 * W, :] = v
    kc = kcat_ref[...]
    vc = vcat_ref[...]
    kc_sw = pltpu.roll(kc, B_HEAD_DIM, 1)
    vc_sw = pltpu.roll(vc, B_HEAD_DIM, 1)
    lo = lax.broadcasted_iota(jnp.int32, (2 * W, LANES), 1) < B_HEAD_DIM

    cj = lax.broadcasted_iota(jnp.int32, (W, W), 0)
    qi = lax.broadcasted_iota(jnp.int32, (W, W), 1)
    own = cj <= qi
    keep = own | (i > 0)
    q_gain = qg_ref[...] * (B_HEAD_DIM ** -0.5)

    tiles = GQA // 2
    nt = (((1,), (1,)), ((), ()))
    tn = (((0,), (0,)), ((), ()))
    scores, values = [], []
    for kvh in range(B_KV_HEADS):
        k_src, k_alt = (kc, kc_sw) if kvh == 0 else (kc_sw, kc)
        v_src, v_alt = (vc, vc_sw) if kvh == 0 else (vc_sw, vc)
        k_even = jnp.where(lo, k_src, 0.0).astype(bf16)
        k_odd = jnp.where(lo, 0.0, k_alt).astype(bf16)
        values.append((jnp.where(lo, v_src, 0.0).astype(bf16), jnp.where(lo, 0.0, v_alt).astype(bf16)))
        qts = []
        for t in range(tiles):
            c0 = (kvh * tiles + t) * LANES
            qt = q_ref[:, c0:c0 + LANES].T
            q2 = qt * qt
            halves = []
            for r0 in (0, B_HEAD_DIM):
                ms = jnp.mean(q2[r0:r0 + B_HEAD_DIM, :], axis=0, keepdims=True)
                halves.append(qt[r0:r0 + B_HEAD_DIM, :] * lax.rsqrt(ms + EPS))
            qts.append(jnp.concatenate(halves, axis=0) * q_gain)
        qst = jnp.concatenate(qts, axis=1).astype(bf16)
        scores.append((jnp.dot(k_even, qst, preferred_element_type=f32),
                       jnp.dot(k_odd, qst, preferred_element_type=f32)))
    for kvh in range(B_KV_HEADS):
        p_par = []
        for par in range(2):
            blocks = []
            for t in range(tiles):
                h = kvh * GQA + 2 * t + par
                sb = scores[kvh][par][:, t * W:(t + 1) * W]
                s = jnp.where(own, sb[W:2 * W, :], sb[0:W, :]) + bias_ref[h]
                s = jnp.where(keep, s, NEG_INF)
                sk = sink_ref[h]
                mx = jnp.maximum(jnp.max(s, axis=0, keepdims=True), sk)
                p = jnp.exp(s - mx)
                den = jnp.sum(p, axis=0, keepdims=True) + jnp.exp(sk - mx)
                p = p * (1.0 / den)
                blocks.append(jnp.concatenate([jnp.where(own, 0.0, p), jnp.where(own, p, 0.0)],
                                              axis=0).astype(bf16))
            p_par.append(jnp.concatenate(blocks, axis=1))
        ot = (lax.dot_general(values[kvh][0], p_par[0], tn, preferred_element_type=f32)
              + lax.dot_general(values[kvh][1], p_par[1], tn, preferred_element_type=f32))
        for t in range(tiles):
            c0 = (kvh * tiles + t) * LANES
            o_ref[:, c0:c0 + LANES] = ot[:, t * W:(t + 1) * W].T.astype(bf16)


def _mix_b_prompt(z, sinks, q_gain_t, k_gain2, bias_p, layer, batch, cast):
    M = z.shape[0]
    nb = M // batch // WINDOW
    job = _cast_job(cast[0], cast[1], batch * nb, lambda b, i: b * nb + i)
    return pl.pallas_call(
        _mix_b_prompt_kernel,
        grid=(batch, nb),
        in_specs=[pl.BlockSpec(memory_space=pltpu.SMEM),
                  pl.BlockSpec((WINDOW, B_WIDTH), lambda b, i: (b * nb + i, Z_BQ // B_WIDTH)),
                  pl.BlockSpec((WINDOW, B_KV_WIDTH), lambda b, i: (b * nb + i, Z_BK // B_KV_WIDTH)),
                  pl.BlockSpec((WINDOW, B_KV_WIDTH), lambda b, i: (b * nb + i, Z_BV // B_KV_WIDTH)),
                  pl.BlockSpec((None, LANES, WINDOW), lambda b, i: (layer, 0, 0)),
                  pl.BlockSpec((None, 1, LANES), lambda b, i: (layer, 0, 0)),
                  pl.BlockSpec((B_HEADS, WINDOW, WINDOW), lambda b, i: (0, 0, 0)),
                  job[0]],
        out_specs=(pl.BlockSpec((WINDOW, B_WIDTH), lambda b, i: (b * nb + i, 0)),
                   pl.BlockSpec((1, WINDOW, B_KV_WIDTH), lambda b, i: (b, 0, 0)),
                   pl.BlockSpec((1, WINDOW, B_KV_WIDTH), lambda b, i: (b, 0, 0)),
                   job[1]),
        out_shape=(jax.ShapeDtypeStruct((M, B_WIDTH), bf16),
                   jax.ShapeDtypeStruct((batch, WINDOW, B_KV_WIDTH), f32),
                   jax.ShapeDtypeStruct((batch, WINDOW, B_KV_WIDTH), f32),
                   job[2]),
        scratch_shapes=[pltpu.VMEM((2 * WINDOW, LANES), f32),
                        pltpu.VMEM((2 * WINDOW, LANES), f32)],
        compiler_params=_cparams(("arbitrary", "arbitrary")),
        name="mix_b_prompt",
    )(sinks, z, z, z, q_gain_t, k_gain2, bias_p, cast[0])


def _mix_c_prompt_kernel(q_ref, k_ref, v0_ref, v1_ref, o0_ref, o1_ref, g_ref, gb_ref, hgt_ref,
                         out_ref, c_out, n_out, m_out, c_s, n_s, m_s):
    L = MLSTM_L
    c = pl.program_id(1)

    @pl.when(c == 0)
    def _():
        c_s[...] = jnp.zeros(c_s.shape, f32)
        n_s[...] = jnp.zeros(n_s.shape, f32)
        m_s[...] = jnp.zeros(m_s.shape, f32)

    row = lax.broadcasted_iota(jnp.int32, (L, L), 0)
    col = lax.broadcasted_iota(jnp.int32, (L, L), 1)
    tri = col <= row
    tri_f = tri.astype(f32)
    seqs = range(q_ref.shape[0])
    units = [(bi, h) for bi in seqs for h in range(C_HEADS)]
    g, b = {}, {}
    for bi in seqs:
        g[bi] = g_ref[bi] + gb_ref[...]
        b[bi] = jnp.dot(tri_f, jax.nn.log_sigmoid(g[bi]), precision=lax.Precision.HIGHEST,
                        preferred_element_type=f32)
    bT = {bi: b[bi].T for bi in seqs}
    vis = row <= col
    nt = (((1,), (1,)), ((), ()))
    tn = (((0,), (0,)), ((), ()))
    qb, kh, vb, ct_prev, n_prev, st, qct, qn_lin = {}, {}, {}, {}, {}, {}, {}, {}
    for u in units:
        bi, h = u
        v_ref = v0_ref if h < 2 else v1_ref
        vsl = slice((h % 2) * C_V_DIM, (h % 2 + 1) * C_V_DIM)
        qb[u] = q_ref[bi, :, h * C_QK_DIM:(h + 1) * C_QK_DIM].astype(bf16)
        kh[u] = k_ref[bi, :, h * C_QK_DIM:(h + 1) * C_QK_DIM] * (C_QK_DIM ** -0.5)
        vb[u] = v_ref[bi, :, vsl].astype(bf16)
        ct_prev[u] = c_s[bi, h]
        n_prev[u] = n_s[bi, h]
        st[u] = lax.dot_general(kh[u].astype(bf16), qb[u], nt, preferred_element_type=f32)
        qct[u] = lax.dot_general(ct_prev[u].astype(bf16), qb[u], nt, preferred_element_type=f32)
        n8 = jnp.broadcast_to(n_prev[u], (SUBLANES, C_QK_DIM)).astype(bf16)
        qn_lin[u] = lax.dot_general(n8, qb[u], nt, preferred_element_type=f32)[0:1, :]
    pt, iw, inv, kw, decay, m_new = {}, {}, {}, {}, {}, {}
    for u in units:
        bi, h = u
        b_row = bT[bi][C_HEADS + h:C_HEADS + h + 1, :]
        src = g[bi][:, h:h + 1] - b[bi][:, C_HEADS + h:C_HEADS + h + 1]
        dlog = jnp.where(vis, b_row + src, -jnp.inf)
        m_prev = m_s[bi, h][0:1, 0:1]
        inter = b_row + m_prev
        mt = jnp.maximum(inter, jnp.max(dlog, axis=0, keepdims=True))
        dt = jnp.exp(dlog - mt)
        pt[u] = st[u] * dt
        iw[u] = jnp.exp(inter - mt)
        qn = iw[u] * qn_lin[u] + jnp.sum(pt[u], axis=0, keepdims=True)
        inv[u] = 1.0 / jnp.maximum(jnp.abs(qn), jnp.exp(-mt))
        m_new[u] = mt[:, L - 1:L]
        decay[u] = jnp.exp(b_row[:, L - 1:L] + m_prev - m_new[u])
        kw[u] = kh[u] * dt[:, L - 1:L]
    svt = {u: lax.dot_general(vb[u], pt[u].astype(bf16), tn, preferred_element_type=f32) for u in units}
    kvt = {u: lax.dot_general(vb[u], kw[u].astype(bf16), tn, preferred_element_type=f32) for u in units}
    for u in units:
        bi, h = u
        o_ref = o0_ref if h < 2 else o1_ref
        vsl = slice((h % 2) * C_V_DIM, (h % 2 + 1) * C_V_DIM)
        ht = (iw[u] * qct[u] + svt[u]) * inv[u]
        ms = jnp.mean(ht * ht, axis=0, keepdims=True)
        hn = (ht * lax.rsqrt(ms + EPS) * hgt_ref[h]).T
        c_s[bi, h] = decay[u] * ct_prev[u] + kvt[u]
        n_s[bi, h] = decay[u] * n_prev[u] + jnp.sum(kw[u], axis=0, keepdims=True)
        m_s[bi, h] = jnp.broadcast_to(m_new[u], (SUBLANES, LANES))
        osl = slice(h * C_V_DIM, (h + 1) * C_V_DIM)
        out_ref[bi, :, osl] = (hn * jax.nn.sigmoid(o_ref[bi, :, vsl])).astype(bf16)

    @pl.when(c == pl.num_programs(1) - 1)
    def _():
        eye = (row == col).astype(f32)
        for bi, h in units:
            c_out[bi, h] = lax.dot_general(c_s[bi, h], eye, tn, precision=lax.Precision.HIGHEST,
                                           preferred_element_type=f32)
        n_out[...] = n_s[...]
        m_out[...] = m_s[...]


def _mix_c_prompt(z, gate_bias, h_gain, layer, batch):
    M = z.shape[0]
    L = MLSTM_L
    T = M // batch
    z3 = z.reshape(batch, T, Z_WIDTH)
    ns = C_SEQS if batch % C_SEQS == 0 else 1
    col = lambda width, off: pl.BlockSpec((ns, L, width), lambda b, c: (b, c, off // width))
    state = lambda *shape: pl.BlockSpec((ns,) + shape, lambda b, c: (b,) + (0,) * len(shape))
    out, c_st, n_st, m_st = pl.pallas_call(
        _mix_c_prompt_kernel,
        grid=(batch // ns, T // L),
        in_specs=[col(C_QK_WIDTH, Z_CQ), col(C_QK_WIDTH, Z_CK),
                  col(C_PAIR, Z_CV), col(C_PAIR, Z_CV + C_PAIR),
                  col(C_PAIR, Z_CO), col(C_PAIR, Z_CO + C_PAIR),
                  col(LANES, Z_GATE),
                  pl.BlockSpec((None, 1, LANES), lambda b, c: (layer, 0, 0)),
                  pl.BlockSpec((None, C_HEADS, C_V_DIM, L), lambda b, c: (layer, 0, 0, 0))],
        out_specs=(pl.BlockSpec((ns, L, C_WIDTH), lambda b, c: (b, c, 0)),
                   state(C_HEADS, C_QK_DIM, C_V_DIM),
                   state(C_HEADS, 1, C_QK_DIM),
                   state(C_HEADS, SUBLANES, LANES)),
        out_shape=(jax.ShapeDtypeStruct((batch, T, C_WIDTH), bf16),
                   jax.ShapeDtypeStruct((batch, C_HEADS, C_QK_DIM, C_V_DIM), f32),
                   jax.ShapeDtypeStruct((batch, C_HEADS, 1, C_QK_DIM), f32),
                   jax.ShapeDtypeStruct((batch, C_HEADS, SUBLANES, LANES), f32)),
        scratch_shapes=[pltpu.VMEM((ns, C_HEADS, C_V_DIM, C_QK_DIM), f32),
                        pltpu.VMEM((ns, C_HEADS, 1, C_QK_DIM), f32),
                        pltpu.VMEM((ns, C_HEADS, SUBLANES, LANES), f32)],
        compiler_params=_cparams(("arbitrary", "arbitrary")),
        name="mix_c_prompt",
    )(z3, z3, z3, z3, z3, z3, z3, gate_bias, h_gain)
    return out.reshape(M, C_WIDTH), c_st, n_st, m_st


_BC_SHARED = (4, 5, 6, 7, 12, 13)


def _mix_bc_sample_kernel(*refs):
    rows = [_mix_bc_sample_row(*[ref if k in _BC_SHARED else ref.at[pl.ds(r, 1)] for k, ref in enumerate(refs)])
            for r in range(refs[0].shape[0])]
    while rows:
        rows = [row for row in rows if next(row, True) is None]


def _mix_bc_sample_row(zr_ref, q_ref, kp_ref, vp_ref, qg_ref, kg_ref, sink_ref, bias_ref,
                       cq_ref, ck_ref, cv_ref, co_ref, gb_ref, hg_ref, c0_ref, n0_ref, m0_ref,
                       bo_ref, kn_ref, co_out, c_out, n_out, m_out):
    W = WINDOW
    zr = zr_ref[0]

    qn = _rms(q_ref[0], qg_ref[...])
    hrow = lax.broadcasted_iota(jnp.int32, (B_HEADS, LANES), 0)
    lane = lax.broadcasted_iota(jnp.int32, (B_HEADS, LANES), 1)
    own = (hrow < GQA) == (lane < B_HEAD_DIM)
    q2 = jnp.where(own, jnp.concatenate([qn, qn], axis=-1), 0.0)
    k_new = _norm_head_pairs(zr[:, Z_BK:Z_BK + B_KV_WIDTH], kg_ref[...])
    v_new = zr[:, Z_BV:Z_BV + B_KV_WIDTH]
    kn_ref[0] = k_new
    scale = B_HEAD_DIM ** -0.5
    s_past = lax.dot_general(q2.astype(bf16), kp_ref[0].astype(bf16), (((1,), (1,)), ((), ())),
                             preferred_element_type=f32) * scale + bias_ref[:, 0:W]
    yield
    s_past = jnp.where(lane >= 1, s_past, NEG_INF)
    s_new = jnp.sum(q2 * k_new, axis=-1, keepdims=True) * scale + bias_ref[:, W:W + 1]
    sk = sink_ref[...]
    mx = jnp.maximum(jnp.maximum(jnp.max(s_past, axis=-1, keepdims=True), s_new), sk)
    p_past = jnp.exp(s_past - mx)
    p_new = jnp.exp(s_new - mx)
    den = jnp.sum(p_past, axis=-1, keepdims=True) + p_new + jnp.exp(sk - mx)
    inv = 1.0 / den
    o2 = (jnp.dot((p_past * inv).astype(bf16), vp_ref[0].astype(bf16), preferred_element_type=f32)
          + (p_new * inv) * v_new)
    o2_sw = pltpu.roll(o2, B_HEAD_DIM, 1)
    bo_ref[0] = jnp.where(hrow < GQA, o2, o2_sw)[:, 0:B_HEAD_DIM].astype(bf16)

    g = zr[:, Z_GATE:Z_GATE + LANES] + gb_ref[...]
    lf = jax.nn.log_sigmoid(g)
    m0 = m0_ref[0]
    lane1 = lax.broadcasted_iota(jnp.int32, (1, LANES), 1)
    m_row = jnp.zeros((1, LANES), f32)
    row8 = lax.broadcasted_iota(jnp.int32, (SUBLANES, C_QK_DIM), 0)
    c_rows = []
    staged = []
    for h in range(C_HEADS):
        ig = g[:, h:h + 1]
        b = lf[:, C_HEADS + h:C_HEADS + h + 1]
        m_prev = m0[:, h:h + 1]
        inter = b + m_prev
        mt = jnp.maximum(inter, ig)
        qh = cq_ref[0, h:h + 1, :]
        kh = ck_ref[0, h:h + 1, :] * (C_QK_DIM ** -0.5)
        vh = cv_ref[0, h:h + 1, :]
        c_prev = c0_ref[0, h]
        n_prev = n0_ref[0, h:h + 1, :]
        s = jnp.sum(qh * kh, axis=-1, keepdims=True) * jnp.exp(ig - mt)
        iw = jnp.exp(inter - mt)
        q8 = jnp.broadcast_to(qh, (SUBLANES, C_QK_DIM)).astype(bf16)
        qc = jnp.dot(q8, c_prev.astype(bf16), preferred_element_type=f32)[0:1, :]
        staged.append((ig, inter, mt, qh, kh, vh, c_prev, n_prev, s, iw, qc))
    yield
    for h, (ig, inter, mt, qh, kh, vh, c_prev, n_prev, s, iw, qc) in enumerate(staged):
        num = iw * qc + s * vh
        qn_ = iw * jnp.sum(qh * n_prev, axis=-1, keepdims=True) + s
        hh = num / jnp.maximum(jnp.abs(qn_), jnp.exp(-mt))
        wc = jnp.exp(ig - mt)
        decay = jnp.exp(inter - mt)
        kw = kh * wc
        kw8 = jnp.where(row8 == 0, jnp.broadcast_to(kw, (SUBLANES, C_QK_DIM)), 0.0).astype(bf16)
        v8 = jnp.broadcast_to(vh, (SUBLANES, C_V_DIM)).astype(bf16)
        c_out[0, h] = decay * c_prev + lax.dot_general(kw8, v8, (((0,), (0,)), ((), ())),
                                                      preferred_element_type=f32)
        n_out[0, h:h + 1, :] = decay * n_prev + kw
        m_row = jnp.where(lane1 == h, mt, m_row)
        c_rows.append(_rms(hh, hg_ref[h:h + 1, :]) * jax.nn.sigmoid(co_ref[0, h:h + 1, :]))
    co_out[0] = jnp.concatenate(c_rows, axis=0).astype(bf16)
    m_out[0] = m_row


def _mix_bc_sample(z, k_past, v_past, q_gain, k_gain2, sinks, bias_s, gate_bias, h_gain4,
                   c0, n0, m0, layer):
    R = z.shape[0]
    zr = z.reshape(R, 1, Z_WIDTH)
    q = z[:, Z_BQ:Z_BQ + B_WIDTH].reshape(R, B_HEADS, B_HEAD_DIM)
    cq = z[:, Z_CQ:Z_CQ + C_QK_WIDTH].reshape(R, C_HEADS, C_QK_DIM)
    ck = z[:, Z_CK:Z_CK + C_QK_WIDTH].reshape(R, C_HEADS, C_QK_DIM)
    cv = z[:, Z_CV:Z_CV + C_WIDTH].reshape(R, C_HEADS, C_V_DIM)
    co = z[:, Z_CO:Z_CO + C_WIDTH].reshape(R, C_HEADS, C_V_DIM)
    rb = DEC_ROWS if R % DEC_ROWS == 0 else 1
    row3 = lambda n: pl.BlockSpec((rb, 1, n), lambda r: (r, 0, 0))
    return pl.pallas_call(
        _mix_bc_sample_kernel,
        grid=(R // rb,),
        in_specs=[row3(Z_WIDTH),
                  pl.BlockSpec((rb, B_HEADS, B_HEAD_DIM), lambda r: (r, 0, 0)),
                  pl.BlockSpec((None, rb, WINDOW, B_KV_WIDTH), lambda r: (layer, r, 0, 0)),
                  pl.BlockSpec((None, rb, WINDOW, B_KV_WIDTH), lambda r: (layer, r, 0, 0)),
                  pl.BlockSpec((None, 1, B_HEAD_DIM), lambda r: (layer, 0, 0)),
                  pl.BlockSpec((None, 1, LANES), lambda r: (layer, 0, 0)),
                  pl.BlockSpec((None, B_HEADS, 1), lambda r: (layer, 0, 0)),
                  pl.BlockSpec((B_HEADS, 2 * WINDOW), lambda r: (0, 0)),
                  pl.BlockSpec((rb, C_HEADS, C_QK_DIM), lambda r: (r, 0, 0)),
                  pl.BlockSpec((rb, C_HEADS, C_QK_DIM), lambda r: (r, 0, 0)),
                  pl.BlockSpec((rb, C_HEADS, C_V_DIM), lambda r: (r, 0, 0)),
                  pl.BlockSpec((rb, C_HEADS, C_V_DIM), lambda r: (r, 0, 0)),
                  pl.BlockSpec((None, 1, LANES), lambda r: (layer, 0, 0)),
                  pl.BlockSpec((None, C_HEADS, C_V_DIM), lambda r: (layer, 0, 0)),
                  pl.BlockSpec((None, rb, C_HEADS, C_QK_DIM, C_V_DIM), lambda r: (layer, r, 0, 0, 0)),
                  pl.BlockSpec((None, rb, C_HEADS, C_QK_DIM), lambda r: (layer, r, 0, 0)),
                  pl.BlockSpec((None, rb, 1, C_HEADS), lambda r: (layer, r, 0, 0))],
        out_specs=(pl.BlockSpec((rb, B_HEADS, B_HEAD_DIM), lambda r: (r, 0, 0)),
                   row3(B_KV_WIDTH),
                   pl.BlockSpec((rb, C_HEADS, C_V_DIM), lambda r: (r, 0, 0)),
                   pl.BlockSpec((rb, C_HEADS, C_QK_DIM, C_V_DIM), lambda r: (r, 0, 0, 0)),
                   pl.BlockSpec((rb, C_HEADS, C_QK_DIM), lambda r: (r, 0, 0)),
                   row3(LANES)),
        out_shape=(jax.ShapeDtypeStruct((R, B_HEADS, B_HEAD_DIM), bf16),
                   jax.ShapeDtypeStruct((R, 1, B_KV_WIDTH), f32),
                   jax.ShapeDtypeStruct((R, C_HEADS, C_V_DIM), bf16),
                   jax.ShapeDtypeStruct((R, C_HEADS, C_QK_DIM, C_V_DIM), f32),
                   jax.ShapeDtypeStruct((R, C_HEADS, C_QK_DIM), f32),
                   jax.ShapeDtypeStruct((R, 1, LANES), f32)),
        compiler_params=_cparams(("parallel",)),
        name="mix_bc_sample",
    )(zr, q, k_past, v_past, q_gain, k_gain2, sinks, bias_s, cq, ck, cv, co, gate_bias, h_gain4,
      c0, n0, m0)


def _prep_w_in(w_in):
    pad = jnp.zeros(w_in.shape[:-1] + (Z_WIDTH - w_in.shape[-1],), bf16)
    return jnp.concatenate([w_in.astype(bf16), pad], axis=-1)


def _row_tile(m, pref):
    return pref if m % pref == 0 else m


def kernel(x_prompt, x_sample, cache_swa_k, cache_swa_v, state_mlstm_C, state_mlstm_n, state_mlstm_m, state_ffn_conv, rel_bias, norm1, w_in, a_v_gain, a_spatial_w, a_spatial_b, b_q_gain, b_k_gain, b_sinks, c_gate_bias, c_h_gain, w_out, norm2, w_up, ffn_conv_w, ffn_conv_b, w_down):
    depth = w_in.shape[0]
    Bp, T, _ = x_prompt.shape
    R = x_sample.shape[0]
    assert x_sample.shape[1] == 1 and T % CHUNK == 0 and T % MLSTM_L == 0

    w_in_l = _prep_w_in(w_in[0])
    norm1_3 = norm1.reshape(depth, 1, D_MODEL)
    norm2_3 = norm2.reshape(depth, 1, D_MODEL)
    v_gain3 = a_v_gain.reshape(depth, 1, A_WIDTH)
    bs_full = jnp.repeat(jnp.swapaxes(a_spatial_b, 1, 2), A_DIM, axis=-1)
    ws0 = jnp.repeat(a_spatial_w[:, :, 0, 0], A_DIM, axis=-1).reshape(depth, 1, A_WIDTH)
    bs0 = jnp.repeat(a_spatial_b[:, :, 0], A_DIM, axis=-1).reshape(depth, 1, A_WIDTH)
    q_gain3 = b_q_gain.reshape(depth, 1, B_HEAD_DIM)
    q_gain_t = jnp.broadcast_to(jnp.tile(b_q_gain, (1, 2)).reshape(depth, LANES, 1), (depth, LANES, WINDOW))
    k_gain2 = jnp.tile(b_k_gain, (1, 2)).reshape(depth, 1, LANES)
    sinks3 = b_sinks.reshape(depth, B_HEADS, 1)
    gate_b = jnp.pad(c_gate_bias, ((0, 0), (0, LANES - 2 * C_HEADS))).reshape(depth, 1, LANES)
    h_gain_t = jnp.broadcast_to(c_h_gain.reshape(depth, C_HEADS, C_V_DIM, 1), (depth, C_HEADS, C_V_DIM, MLSTM_L))
    h_gain4 = c_h_gain.reshape(depth, C_HEADS, C_V_DIM)
    conv_b3 = ffn_conv_b.reshape(depth, 1, 2 * D_FF)
    k_cache = cache_swa_k.reshape(depth, R, WINDOW, B_KV_WIDTH)
    v_cache = cache_swa_v.reshape(depth, R, WINDOW, B_KV_WIDTH)
    m_state = state_mlstm_m.reshape(depth, R, 1, C_HEADS)
    conv_hist = jnp.swapaxes(state_ffn_conv, 1, 2)

    bias_p, bias_s = _bias_tables(rel_bias)

    Mp = Bp * T
    tm = _row_tile(T, 1024)
    a_rows = _row_tile(T, 512)
    xp = x_prompt.reshape(Mp, D_MODEL)
    xs = x_sample.reshape(R, D_MODEL)
    P = [[] for _ in range(6)]
    S = [[] for _ in range(7)]
    for l in range(depth):
        z, w_out_l = _mm_in(xp, norm1_3, w_in_l, l, tm, cast=(w_out, l))
        a_o = _mix_a_prompt(z, v_gain3, a_spatial_w, bs_full, l, a_rows)
        b_o, k_last, v_last, w_up_l = _mix_b_prompt(z, b_sinks[l], q_gain_t, k_gain2, bias_p, l, Bp, (w_up, l))
        c_o, c_st, n_st, m_st = _mix_c_prompt(z, gate_b, h_gain_t, l, Bp)
        x1 = _mm_out(a_o, b_o, c_o, xp, w_out_l, tm)
        act, cs, w_down_l = _mm_up_prompt(x1, norm2_3, w_up_l, ffn_conv_w, conv_b3, l, Bp, tm, (w_down, l))
        if l + 1 < depth:
            xp, w_in_next = _mm_down(act, x1, w_down_l, tm, cast=(w_in, l + 1), cast_width=Z_WIDTH)
        else:
            xp = _mm_down(act, x1, w_down_l, tm)
        P[0].append(k_last.reshape(Bp, WINDOW, B_KV_HEADS, B_HEAD_DIM))
        P[1].append(v_last.reshape(Bp, WINDOW, B_KV_HEADS, B_HEAD_DIM))
        P[2].append(c_st)
        P[3].append(n_st.reshape(Bp, C_HEADS, C_QK_DIM))
        P[4].append(m_st[:, :, 0, 0])
        seq_tiles = T // tm
        tail = cs[seq_tiles - 1::seq_tiles, :, SUBLANES - (CONV_W - 1):, :]
        P[5].append(jnp.swapaxes(tail, 1, 2).reshape(Bp, CONV_W - 1, 2 * D_FF))

        zs = _mm_in(xs, norm1_3, w_in_l, l, R)
        a_s, vn_s = _mix_a_sample(zs, v_gain3, ws0, bs0, l)
        b_s, kn_s, c_s, c_new, n_new, m_new = _mix_bc_sample(
            zs, k_cache, v_cache, q_gain3, k_gain2, sinks3, bias_s, gate_b, h_gain4,
            state_mlstm_C, state_mlstm_n, m_state, l)
        x1s = _mm_out(a_s, b_s.reshape(R, B_WIDTH), c_s.reshape(R, C_WIDTH), xs, w_out_l, R)
        act_s, zg_s, za_s = _mm_up_sample(x1s, norm2_3, w_up_l, ffn_conv_w, conv_b3, conv_hist, l)
        xs = _mm_down(act_s, x1s, w_down_l, R)
        if l + 1 < depth:
            w_in_l = w_in_next
        S[0].append(vn_s.reshape(R, 1, A_WIDTH))
        S[1].append(kn_s.reshape(R, 1, B_KV_HEADS, B_HEAD_DIM))
        S[2].append(zs[:, Z_BV:Z_BV + B_KV_WIDTH].reshape(R, 1, B_KV_HEADS, B_HEAD_DIM))
        S[3].append(c_new)
        S[4].append(n_new)
        S[5].append(m_new[:, 0, 0:C_HEADS])
        z_new = jnp.concatenate([zg_s, za_s], axis=-1)
        S[6].append(jnp.stack([state_ffn_conv[l][:, CONV_W - 2], z_new], axis=1))

    st = lambda lst: jnp.stack(lst, axis=0)
    return (xp.reshape(Bp, T, D_MODEL), xs.reshape(R, 1, D_MODEL),
            st(P[0]), st(P[1]), st(S[1]), st(S[2]),
            st(P[2]), st(P[3]), st(P[4]),
            st(S[3]), st(S[4]), st(S[5]),
            st(P[5]), st(S[6]),
            st(S[0]))
```

```python
import functools
import math

import numpy as np
import jax
import jax.numpy as jnp
from jax import lax
from jax.experimental import pallas as pl
from jax.experimental.pallas import tpu as pltpu

f32 = jnp.float32
bf16 = jnp.bfloat16

D_MODEL = 2048
EPS = 1e-6
NEG_INF = -1e30
SQRT_HALF = 0.7071067811865476

A_GROUPS = 4
A_DIM = 128
A_WIDTH = 512
CHUNK = 128
B_HEADS = 16
B_KV_HEADS = 2
B_HEAD_DIM = 64
GQA = 8
B_WIDTH = 1024
B_KV_WIDTH = 128
WINDOW = 128
N_BUCKETS = 32
MAX_DISTANCE = 128
C_HEADS = 4
C_QK_DIM = 64
C_V_DIM = 128
C_QK_WIDTH = 256
C_WIDTH = 512
IN_SPLITS = (A_WIDTH, A_WIDTH, B_WIDTH, B_KV_WIDTH, B_KV_WIDTH,
             C_QK_WIDTH, C_QK_WIDTH, C_WIDTH, C_WIDTH, C_HEADS, C_HEADS)
IN_OFFSETS = tuple(int(o) for o in np.cumsum(IN_SPLITS)[:-1])
D_FF = 5632
CONV_W = 3

Z_WIDTH = 4096
Z_AU, Z_AV, Z_BQ, Z_BK, Z_BV, Z_CQ, Z_CK, Z_CV, Z_CO, Z_GATE = (0,) + IN_OFFSETS[:9]
C_PAIR = 2 * C_V_DIM

MLSTM_L = 128
UP_CHUNK = 256
UP_ROWS = 256
C_SEQS = 2
DEC_ROWS = 8
MM_TILE_N = {(D_MODEL, Z_WIDTH): 1024, (D_MODEL, D_MODEL): 1024, (D_MODEL, 2 * D_FF): 512, (D_FF, D_MODEL): 512}
LANES = 128
SUBLANES = 8
VMEM_LIMIT = 52 * 1024 * 1024


def _cparams(sem, flags=None):
    return pltpu.CompilerParams(dimension_semantics=sem, vmem_limit_bytes=VMEM_LIMIT, flags=flags)


def _gelu(x):
    return 0.5 * x * (1.0 + lax.erf(x * SQRT_HALF))


def _rms(x, gain):
    return x * lax.rsqrt(jnp.mean(x * x, axis=-1, keepdims=True) + EPS) * gain


def _norm_head_pairs(x, gain2):
    lo = lax.broadcasted_iota(jnp.int32, x.shape, 1) < B_HEAD_DIM
    x2 = x * x
    s_lo = jnp.sum(jnp.where(lo, x2, 0.0), axis=-1, keepdims=True)
    s_hi = jnp.sum(jnp.where(lo, 0.0, x2), axis=-1, keepdims=True)
    ms = jnp.where(lo, s_lo, s_hi) * (1.0 / B_HEAD_DIM)
    return x * lax.rsqrt(ms + EPS) * gain2


def _t5_bucket_np(dist):
    n = np.maximum(dist, 0)
    max_exact = N_BUCKETS // 2
    nf = np.maximum(n, 1).astype(np.float32)
    large = max_exact + (np.log(nf / np.float32(max_exact)) / np.float32(math.log(MAX_DISTANCE / max_exact))
                         * np.float32(N_BUCKETS - max_exact)).astype(np.int32)
    return np.where(n < max_exact, n, np.minimum(large, N_BUCKETS - 1)).astype(np.int32)


def _bias_kernel(rb_ref, bkp_ref, bks_ref, op_ref, os_ref):
    bkp = bkp_ref[...]
    bks = bks_ref[...]
    for h in range(B_HEADS):
        accp = jnp.zeros(bkp.shape, f32)
        accs = jnp.zeros(bks.shape, f32)
        for b in range(N_BUCKETS):
            val = rb_ref[b, h]
            accp = jnp.where(bkp == b, val, accp)
            accs = jnp.where(bks == b, val, accs)
        op_ref[h] = accp
        os_ref[h:h + 1, :] = accs[0:1, :]


def _bias_tables(rel_bias):
    cj = np.arange(WINDOW)[:, None]
    qi = np.arange(WINDOW)[None, :]
    bkp = _t5_bucket_np((qi - cj) % WINDOW)
    j = np.arange(2 * WINDOW)
    dist_s = np.where(j < WINDOW, WINDOW - j, 0)
    bks = np.broadcast_to(_t5_bucket_np(dist_s)[None, :], (SUBLANES, 2 * WINDOW)).copy()
    return pl.pallas_call(
        _bias_kernel,
        out_shape=(jax.ShapeDtypeStruct((B_HEADS, WINDOW, WINDOW), f32),
                   jax.ShapeDtypeStruct((B_HEADS, 2 * WINDOW), f32)),
        in_specs=[pl.BlockSpec(memory_space=pltpu.SMEM),
                  pl.BlockSpec(memory_space=pltpu.VMEM),
                  pl.BlockSpec(memory_space=pltpu.VMEM)],
        out_specs=(pl.BlockSpec(memory_space=pltpu.VMEM), pl.BlockSpec(memory_space=pltpu.VMEM)),
        name="bias_tables",
    )(rel_bias, jnp.asarray(bkp), jnp.asarray(bks))


def _cast_job(src, layer, n_chunks, chunk_of_step):
    _, k, n = src.shape
    rows = k // n_chunks
    return (pl.BlockSpec((None, rows, n), lambda *g: (layer, chunk_of_step(*g), 0)),
            pl.BlockSpec((rows, n), lambda *g: (chunk_of_step(*g), 0)),
            jax.ShapeDtypeStruct((k, n), bf16))


def _cast_chunk(src_ref, dst_ref):
    dst_ref[...] = src_ref[...].astype(bf16)


def _mm_in_kernel(*refs, cast):
    if cast:
        x_ref, g_ref, w_ref, src_ref, z_ref, dst_ref, h_ref = refs
        _cast_chunk(src_ref, dst_ref)
    else:
        x_ref, g_ref, w_ref, z_ref, h_ref = refs

    @pl.when(pl.program_id(1) == 0)
    def _():
        h_ref[...] = _rms(x_ref[...], g_ref[...]).astype(bf16)

    z_ref[...] = jnp.dot(h_ref[...], w_ref[...], preferred_element_type=f32)


def _mm_in(x, gain, w, layer, tm, cast=None):
    M = x.shape[0]
    tn = MM_TILE_N[w.shape[-2:]]
    nj = Z_WIDTH // tn
    in_specs = [pl.BlockSpec((tm, D_MODEL), lambda i, j: (i, 0)),
                pl.BlockSpec((None, 1, D_MODEL), lambda i, j: (layer, 0, 0)),
                pl.BlockSpec((None, D_MODEL, tn), lambda i, j: (layer, 0, j))]
    out_specs = [pl.BlockSpec((tm, tn), lambda i, j: (i, j))]
    out_shape = [jax.ShapeDtypeStruct((M, Z_WIDTH), f32)]
    args = [x, gain, w]
    if cast:
        job = _cast_job(cast[0], cast[1], (M // tm) * nj, lambda i, j: i * nj + j)
        in_specs.append(job[0]); out_specs.append(job[1]); out_shape.append(job[2]); args.append(cast[0])
    out = pl.pallas_call(
        functools.partial(_mm_in_kernel, cast=bool(cast)),
        grid=(M // tm, nj),
        in_specs=in_specs, out_specs=out_specs, out_shape=out_shape,
        scratch_shapes=[pltpu.VMEM((tm, D_MODEL), bf16)],
        compiler_params=_cparams(("arbitrary", "arbitrary")),
        name="mm_in",
    )(*args)
    return out if cast else out[0]


def _mm_out_kernel(a_ref, b_ref, c_ref, x_ref, w_ref, o_ref, lhs_ref):
    @pl.when(pl.program_id(1) == 0)
    def _():
        lhs_ref[:, 0:A_WIDTH] = a_ref[...]
        lhs_ref[:, A_WIDTH:A_WIDTH + B_WIDTH] = b_ref[...]
        lhs_ref[:, A_WIDTH + B_WIDTH:D_MODEL] = c_ref[...]

    o_ref[...] = x_ref[...] + jnp.dot(lhs_ref[...], w_ref[...], preferred_element_type=f32)


def _mm_out(a, b, c, x, w, tm):
    M = x.shape[0]
    tn = MM_TILE_N[w.shape]
    return pl.pallas_call(
        _mm_out_kernel,
        grid=(M // tm, D_MODEL // tn),
        in_specs=[pl.BlockSpec((tm, A_WIDTH), lambda i, j: (i, 0)),
                  pl.BlockSpec((tm, B_WIDTH), lambda i, j: (i, 0)),
                  pl.BlockSpec((tm, C_WIDTH), lambda i, j: (i, 0)),
                  pl.BlockSpec((tm, tn), lambda i, j: (i, j)),
                  pl.BlockSpec((D_MODEL, tn), lambda i, j: (0, j))],
        out_specs=pl.BlockSpec((tm, tn), lambda i, j: (i, j)),
        out_shape=jax.ShapeDtypeStruct((M, D_MODEL), f32),
        scratch_shapes=[pltpu.VMEM((tm, D_MODEL), bf16)],
        compiler_params=_cparams(("parallel", "arbitrary")),
        name="mm_out",
    )(a, b, c, x, w)


def _mm_down_kernel(a_ref, x_ref, w_ref, o_ref):
    o_ref[...] = x_ref[...] + jnp.dot(a_ref[...], w_ref[...], preferred_element_type=f32)


def _mm_down(act, x, w, tm):
    M = x.shape[0]
    tn = MM_TILE_N[w.shape]
    return pl.pallas_call(
        _mm_down_kernel,
        grid=(M // tm, D_MODEL // tn),
        in_specs=[pl.BlockSpec((tm, D_FF), lambda i, j: (i, 0)),
                  pl.BlockSpec((tm, tn), lambda i, j: (i, j)),
                  pl.BlockSpec((D_FF, tn), lambda i, j: (0, j))],
        out_specs=pl.BlockSpec((tm, tn), lambda i, j: (i, j)),
        out_shape=jax.ShapeDtypeStruct((M, D_MODEL), f32),
        compiler_params=_cparams(("parallel", "arbitrary")),
        name="mm_down",
    )(act, x, w)


def _silu(x):
    return x * jax.nn.sigmoid(x)


def _mm_up_prompt_kernel(x_ref, n2_ref, wg_ref, wa_ref, cwg_ref, cwa_ref, cbg_ref, cba_ref, src_ref,
                         act_ref, cs_ref, dst_ref, h_ref, carry_ref, zb_ref,
                         *, tm, nj, n_steps, tiles_per_seq):
    _cast_chunk(src_ref, dst_ref)
    s = pl.program_id(0)
    sa = jnp.minimum(s, n_steps - 1)
    ia = sa // nj
    ja = sa % nj
    tf = act_ref.shape[1]

    @pl.when(s == 0)
    def _():
        zb_ref[...] = jnp.zeros(zb_ref.shape, f32)
        carry_ref[...] = jnp.zeros(carry_ref.shape, f32)

    @pl.when(ja == 0)
    def _():
        h_ref[...] = _rms(x_ref[...], n2_ref[...]).astype(bf16)

    chunks = [slice(c0, c0 + UP_CHUNK) for c0 in range(0, tf, UP_CHUNK)]

    rblocks = [(r0, min(UP_ROWS, tm - r0)) for r0 in range(0, tm, UP_ROWS)]

    def conv(idx, cw_ref, cb_ref, cs, r0, nr):
        zz = zb_ref[idx, r0:r0 + SUBLANES + nr, cs]
        z1 = pltpu.roll(zz, 1, 0)[SUBLANES:]
        z2 = pltpu.roll(zz, 2, 0)[SUBLANES:]
        return (cb_ref[:, cs] + z2 * cw_ref[0:1, cs] + z1 * cw_ref[1:2, cs] + zz[SUBLANES:] * cw_ref[2:3, cs])

    for cs in chunks:
        for r0, nr in rblocks:
            g = conv(0, cwg_ref, cbg_ref, cs, r0, nr)
            a = conv(1, cwa_ref, cba_ref, cs, r0, nr)
            act_ref[r0:r0 + nr, cs] = (_silu(g) * a).astype(bf16)

    seq_start = (ia % tiles_per_seq) == 0
    for cs in chunks:
        for r0, nr in rblocks:
            for idx, w_ref in ((0, wg_ref), (1, wa_ref)):
                z = jnp.dot(h_ref[r0:r0 + nr, :], w_ref[:, cs], preferred_element_type=f32)
                if r0 == 0:
                    zb_ref[idx, 0:SUBLANES, cs] = jnp.where(seq_start, 0.0, carry_ref[idx, ja, :, cs])
                zb_ref[idx, SUBLANES + r0:SUBLANES + r0 + nr, cs] = z
                if r0 + nr == tm:
                    tail = z[nr - SUBLANES:nr, :]
                    carry_ref[idx, ja, :, cs] = tail
                    cs_ref[0, idx, :, cs] = tail


def _mm_up_prompt(x, norm2, w_up, conv_w, conv_b, layer, batch, tm, cast):
    M = x.shape[0]
    seq = M // batch
    tiles_per_seq = seq // tm
    tf = MM_TILE_N[w_up.shape]
    nj = D_FF // tf
    n_steps = (M // tm) * nj
    job = _cast_job(cast[0], cast[1], n_steps, lambda s: jnp.minimum(s, n_steps - 1))
    kern = functools.partial(_mm_up_prompt_kernel, tm=tm, nj=nj, n_steps=n_steps, tiles_per_seq=tiles_per_seq)
    ia = lambda s: jnp.minimum(s, n_steps - 1) // nj
    ja = lambda s: jnp.minimum(s, n_steps - 1) % nj
    ib = lambda s: jnp.maximum(s - 1, 0) // nj
    jb = lambda s: jnp.maximum(s - 1, 0) % nj
    return pl.pallas_call(
        kern,
        grid=(n_steps + 1,),
        in_specs=[pl.BlockSpec((tm, D_MODEL), lambda s: (ia(s), 0)),
                  pl.BlockSpec((None, 1, D_MODEL), lambda s: (layer, 0, 0)),
                  pl.BlockSpec((D_MODEL, tf), lambda s: (0, ja(s))),
                  pl.BlockSpec((D_MODEL, tf), lambda s: (0, nj + ja(s))),
                  pl.BlockSpec((None, CONV_W, tf), lambda s: (layer, 0, jb(s))),
                  pl.BlockSpec((None, CONV_W, tf), lambda s: (layer, 0, nj + jb(s))),
                  pl.BlockSpec((None, 1, tf), lambda s: (layer, 0, jb(s))),
                  pl.BlockSpec((None, 1, tf), lambda s: (layer, 0, nj + jb(s))),
                  job[0]],
        out_specs=(pl.BlockSpec((tm, tf), lambda s: (ib(s), jb(s))),
                   pl.BlockSpec((1, 2, SUBLANES, tf), lambda s: (ia(s), 0, 0, ja(s))),
                   job[1]),
        out_shape=(jax.ShapeDtypeStruct((M, D_FF), bf16),
                   jax.ShapeDtypeStruct((M // tm, 2, SUBLANES, D_FF), f32),
                   job[2]),
        scratch_shapes=[pltpu.VMEM((tm, D_MODEL), bf16),
                        pltpu.VMEM((2, nj, SUBLANES, tf), f32),
                        pltpu.VMEM((2, tm + SUBLANES, tf), f32)],
        compiler_params=_cparams(("arbitrary",)),
        name="mm_up_prompt",
    )(x, norm2, w_up, w_up, conv_w, conv_w, conv_b, conv_b, cast[0])


def _mm_up_sample_kernel(x_ref, n2_ref, wg_ref, wa_ref, cwg_ref, cwa_ref, cbg_ref, cba_ref,
                         b0g_ref, b1g_ref, b0a_ref, b1a_ref, act_ref, zg_ref, za_ref, h_ref):
    @pl.when(pl.program_id(0) == 0)
    def _():
        h_ref[...] = _rms(x_ref[...], n2_ref[...]).astype(bf16)

    def conv_half(w_ref, cw_ref, cb_ref, b0_ref, b1_ref, z_ref):
        z = jnp.dot(h_ref[...], w_ref[...], preferred_element_type=f32)
        z_ref[...] = z
        return (cb_ref[...] + b0_ref[...] * cw_ref[0:1, :] + b1_ref[...] * cw_ref[1:2, :]
                + z * cw_ref[2:3, :])

    g = conv_half(wg_ref, cwg_ref, cbg_ref, b0g_ref, b1g_ref, zg_ref)
    a = conv_half(wa_ref, cwa_ref, cba_ref, b0a_ref, b1a_ref, za_ref)
    act_ref[...] = (_silu(g) * a).astype(bf16)


def _mm_up_sample(x, norm2, w_up, conv_w, conv_b, buf, layer):
    M = x.shape[0]
    tf = MM_TILE_N[w_up.shape]
    nj = D_FF // tf
    wspec = lambda off: pl.BlockSpec((D_MODEL, tf), lambda j: (0, off + j))
    cwspec = lambda off: pl.BlockSpec((None, CONV_W, tf), lambda j: (layer, 0, off + j))
    cbspec = lambda off: pl.BlockSpec((None, 1, tf), lambda j: (layer, 0, off + j))
    bufspec = lambda row, off: pl.BlockSpec((None, None, M, tf), lambda j: (layer, row, 0, off + j))
    return pl.pallas_call(
        _mm_up_sample_kernel,
        grid=(nj,),
        in_specs=[pl.BlockSpec((M, D_MODEL), lambda j: (0, 0)),
                  pl.BlockSpec((None, 1, D_MODEL), lambda j: (layer, 0, 0)),
                  wspec(0), wspec(nj), cwspec(0), cwspec(nj), cbspec(0), cbspec(nj),
                  bufspec(0, 0), bufspec(1, 0), bufspec(0, nj), bufspec(1, nj)],
        out_specs=(pl.BlockSpec((M, tf), lambda j: (0, j)),
                   pl.BlockSpec((M, tf), lambda j: (0, j)),
                   pl.BlockSpec((M, tf), lambda j: (0, j))),
        out_shape=(jax.ShapeDtypeStruct((M, D_FF), bf16),
                   jax.ShapeDtypeStruct((M, D_FF), f32),
                   jax.ShapeDtypeStruct((M, D_FF), f32)),
        scratch_shapes=[pltpu.VMEM((M, D_MODEL), bf16)],
        compiler_params=_cparams(("arbitrary",)),
        name="mm_up_sample",
    )(x, norm2, w_up, w_up, conv_w, conv_w, conv_b, conv_b, buf, buf, buf, buf)


def _mix_a_prompt_kernel(u_ref, v_ref, vg_ref, ws_ref, bs_ref, o_ref):
    row = lax.broadcasted_iota(jnp.int32, (CHUNK, CHUNK), 0)
    col = lax.broadcasted_iota(jnp.int32, (CHUNK, CHUNK), 1)
    tri = col <= row
    for g in range(A_GROUPS):
        sl = slice(g * A_DIM, (g + 1) * A_DIM)
        ws = jnp.where(tri, ws_ref[g], 0.0).astype(bf16)
        for r0 in range(0, u_ref.shape[0], CHUNK):
            rs = slice(r0, r0 + CHUNK)
            vn = _rms(_gelu(v_ref[rs, sl]), vg_ref[:, sl])
            mixv = jnp.dot(ws, vn.astype(bf16), preferred_element_type=f32) + bs_ref[:, sl]
            o_ref[rs, sl] = (_gelu(u_ref[rs, sl]) * mixv).astype(bf16)


def _mix_a_prompt(z, v_gain, w_s, b_s_full, layer, rows):
    M = z.shape[0]
    return pl.pallas_call(
        _mix_a_prompt_kernel,
        grid=(M // rows,),
        in_specs=[pl.BlockSpec((rows, A_WIDTH), lambda r: (r, Z_AU // A_WIDTH)),
                  pl.BlockSpec((rows, A_WIDTH), lambda r: (r, Z_AV // A_WIDTH)),
                  pl.BlockSpec((None, 1, A_WIDTH), lambda r: (layer, 0, 0)),
                  pl.BlockSpec((None, A_GROUPS, CHUNK, CHUNK), lambda r: (layer, 0, 0, 0)),
                  pl.BlockSpec((None, CHUNK, A_WIDTH), lambda r: (layer, 0, 0))],
        out_specs=pl.BlockSpec((rows, A_WIDTH), lambda r: (r, 0)),
        out_shape=jax.ShapeDtypeStruct((M, A_WIDTH), bf16),
        compiler_params=_cparams(("parallel",)),
        name="mix_a_prompt",
    )(z, z, v_gain, w_s, b_s_full)


def _mix_a_sample_kernel(u_ref, v_ref, vg_ref, ws_ref, bs_ref, o_ref, vn_ref):
    for g in range(A_GROUPS):
        sl = slice(g * A_DIM, (g + 1) * A_DIM)
        vn = _rms(_gelu(v_ref[:, sl]), vg_ref[:, sl])
        vn_ref[:, sl] = vn
        mixv = ws_ref[:, sl] * vn + bs_ref[:, sl]
        o_ref[:, sl] = (_gelu(u_ref[:, sl]) * mixv).astype(bf16)


def _mix_a_sample(z, v_gain, ws0, bs0, layer):
    M = z.shape[0]
    vec = pl.BlockSpec((None, 1, A_WIDTH), lambda r: (layer, 0, 0))
    return pl.pallas_call(
        _mix_a_sample_kernel,
        grid=(1,),
        in_specs=[pl.BlockSpec((M, A_WIDTH), lambda r: (0, Z_AU // A_WIDTH)),
                  pl.BlockSpec((M, A_WIDTH), lambda r: (0, Z_AV // A_WIDTH)),
                  vec, vec, vec],
        out_specs=(pl.BlockSpec((M, A_WIDTH), lambda r: (0, 0)),
                   pl.BlockSpec((M, A_WIDTH), lambda r: (0, 0))),
        out_shape=(jax.ShapeDtypeStruct((M, A_WIDTH), bf16),
                   jax.ShapeDtypeStruct((M, A_WIDTH), f32)),
        compiler_params=_cparams(("arbitrary",)),
        name="mix_a_sample",
    )(z, z, v_gain, ws0, bs0)


def _mix_b_prompt_kernel(sink_ref, q_ref, k_ref, v_ref, qg_ref, kg_ref, bias_ref, src_ref,
                         o_ref, klast_ref, vlast_ref, dst_ref, kcat_ref, vcat_ref):
    _cast_chunk(src_ref, dst_ref)
    i = pl.program_id(1)
    W = WINDOW
    kn = _norm_head_pairs(k_ref[...], kg_ref[...])
    v = v_ref[...]
    klast_ref[0] = kn
    vlast_ref[0] = v

    @pl.when(i == 0)
    def _():
        kcat_ref[0:W, :] = jnp.zeros((W, LANES), f32)
        vcat_ref[0:W, :] = jnp.zeros((W, LANES), f32)

    @pl.when(i > 0)
    def _():
        kcat_ref[0:W, :] = kcat_ref[W:2 * W, :]
        vcat_ref[0:W, :] = vcat_ref[W:2 * W, :]

    kcat_ref[W:2 * W, :] = kn
    vcat_ref[W:2 * W, :] = v
    kc = kcat_ref[...]
    vc = vcat_ref[...]
    kc_sw = pltpu.roll(kc, B_HEAD_DIM, 1)
    vc_sw = pltpu.roll(vc, B_HEAD_DIM, 1)
    lo = lax.broadcasted_iota(jnp.int32, (2 * W, LANES), 1) < B_HEAD_DIM

    cj = lax.broadcasted_iota(jnp.int32, (W, W), 0)
    qi = lax.broadcasted_iota(jnp.int32, (W, W), 1)
    own = cj <= qi
    keep = own | (i > 0)
    q_gain = qg_ref[...] * (B_HEAD_DIM ** -0.5)

    tiles = GQA // 2
    nt = (((1,), (1,)), ((), ()))
    tn = (((0,), (0,)), ((), ()))
    scores, values = [], []
    for kvh in range(B_KV_HEADS):
        k_src, k_alt = (kc, kc_sw) if kvh == 0 else (kc_sw, kc)
        v_src, v_alt = (vc, vc_sw) if kvh == 0 else (vc_sw, vc)
        k_even = jnp.where(lo, k_src, 0.0).astype(bf16)
        k_odd = jnp.where(lo, 0.0, k_alt).astype(bf16)
        values.append((jnp.where(lo, v_src, 0.0).astype(bf16), jnp.where(lo, 0.0, v_alt).astype(bf16)))
        qts = []
        for t in range(tiles):
            c0 = (kvh * tiles + t) * LANES
            qt = q_ref[:, c0:c0 + LANES].T
            q2 = qt * qt
            halves = []
            for r0 in (0, B_HEAD_DIM):
                ms = jnp.mean(q2[r0:r0 + B_HEAD_DIM, :], axis=0, keepdims=True)
                halves.append(qt[r0:r0 + B_HEAD_DIM, :] * lax.rsqrt(ms + EPS))
            qts.append(jnp.concatenate(halves, axis=0) * q_gain)
        qst = jnp.concatenate(qts, axis=1).astype(bf16)
        scores.append((jnp.dot(k_even, qst, preferred_element_type=f32),
                       jnp.dot(k_odd, qst, preferred_element_type=f32)))
    for kvh in range(B_KV_HEADS):
        p_par = []
        for par in range(2):
            blocks = []
            for t in range(tiles):
                h = kvh * GQA + 2 * t + par
                sb = scores[kvh][par][:, t * W:(t + 1) * W]
                s = jnp.where(own, sb[W:2 * W, :], sb[0:W, :]) + bias_ref[h]
                s = jnp.where(keep, s, NEG_INF)
                sk = sink_ref[h]
                mx = jnp.maximum(jnp.max(s, axis=0, keepdims=True), sk)
                p = jnp.exp(s - mx)
                den = jnp.sum(p, axis=0, keepdims=True) + jnp.exp(sk - mx)
                p = p * (1.0 / den)
                blocks.append(jnp.concatenate([jnp.where(own, 0.0, p), jnp.where(own, p, 0.0)],
                                              axis=0).astype(bf16))
            p_par.append(jnp.concatenate(blocks, axis=1))
        ot = (lax.dot_general(values[kvh][0], p_par[0], tn, preferred_element_type=f32)
              + lax.dot_general(values[kvh][1], p_par[1], tn, preferred_element_type=f32))
        for t in range(tiles):
            c0 = (kvh * tiles + t) * LANES
            o_ref[:, c0:c0 + LANES] = ot[:, t * W:(t + 1) * W].T.astype(bf16)


def _mix_b_prompt(z, sinks, q_gain_t, k_gain2, bias_p, layer, batch, cast):
    M = z.shape[0]
    nb = M // batch // WINDOW
    job = _cast_job(cast[0], cast[1], batch * nb, lambda b, i: b * nb + i)
    return pl.pallas_call(
        _mix_b_prompt_kernel,
        grid=(batch, nb),
        in_specs=[pl.BlockSpec(memory_space=pltpu.SMEM),
                  pl.BlockSpec((WINDOW, B_WIDTH), lambda b, i: (b * nb + i, Z_BQ // B_WIDTH)),
                  pl.BlockSpec((WINDOW, B_KV_WIDTH), lambda b, i: (b * nb + i, Z_BK // B_KV_WIDTH)),
                  pl.BlockSpec((WINDOW, B_KV_WIDTH), lambda b, i: (b * nb + i, Z_BV // B_KV_WIDTH)),
                  pl.BlockSpec((None, LANES, WINDOW), lambda b, i: (layer, 0, 0)),
                  pl.BlockSpec((None, 1, LANES), lambda b, i: (layer, 0, 0)),
                  pl.BlockSpec((B_HEADS, WINDOW, WINDOW), lambda b, i: (0, 0, 0)),
                  job[0]],
        out_specs=(pl.BlockSpec((WINDOW, B_WIDTH), lambda b, i: (b * nb + i, 0)),
                   pl.BlockSpec((1, WINDOW, B_KV_WIDTH), lambda b, i: (b, 0, 0)),
                   pl.BlockSpec((1, WINDOW, B_KV_WIDTH), lambda b, i: (b, 0, 0)),
                   job[1]),
        out_shape=(jax.ShapeDtypeStruct((M, B_WIDTH), bf16),
                   jax.ShapeDtypeStruct((batch, WINDOW, B_KV_WIDTH), f32),
                   jax.ShapeDtypeStruct((batch, WINDOW, B_KV_WIDTH), f32),
                   job[2]),
        scratch_shapes=[pltpu.VMEM((2 * WINDOW, LANES), f32),
                        pltpu.VMEM((2 * WINDOW, LANES), f32)],
        compiler_params=_cparams(("arbitrary", "arbitrary")),
        name="mix_b_prompt",
    )(sinks, z, z, z, q_gain_t, k_gain2, bias_p, cast[0])


def _mix_c_prompt_kernel(q_ref, k_ref, v0_ref, v1_ref, o0_ref, o1_ref, g_ref, gb_ref, hgt_ref,
                         out_ref, c_out, n_out, m_out, c_s, n_s, m_s):
    L = MLSTM_L
    c = pl.program_id(1)

    @pl.when(c == 0)
    def _():
        c_s[...] = jnp.zeros(c_s.shape, f32)
        n_s[...] = jnp.zeros(n_s.shape, f32)
        m_s[...] = jnp.zeros(m_s.shape, f32)

    row = lax.broadcasted_iota(jnp.int32, (L, L), 0)
    col = lax.broadcasted_iota(jnp.int32, (L, L), 1)
    tri = col <= row
    tri_f = tri.astype(f32)
    seqs = range(q_ref.shape[0])
    units = [(bi, h) for bi in seqs for h in range(C_HEADS)]
    g, b = {}, {}
    for bi in seqs:
        g[bi] = g_ref[bi] + gb_ref[...]
        b[bi] = jnp.dot(tri_f, jax.nn.log_sigmoid(g[bi]), precision=lax.Precision.HIGHEST,
                        preferred_element_type=f32)
    bT = {bi: b[bi].T for bi in seqs}
    vis = row <= col
    nt = (((1,), (1,)), ((), ()))
    tn = (((0,), (0,)), ((), ()))
    qb, kh, vb, ct_prev, n_prev, st, qct, qn_lin = {}, {}, {}, {}, {}, {}, {}, {}
    for u in units:
        bi, h = u
        v_ref = v0_ref if h < 2 else v1_ref
        vsl = slice((h % 2) * C_V_DIM, (h % 2 + 1) * C_V_DIM)
        qb[u] = q_ref[bi, :, h * C_QK_DIM:(h + 1) * C_QK_DIM].astype(bf16)
        kh[u] = k_ref[bi, :, h * C_QK_DIM:(h + 1) * C_QK_DIM] * (C_QK_DIM ** -0.5)
        vb[u] = v_ref[bi, :, vsl].astype(bf16)
        ct_prev[u] = c_s[bi, h]
        n_prev[u] = n_s[bi, h]
        st[u] = lax.dot_general(kh[u].astype(bf16), qb[u], nt, preferred_element_type=f32)
        qct[u] = lax.dot_general(ct_prev[u].astype(bf16), qb[u], nt, preferred_element_type=f32)
        n8 = jnp.broadcast_to(n_prev[u], (SUBLANES, C_QK_DIM)).astype(bf16)
        qn_lin[u] = lax.dot_general(n8, qb[u], nt, preferred_element_type=f32)[0:1, :]
    pt, iw, inv, kw, decay, m_new = {}, {}, {}, {}, {}, {}
    for u in units:
        bi, h = u
        b_row = bT[bi][C_HEADS + h:C_HEADS + h + 1, :]
        src = g[bi][:, h:h + 1] - b[bi][:, C_HEADS + h:C_HEADS + h + 1]
        dlog = jnp.where(vis, b_row + src, -jnp.inf)
        m_prev = m_s[bi, h][0:1, 0:1]
        inter = b_row + m_prev
        mt = jnp.maximum(inter, jnp.max(dlog, axis=0, keepdims=True))
        dt = jnp.exp(dlog - mt)
        pt[u] = st[u] * dt
        iw[u] = jnp.exp(inter - mt)
        qn = iw[u] * qn_lin[u] + jnp.sum(pt[u], axis=0, keepdims=True)
        inv[u] = 1.0 / jnp.maximum(jnp.abs(qn), jnp.exp(-mt))
        m_new[u] = mt[:, L - 1:L]
        decay[u] = jnp.exp(b_row[:, L - 1:L] + m_prev - m_new[u])
        kw[u] = kh[u] * dt[:, L - 1:L]
    svt = {u: lax.dot_general(vb[u], pt[u].astype(bf16), tn, preferred_element_type=f32) for u in units}
    kvt = {u: lax.dot_general(vb[u], kw[u].astype(bf16), tn, preferred_element_type=f32) for u in units}
    for u in units:
        bi, h = u
        o_ref = o0_ref if h < 2 else o1_ref
        vsl = slice((h % 2) * C_V_DIM, (h % 2 + 1) * C_V_DIM)
        ht = (iw[u] * qct[u] + svt[u]) * inv[u]
        ms = jnp.mean(ht * ht, axis=0, keepdims=True)
        hn = (ht * lax.rsqrt(ms + EPS) * hgt_ref[h]).T
        c_s[bi, h] = decay[u] * ct_prev[u] + kvt[u]
        n_s[bi, h] = decay[u] * n_prev[u] + jnp.sum(kw[u], axis=0, keepdims=True)
        m_s[bi, h] = jnp.broadcast_to(m_new[u], (SUBLANES, LANES))
        osl = slice(h * C_V_DIM, (h + 1) * C_V_DIM)
        out_ref[bi, :, osl] = (hn * jax.nn.sigmoid(o_ref[bi, :, vsl])).astype(bf16)

    @pl.when(c == pl.num_programs(1) - 1)
    def _():
        eye = (row == col).astype(f32)
        for bi, h in units:
            c_out[bi, h] = lax.dot_general(c_s[bi, h], eye, tn, precision=lax.Precision.HIGHEST,
                                           preferred_element_type=f32)
        n_out[...] = n_s[...]
        m_out[...] = m_s[...]


def _mix_c_prompt(z, gate_bias, h_gain, layer, batch):
    M = z.shape[0]
    L = MLSTM_L
    T = M // batch
    z3 = z.reshape(batch, T, Z_WIDTH)
    ns = C_SEQS if batch % C_SEQS == 0 else 1
    col = lambda width, off: pl.BlockSpec((ns, L, width), lambda b, c: (b, c, off // width))
    state = lambda *shape: pl.BlockSpec((ns,) + shape, lambda b, c: (b,) + (0,) * len(shape))
    out, c_st, n_st, m_st = pl.pallas_call(
        _mix_c_prompt_kernel,
        grid=(batch // ns, T // L),
        in_specs=[col(C_QK_WIDTH, Z_CQ), col(C_QK_WIDTH, Z_CK),
                  col(C_PAIR, Z_CV), col(C_PAIR, Z_CV + C_PAIR),
                  col(C_PAIR, Z_CO), col(C_PAIR, Z_CO + C_PAIR),
                  col(LANES, Z_GATE),
                  pl.BlockSpec((None, 1, LANES), lambda b, c: (layer, 0, 0)),
                  pl.BlockSpec((None, C_HEADS, C_V_DIM, L), lambda b, c: (layer, 0, 0, 0))],
        out_specs=(pl.BlockSpec((ns, L, C_WIDTH), lambda b, c: (b, c, 0)),
                   state(C_HEADS, C_QK_DIM, C_V_DIM),
                   state(C_HEADS, 1, C_QK_DIM),
                   state(C_HEADS, SUBLANES, LANES)),
        out_shape=(jax.ShapeDtypeStruct((batch, T, C_WIDTH), bf16),
                   jax.ShapeDtypeStruct((batch, C_HEADS, C_QK_DIM, C_V_DIM), f32),
                   jax.ShapeDtypeStruct((batch, C_HEADS, 1, C_QK_DIM), f32),
                   jax.ShapeDtypeStruct((batch, C_HEADS, SUBLANES, LANES), f32)),
        scratch_shapes=[pltpu.VMEM((ns, C_HEADS, C_V_DIM, C_QK_DIM), f32),
                        pltpu.VMEM((ns, C_HEADS, 1, C_QK_DIM), f32),
                        pltpu.VMEM((ns, C_HEADS, SUBLANES, LANES), f32)],
        compiler_params=_cparams(("arbitrary", "arbitrary")),
        name="mix_c_prompt",
    )(z3, z3, z3, z3, z3, z3, z3, gate_bias, h_gain)
    return out.reshape(M, C_WIDTH), c_st, n_st, m_st


_BC_SHARED = (4, 5, 6, 7, 12, 13)


def _mix_bc_sample_kernel(*refs):
    rows = [_mix_bc_sample_row(*[ref if k in _BC_SHARED else ref.at[pl.ds(r, 1)] for k, ref in enumerate(refs)])
            for r in range(refs[0].shape[0])]
    while rows:
        rows = [row for row in rows if next(row, True) is None]


def _mix_bc_sample_row(zr_ref, q_ref, kp_ref, vp_ref, qg_ref, kg_ref, sink_ref, bias_ref,
                       cq_ref, ck_ref, cv_ref, co_ref, gb_ref, hg_ref, c0_ref, n0_ref, m0_ref,
                       bo_ref, kn_ref, co_out, c_out, n_out, m_out):
    W = WINDOW
    zr = zr_ref[0]

    qn = _rms(q_ref[0], qg_ref[...])
    hrow = lax.broadcasted_iota(jnp.int32, (B_HEADS, LANES), 0)
    lane = lax.broadcasted_iota(jnp.int32, (B_HEADS, LANES), 1)
    own = (hrow < GQA) == (lane < B_HEAD_DIM)
    q2 = jnp.where(own, jnp.concatenate([qn, qn], axis=-1), 0.0)
    k_new = _norm_head_pairs(zr[:, Z_BK:Z_BK + B_KV_WIDTH], kg_ref[...])
    v_new = zr[:, Z_BV:Z_BV + B_KV_WIDTH]
    kn_ref[0] = k_new
    scale = B_HEAD_DIM ** -0.5
    s_past = lax.dot_general(q2.astype(bf16), kp_ref[0].astype(bf16), (((1,), (1,)), ((), ())),
                             preferred_element_type=f32) * scale + bias_ref[:, 0:W]
    yield
    s_past = jnp.where(lane >= 1, s_past, NEG_INF)
    s_new = jnp.sum(q2 * k_new, axis=-1, keepdims=True) * scale + bias_ref[:, W:W + 1]
    sk = sink_ref[...]
    mx = jnp.maximum(jnp.maximum(jnp.max(s_past, axis=-1, keepdims=True), s_new), sk)
    p_past = jnp.exp(s_past - mx)
    p_new = jnp.exp(s_new - mx)
    den = jnp.sum(p_past, axis=-1, keepdims=True) + p_new + jnp.exp(sk - mx)
    inv = 1.0 / den
    o2 = (jnp.dot((p_past * inv).astype(bf16), vp_ref[0].astype(bf16), preferred_element_type=f32)
          + (p_new * inv) * v_new)
    o2_sw = pltpu.roll(o2, B_HEAD_DIM, 1)
    bo_ref[0] = jnp.where(hrow < GQA, o2, o2_sw)[:, 0:B_HEAD_DIM].astype(bf16)

    g = zr[:, Z_GATE:Z_GATE + LANES] + gb_ref[...]
    lf = jax.nn.log_sigmoid(g)
    m0 = m0_ref[0]
    lane1 = lax.broadcasted_iota(jnp.int32, (1, LANES), 1)
    m_row = jnp.zeros((1, LANES), f32)
    row8 = lax.broadcasted_iota(jnp.int32, (SUBLANES, C_QK_DIM), 0)
    c_rows = []
    staged = []
    for h in range(C_HEADS):
        ig = g[:, h:h + 1]
        b = lf[:, C_HEADS + h:C_HEADS + h + 1]
        m_prev = m0[:, h:h + 1]
        inter = b + m_prev
        mt = jnp.maximum(inter, ig)
        qh = cq_ref[0, h:h + 1, :]
        kh = ck_ref[0, h:h + 1, :] * (C_QK_DIM ** -0.5)
        vh = cv_ref[0, h:h + 1, :]
        c_prev = c0_ref[0, h]
        n_prev = n0_ref[0, h:h + 1, :]
        s = jnp.sum(qh * kh, axis=-1, keepdims=True) * jnp.exp(ig - mt)
        iw = jnp.exp(inter - mt)
        q8 = jnp.broadcast_to(qh, (SUBLANES, C_QK_DIM)).astype(bf16)
        qc = jnp.dot(q8, c_prev.astype(bf16), preferred_element_type=f32)[0:1, :]
        staged.append((ig, inter, mt, qh, kh, vh, c_prev, n_prev, s, iw, qc))
    yield
    for h, (ig, inter, mt, qh, kh, vh, c_prev, n_prev, s, iw, qc) in enumerate(staged):
        num = iw * qc + s * vh
        qn_ = iw * jnp.sum(qh * n_prev, axis=-1, keepdims=True) + s
        hh = num / jnp.maximum(jnp.abs(qn_), jnp.exp(-mt))
        wc = jnp.exp(ig - mt)
        decay = jnp.exp(inter - mt)
        kw = kh * wc
        kw8 = jnp.where(row8 == 0, jnp.broadcast_to(kw, (SUBLANES, C_QK_DIM)), 0.0).astype(bf16)
        v8 = jnp.broadcast_to(vh, (SUBLANES, C_V_DIM)).astype(bf16)
        c_out[0, h] = decay * c_prev + lax.dot_general(kw8, v8, (((0,), (0,)), ((), ())),
                                                      preferred_element_type=f32)
        n_out[0, h:h + 1, :] = decay * n_prev + kw
        m_row = jnp.where(lane1 == h, mt, m_row)
        c_rows.append(_rms(hh, hg_ref[h:h + 1, :]) * jax.nn.sigmoid(co_ref[0, h:h + 1, :]))
    co_out[0] = jnp.concatenate(c_rows, axis=0).astype(bf16)
    m_out[0] = m_row


def _mix_bc_sample(z, k_past, v_past, q_gain, k_gain2, sinks, bias_s, gate_bias, h_gain4,
                   c0, n0, m0, layer):
    R = z.shape[0]
    zr = z.reshape(R, 1, Z_WIDTH)
    q = z[:, Z_BQ:Z_BQ + B_WIDTH].reshape(R, B_HEADS, B_HEAD_DIM)
    cq = z[:, Z_CQ:Z_CQ + C_QK_WIDTH].reshape(R, C_HEADS, C_QK_DIM)
    ck = z[:, Z_CK:Z_CK + C_QK_WIDTH].reshape(R, C_HEADS, C_QK_DIM)
    cv = z[:, Z_CV:Z_CV + C_WIDTH].reshape(R, C_HEADS, C_V_DIM)
    co = z[:, Z_CO:Z_CO + C_WIDTH].reshape(R, C_HEADS, C_V_DIM)
    rb = DEC_ROWS if R % DEC_ROWS == 0 else 1
    row3 = lambda n: pl.BlockSpec((rb, 1, n), lambda r: (r, 0, 0))
    return pl.pallas_call(
        _mix_bc_sample_kernel,
        grid=(R // rb,),
        in_specs=[row3(Z_WIDTH),
                  pl.BlockSpec((rb, B_HEADS, B_HEAD_DIM), lambda r: (r, 0, 0)),
                  pl.BlockSpec((None, rb, WINDOW, B_KV_WIDTH), lambda r: (layer, r, 0, 0)),
                  pl.BlockSpec((None, rb, WINDOW, B_KV_WIDTH), lambda r: (layer, r, 0, 0)),
                  pl.BlockSpec((None, 1, B_HEAD_DIM), lambda r: (layer, 0, 0)),
                  pl.BlockSpec((None, 1, LANES), lambda r: (layer, 0, 0)),
                  pl.BlockSpec((None, B_HEADS, 1), lambda r: (layer, 0, 0)),
                  pl.BlockSpec((B_HEADS, 2 * WINDOW), lambda r: (0, 0)),
                  pl.BlockSpec((rb, C_HEADS, C_QK_DIM), lambda r: (r, 0, 0)),
                  pl.BlockSpec((rb, C_HEADS, C_QK_DIM), lambda r: (r, 0, 0)),
                  pl.BlockSpec((rb, C_HEADS, C_V_DIM), lambda r: (r, 0, 0)),
                  pl.BlockSpec((rb, C_HEADS, C_V_DIM), lambda r: (r, 0, 0)),
                  pl.BlockSpec((None, 1, LANES), lambda r: (layer, 0, 0)),
                  pl.BlockSpec((None, C_HEADS, C_V_DIM), lambda r: (layer, 0, 0)),
                  pl.BlockSpec((None, rb, C_HEADS, C_QK_DIM, C_V_DIM), lambda r: (layer, r, 0, 0, 0)),
                  pl.BlockSpec((None, rb, C_HEADS, C_QK_DIM), lambda r: (layer, r, 0, 0)),
                  pl.BlockSpec((None, rb, 1, C_HEADS), lambda r: (layer, r, 0, 0))],
        out_specs=(pl.BlockSpec((rb, B_HEADS, B_HEAD_DIM), lambda r: (r, 0, 0)),
                   row3(B_KV_WIDTH),
                   pl.BlockSpec((rb, C_HEADS, C_V_DIM), lambda r: (r, 0, 0)),
                   pl.BlockSpec((rb, C_HEADS, C_QK_DIM, C_V_DIM), lambda r: (r, 0, 0, 0)),
                   pl.BlockSpec((rb, C_HEADS, C_QK_DIM), lambda r: (r, 0, 0)),
                   row3(LANES)),
        out_shape=(jax.ShapeDtypeStruct((R, B_HEADS, B_HEAD_DIM), bf16),
                   jax.ShapeDtypeStruct((R, 1, B_KV_WIDTH), f32),
                   jax.ShapeDtypeStruct((R, C_HEADS, C_V_DIM), bf16),
                   jax.ShapeDtypeStruct((R, C_HEADS, C_QK_DIM, C_V_DIM), f32),
                   jax.ShapeDtypeStruct((R, C_HEADS, C_QK_DIM), f32),
                   jax.ShapeDtypeStruct((R, 1, LANES), f32)),
        compiler_params=_cparams(("parallel",)),
        name="mix_bc_sample",
    )(zr, q, k_past, v_past, q_gain, k_gain2, sinks, bias_s, cq, ck, cv, co, gate_bias, h_gain4,
      c0, n0, m0)


def _prep_w_in(w_in):
    pad = jnp.zeros(w_in.shape[:-1] + (Z_WIDTH - w_in.shape[-1],), bf16)
    return jnp.concatenate([w_in.astype(bf16), pad], axis=-1)


def _row_tile(m, pref):
    return pref if m % pref == 0 else m


def kernel(x_prompt, x_sample, cache_swa_k, cache_swa_v, state_mlstm_C, state_mlstm_n, state_mlstm_m, state_ffn_conv, rel_bias, norm1, w_in, a_v_gain, a_spatial_w, a_spatial_b, b_q_gain, b_k_gain, b_sinks, c_gate_bias, c_h_gain, w_out, norm2, w_up, ffn_conv_w, ffn_conv_b, w_down):
    depth = w_in.shape[0]
    Bp, T, _ = x_prompt.shape
    R = x_sample.shape[0]
    assert x_sample.shape[1] == 1 and T % CHUNK == 0 and T % MLSTM_L == 0

    w_in_p = _prep_w_in(w_in)
    norm1_3 = norm1.reshape(depth, 1, D_MODEL)
    norm2_3 = norm2.reshape(depth, 1, D_MODEL)
    v_gain3 = a_v_gain.reshape(depth, 1, A_WIDTH)
    bs_full = jnp.repeat(jnp.swapaxes(a_spatial_b, 1, 2), A_DIM, axis=-1)
    ws0 = jnp.repeat(a_spatial_w[:, :, 0, 0], A_DIM, axis=-1).reshape(depth, 1, A_WIDTH)
    bs0 = jnp.repeat(a_spatial_b[:, :, 0], A_DIM, axis=-1).reshape(depth, 1, A_WIDTH)
    q_gain3 = b_q_gain.reshape(depth, 1, B_HEAD_DIM)
    q_gain_t = jnp.broadcast_to(jnp.tile(b_q_gain, (1, 2)).reshape(depth, LANES, 1), (depth, LANES, WINDOW))
    k_gain2 = jnp.tile(b_k_gain, (1, 2)).reshape(depth, 1, LANES)
    sinks3 = b_sinks.reshape(depth, B_HEADS, 1)
    gate_b = jnp.pad(c_gate_bias, ((0, 0), (0, LANES - 2 * C_HEADS))).reshape(depth, 1, LANES)
    h_gain_t = jnp.broadcast_to(c_h_gain.reshape(depth, C_HEADS, C_V_DIM, 1), (depth, C_HEADS, C_V_DIM, MLSTM_L))
    h_gain4 = c_h_gain.reshape(depth, C_HEADS, C_V_DIM)
    conv_b3 = ffn_conv_b.reshape(depth, 1, 2 * D_FF)
    k_cache = cache_swa_k.reshape(depth, R, WINDOW, B_KV_WIDTH)
    v_cache = cache_swa_v.reshape(depth, R, WINDOW, B_KV_WIDTH)
    m_state = state_mlstm_m.reshape(depth, R, 1, C_HEADS)
    conv_hist = jnp.swapaxes(state_ffn_conv, 1, 2)

    bias_p, bias_s = _bias_tables(rel_bias)

    Mp = Bp * T
    tm = _row_tile(T, 1024)
    a_rows = _row_tile(T, 512)
    xp = x_prompt.reshape(Mp, D_MODEL)
    xs = x_sample.reshape(R, D_MODEL)
    P = [[] for _ in range(6)]
    S = [[] for _ in range(7)]
    for l in range(depth):
        z, w_out_l = _mm_in(xp, norm1_3, w_in_p, l, tm, cast=(w_out, l))
        a_o = _mix_a_prompt(z, v_gain3, a_spatial_w, bs_full, l, a_rows)
        b_o, k_last, v_last, w_up_l = _mix_b_prompt(z, b_sinks[l], q_gain_t, k_gain2, bias_p, l, Bp, (w_up, l))
        c_o, c_st, n_st, m_st = _mix_c_prompt(z, gate_b, h_gain_t, l, Bp)
        x1 = _mm_out(a_o, b_o, c_o, xp, w_out_l, tm)
        act, cs, w_down_l = _mm_up_prompt(x1, norm2_3, w_up_l, ffn_conv_w, conv_b3, l, Bp, tm, (w_down, l))
        xp = _mm_down(act, x1, w_down_l, tm)
        P[0].append(k_last.reshape(Bp, WINDOW, B_KV_HEADS, B_HEAD_DIM))
        P[1].append(v_last.reshape(Bp, WINDOW, B_KV_HEADS, B_HEAD_DIM))
        P[2].append(c_st)
        P[3].append(n_st.reshape(Bp, C_HEADS, C_QK_DIM))
        P[4].append(m_st[:, :, 0, 0])
        seq_tiles = T // tm
        tail = cs[seq_tiles - 1::seq_tiles, :, SUBLANES - (CONV_W - 1):, :]
        P[5].append(jnp.swapaxes(tail, 1, 2).reshape(Bp, CONV_W - 1, 2 * D_FF))

        zs = _mm_in(xs, norm1_3, w_in_p, l, R)
        a_s, vn_s = _mix_a_sample(zs, v_gain3, ws0, bs0, l)
        b_s, kn_s, c_s, c_new, n_new, m_new = _mix_bc_sample(
            zs, k_cache, v_cache, q_gain3, k_gain2, sinks3, bias_s, gate_b, h_gain4,
            state_mlstm_C, state_mlstm_n, m_state, l)
        x1s = _mm_out(a_s, b_s.reshape(R, B_WIDTH), c_s.reshape(R, C_WIDTH), xs, w_out_l, R)
        act_s, zg_s, za_s = _mm_up_sample(x1s, norm2_3, w_up_l, ffn_conv_w, conv_b3, conv_hist, l)
        xs = _mm_down(act_s, x1s, w_down_l, R)
        S[0].append(vn_s.reshape(R, 1, A_WIDTH))
        S[1].append(kn_s.reshape(R, 1, B_KV_HEADS, B_HEAD_DIM))
        S[2].append(zs[:, Z_BV:Z_BV + B_KV_WIDTH].reshape(R, 1, B_KV_HEADS, B_HEAD_DIM))
        S[3].append(c_new)
        S[4].append(n_new)
        S[5].append(m_new[:, 0, 0:C_HEADS])
        z_new = jnp.concatenate([zg_s, za_s], axis=-1)
        S[6].append(jnp.stack([state_ffn_conv[l][:, CONV_W - 2], z_new], axis=1))

    st = lambda lst: jnp.stack(lst, axis=0)
    return (xp.reshape(Bp, T, D_MODEL), xs.reshape(R, 1, D_MODEL),
            st(P[0]), st(P[1]), st(S[1]), st(S[2]),
            st(P[2]), st(P[3]), st(P[4]),
            st(S[3]), st(S[4]), st(S[5]),
            st(P[5]), st(S[6]),
            st(S[0]))
```

```python
import functools
import math

import numpy as np
import jax
import jax.numpy as jnp
from jax import lax
from jax.experimental import pallas as pl
from jax.experimental.pallas import tpu as pltpu

f32 = jnp.float32
bf16 = jnp.bfloat16

D_MODEL = 2048
EPS = 1e-6
NEG_INF = -1e30
SQRT_HALF = 0.7071067811865476

A_GROUPS = 4
A_DIM = 128
A_WIDTH = 512
CHUNK = 128
B_HEADS = 16
B_KV_HEADS = 2
B_HEAD_DIM = 64
GQA = 8
B_WIDTH = 1024
B_KV_WIDTH = 128
WINDOW = 128
N_BUCKETS = 32
MAX_DISTANCE = 128
C_HEADS = 4
C_QK_DIM = 64
C_V_DIM = 128
C_QK_WIDTH = 256
C_WIDTH = 512
IN_SPLITS = (A_WIDTH, A_WIDTH, B_WIDTH, B_KV_WIDTH, B_KV_WIDTH,
             C_QK_WIDTH, C_QK_WIDTH, C_WIDTH, C_WIDTH, C_HEADS, C_HEADS)
IN_OFFSETS = tuple(int(o) for o in np.cumsum(IN_SPLITS)[:-1])
D_FF = 5632
CONV_W = 3

Z_WIDTH = 4096
Z_AU, Z_AV, Z_BQ, Z_BK, Z_BV, Z_CQ, Z_CK, Z_CV, Z_CO, Z_GATE = (0,) + IN_OFFSETS[:9]
C_PAIR = 2 * C_V_DIM

MLSTM_L = 128
UP_CHUNK = 256
UP_ROWS = 256
C_SEQS = 2
DEC_ROWS = 8
MM_TILE_N = {(D_MODEL, Z_WIDTH): 1024, (D_MODEL, D_MODEL): 1024, (D_MODEL, 2 * D_FF): 512, (D_FF, D_MODEL): 512}
LANES = 128
SUBLANES = 8
VMEM_LIMIT = 52 * 1024 * 1024


def _cparams(sem, flags=None):
    return pltpu.CompilerParams(dimension_semantics=sem, vmem_limit_bytes=VMEM_LIMIT, flags=flags)


def _gelu(x):
    return 0.5 * x * (1.0 + lax.erf(x * SQRT_HALF))


def _rms(x, gain):
    return x * lax.rsqrt(jnp.mean(x * x, axis=-1, keepdims=True) + EPS) * gain


def _norm_head_pairs(x, gain2):
    lo = lax.broadcasted_iota(jnp.int32, x.shape, 1) < B_HEAD_DIM
    x2 = x * x
    s_lo = jnp.sum(jnp.where(lo, x2, 0.0), axis=-1, keepdims=True)
    s_hi = jnp.sum(jnp.where(lo, 0.0, x2), axis=-1, keepdims=True)
    ms = jnp.where(lo, s_lo, s_hi) * (1.0 / B_HEAD_DIM)
    return x * lax.rsqrt(ms + EPS) * gain2


def _t5_bucket_np(dist):
    n = np.maximum(dist, 0)
    max_exact = N_BUCKETS // 2
    nf = np.maximum(n, 1).astype(np.float32)
    large = max_exact + (np.log(nf / np.float32(max_exact)) / np.float32(math.log(MAX_DISTANCE / max_exact))
                         * np.float32(N_BUCKETS - max_exact)).astype(np.int32)
    return np.where(n < max_exact, n, np.minimum(large, N_BUCKETS - 1)).astype(np.int32)


def _bias_kernel(rb_ref, bkp_ref, bks_ref, op_ref, os_ref):
    bkp = bkp_ref[...]
    bks = bks_ref[...]
    for h in range(B_HEADS):
        accp = jnp.zeros(bkp.shape, f32)
        accs = jnp.zeros(bks.shape, f32)
        for b in range(N_BUCKETS):
            val = rb_ref[b, h]
            accp = jnp.where(bkp == b, val, accp)
            accs = jnp.where(bks == b, val, accs)
        op_ref[h] = accp
        os_ref[h:h + 1, :] = accs[0:1, :]


def _bias_tables(rel_bias):
    cj = np.arange(WINDOW)[:, None]
    qi = np.arange(WINDOW)[None, :]
    bkp = _t5_bucket_np((qi - cj) % WINDOW)
    j = np.arange(2 * WINDOW)
    dist_s = np.where(j < WINDOW, WINDOW - j, 0)
    bks = np.broadcast_to(_t5_bucket_np(dist_s)[None, :], (SUBLANES, 2 * WINDOW)).copy()
    return pl.pallas_call(
        _bias_kernel,
        out_shape=(jax.ShapeDtypeStruct((B_HEADS, WINDOW, WINDOW), f32),
                   jax.ShapeDtypeStruct((B_HEADS, 2 * WINDOW), f32)),
        in_specs=[pl.BlockSpec(memory_space=pltpu.SMEM),
                  pl.BlockSpec(memory_space=pltpu.VMEM),
                  pl.BlockSpec(memory_space=pltpu.VMEM)],
        out_specs=(pl.BlockSpec(memory_space=pltpu.VMEM), pl.BlockSpec(memory_space=pltpu.VMEM)),
        name="bias_tables",
    )(rel_bias, jnp.asarray(bkp), jnp.asarray(bks))


def _cast_job(src, layer, n_chunks, chunk_of_step):
    _, k, n = src.shape
    rows = k // n_chunks
    return (pl.BlockSpec((None, rows, n), lambda *g: (layer, chunk_of_step(*g), 0)),
            pl.BlockSpec((rows, n), lambda *g: (chunk_of_step(*g), 0)),
            jax.ShapeDtypeStruct((k, n), bf16))


def _cast_chunk(src_ref, dst_ref):
    dst_ref[...] = src_ref[...].astype(bf16)


def _mm_in_kernel(*refs, cast):
    if cast:
        x_ref, g_ref, w_ref, src_ref, z_ref, dst_ref, h_ref = refs
        _cast_chunk(src_ref, dst_ref)
    else:
        x_ref, g_ref, w_ref, z_ref, h_ref = refs

    @pl.when(pl.program_id(1) == 0)
    def _():
        h_ref[...] = _rms(x_ref[...], g_ref[...]).astype(bf16)

    z_ref[...] = jnp.dot(h_ref[...], w_ref[...], preferred_element_type=f32)


def _mm_in(x, gain, w, layer, tm, cast=None):
    M = x.shape[0]
    tn = MM_TILE_N[w.shape[-2:]]
    nj = Z_WIDTH // tn
    in_specs = [pl.BlockSpec((tm, D_MODEL), lambda i, j: (i, 0)),
                pl.BlockSpec((None, 1, D_MODEL), lambda i, j: (layer, 0, 0)),
                pl.BlockSpec((None, D_MODEL, tn), lambda i, j: (layer, 0, j))]
    out_specs = [pl.BlockSpec((tm, tn), lambda i, j: (i, j))]
    out_shape = [jax.ShapeDtypeStruct((M, Z_WIDTH), f32)]
    args = [x, gain, w]
    if cast:
        job = _cast_job(cast[0], cast[1], (M // tm) * nj, lambda i, j: i * nj + j)
        in_specs.append(job[0]); out_specs.append(job[1]); out_shape.append(job[2]); args.append(cast[0])
    out = pl.pallas_call(
        functools.partial(_mm_in_kernel, cast=bool(cast)),
        grid=(M // tm, nj),
        in_specs=in_specs, out_specs=out_specs, out_shape=out_shape,
        scratch_shapes=[pltpu.VMEM((tm, D_MODEL), bf16)],
        compiler_params=_cparams(("arbitrary", "arbitrary")),
        name="mm_in",
    )(*args)
    return out if cast else out[0]


def _mm_out_kernel(a_ref, b_ref, c_ref, x_ref, w_ref, o_ref, lhs_ref):
    @pl.when(pl.program_id(1) == 0)
    def _():
        lhs_ref[:, 0:A_WIDTH] = a_ref[...]
        lhs_ref[:, A_WIDTH:A_WIDTH + B_WIDTH] = b_ref[...]
        lhs_ref[:, A_WIDTH + B_WIDTH:D_MODEL] = c_ref[...]

    o_ref[...] = x_ref[...] + jnp.dot(lhs_ref[...], w_ref[...], preferred_element_type=f32)


def _mm_out(a, b, c, x, w, tm):
    M = x.shape[0]
    tn = MM_TILE_N[w.shape]
    return pl.pallas_call(
        _mm_out_kernel,
        grid=(M // tm, D_MODEL // tn),
        in_specs=[pl.BlockSpec((tm, A_WIDTH), lambda i, j: (i, 0)),
                  pl.BlockSpec((tm, B_WIDTH), lambda i, j: (i, 0)),
                  pl.BlockSpec((tm, C_WIDTH), lambda i, j: (i, 0)),
                  pl.BlockSpec((tm, tn), lambda i, j: (i, j)),
                  pl.BlockSpec((D_MODEL, tn), lambda i, j: (0, j))],
        out_specs=pl.BlockSpec((tm, tn), lambda i, j: (i, j)),
        out_shape=jax.ShapeDtypeStruct((M, D_MODEL), f32),
        scratch_shapes=[pltpu.VMEM((tm, D_MODEL), bf16)],
        compiler_params=_cparams(("parallel", "arbitrary")),
        name="mm_out",
    )(a, b, c, x, w)


def _mm_down_kernel(a_ref, x_ref, w_ref, o_ref):
    o_ref[...] = x_ref[...] + jnp.dot(a_ref[...], w_ref[...], preferred_element_type=f32)


def _mm_down(act, x, w, tm):
    M = x.shape[0]
    tn = MM_TILE_N[w.shape]
    return pl.pallas_call(
        _mm_down_kernel,
        grid=(M // tm, D_MODEL // tn),
        in_specs=[pl.BlockSpec((tm, D_FF), lambda i, j: (i, 0)),
                  pl.BlockSpec((tm, tn), lambda i, j: (i, j)),
                  pl.BlockSpec((D_FF, tn), lambda i, j: (0, j))],
        out_specs=pl.BlockSpec((tm, tn), lambda i, j: (i, j)),
        out_shape=jax.ShapeDtypeStruct((M, D_MODEL), f32),
        compiler_params=_cparams(("parallel", "arbitrary")),
        name="mm_down",
    )(act, x, w)


def _silu(x):
    return x * jax.nn.sigmoid(x)


def _mm_up_prompt_kernel(x_ref, n2_ref, wg_ref, wa_ref, cwg_ref, cwa_ref, cbg_ref, cba_ref, src_ref,
                         act_ref, cs_ref, dst_ref, h_ref, carry_ref, zb_ref,
                         *, tm, nj, n_steps, tiles_per_seq):
    _cast_chunk(src_ref, dst_ref)
    s = pl.program_id(0)
    sa = jnp.minimum(s, n_steps - 1)
    ia = sa // nj
    ja = sa % nj
    tf = act_ref.shape[1]

    @pl.when(s == 0)
    def _():
        zb_ref[...] = jnp.zeros(zb_ref.shape, f32)
        carry_ref[...] = jnp.zeros(carry_ref.shape, f32)

    @pl.when(ja == 0)
    def _():
        h_ref[...] = _rms(x_ref[...], n2_ref[...]).astype(bf16)

    chunks = [slice(c0, c0 + UP_CHUNK) for c0 in range(0, tf, UP_CHUNK)]

    rblocks = [(r0, min(UP_ROWS, tm - r0)) for r0 in range(0, tm, UP_ROWS)]

    def conv(idx, cw_ref, cb_ref, cs, r0, nr):
        zz = zb_ref[idx, r0:r0 + SUBLANES + nr, cs]
        z1 = pltpu.roll(zz, 1, 0)[SUBLANES:]
        z2 = pltpu.roll(zz, 2, 0)[SUBLANES:]
        return (cb_ref[:, cs] + z2 * cw_ref[0:1, cs] + z1 * cw_ref[1:2, cs] + zz[SUBLANES:] * cw_ref[2:3, cs])

    for cs in chunks:
        for r0, nr in rblocks:
            g = conv(0, cwg_ref, cbg_ref, cs, r0, nr)
            a = conv(1, cwa_ref, cba_ref, cs, r0, nr)
            act_ref[r0:r0 + nr, cs] = (_silu(g) * a).astype(bf16)

    seq_start = (ia % tiles_per_seq) == 0
    for cs in chunks:
        for r0, nr in rblocks:
            for idx, w_ref in ((0, wg_ref), (1, wa_ref)):
                z = jnp.dot(h_ref[r0:r0 + nr, :], w_ref[:, cs], preferred_element_type=f32)
                if r0 == 0:
                    zb_ref[idx, 0:SUBLANES, cs] = jnp.where(seq_start, 0.0, carry_ref[idx, ja, :, cs])
                zb_ref[idx, SUBLANES + r0:SUBLANES + r0 + nr, cs] = z
                if r0 + nr == tm:
                    tail = z[nr - SUBLANES:nr, :]
                    carry_ref[idx, ja, :, cs] = tail
                    cs_ref[0, idx, :, cs] = tail


def _mm_up_prompt(x, norm2, w_up, conv_w, conv_b, layer, batch, tm, cast):
    M = x.shape[0]
    seq = M // batch
    tiles_per_seq = seq // tm
    tf = MM_TILE_N[w_up.shape]
    nj = D_FF // tf
    n_steps = (M // tm) * nj
    job = _cast_job(cast[0], cast[1], n_steps, lambda s: jnp.minimum(s, n_steps - 1))
    kern = functools.partial(_mm_up_prompt_kernel, tm=tm, nj=nj, n_steps=n_steps, tiles_per_seq=tiles_per_seq)
    ia = lambda s: jnp.minimum(s, n_steps - 1) // nj
    ja = lambda s: jnp.minimum(s, n_steps - 1) % nj
    ib = lambda s: jnp.maximum(s - 1, 0) // nj
    jb = lambda s: jnp.maximum(s - 1, 0) % nj
    return pl.pallas_call(
        kern,
        grid=(n_steps + 1,),
        in_specs=[pl.BlockSpec((tm, D_MODEL), lambda s: (ia(s), 0)),
                  pl.BlockSpec((None, 1, D_MODEL), lambda s: (layer, 0, 0)),
                  pl.BlockSpec((D_MODEL, tf), lambda s: (0, ja(s))),
                  pl.BlockSpec((D_MODEL, tf), lambda s: (0, nj + ja(s))),
                  pl.BlockSpec((None, CONV_W, tf), lambda s: (layer, 0, jb(s))),
                  pl.BlockSpec((None, CONV_W, tf), lambda s: (layer, 0, nj + jb(s))),
                  pl.BlockSpec((None, 1, tf), lambda s: (layer, 0, jb(s))),
                  pl.BlockSpec((None, 1, tf), lambda s: (layer, 0, nj + jb(s))),
                  job[0]],
        out_specs=(pl.BlockSpec((tm, tf), lambda s: (ib(s), jb(s))),
                   pl.BlockSpec((1, 2, SUBLANES, tf), lambda s: (ia(s), 0, 0, ja(s))),
                   job[1]),
        out_shape=(jax.ShapeDtypeStruct((M, D_FF), bf16),
                   jax.ShapeDtypeStruct((M // tm, 2, SUBLANES, D_FF), f32),
                   job[2]),
        scratch_shapes=[pltpu.VMEM((tm, D_MODEL), bf16),
                        pltpu.VMEM((2, nj, SUBLANES, tf), f32),
                        pltpu.VMEM((2, tm + SUBLANES, tf), f32)],
        compiler_params=_cparams(("arbitrary",)),
        name="mm_up_prompt",
    )(x, norm2, w_up, w_up, conv_w, conv_w, conv_b, conv_b, cast[0])


def _mm_up_sample_kernel(x_ref, n2_ref, wg_ref, wa_ref, cwg_ref, cwa_ref, cbg_ref, cba_ref,
                         b0g_ref, b1g_ref, b0a_ref, b1a_ref, act_ref, zg_ref, za_ref, h_ref):
    @pl.when(pl.program_id(0) == 0)
    def _():
        h_ref[...] = _rms(x_ref[...], n2_ref[...]).astype(bf16)

    def conv_half(w_ref, cw_ref, cb_ref, b0_ref, b1_ref, z_ref):
        z = jnp.dot(h_ref[...], w_ref[...], preferred_element_type=f32)
        z_ref[...] = z
        return (cb_ref[...] + b0_ref[...] * cw_ref[0:1, :] + b1_ref[...] * cw_ref[1:2, :]
                + z * cw_ref[2:3, :])

    g = conv_half(wg_ref, cwg_ref, cbg_ref, b0g_ref, b1g_ref, zg_ref)
    a = conv_half(wa_ref, cwa_ref, cba_ref, b0a_ref, b1a_ref, za_ref)
    act_ref[...] = (_silu(g) * a).astype(bf16)


def _mm_up_sample(x, norm2, w_up, conv_w, conv_b, buf, layer):
    M = x.shape[0]
    tf = MM_TILE_N[w_up.shape]
    nj = D_FF // tf
    wspec = lambda off: pl.BlockSpec((D_MODEL, tf), lambda j: (0, off + j))
    cwspec = lambda off: pl.BlockSpec((None, CONV_W, tf), lambda j: (layer, 0, off + j))
    cbspec = lambda off: pl.BlockSpec((None, 1, tf), lambda j: (layer, 0, off + j))
    bufspec = lambda row, off: pl.BlockSpec((None, None, M, tf), lambda j: (layer, row, 0, off + j))
    return pl.pallas_call(
        _mm_up_sample_kernel,
        grid=(nj,),
        in_specs=[pl.BlockSpec((M, D_MODEL), lambda j: (0, 0)),
                  pl.BlockSpec((None, 1, D_MODEL), lambda j: (layer, 0, 0)),
                  wspec(0), wspec(nj), cwspec(0), cwspec(nj), cbspec(0), cbspec(nj),
                  bufspec(0, 0), bufspec(1, 0), bufspec(0, nj), bufspec(1, nj)],
        out_specs=(pl.BlockSpec((M, tf), lambda j: (0, j)),
                   pl.BlockSpec((M, tf), lambda j: (0, j)),
                   pl.BlockSpec((M, tf), lambda j: (0, j))),
        out_shape=(jax.ShapeDtypeStruct((M, D_FF), bf16),
                   jax.ShapeDtypeStruct((M, D_FF), f32),
                   jax.ShapeDtypeStruct((M, D_FF), f32)),
        scratch_shapes=[pltpu.VMEM((M, D_MODEL), bf16)],
        compiler_params=_cparams(("arbitrary",)),
        name="mm_up_sample",
    )(x, norm2, w_up, w_up, conv_w, conv_w, conv_b, conv_b, buf, buf, buf, buf)


def _mix_a_prompt_kernel(u_ref, v_ref, vg_ref, ws_ref, bs_ref, o_ref):
    row = lax.broadcasted_iota(jnp.int32, (CHUNK, CHUNK), 0)
    col = lax.broadcasted_iota(jnp.int32, (CHUNK, CHUNK), 1)
    tri = col <= row
    for g in range(A_GROUPS):
        sl = slice(g * A_DIM, (g + 1) * A_DIM)
        ws = jnp.where(tri, ws_ref[g], 0.0).astype(bf16)
        for r0 in range(0, u_ref.shape[0], CHUNK):
            rs = slice(r0, r0 + CHUNK)
            vn = _rms(_gelu(v_ref[rs, sl]), vg_ref[:, sl])
            mixv = jnp.dot(ws, vn.astype(bf16), preferred_element_type=f32) + bs_ref[:, sl]
            o_ref[rs, sl] = (_gelu(u_ref[rs, sl]) * mixv).astype(bf16)


def _mix_a_prompt(z, v_gain, w_s, b_s_full, layer, rows):
    M = z.shape[0]
    return pl.pallas_call(
        _mix_a_prompt_kernel,
        grid=(M // rows,),
        in_specs=[pl.BlockSpec((rows, A_WIDTH), lambda r: (r, Z_AU // A_WIDTH)),
                  pl.BlockSpec((rows, A_WIDTH), lambda r: (r, Z_AV // A_WIDTH)),
                  pl.BlockSpec((None, 1, A_WIDTH), lambda r: (layer, 0, 0)),
                  pl.BlockSpec((None, A_GROUPS, CHUNK, CHUNK), lambda r: (layer, 0, 0, 0)),
                  pl.BlockSpec((None, CHUNK, A_WIDTH), lambda r: (layer, 0, 0))],
        out_specs=pl.BlockSpec((rows, A_WIDTH), lambda r: (r, 0)),
        out_shape=jax.ShapeDtypeStruct((M, A_WIDTH), bf16),
        compiler_params=_cparams(("parallel",)),
        name="mix_a_prompt",
    )(z, z, v_gain, w_s, b_s_full)


def _mix_a_sample_kernel(u_ref, v_ref, vg_ref, ws_ref, bs_ref, o_ref, vn_ref):
    for g in range(A_GROUPS):
        sl = slice(g * A_DIM, (g + 1) * A_DIM)
        vn = _rms(_gelu(v_ref[:, sl]), vg_ref[:, sl])
        vn_ref[:, sl] = vn
        mixv = ws_ref[:, sl] * vn + bs_ref[:, sl]
        o_ref[:, sl] = (_gelu(u_ref[:, sl]) * mixv).astype(bf16)


def _mix_a_sample(z, v_gain, ws0, bs0, layer):
    M = z.shape[0]
    vec = pl.BlockSpec((None, 1, A_WIDTH), lambda r: (layer, 0, 0))
    return pl.pallas_call(
        _mix_a_sample_kernel,
        grid=(1,),
        in_specs=[pl.BlockSpec((M, A_WIDTH), lambda r: (0, Z_AU // A_WIDTH)),
                  pl.BlockSpec((M, A_WIDTH), lambda r: (0, Z_AV // A_WIDTH)),
                  vec, vec, vec],
        out_specs=(pl.BlockSpec((M, A_WIDTH), lambda r: (0, 0)),
                   pl.BlockSpec((M, A_WIDTH), lambda r: (0, 0))),
        out_shape=(jax.ShapeDtypeStruct((M, A_WIDTH), bf16),
                   jax.ShapeDtypeStruct((M, A_WIDTH), f32)),
        compiler_params=_cparams(("arbitrary",)),
        name="mix_a_sample",
    )(z, z, v_gain, ws0, bs0)


def _mix_b_prompt_kernel(sink_ref, q_ref, k_ref, v_ref, qg_ref, kg_ref, bias_ref, src_ref,
                         o_ref, klast_ref, vlast_ref, dst_ref, kcat_ref, vcat_ref):
    _cast_chunk(src_ref, dst_ref)
    i = pl.program_id(1)
    W = WINDOW
    kn = _norm_head_pairs(k_ref[...], kg_ref[...])
    v = v_ref[...]
    klast_ref[0] = kn
    vlast_ref[0] = v

    @pl.when(i == 0)
    def _():
        kcat_ref[0:W, :] = jnp.zeros((W, LANES), f32)
        vcat_ref[0:W, :] = jnp.zeros((W, LANES), f32)

    @pl.when(i > 0)
    def _():
        kcat_ref[0:W, :] = kcat_ref[W:2 * W, :]
        vcat_ref[0:W, :] = vcat_ref[W:2 * W, :]

    kcat_ref[W:2 * W, :] = kn
    vcat_ref[W:2 * W, :] = v
    kc = kcat_ref[...]
    vc = vcat_ref[...]
    kc_sw = pltpu.roll(kc, B_HEAD_DIM, 1)
    vc_sw = pltpu.roll(vc, B_HEAD_DIM, 1)
    lo = lax.broadcasted_iota(jnp.int32, (2 * W, LANES), 1) < B_HEAD_DIM

    cj = lax.broadcasted_iota(jnp.int32, (W, W), 0)
    qi = lax.broadcasted_iota(jnp.int32, (W, W), 1)
    own = cj <= qi
    keep = own | (i > 0)
    q_gain = qg_ref[...] * (B_HEAD_DIM ** -0.5)

    tiles = GQA // 2
    nt = (((1,), (1,)), ((), ()))
    tn = (((0,), (0,)), ((), ()))
    scores, values = [], []
    for kvh in range(B_KV_HEADS):
        k_src, k_alt = (kc, kc_sw) if kvh == 0 else (kc_sw, kc)
        v_src, v_alt = (vc, vc_sw) if kvh == 0 else (vc_sw, vc)
        k_even = jnp.where(lo, k_src, 0.0).astype(bf16)
        k_odd = jnp.where(lo, 0.0, k_alt).astype(bf16)
        values.append((jnp.where(lo, v_src, 0.0).astype(bf16), jnp.where(lo, 0.0, v_alt).astype(bf16)))
        qts = []
        for t in range(tiles):
            c0 = (kvh * tiles + t) * LANES
            qt = q_ref[:, c0:c0 + LANES].T
            q2 = qt * qt
            halves = []
            for r0 in (0, B_HEAD_DIM):
                ms = jnp.mean(q2[r0:r0 + B_HEAD_DIM, :], axis=0, keepdims=True)
                halves.append(qt[r0:r0 + B_HEAD_DIM, :] * lax.rsqrt(ms + EPS))
            qts.append(jnp.concatenate(halves, axis=0) * q_gain)
        qst = jnp.concatenate(qts, axis=1).astype(bf16)
        scores.append((jnp.dot(k_even, qst, preferred_element_type=f32),
                       jnp.dot(k_odd, qst, preferred_element_type=f32)))
    for kvh in range(B_KV_HEADS):
        p_par = []
        for par in range(2):
            blocks = []
            for t in range(tiles):
                h = kvh * GQA + 2 * t + par
                sb = scores[kvh][par][:, t * W:(t + 1) * W]
                s = jnp.where(own, sb[W:2 * W, :], sb[0:W, :]) + bias_ref[h]
                s = jnp.where(keep, s, NEG_INF)
                sk = sink_ref[h]
                mx = jnp.maximum(jnp.max(s, axis=0, keepdims=True), sk)
                p = jnp.exp(s - mx)
                den = jnp.sum(p, axis=0, keepdims=True) + jnp.exp(sk - mx)
                p = p * (1.0 / den)
                blocks.append(jnp.concatenate([jnp.where(own, 0.0, p), jnp.where(own, p, 0.0)],
                                              axis=0).astype(bf16))
            p_par.append(jnp.concatenate(blocks, axis=1))
        ot = (lax.dot_general(values[kvh][0], p_par[0], tn, preferred_element_type=f32)
              + lax.dot_general(values[kvh][1], p_par[1], tn, preferred_element_type=f32))
        for t in range(tiles):
            c0 = (kvh * tiles + t) * LANES
            o_ref[:, c0:c0 + LANES] = ot[:, t * W:(t + 1) * W].T.astype(bf16)


def _mix_b_prompt(z, sinks, q_gain_t, k_gain2, bias_p, layer, batch, cast):
    M = z.shape[0]
    nb = M // batch // WINDOW
    job = _cast_job(cast[0], cast[1], batch * nb, lambda b, i: b * nb + i)
    return pl.pallas_call(
        _mix_b_prompt_kernel,
        grid=(batch, nb),
        in_specs=[pl.BlockSpec(memory_space=pltpu.SMEM),
                  pl.BlockSpec((WINDOW, B_WIDTH), lambda b, i: (b * nb + i, Z_BQ // B_WIDTH)),
                  pl.BlockSpec((WINDOW, B_KV_WIDTH), lambda b, i: (b * nb + i, Z_BK // B_KV_WIDTH)),
                  pl.BlockSpec((WINDOW, B_KV_WIDTH), lambda b, i: (b * nb + i, Z_BV // B_KV_WIDTH)),
                  pl.BlockSpec((None, LANES, WINDOW), lambda b, i: (layer, 0, 0)),
                  pl.BlockSpec((None, 1, LANES), lambda b, i: (layer, 0, 0)),
                  pl.BlockSpec((B_HEADS, WINDOW, WINDOW), lambda b, i: (0, 0, 0)),
                  job[0]],
        out_specs=(pl.BlockSpec((WINDOW, B_WIDTH), lambda b, i: (b * nb + i, 0)),
                   pl.BlockSpec((1, WINDOW, B_KV_WIDTH), lambda b, i: (b, 0, 0)),
                   pl.BlockSpec((1, WINDOW, B_KV_WIDTH), lambda b, i: (b, 0, 0)),
                   job[1]),
        out_shape=(jax.ShapeDtypeStruct((M, B_WIDTH), bf16),
                   jax.ShapeDtypeStruct((batch, WINDOW, B_KV_WIDTH), f32),
                   jax.ShapeDtypeStruct((batch, WINDOW, B_KV_WIDTH), f32),
                   job[2]),
        scratch_shapes=[pltpu.VMEM((2 * WINDOW, LANES), f32),
                        pltpu.VMEM((2 * WINDOW, LANES), f32)],
        compiler_params=_cparams(("arbitrary", "arbitrary")),
        name="mix_b_prompt",
    )(sinks, z, z, z, q_gain_t, k_gain2, bias_p, cast[0])


def _mix_c_prompt_kernel(q_ref, k_ref, v0_ref, v1_ref, o0_ref, o1_ref, g_ref, gb_ref, hgt_ref,
                         out_ref, c_out, n_out, m_out, c_s, n_s, m_s):
    L = MLSTM_L
    c = pl.program_id(1)

    @pl.when(c == 0)
    def _():
        c_s[...] = jnp.zeros(c_s.shape, f32)
        n_s[...] = jnp.zeros(n_s.shape, f32)
        m_s[...] = jnp.zeros(m_s.shape, f32)

    row = lax.broadcasted_iota(jnp.int32, (L, L), 0)
    col = lax.broadcasted_iota(jnp.int32, (L, L), 1)
    tri = col <= row
    tri_f = tri.astype(f32)
    seqs = range(q_ref.shape[0])
    units = [(bi, h) for bi in seqs for h in range(C_HEADS)]
    g, b = {}, {}
    for bi in seqs:
        g[bi] = g_ref[bi] + gb_ref[...]
        b[bi] = jnp.dot(tri_f, jax.nn.log_sigmoid(g[bi]), precision=lax.Precision.HIGHEST,
                        preferred_element_type=f32)
    bT = {bi: b[bi].T for bi in seqs}
    vis = row <= col
    nt = (((1,), (1,)), ((), ()))
    tn = (((0,), (0,)), ((), ()))
    qb, kh, vb, ct_prev, n_prev, st, qct, qn_lin = {}, {}, {}, {}, {}, {}, {}, {}
    for u in units:
        bi, h = u
        v_ref = v0_ref if h < 2 else v1_ref
        vsl = slice((h % 2) * C_V_DIM, (h % 2 + 1) * C_V_DIM)
        qb[u] = q_ref[bi, :, h * C_QK_DIM:(h + 1) * C_QK_DIM].astype(bf16)
        kh[u] = k_ref[bi, :, h * C_QK_DIM:(h + 1) * C_QK_DIM] * (C_QK_DIM ** -0.5)
        vb[u] = v_ref[bi, :, vsl].astype(bf16)
        ct_prev[u] = c_s[bi, h]
        n_prev[u] = n_s[bi, h]
        st[u] = lax.dot_general(kh[u].astype(bf16), qb[u], nt, preferred_element_type=f32)
        qct[u] = lax.dot_general(ct_prev[u].astype(bf16), qb[u], nt, preferred_element_type=f32)
        n8 = jnp.broadcast_to(n_prev[u], (SUBLANES, C_QK_DIM)).astype(bf16)
        qn_lin[u] = lax.dot_general(n8, qb[u], nt, preferred_element_type=f32)[0:1, :]
    pt, iw, inv, kw, decay, m_new = {}, {}, {}, {}, {}, {}
    for u in units:
        bi, h = u
        b_row = bT[bi][C_HEADS + h:C_HEADS + h + 1, :]
        src = g[bi][:, h:h + 1] - b[bi][:, C_HEADS + h:C_HEADS + h + 1]
        dlog = jnp.where(vis, b_row + src, -jnp.inf)
        m_prev = m_s[bi, h][0:1, 0:1]
        inter = b_row + m_prev
        mt = jnp.maximum(inter, jnp.max(dlog, axis=0, keepdims=True))
        dt = jnp.exp(dlog - mt)
        pt[u] = st[u] * dt
        iw[u] = jnp.exp(inter - mt)
        qn = iw[u] * qn_lin[u] + jnp.sum(pt[u], axis=0, keepdims=True)
        inv[u] = 1.0 / jnp.maximum(jnp.abs(qn), jnp.exp(-mt))
        m_new[u] = mt[:, L - 1:L]
        decay[u] = jnp.exp(b_row[:, L - 1:L] + m_prev - m_new[u])
        kw[u] = kh[u] * dt[:, L - 1:L]
    svt = {u: lax.dot_general(vb[u], pt[u].astype(bf16), tn, preferred_element_type=f32) for u in units}
    kvt = {u: lax.dot_general(vb[u], kw[u].astype(bf16), tn, preferred_element_type=f32) for u in units}
    for u in units:
        bi, h = u
        o_ref = o0_ref if h < 2 else o1_ref
        vsl = slice((h % 2) * C_V_DIM, (h % 2 + 1) * C_V_DIM)
        ht = (iw[u] * qct[u] + svt[u]) * inv[u]
        ms = jnp.mean(ht * ht, axis=0, keepdims=True)
        hn = (ht * lax.rsqrt(ms + EPS) * hgt_ref[h]).T
        c_s[bi, h] = decay[u] * ct_prev[u] + kvt[u]
        n_s[bi, h] = decay[u] * n_prev[u] + jnp.sum(kw[u], axis=0, keepdims=True)
        m_s[bi, h] = jnp.broadcast_to(m_new[u], (SUBLANES, LANES))
        osl = slice(h * C_V_DIM, (h + 1) * C_V_DIM)
        out_ref[bi, :, osl] = (hn * jax.nn.sigmoid(o_ref[bi, :, vsl])).astype(bf16)

    @pl.when(c == pl.num_programs(1) - 1)
    def _():
        eye = (row == col).astype(f32)
        for bi, h in units:
            c_out[bi, h] = lax.dot_general(c_s[bi, h], eye, tn, precision=lax.Precision.HIGHEST,
                                           preferred_element_type=f32)
        n_out[...] = n_s[...]
        m_out[...] = m_s[...]


def _mix_c_prompt(z, gate_bias, h_gain, layer, batch):
    M = z.shape[0]
    L = MLSTM_L
    T = M // batch
    z3 = z.reshape(batch, T, Z_WIDTH)
    ns = C_SEQS if batch % C_SEQS == 0 else 1
    col = lambda width, off: pl.BlockSpec((ns, L, width), lambda b, c: (b, c, off // width))
    state = lambda *shape: pl.BlockSpec((ns,) + shape, lambda b, c: (b,) + (0,) * len(shape))
    out, c_st, n_st, m_st = pl.pallas_call(
        _mix_c_prompt_kernel,
        grid=(batch // ns, T // L),
        in_specs=[col(C_QK_WIDTH, Z_CQ), col(C_QK_WIDTH, Z_CK),
                  col(C_PAIR, Z_CV), col(C_PAIR, Z_CV + C_PAIR),
                  col(C_PAIR, Z_CO), col(C_PAIR, Z_CO + C_PAIR),
                  col(LANES, Z_GATE),
                  pl.BlockSpec((None, 1, LANES), lambda b, c: (layer, 0, 0)),
                  pl.BlockSpec((None, C_HEADS, C_V_DIM, L), lambda b, c: (layer, 0, 0, 0))],
        out_specs=(pl.BlockSpec((ns, L, C_WIDTH), lambda b, c: (b, c, 0)),
                   state(C_HEADS, C_QK_DIM, C_V_DIM),
                   state(C_HEADS, 1, C_QK_DIM),
                   state(C_HEADS, SUBLANES, LANES)),
        out_shape=(jax.ShapeDtypeStruct((batch, T, C_WIDTH), bf16),
                   jax.ShapeDtypeStruct((batch, C_HEADS, C_QK_DIM, C_V_DIM), f32),
                   jax.ShapeDtypeStruct((batch, C_HEADS, 1, C_QK_DIM), f32),
                   jax.ShapeDtypeStruct((batch, C_HEADS, SUBLANES, LANES), f32)),
        scratch_shapes=[pltpu.VMEM((ns, C_HEADS, C_V_DIM, C_QK_DIM), f32),
                        pltpu.VMEM((ns, C_HEADS, 1, C_QK_DIM), f32),
                        pltpu.VMEM((ns, C_HEADS, SUBLANES, LANES), f32)],
        compiler_params=_cparams(("arbitrary", "arbitrary")),
        name="mix_c_prompt",
    )(z3, z3, z3, z3, z3, z3, z3, gate_bias, h_gain)
    return out.reshape(M, C_WIDTH), c_st, n_st, m_st


_BC_SHARED = (4, 5, 6, 7, 12, 13)


def _mix_bc_sample_kernel(*refs):
    rows = [_mix_bc_sample_row(*[ref if k in _BC_SHARED else ref.at[pl.ds(r, 1)] for k, ref in enumerate(refs)])
            for r in range(refs[0].shape[0])]
    while rows:
        rows = [row for row in rows if next(row, True) is None]


def _mix_bc_sample_row(zr_ref, q_ref, kp_ref, vp_ref, qg_ref, kg_ref, sink_ref, bias_ref,
                       cq_ref, ck_ref, cv_ref, co_ref, gb_ref, hg_ref, c0_ref, n0_ref, m0_ref,
                       bo_ref, kn_ref, co_out, c_out, n_out, m_out):
    W = WINDOW
    zr = zr_ref[0]

    qn = _rms(q_ref[0], qg_ref[...])
    hrow = lax.broadcasted_iota(jnp.int32, (B_HEADS, LANES), 0)
    lane = lax.broadcasted_iota(jnp.int32, (B_HEADS, LANES), 1)
    own = (hrow < GQA) == (lane < B_HEAD_DIM)
    q2 = jnp.where(own, jnp.concatenate([qn, qn], axis=-1), 0.0)
    k_new = _norm_head_pairs(zr[:, Z_BK:Z_BK + B_KV_WIDTH], kg_ref[...])
    v_new = zr[:, Z_BV:Z_BV + B_KV_WIDTH]
    kn_ref[0] = k_new
    scale = B_HEAD_DIM ** -0.5
    s_past = lax.dot_general(q2.astype(bf16), kp_ref[0].astype(bf16), (((1,), (1,)), ((), ())),
                             preferred_element_type=f32) * scale + bias_ref[:, 0:W]
    yield
    s_past = jnp.where(lane >= 1, s_past, NEG_INF)
    s_new = jnp.sum(q2 * k_new, axis=-1, keepdims=True) * scale + bias_ref[:, W:W + 1]
    sk = sink_ref[...]
    mx = jnp.maximum(jnp.maximum(jnp.max(s_past, axis=-1, keepdims=True), s_new), sk)
    p_past = jnp.exp(s_past - mx)
    p_new = jnp.exp(s_new - mx)
    den = jnp.sum(p_past, axis=-1, keepdims=True) + p_new + jnp.exp(sk - mx)
    inv = 1.0 / den
    o2 = (jnp.dot((p_past * inv).astype(bf16), vp_ref[0].astype(bf16), preferred_element_type=f32)
          + (p_new * inv) * v_new)
    o2_sw = pltpu.roll(o2, B_HEAD_DIM, 1)
    bo_ref[0] = jnp.where(hrow < GQA, o2, o2_sw)[:, 0:B_HEAD_DIM].astype(bf16)

    g = zr[:, Z_GATE:Z_GATE + LANES] + gb_ref[...]
    lf = jax.nn.log_sigmoid(g)
    hrow4 = lax.broadcasted_iota(jnp.int32, (C_HEADS, LANES), 0)
    lane4 = lax.broadcasted_iota(jnp.int32, (C_HEADS, LANES), 1)

    def head_column(row_vec, first_lane):
        spread = jnp.broadcast_to(row_vec, (C_HEADS, LANES))
        return jnp.sum(jnp.where(lane4 == hrow4 + first_lane, spread, 0.0), axis=-1, keepdims=True)

    ig = head_column(g, 0)
    b = head_column(lf, C_HEADS)
    m0 = m0_ref[0]
    m_prev = jnp.sum(jnp.where(lane4[:, 0:C_HEADS] == hrow4[:, 0:C_HEADS],
                               jnp.broadcast_to(m0, (C_HEADS, C_HEADS)), 0.0), axis=-1, keepdims=True)
    inter = b + m_prev
    mt = jnp.maximum(inter, ig)
    q4 = cq_ref[0]
    k4 = ck_ref[0] * (C_QK_DIM ** -0.5)
    v4 = cv_ref[0]
    n4 = n0_ref[0]
    c0 = c0_ref[0]
    s = jnp.sum(q4 * k4, axis=-1, keepdims=True) * jnp.exp(ig - mt)
    iw = jnp.exp(inter - mt)
    blk_row = lax.broadcasted_iota(jnp.int32, (C_HEADS, C_QK_WIDTH), 0)
    blk_lane = lax.broadcasted_iota(jnp.int32, (C_HEADS, C_QK_WIDTH), 1) // C_QK_DIM
    pad4 = jnp.zeros((SUBLANES - C_HEADS, C_QK_WIDTH), f32)

    def head_blocks(x4):
        tiled = jnp.concatenate([x4] * C_HEADS, axis=-1)
        return jnp.concatenate([jnp.where(blk_lane == blk_row, tiled, 0.0), pad4], axis=0).astype(bf16)

    c0s = c0.reshape(C_QK_WIDTH, C_V_DIM)
    qc = jnp.dot(head_blocks(q4), c0s.astype(bf16), preferred_element_type=f32)[0:C_HEADS, :]
    yield
    num = iw * qc + s * v4
    qn_ = iw * jnp.sum(q4 * n4, axis=-1, keepdims=True) + s
    hh = num / jnp.maximum(jnp.abs(qn_), jnp.exp(-mt))
    decay = iw
    kw = k4 * jnp.exp(ig - mt)
    v8 = jnp.concatenate([v4, jnp.zeros((SUBLANES - C_HEADS, C_V_DIM), f32)], axis=0).astype(bf16)
    outer = lax.dot_general(head_blocks(kw), v8, (((0,), (0,)), ((), ())),
                            preferred_element_type=f32)
    for h in range(C_HEADS):
        c_out[0, h] = decay[h:h + 1, :] * c0[h] + outer[h * C_QK_DIM:(h + 1) * C_QK_DIM, :]
    n_out[0] = decay * n4 + kw
    m_out[0] = jnp.sum(jnp.where(lane4 == hrow4, jnp.broadcast_to(mt, (C_HEADS, LANES)), 0.0),
                       axis=0, keepdims=True)
    co_out[0] = (_rms(hh, hg_ref[...]) * jax.nn.sigmoid(co_ref[0])).astype(bf16)


def _mix_bc_sample(z, k_past, v_past, q_gain, k_gain2, sinks, bias_s, gate_bias, h_gain4,
                   c0, n0, m0, layer):
    R = z.shape[0]
    zr = z.reshape(R, 1, Z_WIDTH)
    q = z[:, Z_BQ:Z_BQ + B_WIDTH].reshape(R, B_HEADS, B_HEAD_DIM)
    cq = z[:, Z_CQ:Z_CQ + C_QK_WIDTH].reshape(R, C_HEADS, C_QK_DIM)
    ck = z[:, Z_CK:Z_CK + C_QK_WIDTH].reshape(R, C_HEADS, C_QK_DIM)
    cv = z[:, Z_CV:Z_CV + C_WIDTH].reshape(R, C_HEADS, C_V_DIM)
    co = z[:, Z_CO:Z_CO + C_WIDTH].reshape(R, C_HEADS, C_V_DIM)
    rb = DEC_ROWS if R % DEC_ROWS == 0 else 1
    row3 = lambda n: pl.BlockSpec((rb, 1, n), lambda r: (r, 0, 0))
    return pl.pallas_call(
        _mix_bc_sample_kernel,
        grid=(R // rb,),
        in_specs=[row3(Z_WIDTH),
                  pl.BlockSpec((rb, B_HEADS, B_HEAD_DIM), lambda r: (r, 0, 0)),
                  pl.BlockSpec((None, rb, WINDOW, B_KV_WIDTH), lambda r: (layer, r, 0, 0)),
                  pl.BlockSpec((None, rb, WINDOW, B_KV_WIDTH), lambda r: (layer, r, 0, 0)),
                  pl.BlockSpec((None, 1, B_HEAD_DIM), lambda r: (layer, 0, 0)),
                  pl.BlockSpec((None, 1, LANES), lambda r: (layer, 0, 0)),
                  pl.BlockSpec((None, B_HEADS, 1), lambda r: (layer, 0, 0)),
                  pl.BlockSpec((B_HEADS, 2 * WINDOW), lambda r: (0, 0)),
                  pl.BlockSpec((rb, C_HEADS, C_QK_DIM), lambda r: (r, 0, 0)),
                  pl.BlockSpec((rb, C_HEADS, C_QK_DIM), lambda r: (r, 0, 0)),
                  pl.BlockSpec((rb, C_HEADS, C_V_DIM), lambda r: (r, 0, 0)),
                  pl.BlockSpec((rb, C_HEADS, C_V_DIM), lambda r: (r, 0, 0)),
                  pl.BlockSpec((None, 1, LANES), lambda r: (layer, 0, 0)),
                  pl.BlockSpec((None, C_HEADS, C_V_DIM), lambda r: (layer, 0, 0)),
                  pl.BlockSpec((None, rb, C_HEADS, C_QK_DIM, C_V_DIM), lambda r: (layer, r, 0, 0, 0)),
                  pl.BlockSpec((None, rb, C_HEADS, C_QK_DIM), lambda r: (layer, r, 0, 0)),
                  pl.BlockSpec((None, rb, 1, C_HEADS), lambda r: (layer, r, 0, 0))],
        out_specs=(pl.BlockSpec((rb, B_HEADS, B_HEAD_DIM), lambda r: (r, 0, 0)),
                   row3(B_KV_WIDTH),
                   pl.BlockSpec((rb, C_HEADS, C_V_DIM), lambda r: (r, 0, 0)),
                   pl.BlockSpec((rb, C_HEADS, C_QK_DIM, C_V_DIM), lambda r: (r, 0, 0, 0)),
                   pl.BlockSpec((rb, C_HEADS, C_QK_DIM), lambda r: (r, 0, 0)),
                   row3(LANES)),
        out_shape=(jax.ShapeDtypeStruct((R, B_HEADS, B_HEAD_DIM), bf16),
                   jax.ShapeDtypeStruct((R, 1, B_KV_WIDTH), f32),
                   jax.ShapeDtypeStruct((R, C_HEADS, C_V_DIM), bf16),
                   jax.ShapeDtypeStruct((R, C_HEADS, C_QK_DIM, C_V_DIM), f32),
                   jax.ShapeDtypeStruct((R, C_HEADS, C_QK_DIM), f32),
                   jax.ShapeDtypeStruct((R, 1, LANES), f32)),
        compiler_params=_cparams(("parallel",)),
        name="mix_bc_sample",
    )(zr, q, k_past, v_past, q_gain, k_gain2, sinks, bias_s, cq, ck, cv, co, gate_bias, h_gain4,
      c0, n0, m0)


def _prep_w_in(w_in):
    pad = jnp.zeros(w_in.shape[:-1] + (Z_WIDTH - w_in.shape[-1],), bf16)
    return jnp.concatenate([w_in.astype(bf16), pad], axis=-1)


def _row_tile(m, pref):
    return pref if m % pref == 0 else m


def kernel(x_prompt, x_sample, cache_swa_k, cache_swa_v, state_mlstm_C, state_mlstm_n, state_mlstm_m, state_ffn_conv, rel_bias, norm1, w_in, a_v_gain, a_spatial_w, a_spatial_b, b_q_gain, b_k_gain, b_sinks, c_gate_bias, c_h_gain, w_out, norm2, w_up, ffn_conv_w, ffn_conv_b, w_down):
    depth = w_in.shape[0]
    Bp, T, _ = x_prompt.shape
    R = x_sample.shape[0]
    assert x_sample.shape[1] == 1 and T % CHUNK == 0 and T % MLSTM_L == 0

    w_in_p = _prep_w_in(w_in)
    norm1_3 = norm1.reshape(depth, 1, D_MODEL)
    norm2_3 = norm2.reshape(depth, 1, D_MODEL)
    v_gain3 = a_v_gain.reshape(depth, 1, A_WIDTH)
    bs_full = jnp.repeat(jnp.swapaxes(a_spatial_b, 1, 2), A_DIM, axis=-1)
    ws0 = jnp.repeat(a_spatial_w[:, :, 0, 0], A_DIM, axis=-1).reshape(depth, 1, A_WIDTH)
    bs0 = jnp.repeat(a_spatial_b[:, :, 0], A_DIM, axis=-1).reshape(depth, 1, A_WIDTH)
    q_gain3 = b_q_gain.reshape(depth, 1, B_HEAD_DIM)
    q_gain_t = jnp.broadcast_to(jnp.tile(b_q_gain, (1, 2)).reshape(depth, LANES, 1), (depth, LANES, WINDOW))
    k_gain2 = jnp.tile(b_k_gain, (1, 2)).reshape(depth, 1, LANES)
    sinks3 = b_sinks.reshape(depth, B_HEADS, 1)
    gate_b = jnp.pad(c_gate_bias, ((0, 0), (0, LANES - 2 * C_HEADS))).reshape(depth, 1, LANES)
    h_gain_t = jnp.broadcast_to(c_h_gain.reshape(depth, C_HEADS, C_V_DIM, 1), (depth, C_HEADS, C_V_DIM, MLSTM_L))
    h_gain4 = c_h_gain.reshape(depth, C_HEADS, C_V_DIM)
    conv_b3 = ffn_conv_b.reshape(depth, 1, 2 * D_FF)
    k_cache = cache_swa_k.reshape(depth, R, WINDOW, B_KV_WIDTH)
    v_cache = cache_swa_v.reshape(depth, R, WINDOW, B_KV_WIDTH)
    m_state = state_mlstm_m.reshape(depth, R, 1, C_HEADS)
    conv_hist = jnp.swapaxes(state_ffn_conv, 1, 2)

    bias_p, bias_s = _bias_tables(rel_bias)

    Mp = Bp * T
    tm = _row_tile(T, 1024)
    a_rows = _row_tile(T, 512)
    xp = x_prompt.reshape(Mp, D_MODEL)
    xs = x_sample.reshape(R, D_MODEL)
    P = [[] for _ in range(6)]
    S = [[] for _ in range(7)]
    for l in range(depth):
        z, w_out_l = _mm_in(xp, norm1_3, w_in_p, l, tm, cast=(w_out, l))
        a_o = _mix_a_prompt(z, v_gain3, a_spatial_w, bs_full, l, a_rows)
        b_o, k_last, v_last, w_up_l = _mix_b_prompt(z, b_sinks[l], q_gain_t, k_gain2, bias_p, l, Bp, (w_up, l))
        c_o, c_st, n_st, m_st = _mix_c_prompt(z, gate_b, h_gain_t, l, Bp)
        x1 = _mm_out(a_o, b_o, c_o, xp, w_out_l, tm)
        act, cs, w_down_l = _mm_up_prompt(x1, norm2_3, w_up_l, ffn_conv_w, conv_b3, l, Bp, tm, (w_down, l))
        xp = _mm_down(act, x1, w_down_l, tm)
        P[0].append(k_last.reshape(Bp, WINDOW, B_KV_HEADS, B_HEAD_DIM))
        P[1].append(v_last.reshape(Bp, WINDOW, B_KV_HEADS, B_HEAD_DIM))
        P[2].append(c_st)
        P[3].append(n_st.reshape(Bp, C_HEADS, C_QK_DIM))
        P[4].append(m_st[:, :, 0, 0])
        seq_tiles = T // tm
        tail = cs[seq_tiles - 1::seq_tiles, :, SUBLANES - (CONV_W - 1):, :]
        P[5].append(jnp.swapaxes(tail, 1, 2).reshape(Bp, CONV_W - 1, 2 * D_FF))

        zs = _mm_in(xs, norm1_3, w_in_p, l, R)
        a_s, vn_s = _mix_a_sample(zs, v_gain3, ws0, bs0, l)
        b_s, kn_s, c_s, c_new, n_new, m_new = _mix_bc_sample(
            zs, k_cache, v_cache, q_gain3, k_gain2, sinks3, bias_s, gate_b, h_gain4,
            state_mlstm_C, state_mlstm_n, m_state, l)
        x1s = _mm_out(a_s, b_s.reshape(R, B_WIDTH), c_s.reshape(R, C_WIDTH), xs, w_out_l, R)
        act_s, zg_s, za_s = _mm_up_sample(x1s, norm2_3, w_up_l, ffn_conv_w, conv_b3, conv_hist, l)
        xs = _mm_down(act_s, x1s, w_down_l, R)
        S[0].append(vn_s.reshape(R, 1, A_WIDTH))
        S[1].append(kn_s.reshape(R, 1, B_KV_HEADS, B_HEAD_DIM))
        S[2].append(zs[:, Z_BV:Z_BV + B_KV_WIDTH].reshape(R, 1, B_KV_HEADS, B_HEAD_DIM))
        S[3].append(c_new)
        S[4].append(n_new)
        S[5].append(m_new[:, 0, 0:C_HEADS])
        z_new = jnp.concatenate([zg_s, za_s], axis=-1)
        S[6].append(jnp.stack([state_ffn_conv[l][:, CONV_W - 2], z_new], axis=1))

    st = lambda lst: jnp.stack(lst, axis=0)
    return (xp.reshape(Bp, T, D_MODEL), xs.reshape(R, 1, D_MODEL),
            st(P[0]), st(P[1]), st(S[1]), st(S[2]),
            st(P[2]), st(P[3]), st(P[4]),
            st(S[3]), st(S[4]), st(S[5]),
            st(P[5]), st(S[6]),
            st(S[0]))
```

```python
import functools
import math

import numpy as np
import jax
import jax.numpy as jnp
from jax import lax
from jax.experimental import pallas as pl
from jax.experimental.pallas import tpu as pltpu

f32 = jnp.float32
bf16 = jnp.bfloat16

D_MODEL = 2048
EPS = 1e-6
NEG_INF = -1e30
SQRT_HALF = 0.7071067811865476

A_GROUPS = 4
A_DIM = 128
A_WIDTH = 512
CHUNK = 128
B_HEADS = 16
B_KV_HEADS = 2
B_HEAD_DIM = 64
GQA = 8
B_WIDTH = 1024
B_KV_WIDTH = 128
WINDOW = 128
N_BUCKETS = 32
MAX_DISTANCE = 128
C_HEADS = 4
C_QK_DIM = 64
C_V_DIM = 128
C_QK_WIDTH = 256
C_WIDTH = 512
IN_SPLITS = (A_WIDTH, A_WIDTH, B_WIDTH, B_KV_WIDTH, B_KV_WIDTH,
             C_QK_WIDTH, C_QK_WIDTH, C_WIDTH, C_WIDTH, C_HEADS, C_HEADS)
IN_OFFSETS = tuple(int(o) for o in np.cumsum(IN_SPLITS)[:-1])
D_FF = 5632
CONV_W = 3

Z_WIDTH = 4096
Z_AU, Z_AV, Z_BQ, Z_BK, Z_BV, Z_CQ, Z_CK, Z_CV, Z_CO, Z_GATE = (0,) + IN_OFFSETS[:9]
C_PAIR = 2 * C_V_DIM

MLSTM_L = 128
UP_CHUNK = 256
UP_ROWS = 256
UP_TILE_DECODE = 1408
C_SEQS = 2
DEC_ROWS = 8
MM_TILE_N = {(D_MODEL, Z_WIDTH): Z_WIDTH, (D_MODEL, D_MODEL): D_MODEL, (D_MODEL, 2 * D_FF): 512, (D_FF, D_MODEL): 512}
LANES = 128
SUBLANES = 8
VMEM_LIMIT = 52 * 1024 * 1024


def _cparams(sem, flags=None):
    return pltpu.CompilerParams(dimension_semantics=sem, vmem_limit_bytes=VMEM_LIMIT, flags=flags)


def _gelu(x):
    return 0.5 * x * (1.0 + lax.erf(x * SQRT_HALF))


def _rms(x, gain):
    return x * lax.rsqrt(jnp.mean(x * x, axis=-1, keepdims=True) + EPS) * gain


def _norm_head_pairs(x, gain2):
    lo = lax.broadcasted_iota(jnp.int32, x.shape, 1) < B_HEAD_DIM
    x2 = x * x
    s_lo = jnp.sum(jnp.where(lo, x2, 0.0), axis=-1, keepdims=True)
    s_hi = jnp.sum(jnp.where(lo, 0.0, x2), axis=-1, keepdims=True)
    ms = jnp.where(lo, s_lo, s_hi) * (1.0 / B_HEAD_DIM)
    return x * lax.rsqrt(ms + EPS) * gain2


def _t5_bucket_np(dist):
    n = np.maximum(dist, 0)
    max_exact = N_BUCKETS // 2
    nf = np.maximum(n, 1).astype(np.float32)
    large = max_exact + (np.log(nf / np.float32(max_exact)) / np.float32(math.log(MAX_DISTANCE / max_exact))
                         * np.float32(N_BUCKETS - max_exact)).astype(np.int32)
    return np.where(n < max_exact, n, np.minimum(large, N_BUCKETS - 1)).astype(np.int32)


def _bias_kernel(rb_ref, bkp_ref, bks_ref, op_ref, os_ref):
    bkp = bkp_ref[...]
    bks = bks_ref[...]
    for h in range(B_HEADS):
        accp = jnp.zeros(bkp.shape, f32)
        accs = jnp.zeros(bks.shape, f32)
        for b in range(N_BUCKETS):
            val = rb_ref[b, h]
            accp = jnp.where(bkp == b, val, accp)
            accs = jnp.where(bks == b, val, accs)
        op_ref[h] = accp
        os_ref[h:h + 1, :] = accs[0:1, :]


def _bias_tables(rel_bias):
    cj = np.arange(WINDOW)[:, None]
    qi = np.arange(WINDOW)[None, :]
    bkp = _t5_bucket_np((qi - cj) % WINDOW)
    j = np.arange(2 * WINDOW)
    dist_s = np.where(j < WINDOW, WINDOW - j, 0)
    bks = np.broadcast_to(_t5_bucket_np(dist_s)[None, :], (SUBLANES, 2 * WINDOW)).copy()
    return pl.pallas_call(
        _bias_kernel,
        out_shape=(jax.ShapeDtypeStruct((B_HEADS, WINDOW, WINDOW), f32),
                   jax.ShapeDtypeStruct((B_HEADS, 2 * WINDOW), f32)),
        in_specs=[pl.BlockSpec(memory_space=pltpu.SMEM),
                  pl.BlockSpec(memory_space=pltpu.VMEM),
                  pl.BlockSpec(memory_space=pltpu.VMEM)],
        out_specs=(pl.BlockSpec(memory_space=pltpu.VMEM), pl.BlockSpec(memory_space=pltpu.VMEM)),
        name="bias_tables",
    )(rel_bias, jnp.asarray(bkp), jnp.asarray(bks))


def _cast_job(src, layer, n_chunks, chunk_of_step):
    _, k, n = src.shape
    rows = k // n_chunks
    return (pl.BlockSpec((None, rows, n), lambda *g: (layer, chunk_of_step(*g), 0)),
            pl.BlockSpec((rows, n), lambda *g: (chunk_of_step(*g), 0)),
            jax.ShapeDtypeStruct((k, n), bf16))


def _cast_chunk(src_ref, dst_ref):
    dst_ref[...] = src_ref[...].astype(bf16)


def _mm_in_kernel(*refs, cast):
    if cast:
        x_ref, g_ref, w_ref, src_ref, z_ref, dst_ref, h_ref = refs
        _cast_chunk(src_ref, dst_ref)
    else:
        x_ref, g_ref, w_ref, z_ref, h_ref = refs

    @pl.when(pl.program_id(1) == 0)
    def _():
        h_ref[...] = _rms(x_ref[...], g_ref[...]).astype(bf16)

    z_ref[...] = jnp.dot(h_ref[...], w_ref[...], preferred_element_type=f32)


def _mm_in(x, gain, w, layer, tm, cast=None):
    M = x.shape[0]
    tn = MM_TILE_N[w.shape[-2:]]
    nj = Z_WIDTH // tn
    in_specs = [pl.BlockSpec((tm, D_MODEL), lambda i, j: (i, 0)),
                pl.BlockSpec((None, 1, D_MODEL), lambda i, j: (layer, 0, 0)),
                pl.BlockSpec((None, D_MODEL, tn), lambda i, j: (layer, 0, j),
                             pipeline_mode=pl.Buffered(1 if nj == 1 else 2))]
    out_specs = [pl.BlockSpec((tm, tn), lambda i, j: (i, j))]
    out_shape = [jax.ShapeDtypeStruct((M, Z_WIDTH), f32)]
    args = [x, gain, w]
    if cast:
        job = _cast_job(cast[0], cast[1], (M // tm) * nj, lambda i, j: i * nj + j)
        in_specs.append(job[0]); out_specs.append(job[1]); out_shape.append(job[2]); args.append(cast[0])
    out = pl.pallas_call(
        functools.partial(_mm_in_kernel, cast=bool(cast)),
        grid=(M // tm, nj),
        in_specs=in_specs, out_specs=out_specs, out_shape=out_shape,
        scratch_shapes=[pltpu.VMEM((tm, D_MODEL), bf16)],
        compiler_params=_cparams(("arbitrary", "arbitrary")),
        name="mm_in",
    )(*args)
    return out if cast else out[0]


def _mm_out_kernel(a_ref, b_ref, c_ref, x_ref, w_ref, o_ref, lhs_ref):
    @pl.when(pl.program_id(1) == 0)
    def _():
        lhs_ref[:, 0:A_WIDTH] = a_ref[...]
        lhs_ref[:, A_WIDTH:A_WIDTH + B_WIDTH] = b_ref[...]
        lhs_ref[:, A_WIDTH + B_WIDTH:D_MODEL] = c_ref[...]

    o_ref[...] = x_ref[...] + jnp.dot(lhs_ref[...], w_ref[...], preferred_element_type=f32)


def _mm_out(a, b, c, x, w, tm):
    M = x.shape[0]
    tn = MM_TILE_N[w.shape]
    return pl.pallas_call(
        _mm_out_kernel,
        grid=(M // tm, D_MODEL // tn),
        in_specs=[pl.BlockSpec((tm, A_WIDTH), lambda i, j: (i, 0)),
                  pl.BlockSpec((tm, B_WIDTH), lambda i, j: (i, 0)),
                  pl.BlockSpec((tm, C_WIDTH), lambda i, j: (i, 0)),
                  pl.BlockSpec((tm, tn), lambda i, j: (i, j)),
                  pl.BlockSpec((D_MODEL, tn), lambda i, j: (0, j),
                               pipeline_mode=pl.Buffered(1 if tn == D_MODEL else 2))],
        out_specs=pl.BlockSpec((tm, tn), lambda i, j: (i, j)),
        out_shape=jax.ShapeDtypeStruct((M, D_MODEL), f32),
        scratch_shapes=[pltpu.VMEM((tm, D_MODEL), bf16)],
        compiler_params=_cparams(("parallel", "arbitrary")),
        name="mm_out",
    )(a, b, c, x, w)


def _mm_down_kernel(a_ref, x_ref, w_ref, o_ref):
    o_ref[...] = x_ref[...] + jnp.dot(a_ref[...], w_ref[...], preferred_element_type=f32)


def _mm_down(act, x, w, tm):
    M = x.shape[0]
    tn = MM_TILE_N[w.shape]
    return pl.pallas_call(
        _mm_down_kernel,
        grid=(M // tm, D_MODEL // tn),
        in_specs=[pl.BlockSpec((tm, D_FF), lambda i, j: (i, 0)),
                  pl.BlockSpec((tm, tn), lambda i, j: (i, j)),
                  pl.BlockSpec((D_FF, tn), lambda i, j: (0, j))],
        out_specs=pl.BlockSpec((tm, tn), lambda i, j: (i, j)),
        out_shape=jax.ShapeDtypeStruct((M, D_MODEL), f32),
        compiler_params=_cparams(("parallel", "arbitrary")),
        name="mm_down",
    )(act, x, w)


def _silu(x):
    return x * jax.nn.sigmoid(x)


def _mm_up_prompt_kernel(x_ref, n2_ref, wg_ref, wa_ref, cwg_ref, cwa_ref, cbg_ref, cba_ref, src_ref,
                         act_ref, cs_ref, dst_ref, h_ref, carry_ref, zb_ref,
                         *, tm, nj, n_steps, tiles_per_seq):
    _cast_chunk(src_ref, dst_ref)
    s = pl.program_id(0)
    sa = jnp.minimum(s, n_steps - 1)
    ia = sa // nj
    ja = sa % nj
    tf = act_ref.shape[1]

    @pl.when(s == 0)
    def _():
        zb_ref[...] = jnp.zeros(zb_ref.shape, f32)
        carry_ref[...] = jnp.zeros(carry_ref.shape, f32)

    @pl.when(ja == 0)
    def _():
        h_ref[...] = _rms(x_ref[...], n2_ref[...]).astype(bf16)

    chunks = [slice(c0, c0 + UP_CHUNK) for c0 in range(0, tf, UP_CHUNK)]

    rblocks = [(r0, min(UP_ROWS, tm - r0)) for r0 in range(0, tm, UP_ROWS)]

    def conv(idx, cw_ref, cb_ref, cs, r0, nr):
        zz = zb_ref[idx, r0:r0 + SUBLANES + nr, cs]
        z1 = pltpu.roll(zz, 1, 0)[SUBLANES:]
        z2 = pltpu.roll(zz, 2, 0)[SUBLANES:]
        return (cb_ref[:, cs] + z2 * cw_ref[0:1, cs] + z1 * cw_ref[1:2, cs] + zz[SUBLANES:] * cw_ref[2:3, cs])

    for cs in chunks:
        for r0, nr in rblocks:
            g = conv(0, cwg_ref, cbg_ref, cs, r0, nr)
            a = conv(1, cwa_ref, cba_ref, cs, r0, nr)
            act_ref[r0:r0 + nr, cs] = (_silu(g) * a).astype(bf16)

    seq_start = (ia % tiles_per_seq) == 0
    for cs in chunks:
        for r0, nr in rblocks:
            for idx, w_ref in ((0, wg_ref), (1, wa_ref)):
                z = jnp.dot(h_ref[r0:r0 + nr, :], w_ref[:, cs], preferred_element_type=f32)
                if r0 == 0:
                    zb_ref[idx, 0:SUBLANES, cs] = jnp.where(seq_start, 0.0, carry_ref[idx, ja, :, cs])
                zb_ref[idx, SUBLANES + r0:SUBLANES + r0 + nr, cs] = z
                if r0 + nr == tm:
                    tail = z[nr - SUBLANES:nr, :]
                    carry_ref[idx, ja, :, cs] = tail
                    cs_ref[0, idx, :, cs] = tail


def _mm_up_prompt(x, norm2, w_up, conv_w, conv_b, layer, batch, tm, cast):
    M = x.shape[0]
    seq = M // batch
    tiles_per_seq = seq // tm
    tf = MM_TILE_N[w_up.shape]
    nj = D_FF // tf
    n_steps = (M // tm) * nj
    job = _cast_job(cast[0], cast[1], n_steps, lambda s: jnp.minimum(s, n_steps - 1))
    kern = functools.partial(_mm_up_prompt_kernel, tm=tm, nj=nj, n_steps=n_steps, tiles_per_seq=tiles_per_seq)
    ia = lambda s: jnp.minimum(s, n_steps - 1) // nj
    ja = lambda s: jnp.minimum(s, n_steps - 1) % nj
    ib = lambda s: jnp.maximum(s - 1, 0) // nj
    jb = lambda s: jnp.maximum(s - 1, 0) % nj
    return pl.pallas_call(
        kern,
        grid=(n_steps + 1,),
        in_specs=[pl.BlockSpec((tm, D_MODEL), lambda s: (ia(s), 0)),
                  pl.BlockSpec((None, 1, D_MODEL), lambda s: (layer, 0, 0)),
                  pl.BlockSpec((D_MODEL, tf), lambda s: (0, ja(s))),
                  pl.BlockSpec((D_MODEL, tf), lambda s: (0, nj + ja(s))),
                  pl.BlockSpec((None, CONV_W, tf), lambda s: (layer, 0, jb(s))),
                  pl.BlockSpec((None, CONV_W, tf), lambda s: (layer, 0, nj + jb(s))),
                  pl.BlockSpec((None, 1, tf), lambda s: (layer, 0, jb(s))),
                  pl.BlockSpec((None, 1, tf), lambda s: (layer, 0, nj + jb(s))),
                  job[0]],
        out_specs=(pl.BlockSpec((tm, tf), lambda s: (ib(s), jb(s))),
                   pl.BlockSpec((1, 2, SUBLANES, tf), lambda s: (ia(s), 0, 0, ja(s))),
                   job[1]),
        out_shape=(jax.ShapeDtypeStruct((M, D_FF), bf16),
                   jax.ShapeDtypeStruct((M // tm, 2, SUBLANES, D_FF), f32),
                   job[2]),
        scratch_shapes=[pltpu.VMEM((tm, D_MODEL), bf16),
                        pltpu.VMEM((2, nj, SUBLANES, tf), f32),
                        pltpu.VMEM((2, tm + SUBLANES, tf), f32)],
        compiler_params=_cparams(("arbitrary",)),
        name="mm_up_prompt",
    )(x, norm2, w_up, w_up, conv_w, conv_w, conv_b, conv_b, cast[0])


def _mm_up_sample_kernel(x_ref, n2_ref, wg_ref, wa_ref, cwg_ref, cwa_ref, cbg_ref, cba_ref,
                         b0g_ref, b1g_ref, b0a_ref, b1a_ref, act_ref, zg_ref, za_ref, h_ref):
    @pl.when(pl.program_id(0) == 0)
    def _():
        h_ref[...] = _rms(x_ref[...], n2_ref[...]).astype(bf16)

    def conv_half(w_ref, cw_ref, cb_ref, b0_ref, b1_ref, z_ref):
        z = jnp.dot(h_ref[...], w_ref[...], preferred_element_type=f32)
        z_ref[...] = z
        return (cb_ref[...] + b0_ref[...] * cw_ref[0:1, :] + b1_ref[...] * cw_ref[1:2, :]
                + z * cw_ref[2:3, :])

    g = conv_half(wg_ref, cwg_ref, cbg_ref, b0g_ref, b1g_ref, zg_ref)
    a = conv_half(wa_ref, cwa_ref, cba_ref, b0a_ref, b1a_ref, za_ref)
    act_ref[...] = (_silu(g) * a).astype(bf16)


def _mm_up_sample(x, norm2, w_up, conv_w, conv_b, buf, layer):
    M = x.shape[0]
    tf = UP_TILE_DECODE
    nj = D_FF // tf
    wspec = lambda off: pl.BlockSpec((D_MODEL, tf), lambda j: (0, off + j))
    cwspec = lambda off: pl.BlockSpec((None, CONV_W, tf), lambda j: (layer, 0, off + j))
    cbspec = lambda off: pl.BlockSpec((None, 1, tf), lambda j: (layer, 0, off + j))
    bufspec = lambda row, off: pl.BlockSpec((None, None, M, tf), lambda j: (layer, row, 0, off + j))
    return pl.pallas_call(
        _mm_up_sample_kernel,
        grid=(nj,),
        in_specs=[pl.BlockSpec((M, D_MODEL), lambda j: (0, 0)),
                  pl.BlockSpec((None, 1, D_MODEL), lambda j: (layer, 0, 0)),
                  wspec(0), wspec(nj), cwspec(0), cwspec(nj), cbspec(0), cbspec(nj),
                  bufspec(0, 0), bufspec(1, 0), bufspec(0, nj), bufspec(1, nj)],
        out_specs=(pl.BlockSpec((M, tf), lambda j: (0, j)),
                   pl.BlockSpec((M, tf), lambda j: (0, j)),
                   pl.BlockSpec((M, tf), lambda j: (0, j))),
        out_shape=(jax.ShapeDtypeStruct((M, D_FF), bf16),
                   jax.ShapeDtypeStruct((M, D_FF), f32),
                   jax.ShapeDtypeStruct((M, D_FF), f32)),
        scratch_shapes=[pltpu.VMEM((M, D_MODEL), bf16)],
        compiler_params=_cparams(("arbitrary",)),
        name="mm_up_sample",
    )(x, norm2, w_up, w_up, conv_w, conv_w, conv_b, conv_b, buf, buf, buf, buf)


def _mix_a_prompt_kernel(u_ref, v_ref, vg_ref, ws_ref, bs_ref, o_ref):
    row = lax.broadcasted_iota(jnp.int32, (CHUNK, CHUNK), 0)
    col = lax.broadcasted_iota(jnp.int32, (CHUNK, CHUNK), 1)
    tri = col <= row
    for g in range(A_GROUPS):
        sl = slice(g * A_DIM, (g + 1) * A_DIM)
        ws = jnp.where(tri, ws_ref[g], 0.0).astype(bf16)
        for r0 in range(0, u_ref.shape[0], CHUNK):
            rs = slice(r0, r0 + CHUNK)
            vn = _rms(_gelu(v_ref[rs, sl]), vg_ref[:, sl])
            mixv = jnp.dot(ws, vn.astype(bf16), preferred_element_type=f32) + bs_ref[:, sl]
            o_ref[rs, sl] = (_gelu(u_ref[rs, sl]) * mixv).astype(bf16)


def _mix_a_prompt(z, v_gain, w_s, b_s_full, layer, rows):
    M = z.shape[0]
    return pl.pallas_call(
        _mix_a_prompt_kernel,
        grid=(M // rows,),
        in_specs=[pl.BlockSpec((rows, A_WIDTH), lambda r: (r, Z_AU // A_WIDTH)),
                  pl.BlockSpec((rows, A_WIDTH), lambda r: (r, Z_AV // A_WIDTH)),
                  pl.BlockSpec((None, 1, A_WIDTH), lambda r: (layer, 0, 0)),
                  pl.BlockSpec((None, A_GROUPS, CHUNK, CHUNK), lambda r: (layer, 0, 0, 0)),
                  pl.BlockSpec((None, CHUNK, A_WIDTH), lambda r: (layer, 0, 0))],
        out_specs=pl.BlockSpec((rows, A_WIDTH), lambda r: (r, 0)),
        out_shape=jax.ShapeDtypeStruct((M, A_WIDTH), bf16),
        compiler_params=_cparams(("parallel",)),
        name="mix_a_prompt",
    )(z, z, v_gain, w_s, b_s_full)


def _mix_a_sample_kernel(u_ref, v_ref, vg_ref, ws_ref, bs_ref, o_ref, vn_ref):
    for g in range(A_GROUPS):
        sl = slice(g * A_DIM, (g + 1) * A_DIM)
        vn = _rms(_gelu(v_ref[:, sl]), vg_ref[:, sl])
        vn_ref[:, sl] = vn
        mixv = ws_ref[:, sl] * vn + bs_ref[:, sl]
        o_ref[:, sl] = (_gelu(u_ref[:, sl]) * mixv).astype(bf16)


def _mix_a_sample(z, v_gain, ws0, bs0, layer):
    M = z.shape[0]
    vec = pl.BlockSpec((None, 1, A_WIDTH), lambda r: (layer, 0, 0))
    return pl.pallas_call(
        _mix_a_sample_kernel,
        grid=(1,),
        in_specs=[pl.BlockSpec((M, A_WIDTH), lambda r: (0, Z_AU // A_WIDTH)),
                  pl.BlockSpec((M, A_WIDTH), lambda r: (0, Z_AV // A_WIDTH)),
                  vec, vec, vec],
        out_specs=(pl.BlockSpec((M, A_WIDTH), lambda r: (0, 0)),
                   pl.BlockSpec((M, A_WIDTH), lambda r: (0, 0))),
        out_shape=(jax.ShapeDtypeStruct((M, A_WIDTH), bf16),
                   jax.ShapeDtypeStruct((M, A_WIDTH), f32)),
        compiler_params=_cparams(("arbitrary",)),
        name="mix_a_sample",
    )(z, z, v_gain, ws0, bs0)


def _mix_b_prompt_kernel(sink_ref, q_ref, k_ref, v_ref, qg_ref, kg_ref, bias_ref, src_ref,
                         o_ref, klast_ref, vlast_ref, dst_ref, kcat_ref, vcat_ref):
    _cast_chunk(src_ref, dst_ref)
    i = pl.program_id(1)
    W = WINDOW
    kn = _norm_head_pairs(k_ref[...], kg_ref[...])
    v = v_ref[...]
    klast_ref[0] = kn
    vlast_ref[0] = v

    @pl.when(i == 0)
    def _():
        kcat_ref[0:W, :] = jnp.zeros((W, LANES), f32)
        vcat_ref[0:W, :] = jnp.zeros((W, LANES), f32)

    @pl.when(i > 0)
    def _():
        kcat_ref[0:W, :] = kcat_ref[W:2 * W, :]
        vcat_ref[0:W, :] = vcat_ref[W:2 * W, :]

    kcat_ref[W:2 * W, :] = kn
    vcat_ref[W:2 * W, :] = v
    kc = kcat_ref[...]
    vc = vcat_ref[...]
    kc_sw = pltpu.roll(kc, B_HEAD_DIM, 1)
    vc_sw = pltpu.roll(vc, B_HEAD_DIM, 1)
    lo = lax.broadcasted_iota(jnp.int32, (2 * W, LANES), 1) < B_HEAD_DIM

    cj = lax.broadcasted_iota(jnp.int32, (W, W), 0)
    qi = lax.broadcasted_iota(jnp.int32, (W, W), 1)
    own = cj <= qi
    keep = own | (i > 0)
    q_gain = qg_ref[...] * (B_HEAD_DIM ** -0.5)

    tiles = GQA // 2
    nt = (((1,), (1,)), ((), ()))
    tn = (((0,), (0,)), ((), ()))
    scores, values = [], []
    for kvh in range(B_KV_HEADS):
        k_src, k_alt = (kc, kc_sw) if kvh == 0 else (kc_sw, kc)
        v_src, v_alt = (vc, vc_sw) if kvh == 0 else (vc_sw, vc)
        k_even = jnp.where(lo, k_src, 0.0).astype(bf16)
        k_odd = jnp.where(lo, 0.0, k_alt).astype(bf16)
        values.append((jnp.where(lo, v_src, 0.0).astype(bf16), jnp.where(lo, 0.0, v_alt).astype(bf16)))
        qts = []
        for t in range(tiles):
            c0 = (kvh * tiles + t) * LANES
            qt = q_ref[:, c0:c0 + LANES].T
            q2 = qt * qt
            halves = []
            for r0 in (0, B_HEAD_DIM):
                ms = jnp.mean(q2[r0:r0 + B_HEAD_DIM, :], axis=0, keepdims=True)
                halves.append(qt[r0:r0 + B_HEAD_DIM, :] * lax.rsqrt(ms + EPS))
            qts.append(jnp.concatenate(halves, axis=0) * q_gain)
        qst = jnp.concatenate(qts, axis=1).astype(bf16)
        scores.append((jnp.dot(k_even, qst, preferred_element_type=f32),
                       jnp.dot(k_odd, qst, preferred_element_type=f32)))
    for kvh in range(B_KV_HEADS):
        p_par = []
        for par in range(2):
            blocks = []
            for t in range(tiles):
                h = kvh * GQA + 2 * t + par
                sb = scores[kvh][par][:, t * W:(t + 1) * W]
                s = jnp.where(own, sb[W:2 * W, :], sb[0:W, :]) + bias_ref[h]
                s = jnp.where(keep, s, NEG_INF)
                sk = sink_ref[h]
                mx = jnp.maximum(jnp.max(s, axis=0, keepdims=True), sk)
                p = jnp.exp(s - mx)
                den = jnp.sum(p, axis=0, keepdims=True) + jnp.exp(sk - mx)
                p = p * (1.0 / den)
                blocks.append(jnp.concatenate([jnp.where(own, 0.0, p), jnp.where(own, p, 0.0)],
                                              axis=0).astype(bf16))
            p_par.append(jnp.concatenate(blocks, axis=1))
        ot = (lax.dot_general(values[kvh][0], p_par[0], tn, preferred_element_type=f32)
              + lax.dot_general(values[kvh][1], p_par[1], tn, preferred_element_type=f32))
        for t in range(tiles):
            c0 = (kvh * tiles + t) * LANES
            o_ref[:, c0:c0 + LANES] = ot[:, t * W:(t + 1) * W].T.astype(bf16)


def _mix_b_prompt(z, sinks, q_gain_t, k_gain2, bias_p, layer, batch, cast):
    M = z.shape[0]
    nb = M // batch // WINDOW
    job = _cast_job(cast[0], cast[1], batch * nb, lambda b, i: b * nb + i)
    return pl.pallas_call(
        _mix_b_prompt_kernel,
        grid=(batch, nb),
        in_specs=[pl.BlockSpec(memory_space=pltpu.SMEM),
                  pl.BlockSpec((WINDOW, B_WIDTH), lambda b, i: (b * nb + i, Z_BQ // B_WIDTH)),
                  pl.BlockSpec((WINDOW, B_KV_WIDTH), lambda b, i: (b * nb + i, Z_BK // B_KV_WIDTH)),
                  pl.BlockSpec((WINDOW, B_KV_WIDTH), lambda b, i: (b * nb + i, Z_BV // B_KV_WIDTH)),
                  pl.BlockSpec((None, LANES, WINDOW), lambda b, i: (layer, 0, 0)),
                  pl.BlockSpec((None, 1, LANES), lambda b, i: (layer, 0, 0)),
                  pl.BlockSpec((B_HEADS, WINDOW, WINDOW), lambda b, i: (0, 0, 0)),
                  job[0]],
        out_specs=(pl.BlockSpec((WINDOW, B_WIDTH), lambda b, i: (b * nb + i, 0)),
                   pl.BlockSpec((1, WINDOW, B_KV_WIDTH), lambda b, i: (b, 0, 0)),
                   pl.BlockSpec((1, WINDOW, B_KV_WIDTH), lambda b, i: (b, 0, 0)),
                   job[1]),
        out_shape=(jax.ShapeDtypeStruct((M, B_WIDTH), bf16),
                   jax.ShapeDtypeStruct((batch, WINDOW, B_KV_WIDTH), f32),
                   jax.ShapeDtypeStruct((batch, WINDOW, B_KV_WIDTH), f32),
                   job[2]),
        scratch_shapes=[pltpu.VMEM((2 * WINDOW, LANES), f32),
                        pltpu.VMEM((2 * WINDOW, LANES), f32)],
        compiler_params=_cparams(("arbitrary", "arbitrary")),
        name="mix_b_prompt",
    )(sinks, z, z, z, q_gain_t, k_gain2, bias_p, cast[0])


def _mix_c_prompt_kernel(q_ref, k_ref, v0_ref, v1_ref, o0_ref, o1_ref, g_ref, gb_ref, hgt_ref,
                         out_ref, c_out, n_out, m_out, c_s, n_s, m_s):
    L = MLSTM_L
    c = pl.program_id(1)

    @pl.when(c == 0)
    def _():
        c_s[...] = jnp.zeros(c_s.shape, f32)
        n_s[...] = jnp.zeros(n_s.shape, f32)
        m_s[...] = jnp.zeros(m_s.shape, f32)

    row = lax.broadcasted_iota(jnp.int32, (L, L), 0)
    col = lax.broadcasted_iota(jnp.int32, (L, L), 1)
    tri = col <= row
    tri_f = tri.astype(f32)
    seqs = range(q_ref.shape[0])
    units = [(bi, h) for bi in seqs for h in range(C_HEADS)]
    g, b = {}, {}
    for bi in seqs:
        g[bi] = g_ref[bi] + gb_ref[...]
        b[bi] = jnp.dot(tri_f, jax.nn.log_sigmoid(g[bi]), precision=lax.Precision.HIGHEST,
                        preferred_element_type=f32)
    bT = {bi: b[bi].T for bi in seqs}
    vis = row <= col
    nt = (((1,), (1,)), ((), ()))
    tn = (((0,), (0,)), ((), ()))
    qb, kh, vb, ct_prev, n_prev, st, qct, qn_lin = {}, {}, {}, {}, {}, {}, {}, {}
    for u in units:
        bi, h = u
        v_ref = v0_ref if h < 2 else v1_ref
        vsl = slice((h % 2) * C_V_DIM, (h % 2 + 1) * C_V_DIM)
        qb[u] = q_ref[bi, :, h * C_QK_DIM:(h + 1) * C_QK_DIM].astype(bf16)
        kh[u] = k_ref[bi, :, h * C_QK_DIM:(h + 1) * C_QK_DIM] * (C_QK_DIM ** -0.5)
        vb[u] = v_ref[bi, :, vsl].astype(bf16)
        ct_prev[u] = c_s[bi, h]
        n_prev[u] = n_s[bi, h]
        st[u] = lax.dot_general(kh[u].astype(bf16), qb[u], nt, preferred_element_type=f32)
        qct[u] = lax.dot_general(ct_prev[u].astype(bf16), qb[u], nt, preferred_element_type=f32)
        n8 = jnp.broadcast_to(n_prev[u], (SUBLANES, C_QK_DIM)).astype(bf16)
        qn_lin[u] = lax.dot_general(n8, qb[u], nt, preferred_element_type=f32)[0:1, :]
    pt, iw, inv, kw, decay, m_new = {}, {}, {}, {}, {}, {}
    for u in units:
        bi, h = u
        b_row = bT[bi][C_HEADS + h:C_HEADS + h + 1, :]
        src = g[bi][:, h:h + 1] - b[bi][:, C_HEADS + h:C_HEADS + h + 1]
        dlog = jnp.where(vis, b_row + src, -jnp.inf)
        m_prev = m_s[bi, h][0:1, 0:1]
        inter = b_row + m_prev
        mt = jnp.maximum(inter, jnp.max(dlog, axis=0, keepdims=True))
        dt = jnp.exp(dlog - mt)
        pt[u] = st[u] * dt
        iw[u] = jnp.exp(inter - mt)
        qn = iw[u] * qn_lin[u] + jnp.sum(pt[u], axis=0, keepdims=True)
        inv[u] = 1.0 / jnp.maximum(jnp.abs(qn), jnp.exp(-mt))
        m_new[u] = mt[:, L - 1:L]
        decay[u] = jnp.exp(b_row[:, L - 1:L] + m_prev - m_new[u])
        kw[u] = kh[u] * dt[:, L - 1:L]
    svt = {u: lax.dot_general(vb[u], pt[u].astype(bf16), tn, preferred_element_type=f32) for u in units}
    kvt = {u: lax.dot_general(vb[u], kw[u].astype(bf16), tn, preferred_element_type=f32) for u in units}
    for u in units:
        bi, h = u
        o_ref = o0_ref if h < 2 else o1_ref
        vsl = slice((h % 2) * C_V_DIM, (h % 2 + 1) * C_V_DIM)
        ht = (iw[u] * qct[u] + svt[u]) * inv[u]
        ms = jnp.mean(ht * ht, axis=0, keepdims=True)
        hn = (ht * lax.rsqrt(ms + EPS) * hgt_ref[h]).T
        c_s[bi, h] = decay[u] * ct_prev[u] + kvt[u]
        n_s[bi, h] = decay[u] * n_prev[u] + jnp.sum(kw[u], axis=0, keepdims=True)
        m_s[bi, h] = jnp.broadcast_to(m_new[u], (SUBLANES, LANES))
        osl = slice(h * C_V_DIM, (h + 1) * C_V_DIM)
        out_ref[bi, :, osl] = (hn * jax.nn.sigmoid(o_ref[bi, :, vsl])).astype(bf16)

    @pl.when(c == pl.num_programs(1) - 1)
    def _():
        eye = (row == col).astype(f32)
        for bi, h in units:
            c_out[bi, h] = lax.dot_general(c_s[bi, h], eye, tn, precision=lax.Precision.HIGHEST,
                                           preferred_element_type=f32)
        n_out[...] = n_s[...]
        m_out[...] = m_s[...]


def _mix_c_prompt(z, gate_bias, h_gain, layer, batch):
    M = z.shape[0]
    L = MLSTM_L
    T = M // batch
    z3 = z.reshape(batch, T, Z_WIDTH)
    ns = C_SEQS if batch % C_SEQS == 0 else 1
    col = lambda width, off: pl.BlockSpec((ns, L, width), lambda b, c: (b, c, off // width))
    state = lambda *shape: pl.BlockSpec((ns,) + shape, lambda b, c: (b,) + (0,) * len(shape))
    out, c_st, n_st, m_st = pl.pallas_call(
        _mix_c_prompt_kernel,
        grid=(batch // ns, T // L),
        in_specs=[col(C_QK_WIDTH, Z_CQ), col(C_QK_WIDTH, Z_CK),
                  col(C_PAIR, Z_CV), col(C_PAIR, Z_CV + C_PAIR),
                  col(C_PAIR, Z_CO), col(C_PAIR, Z_CO + C_PAIR),
                  col(LANES, Z_GATE),
                  pl.BlockSpec((None, 1, LANES), lambda b, c: (layer, 0, 0)),
                  pl.BlockSpec((None, C_HEADS, C_V_DIM, L), lambda b, c: (layer, 0, 0, 0))],
        out_specs=(pl.BlockSpec((ns, L, C_WIDTH), lambda b, c: (b, c, 0)),
                   state(C_HEADS, C_QK_DIM, C_V_DIM),
                   state(C_HEADS, 1, C_QK_DIM),
                   state(C_HEADS, SUBLANES, LANES)),
        out_shape=(jax.ShapeDtypeStruct((batch, T, C_WIDTH), bf16),
                   jax.ShapeDtypeStruct((batch, C_HEADS, C_QK_DIM, C_V_DIM), f32),
                   jax.ShapeDtypeStruct((batch, C_HEADS, 1, C_QK_DIM), f32),
                   jax.ShapeDtypeStruct((batch, C_HEADS, SUBLANES, LANES), f32)),
        scratch_shapes=[pltpu.VMEM((ns, C_HEADS, C_V_DIM, C_QK_DIM), f32),
                        pltpu.VMEM((ns, C_HEADS, 1, C_QK_DIM), f32),
                        pltpu.VMEM((ns, C_HEADS, SUBLANES, LANES), f32)],
        compiler_params=_cparams(("arbitrary", "arbitrary")),
        name="mix_c_prompt",
    )(z3, z3, z3, z3, z3, z3, z3, gate_bias, h_gain)
    return out.reshape(M, C_WIDTH), c_st, n_st, m_st


_BC_SHARED = (4, 5, 6, 7, 12, 13)


def _mix_bc_sample_kernel(*refs):
    rows = [_mix_bc_sample_row(*[ref if k in _BC_SHARED else ref.at[pl.ds(r, 1)] for k, ref in enumerate(refs)])
            for r in range(refs[0].shape[0])]
    while rows:
        rows = [row for row in rows if next(row, True) is None]


def _mix_bc_sample_row(zr_ref, q_ref, kp_ref, vp_ref, qg_ref, kg_ref, sink_ref, bias_ref,
                       cq_ref, ck_ref, cv_ref, co_ref, gb_ref, hg_ref, c0_ref, n0_ref, m0_ref,
                       bo_ref, kn_ref, co_out, c_out, n_out, m_out):
    W = WINDOW
    zr = zr_ref[0]

    qn = _rms(q_ref[0], qg_ref[...])
    hrow = lax.broadcasted_iota(jnp.int32, (B_HEADS, LANES), 0)
    lane = lax.broadcasted_iota(jnp.int32, (B_HEADS, LANES), 1)
    own = (hrow < GQA) == (lane < B_HEAD_DIM)
    q2 = jnp.where(own, jnp.concatenate([qn, qn], axis=-1), 0.0)
    k_new = _norm_head_pairs(zr[:, Z_BK:Z_BK + B_KV_WIDTH], kg_ref[...])
    v_new = zr[:, Z_BV:Z_BV + B_KV_WIDTH]
    kn_ref[0] = k_new
    scale = B_HEAD_DIM ** -0.5
    s_past = lax.dot_general(q2.astype(bf16), kp_ref[0].astype(bf16), (((1,), (1,)), ((), ())),
                             preferred_element_type=f32) * scale + bias_ref[:, 0:W]
    yield
    s_past = jnp.where(lane >= 1, s_past, NEG_INF)
    s_new = jnp.sum(q2 * k_new, axis=-1, keepdims=True) * scale + bias_ref[:, W:W + 1]
    sk = sink_ref[...]
    mx = jnp.maximum(jnp.maximum(jnp.max(s_past, axis=-1, keepdims=True), s_new), sk)
    p_past = jnp.exp(s_past - mx)
    p_new = jnp.exp(s_new - mx)
    den = jnp.sum(p_past, axis=-1, keepdims=True) + p_new + jnp.exp(sk - mx)
    inv = 1.0 / den
    o2 = (jnp.dot((p_past * inv).astype(bf16), vp_ref[0].astype(bf16), preferred_element_type=f32)
          + (p_new * inv) * v_new)
    o2_sw = pltpu.roll(o2, B_HEAD_DIM, 1)
    bo_ref[0] = jnp.where(hrow < GQA, o2, o2_sw)[:, 0:B_HEAD_DIM].astype(bf16)

    g = zr[:, Z_GATE:Z_GATE + LANES] + gb_ref[...]
    lf = jax.nn.log_sigmoid(g)
    hrow4 = lax.broadcasted_iota(jnp.int32, (C_HEADS, LANES), 0)
    lane4 = lax.broadcasted_iota(jnp.int32, (C_HEADS, LANES), 1)

    def head_column(row_vec, first_lane):
        spread = jnp.broadcast_to(row_vec, (C_HEADS, LANES))
        return jnp.sum(jnp.where(lane4 == hrow4 + first_lane, spread, 0.0), axis=-1, keepdims=True)

    ig = head_column(g, 0)
    b = head_column(lf, C_HEADS)
    m0 = m0_ref[0]
    m_prev = jnp.sum(jnp.where(lane4[:, 0:C_HEADS] == hrow4[:, 0:C_HEADS],
                               jnp.broadcast_to(m0, (C_HEADS, C_HEADS)), 0.0), axis=-1, keepdims=True)
    inter = b + m_prev
    mt = jnp.maximum(inter, ig)
    q4 = cq_ref[0]
    k4 = ck_ref[0] * (C_QK_DIM ** -0.5)
    v4 = cv_ref[0]
    n4 = n0_ref[0]
    c0 = c0_ref[0]
    s = jnp.sum(q4 * k4, axis=-1, keepdims=True) * jnp.exp(ig - mt)
    iw = jnp.exp(inter - mt)
    blk_row = lax.broadcasted_iota(jnp.int32, (C_HEADS, C_QK_WIDTH), 0)
    blk_lane = lax.broadcasted_iota(jnp.int32, (C_HEADS, C_QK_WIDTH), 1) // C_QK_DIM
    pad4 = jnp.zeros((SUBLANES - C_HEADS, C_QK_WIDTH), f32)

    def head_blocks(x4):
        tiled = jnp.concatenate([x4] * C_HEADS, axis=-1)
        return jnp.concatenate([jnp.where(blk_lane == blk_row, tiled, 0.0), pad4], axis=0).astype(bf16)

    c0s = c0.reshape(C_QK_WIDTH, C_V_DIM)
    qc = jnp.dot(head_blocks(q4), c0s.astype(bf16), preferred_element_type=f32)[0:C_HEADS, :]
    yield
    num = iw * qc + s * v4
    qn_ = iw * jnp.sum(q4 * n4, axis=-1, keepdims=True) + s
    hh = num / jnp.maximum(jnp.abs(qn_), jnp.exp(-mt))
    decay = iw
    kw = k4 * jnp.exp(ig - mt)
    v8 = jnp.concatenate([v4, jnp.zeros((SUBLANES - C_HEADS, C_V_DIM), f32)], axis=0).astype(bf16)
    outer = lax.dot_general(head_blocks(kw), v8, (((0,), (0,)), ((), ())),
                            preferred_element_type=f32)
    for h in range(C_HEADS):
        c_out[0, h] = decay[h:h + 1, :] * c0[h] + outer[h * C_QK_DIM:(h + 1) * C_QK_DIM, :]
    n_out[0] = decay * n4 + kw
    m_out[0] = jnp.sum(jnp.where(lane4 == hrow4, jnp.broadcast_to(mt, (C_HEADS, LANES)), 0.0),
                       axis=0, keepdims=True)
    co_out[0] = (_rms(hh, hg_ref[...]) * jax.nn.sigmoid(co_ref[0])).astype(bf16)


def _mix_bc_sample(z, k_past, v_past, q_gain, k_gain2, sinks, bias_s, gate_bias, h_gain4,
                   c0, n0, m0, layer):
    R = z.shape[0]
    zr = z.reshape(R, 1, Z_WIDTH)
    q = z[:, Z_BQ:Z_BQ + B_WIDTH].reshape(R, B_HEADS, B_HEAD_DIM)
    cq = z[:, Z_CQ:Z_CQ + C_QK_WIDTH].reshape(R, C_HEADS, C_QK_DIM)
    ck = z[:, Z_CK:Z_CK + C_QK_WIDTH].reshape(R, C_HEADS, C_QK_DIM)
    cv = z[:, Z_CV:Z_CV + C_WIDTH].reshape(R, C_HEADS, C_V_DIM)
    co = z[:, Z_CO:Z_CO + C_WIDTH].reshape(R, C_HEADS, C_V_DIM)
    rb = DEC_ROWS if R % DEC_ROWS == 0 else 1
    row3 = lambda n: pl.BlockSpec((rb, 1, n), lambda r: (r, 0, 0))
    return pl.pallas_call(
        _mix_bc_sample_kernel,
        grid=(R // rb,),
        in_specs=[row3(Z_WIDTH),
                  pl.BlockSpec((rb, B_HEADS, B_HEAD_DIM), lambda r: (r, 0, 0)),
                  pl.BlockSpec((None, rb, WINDOW, B_KV_WIDTH), lambda r: (layer, r, 0, 0)),
                  pl.BlockSpec((None, rb, WINDOW, B_KV_WIDTH), lambda r: (layer, r, 0, 0)),
                  pl.BlockSpec((None, 1, B_HEAD_DIM), lambda r: (layer, 0, 0)),
                  pl.BlockSpec((None, 1, LANES), lambda r: (layer, 0, 0)),
                  pl.BlockSpec((None, B_HEADS, 1), lambda r: (layer, 0, 0)),
                  pl.BlockSpec((B_HEADS, 2 * WINDOW), lambda r: (0, 0)),
                  pl.BlockSpec((rb, C_HEADS, C_QK_DIM), lambda r: (r, 0, 0)),
                  pl.BlockSpec((rb, C_HEADS, C_QK_DIM), lambda r: (r, 0, 0)),
                  pl.BlockSpec((rb, C_HEADS, C_V_DIM), lambda r: (r, 0, 0)),
                  pl.BlockSpec((rb, C_HEADS, C_V_DIM), lambda r: (r, 0, 0)),
                  pl.BlockSpec((None, 1, LANES), lambda r: (layer, 0, 0)),
                  pl.BlockSpec((None, C_HEADS, C_V_DIM), lambda r: (layer, 0, 0)),
                  pl.BlockSpec((None, rb, C_HEADS, C_QK_DIM, C_V_DIM), lambda r: (layer, r, 0, 0, 0)),
                  pl.BlockSpec((None, rb, C_HEADS, C_QK_DIM), lambda r: (layer, r, 0, 0)),
                  pl.BlockSpec((None, rb, 1, C_HEADS), lambda r: (layer, r, 0, 0))],
        out_specs=(pl.BlockSpec((rb, B_HEADS, B_HEAD_DIM), lambda r: (r, 0, 0)),
                   row3(B_KV_WIDTH),
                   pl.BlockSpec((rb, C_HEADS, C_V_DIM), lambda r: (r, 0, 0)),
                   pl.BlockSpec((rb, C_HEADS, C_QK_DIM, C_V_DIM), lambda r: (r, 0, 0, 0)),
                   pl.BlockSpec((rb, C_HEADS, C_QK_DIM), lambda r: (r, 0, 0)),
                   row3(LANES)),
        out_shape=(jax.ShapeDtypeStruct((R, B_HEADS, B_HEAD_DIM), bf16),
                   jax.ShapeDtypeStruct((R, 1, B_KV_WIDTH), f32),
                   jax.ShapeDtypeStruct((R, C_HEADS, C_V_DIM), bf16),
                   jax.ShapeDtypeStruct((R, C_HEADS, C_QK_DIM, C_V_DIM), f32),
                   jax.ShapeDtypeStruct((R, C_HEADS, C_QK_DIM), f32),
                   jax.ShapeDtypeStruct((R, 1, LANES), f32)),
        compiler_params=_cparams(("parallel",)),
        name="mix_bc_sample",
    )(zr, q, k_past, v_past, q_gain, k_gain2, sinks, bias_s, cq, ck, cv, co, gate_bias, h_gain4,
      c0, n0, m0)


def _prep_w_in(w_in):
    pad = jnp.zeros(w_in.shape[:-1] + (Z_WIDTH - w_in.shape[-1],), w_in.dtype)
    return jnp.concatenate([w_in, pad], axis=-1).astype(bf16)


def _row_tile(m, pref):
    return pref if m % pref == 0 else m


def kernel(x_prompt, x_sample, cache_swa_k, cache_swa_v, state_mlstm_C, state_mlstm_n, state_mlstm_m, state_ffn_conv, rel_bias, norm1, w_in, a_v_gain, a_spatial_w, a_spatial_b, b_q_gain, b_k_gain, b_sinks, c_gate_bias, c_h_gain, w_out, norm2, w_up, ffn_conv_w, ffn_conv_b, w_down):
    depth = w_in.shape[0]
    Bp, T, _ = x_prompt.shape
    R = x_sample.shape[0]
    assert x_sample.shape[1] == 1 and T % CHUNK == 0 and T % MLSTM_L == 0

    w_in_p = _prep_w_in(w_in)
    norm1_3 = norm1.reshape(depth, 1, D_MODEL)
    norm2_3 = norm2.reshape(depth, 1, D_MODEL)
    v_gain3 = a_v_gain.reshape(depth, 1, A_WIDTH)
    bs_full = jnp.repeat(jnp.swapaxes(a_spatial_b, 1, 2), A_DIM, axis=-1)
    ws0 = jnp.repeat(a_spatial_w[:, :, 0, 0], A_DIM, axis=-1).reshape(depth, 1, A_WIDTH)
    bs0 = jnp.repeat(a_spatial_b[:, :, 0], A_DIM, axis=-1).reshape(depth, 1, A_WIDTH)
    q_gain3 = b_q_gain.reshape(depth, 1, B_HEAD_DIM)
    q_gain_t = jnp.broadcast_to(jnp.tile(b_q_gain, (1, 2)).reshape(depth, LANES, 1), (depth, LANES, WINDOW))
    k_gain2 = jnp.tile(b_k_gain, (1, 2)).reshape(depth, 1, LANES)
    sinks3 = b_sinks.reshape(depth, B_HEADS, 1)
    gate_b = jnp.pad(c_gate_bias, ((0, 0), (0, LANES - 2 * C_HEADS))).reshape(depth, 1, LANES)
    h_gain_t = jnp.broadcast_to(c_h_gain.reshape(depth, C_HEADS, C_V_DIM, 1), (depth, C_HEADS, C_V_DIM, MLSTM_L))
    h_gain4 = c_h_gain.reshape(depth, C_HEADS, C_V_DIM)
    conv_b3 = ffn_conv_b.reshape(depth, 1, 2 * D_FF)
    k_cache = cache_swa_k.reshape(depth, R, WINDOW, B_KV_WIDTH)
    v_cache = cache_swa_v.reshape(depth, R, WINDOW, B_KV_WIDTH)
    m_state = state_mlstm_m.reshape(depth, R, 1, C_HEADS)
    conv_hist = jnp.swapaxes(state_ffn_conv, 1, 2)

    bias_p, bias_s = _bias_tables(rel_bias)

    Mp = Bp * T
    tm = _row_tile(T, 1024)
    tm_io = _row_tile(T, 512)
    a_rows = _row_tile(T, 512)
    xp = x_prompt.reshape(Mp, D_MODEL)
    xs = x_sample.reshape(R, D_MODEL)
    P = [[] for _ in range(6)]
    S = [[] for _ in range(7)]
    for l in range(depth):
        z, w_out_l = _mm_in(xp, norm1_3, w_in_p, l, tm_io, cast=(w_out, l))
        a_o = _mix_a_prompt(z, v_gain3, a_spatial_w, bs_full, l, a_rows)
        b_o, k_last, v_last, w_up_l = _mix_b_prompt(z, b_sinks[l], q_gain_t, k_gain2, bias_p, l, Bp, (w_up, l))
        c_o, c_st, n_st, m_st = _mix_c_prompt(z, gate_b, h_gain_t, l, Bp)
        x1 = _mm_out(a_o, b_o, c_o, xp, w_out_l, tm_io)
        act, cs, w_down_l = _mm_up_prompt(x1, norm2_3, w_up_l, ffn_conv_w, conv_b3, l, Bp, tm, (w_down, l))
        xp = _mm_down(act, x1, w_down_l, tm)
        P[0].append(k_last.reshape(Bp, WINDOW, B_KV_HEADS, B_HEAD_DIM))
        P[1].append(v_last.reshape(Bp, WINDOW, B_KV_HEADS, B_HEAD_DIM))
        P[2].append(c_st)
        P[3].append(n_st.reshape(Bp, C_HEADS, C_QK_DIM))
        P[4].append(m_st[:, :, 0, 0])
        seq_tiles = T // tm
        tail = cs[seq_tiles - 1::seq_tiles, :, SUBLANES - (CONV_W - 1):, :]
        P[5].append(jnp.swapaxes(tail, 1, 2).reshape(Bp, CONV_W - 1, 2 * D_FF))

        zs = _mm_in(xs, norm1_3, w_in_p, l, R)
        a_s, vn_s = _mix_a_sample(zs, v_gain3, ws0, bs0, l)
        b_s, kn_s, c_s, c_new, n_new, m_new = _mix_bc_sample(
            zs, k_cache, v_cache, q_gain3, k_gain2, sinks3, bias_s, gate_b, h_gain4,
            state_mlstm_C, state_mlstm_n, m_state, l)
        x1s = _mm_out(a_s, b_s.reshape(R, B_WIDTH), c_s.reshape(R, C_WIDTH), xs, w_out_l, R)
        act_s, zg_s, za_s = _mm_up_sample(x1s, norm2_3, w_up_l, ffn_conv_w, conv_b3, conv_hist, l)
        xs = _mm_down(act_s, x1s, w_down_l, R)
        S[0].append(vn_s.reshape(R, 1, A_WIDTH))
        S[1].append(kn_s.reshape(R, 1, B_KV_HEADS, B_HEAD_DIM))
        S[2].append(zs[:, Z_BV:Z_BV + B_KV_WIDTH].reshape(R, 1, B_KV_HEADS, B_HEAD_DIM))
        S[3].append(c_new)
        S[4].append(n_new)
        S[5].append(m_new[:, 0, 0:C_HEADS])
        z_new = jnp.concatenate([zg_s, za_s], axis=-1)
        S[6].append(jnp.stack([state_ffn_conv[l][:, CONV_W - 2], z_new], axis=1))

    st = lambda lst: jnp.stack(lst, axis=0)
    return (xp.reshape(Bp, T, D_MODEL), xs.reshape(R, 1, D_MODEL),
            st(P[0]), st(P[1]), st(S[1]), st(S[2]),
            st(P[2]), st(P[3]), st(P[4]),
            st(S[3]), st(S[4]), st(S[5]),
            st(P[5]), st(S[6]),
            st(S[0]))
```

```python
import functools
import math

import numpy as np
import jax
import jax.numpy as jnp
from jax import lax
from jax.experimental import pallas as pl
from jax.experimental.pallas import tpu as pltpu

f32 = jnp.float32
bf16 = jnp.bfloat16

D_MODEL = 2048
EPS = 1e-6
NEG_INF = -1e30
SQRT_HALF = 0.7071067811865476

A_GROUPS = 4
A_DIM = 128
A_WIDTH = 512
CHUNK = 128
B_HEADS = 16
B_KV_HEADS = 2
B_HEAD_DIM = 64
GQA = 8
B_WIDTH = 1024
B_KV_WIDTH = 128
WINDOW = 128
N_BUCKETS = 32
MAX_DISTANCE = 128
C_HEADS = 4
C_QK_DIM = 64
C_V_DIM = 128
C_QK_WIDTH = 256
C_WIDTH = 512
IN_SPLITS = (A_WIDTH, A_WIDTH, B_WIDTH, B_KV_WIDTH, B_KV_WIDTH,
             C_QK_WIDTH, C_QK_WIDTH, C_WIDTH, C_WIDTH, C_HEADS, C_HEADS)
IN_OFFSETS = tuple(int(o) for o in np.cumsum(IN_SPLITS)[:-1])
D_FF = 5632
CONV_W = 3

Z_WIDTH = 4096
Z_AU, Z_AV, Z_BQ, Z_BK, Z_BV, Z_CQ, Z_CK, Z_CV, Z_CO, Z_GATE = (0,) + IN_OFFSETS[:9]
C_PAIR = 2 * C_V_DIM

MLSTM_L = 128
UP_CHUNK = 256
UP_ROWS = 256
UP_TILE_DECODE = 1408
C_SEQS = 2
DEC_ROWS = 8
MM_TILE_N = {(D_MODEL, Z_WIDTH): Z_WIDTH, (D_MODEL, D_MODEL): D_MODEL, (D_MODEL, 2 * D_FF): 512,
             (D_FF, D_MODEL): D_MODEL}
LANES = 128
SUBLANES = 8
VMEM_LIMIT = 52 * 1024 * 1024


def _cparams(sem, flags=None):
    return pltpu.CompilerParams(dimension_semantics=sem, vmem_limit_bytes=VMEM_LIMIT, flags=flags)


def _gelu(x):
    return 0.5 * x * (1.0 + lax.erf(x * SQRT_HALF))


def _rms(x, gain):
    return x * lax.rsqrt(jnp.mean(x * x, axis=-1, keepdims=True) + EPS) * gain


def _norm_head_pairs(x, gain2):
    lo = lax.broadcasted_iota(jnp.int32, x.shape, 1) < B_HEAD_DIM
    x2 = x * x
    s_lo = jnp.sum(jnp.where(lo, x2, 0.0), axis=-1, keepdims=True)
    s_hi = jnp.sum(jnp.where(lo, 0.0, x2), axis=-1, keepdims=True)
    ms = jnp.where(lo, s_lo, s_hi) * (1.0 / B_HEAD_DIM)
    return x * lax.rsqrt(ms + EPS) * gain2


def _t5_bucket_np(dist):
    n = np.maximum(dist, 0)
    max_exact = N_BUCKETS // 2
    nf = np.maximum(n, 1).astype(np.float32)
    large = max_exact + (np.log(nf / np.float32(max_exact)) / np.float32(math.log(MAX_DISTANCE / max_exact))
                         * np.float32(N_BUCKETS - max_exact)).astype(np.int32)
    return np.where(n < max_exact, n, np.minimum(large, N_BUCKETS - 1)).astype(np.int32)


def _bias_kernel(rb_ref, bkp_ref, bks_ref, op_ref, os_ref):
    bkp = bkp_ref[...]
    bks = bks_ref[...]
    for h in range(B_HEADS):
        accp = jnp.zeros(bkp.shape, f32)
        accs = jnp.zeros(bks.shape, f32)
        for b in range(N_BUCKETS):
            val = rb_ref[b, h]
            accp = jnp.where(bkp == b, val, accp)
            accs = jnp.where(bks == b, val, accs)
        op_ref[h] = accp
        os_ref[h:h + 1, :] = accs[0:1, :]


def _bias_tables(rel_bias):
    cj = np.arange(WINDOW)[:, None]
    qi = np.arange(WINDOW)[None, :]
    bkp = _t5_bucket_np((qi - cj) % WINDOW)
    j = np.arange(2 * WINDOW)
    dist_s = np.where(j < WINDOW, WINDOW - j, 0)
    bks = np.broadcast_to(_t5_bucket_np(dist_s)[None, :], (SUBLANES, 2 * WINDOW)).copy()
    return pl.pallas_call(
        _bias_kernel,
        out_shape=(jax.ShapeDtypeStruct((B_HEADS, WINDOW, WINDOW), f32),
                   jax.ShapeDtypeStruct((B_HEADS, 2 * WINDOW), f32)),
        in_specs=[pl.BlockSpec(memory_space=pltpu.SMEM),
                  pl.BlockSpec(memory_space=pltpu.VMEM),
                  pl.BlockSpec(memory_space=pltpu.VMEM)],
        out_specs=(pl.BlockSpec(memory_space=pltpu.VMEM), pl.BlockSpec(memory_space=pltpu.VMEM)),
        name="bias_tables",
    )(rel_bias, jnp.asarray(bkp), jnp.asarray(bks))


def _cast_job(src, layer, n_chunks, chunk_of_step):
    _, k, n = src.shape
    rows = k // n_chunks
    return (pl.BlockSpec((None, rows, n), lambda *g: (layer, chunk_of_step(*g), 0)),
            pl.BlockSpec((rows, n), lambda *g: (chunk_of_step(*g), 0)),
            jax.ShapeDtypeStruct((k, n), bf16))


def _cast_chunk(src_ref, dst_ref):
    dst_ref[...] = src_ref[...].astype(bf16)


def _mm_in_kernel(*refs, cast):
    if cast:
        x_ref, g_ref, w_ref, src_ref, z_ref, dst_ref, h_ref = refs
        _cast_chunk(src_ref, dst_ref)
    else:
        x_ref, g_ref, w_ref, z_ref, h_ref = refs

    @pl.when(pl.program_id(1) == 0)
    def _():
        h_ref[...] = _rms(x_ref[...], g_ref[...]).astype(bf16)

    z_ref[...] = jnp.dot(h_ref[...], w_ref[...], preferred_element_type=f32)


def _mm_in(x, gain, w, layer, tm, cast=None):
    M = x.shape[0]
    tn = MM_TILE_N[w.shape[-2:]]
    nj = Z_WIDTH // tn
    in_specs = [pl.BlockSpec((tm, D_MODEL), lambda i, j: (i, 0)),
                pl.BlockSpec((None, 1, D_MODEL), lambda i, j: (layer, 0, 0)),
                pl.BlockSpec((None, D_MODEL, tn), lambda i, j: (layer, 0, j),
                             pipeline_mode=pl.Buffered(1 if nj == 1 else 2))]
    out_specs = [pl.BlockSpec((tm, tn), lambda i, j: (i, j))]
    out_shape = [jax.ShapeDtypeStruct((M, Z_WIDTH), f32)]
    args = [x, gain, w]
    if cast:
        job = _cast_job(cast[0], cast[1], (M // tm) * nj, lambda i, j: i * nj + j)
        in_specs.append(job[0]); out_specs.append(job[1]); out_shape.append(job[2]); args.append(cast[0])
    out = pl.pallas_call(
        functools.partial(_mm_in_kernel, cast=bool(cast)),
        grid=(M // tm, nj),
        in_specs=in_specs, out_specs=out_specs, out_shape=out_shape,
        scratch_shapes=[pltpu.VMEM((tm, D_MODEL), bf16)],
        compiler_params=_cparams(("arbitrary", "arbitrary")),
        name="mm_in",
    )(*args)
    return out if cast else out[0]


def _mm_out_kernel(a_ref, b_ref, c_ref, x_ref, w_ref, o_ref, lhs_ref):
    @pl.when(pl.program_id(1) == 0)
    def _():
        lhs_ref[:, 0:A_WIDTH] = a_ref[...]
        lhs_ref[:, A_WIDTH:A_WIDTH + B_WIDTH] = b_ref[...]
        lhs_ref[:, A_WIDTH + B_WIDTH:D_MODEL] = c_ref[...]

    o_ref[...] = x_ref[...] + jnp.dot(lhs_ref[...], w_ref[...], preferred_element_type=f32)


def _mm_out(a, b, c, x, w, tm):
    M = x.shape[0]
    tn = MM_TILE_N[w.shape]
    return pl.pallas_call(
        _mm_out_kernel,
        grid=(M // tm, D_MODEL // tn),
        in_specs=[pl.BlockSpec((tm, A_WIDTH), lambda i, j: (i, 0)),
                  pl.BlockSpec((tm, B_WIDTH), lambda i, j: (i, 0)),
                  pl.BlockSpec((tm, C_WIDTH), lambda i, j: (i, 0)),
                  pl.BlockSpec((tm, tn), lambda i, j: (i, j)),
                  pl.BlockSpec((D_MODEL, tn), lambda i, j: (0, j),
                               pipeline_mode=pl.Buffered(1 if tn == D_MODEL else 2))],
        out_specs=pl.BlockSpec((tm, tn), lambda i, j: (i, j)),
        out_shape=jax.ShapeDtypeStruct((M, D_MODEL), f32),
        scratch_shapes=[pltpu.VMEM((tm, D_MODEL), bf16)],
        compiler_params=_cparams(("parallel", "arbitrary")),
        name="mm_out",
    )(a, b, c, x, w)


def _mm_down_kernel(a_ref, x_ref, w_ref, o_ref):
    o_ref[...] = x_ref[...] + jnp.dot(a_ref[...], w_ref[...], preferred_element_type=f32)


def _mm_down(act, x, w, tm):
    M = x.shape[0]
    tn = MM_TILE_N[w.shape]
    return pl.pallas_call(
        _mm_down_kernel,
        grid=(M // tm, D_MODEL // tn),
        in_specs=[pl.BlockSpec((tm, D_FF), lambda i, j: (i, 0)),
                  pl.BlockSpec((tm, tn), lambda i, j: (i, j)),
                  pl.BlockSpec((D_FF, tn), lambda i, j: (0, j),
                               pipeline_mode=pl.Buffered(1 if tn == D_MODEL else 2))],
        out_specs=pl.BlockSpec((tm, tn), lambda i, j: (i, j)),
        out_shape=jax.ShapeDtypeStruct((M, D_MODEL), f32),
        compiler_params=_cparams(("parallel", "arbitrary")),
        name="mm_down",
    )(act, x, w)


def _silu(x):
    return x * jax.nn.sigmoid(x)


def _mm_up_prompt_kernel(x_ref, n2_ref, wg_ref, wa_ref, cwg_ref, cwa_ref, cbg_ref, cba_ref, src_ref,
                         act_ref, cs_ref, dst_ref, h_ref, carry_ref, zb_ref,
                         *, tm, nj, n_steps, tiles_per_seq):
    _cast_chunk(src_ref, dst_ref)
    s = pl.program_id(0)
    sa = jnp.minimum(s, n_steps - 1)
    ia = sa // nj
    ja = sa % nj
    tf = act_ref.shape[1]

    @pl.when(s == 0)
    def _():
        zb_ref[...] = jnp.zeros(zb_ref.shape, f32)
        carry_ref[...] = jnp.zeros(carry_ref.shape, f32)

    @pl.when(ja == 0)
    def _():
        h_ref[...] = _rms(x_ref[...], n2_ref[...]).astype(bf16)

    chunks = [slice(c0, c0 + UP_CHUNK) for c0 in range(0, tf, UP_CHUNK)]

    rblocks = [(r0, min(UP_ROWS, tm - r0)) for r0 in range(0, tm, UP_ROWS)]

    def conv(idx, cw_ref, cb_ref, cs, r0, nr):
        zz = zb_ref[idx, r0:r0 + SUBLANES + nr, cs]
        z1 = pltpu.roll(zz, 1, 0)[SUBLANES:]
        z2 = pltpu.roll(zz, 2, 0)[SUBLANES:]
        return (cb_ref[:, cs] + z2 * cw_ref[0:1, cs] + z1 * cw_ref[1:2, cs] + zz[SUBLANES:] * cw_ref[2:3, cs])

    for cs in chunks:
        for r0, nr in rblocks:
            g = conv(0, cwg_ref, cbg_ref, cs, r0, nr)
            a = conv(1, cwa_ref, cba_ref, cs, r0, nr)
            act_ref[r0:r0 + nr, cs] = (_silu(g) * a).astype(bf16)

    seq_start = (ia % tiles_per_seq) == 0
    for cs in chunks:
        for r0, nr in rblocks:
            for idx, w_ref in ((0, wg_ref), (1, wa_ref)):
                z = jnp.dot(h_ref[r0:r0 + nr, :], w_ref[:, cs], preferred_element_type=f32)
                if r0 == 0:
                    zb_ref[idx, 0:SUBLANES, cs] = jnp.where(seq_start, 0.0, carry_ref[idx, ja, :, cs])
                zb_ref[idx, SUBLANES + r0:SUBLANES + r0 + nr, cs] = z
                if r0 + nr == tm:
                    tail = z[nr - SUBLANES:nr, :]
                    carry_ref[idx, ja, :, cs] = tail
                    cs_ref[0, idx, :, cs] = tail


def _mm_up_prompt(x, norm2, w_up, conv_w, conv_b, layer, batch, tm, cast):
    M = x.shape[0]
    seq = M // batch
    tiles_per_seq = seq // tm
    tf = MM_TILE_N[w_up.shape]
    nj = D_FF // tf
    n_steps = (M // tm) * nj
    job = _cast_job(cast[0], cast[1], n_steps, lambda s: jnp.minimum(s, n_steps - 1))
    kern = functools.partial(_mm_up_prompt_kernel, tm=tm, nj=nj, n_steps=n_steps, tiles_per_seq=tiles_per_seq)
    ia = lambda s: jnp.minimum(s, n_steps - 1) // nj
    ja = lambda s: jnp.minimum(s, n_steps - 1) % nj
    ib = lambda s: jnp.maximum(s - 1, 0) // nj
    jb = lambda s: jnp.maximum(s - 1, 0) % nj
    return pl.pallas_call(
        kern,
        grid=(n_steps + 1,),
        in_specs=[pl.BlockSpec((tm, D_MODEL), lambda s: (ia(s), 0)),
                  pl.BlockSpec((None, 1, D_MODEL), lambda s: (layer, 0, 0)),
                  pl.BlockSpec((D_MODEL, tf), lambda s: (0, ja(s))),
                  pl.BlockSpec((D_MODEL, tf), lambda s: (0, nj + ja(s))),
                  pl.BlockSpec((None, CONV_W, tf), lambda s: (layer, 0, jb(s))),
                  pl.BlockSpec((None, CONV_W, tf), lambda s: (layer, 0, nj + jb(s))),
                  pl.BlockSpec((None, 1, tf), lambda s: (layer, 0, jb(s))),
                  pl.BlockSpec((None, 1, tf), lambda s: (layer, 0, nj + jb(s))),
                  job[0]],
        out_specs=(pl.BlockSpec((tm, tf), lambda s: (ib(s), jb(s))),
                   pl.BlockSpec((1, 2, SUBLANES, tf), lambda s: (ia(s), 0, 0, ja(s))),
                   job[1]),
        out_shape=(jax.ShapeDtypeStruct((M, D_FF), bf16),
                   jax.ShapeDtypeStruct((M // tm, 2, SUBLANES, D_FF), f32),
                   job[2]),
        scratch_shapes=[pltpu.VMEM((tm, D_MODEL), bf16),
                        pltpu.VMEM((2, nj, SUBLANES, tf), f32),
                        pltpu.VMEM((2, tm + SUBLANES, tf), f32)],
        compiler_params=_cparams(("arbitrary",)),
        name="mm_up_prompt",
    )(x, norm2, w_up, w_up, conv_w, conv_w, conv_b, conv_b, cast[0])


def _mm_up_sample_kernel(x_ref, n2_ref, wg_ref, wa_ref, cwg_ref, cwa_ref, cbg_ref, cba_ref,
                         b0g_ref, b1g_ref, b0a_ref, b1a_ref, act_ref, zg_ref, za_ref, h_ref):
    @pl.when(pl.program_id(0) == 0)
    def _():
        h_ref[...] = _rms(x_ref[...], n2_ref[...]).astype(bf16)

    def conv_half(w_ref, cw_ref, cb_ref, b0_ref, b1_ref, z_ref):
        z = jnp.dot(h_ref[...], w_ref[...], preferred_element_type=f32)
        z_ref[...] = z
        return (cb_ref[...] + b0_ref[...] * cw_ref[0:1, :] + b1_ref[...] * cw_ref[1:2, :]
                + z * cw_ref[2:3, :])

    g = conv_half(wg_ref, cwg_ref, cbg_ref, b0g_ref, b1g_ref, zg_ref)
    a = conv_half(wa_ref, cwa_ref, cba_ref, b0a_ref, b1a_ref, za_ref)
    act_ref[...] = (_silu(g) * a).astype(bf16)


def _mm_up_sample(x, norm2, w_up, conv_w, conv_b, buf, layer):
    M = x.shape[0]
    tf = UP_TILE_DECODE
    nj = D_FF // tf
    wspec = lambda off: pl.BlockSpec((D_MODEL, tf), lambda j: (0, off + j))
    cwspec = lambda off: pl.BlockSpec((None, CONV_W, tf), lambda j: (layer, 0, off + j))
    cbspec = lambda off: pl.BlockSpec((None, 1, tf), lambda j: (layer, 0, off + j))
    bufspec = lambda row, off: pl.BlockSpec((None, None, M, tf), lambda j: (layer, row, 0, off + j))
    return pl.pallas_call(
        _mm_up_sample_kernel,
        grid=(nj,),
        in_specs=[pl.BlockSpec((M, D_MODEL), lambda j: (0, 0)),
                  pl.BlockSpec((None, 1, D_MODEL), lambda j: (layer, 0, 0)),
                  wspec(0), wspec(nj), cwspec(0), cwspec(nj), cbspec(0), cbspec(nj),
                  bufspec(0, 0), bufspec(1, 0), bufspec(0, nj), bufspec(1, nj)],
        out_specs=(pl.BlockSpec((M, tf), lambda j: (0, j)),
                   pl.BlockSpec((M, tf), lambda j: (0, j)),
                   pl.BlockSpec((M, tf), lambda j: (0, j))),
        out_shape=(jax.ShapeDtypeStruct((M, D_FF), bf16),
                   jax.ShapeDtypeStruct((M, D_FF), f32),
                   jax.ShapeDtypeStruct((M, D_FF), f32)),
        scratch_shapes=[pltpu.VMEM((M, D_MODEL), bf16)],
        compiler_params=_cparams(("arbitrary",)),
        name="mm_up_sample",
    )(x, norm2, w_up, w_up, conv_w, conv_w, conv_b, conv_b, buf, buf, buf, buf)


def _mix_a_prompt_kernel(u_ref, v_ref, vg_ref, ws_ref, bs_ref, o_ref):
    row = lax.broadcasted_iota(jnp.int32, (CHUNK, CHUNK), 0)
    col = lax.broadcasted_iota(jnp.int32, (CHUNK, CHUNK), 1)
    tri = col <= row
    for g in range(A_GROUPS):
        sl = slice(g * A_DIM, (g + 1) * A_DIM)
        ws = jnp.where(tri, ws_ref[g], 0.0).astype(bf16)
        for r0 in range(0, u_ref.shape[0], CHUNK):
            rs = slice(r0, r0 + CHUNK)
            vn = _rms(_gelu(v_ref[rs, sl]), vg_ref[:, sl])
            mixv = jnp.dot(ws, vn.astype(bf16), preferred_element_type=f32) + bs_ref[:, sl]
            o_ref[rs, sl] = (_gelu(u_ref[rs, sl]) * mixv).astype(bf16)


def _mix_a_prompt(z, v_gain, w_s, b_s_full, layer, rows):
    M = z.shape[0]
    return pl.pallas_call(
        _mix_a_prompt_kernel,
        grid=(M // rows,),
        in_specs=[pl.BlockSpec((rows, A_WIDTH), lambda r: (r, Z_AU // A_WIDTH)),
                  pl.BlockSpec((rows, A_WIDTH), lambda r: (r, Z_AV // A_WIDTH)),
                  pl.BlockSpec((None, 1, A_WIDTH), lambda r: (layer, 0, 0)),
                  pl.BlockSpec((None, A_GROUPS, CHUNK, CHUNK), lambda r: (layer, 0, 0, 0)),
                  pl.BlockSpec((None, CHUNK, A_WIDTH), lambda r: (layer, 0, 0))],
        out_specs=pl.BlockSpec((rows, A_WIDTH), lambda r: (r, 0)),
        out_shape=jax.ShapeDtypeStruct((M, A_WIDTH), bf16),
        compiler_params=_cparams(("parallel",)),
        name="mix_a_prompt",
    )(z, z, v_gain, w_s, b_s_full)


def _mix_a_sample_kernel(u_ref, v_ref, vg_ref, ws_ref, bs_ref, o_ref, vn_ref):
    for g in range(A_GROUPS):
        sl = slice(g * A_DIM, (g + 1) * A_DIM)
        vn = _rms(_gelu(v_ref[:, sl]), vg_ref[:, sl])
        vn_ref[:, sl] = vn
        mixv = ws_ref[:, sl] * vn + bs_ref[:, sl]
        o_ref[:, sl] = (_gelu(u_ref[:, sl]) * mixv).astype(bf16)


def _mix_a_sample(z, v_gain, ws0, bs0, layer):
    M = z.shape[0]
    vec = pl.BlockSpec((None, 1, A_WIDTH), lambda r: (layer, 0, 0))
    return pl.pallas_call(
        _mix_a_sample_kernel,
        grid=(1,),
        in_specs=[pl.BlockSpec((M, A_WIDTH), lambda r: (0, Z_AU // A_WIDTH)),
                  pl.BlockSpec((M, A_WIDTH), lambda r: (0, Z_AV // A_WIDTH)),
                  vec, vec, vec],
        out_specs=(pl.BlockSpec((M, A_WIDTH), lambda r: (0, 0)),
                   pl.BlockSpec((M, A_WIDTH), lambda r: (0, 0))),
        out_shape=(jax.ShapeDtypeStruct((M, A_WIDTH), bf16),
                   jax.ShapeDtypeStruct((M, A_WIDTH), f32)),
        compiler_params=_cparams(("arbitrary",)),
        name="mix_a_sample",
    )(z, z, v_gain, ws0, bs0)


def _mix_b_prompt_kernel(sink_ref, q_ref, k_ref, v_ref, qg_ref, kg_ref, bias_ref, src_ref,
                         o_ref, klast_ref, vlast_ref, dst_ref, kcat_ref, vcat_ref):
    _cast_chunk(src_ref, dst_ref)
    i = pl.program_id(1)
    W = WINDOW
    kn = _norm_head_pairs(k_ref[...], kg_ref[...])
    v = v_ref[...]
    klast_ref[0] = kn
    vlast_ref[0] = v

    @pl.when(i == 0)
    def _():
        kcat_ref[0:W, :] = jnp.zeros((W, LANES), f32)
        vcat_ref[0:W, :] = jnp.zeros((W, LANES), f32)

    @pl.when(i > 0)
    def _():
        kcat_ref[0:W, :] = kcat_ref[W:2 * W, :]
        vcat_ref[0:W, :] = vcat_ref[W:2 * W, :]

    kcat_ref[W:2 * W, :] = kn
    vcat_ref[W:2 * W, :] = v
    kc = kcat_ref[...]
    vc = vcat_ref[...]
    kc_sw = pltpu.roll(kc, B_HEAD_DIM, 1)
    vc_sw = pltpu.roll(vc, B_HEAD_DIM, 1)
    lo = lax.broadcasted_iota(jnp.int32, (2 * W, LANES), 1) < B_HEAD_DIM

    cj = lax.broadcasted_iota(jnp.int32, (W, W), 0)
    qi = lax.broadcasted_iota(jnp.int32, (W, W), 1)
    own = cj <= qi
    keep = own | (i > 0)
    q_gain = qg_ref[...] * (B_HEAD_DIM ** -0.5)

    tiles = GQA // 2
    nt = (((1,), (1,)), ((), ()))
    tn = (((0,), (0,)), ((), ()))
    scores, values = [], []
    for kvh in range(B_KV_HEADS):
        k_src, k_alt = (kc, kc_sw) if kvh == 0 else (kc_sw, kc)
        v_src, v_alt = (vc, vc_sw) if kvh == 0 else (vc_sw, vc)
        k_even = jnp.where(lo, k_src, 0.0).astype(bf16)
        k_odd = jnp.where(lo, 0.0, k_alt).astype(bf16)
        values.append((jnp.where(lo, v_src, 0.0).astype(bf16), jnp.where(lo, 0.0, v_alt).astype(bf16)))
        qts = []
        for t in range(tiles):
            c0 = (kvh * tiles + t) * LANES
            qt = q_ref[:, c0:c0 + LANES].T
            q2 = qt * qt
            halves = []
            for r0 in (0, B_HEAD_DIM):
                ms = jnp.mean(q2[r0:r0 + B_HEAD_DIM, :], axis=0, keepdims=True)
                halves.append(qt[r0:r0 + B_HEAD_DIM, :] * lax.rsqrt(ms + EPS))
            qts.append(jnp.concatenate(halves, axis=0) * q_gain)
        qst = jnp.concatenate(qts, axis=1).astype(bf16)
        scores.append((jnp.dot(k_even, qst, preferred_element_type=f32),
                       jnp.dot(k_odd, qst, preferred_element_type=f32)))
    for kvh in range(B_KV_HEADS):
        p_par = []
        for par in range(2):
            blocks = []
            for t in range(tiles):
                h = kvh * GQA + 2 * t + par
                sb = scores[kvh][par][:, t * W:(t + 1) * W]
                s = jnp.where(own, sb[W:2 * W, :], sb[0:W, :]) + bias_ref[h]
                s = jnp.where(keep, s, NEG_INF)
                sk = sink_ref[h]
                mx = jnp.maximum(jnp.max(s, axis=0, keepdims=True), sk)
                p = jnp.exp(s - mx)
                den = jnp.sum(p, axis=0, keepdims=True) + jnp.exp(sk - mx)
                p = p * (1.0 / den)
                blocks.append(jnp.concatenate([jnp.where(own, 0.0, p), jnp.where(own, p, 0.0)],
                                              axis=0).astype(bf16))
            p_par.append(jnp.concatenate(blocks, axis=1))
        ot = (lax.dot_general(values[kvh][0], p_par[0], tn, preferred_element_type=f32)
              + lax.dot_general(values[kvh][1], p_par[1], tn, preferred_element_type=f32))
        for t in range(tiles):
            c0 = (kvh * tiles + t) * LANES
            o_ref[:, c0:c0 + LANES] = ot[:, t * W:(t + 1) * W].T.astype(bf16)


def _mix_b_prompt(z, sinks, q_gain_t, k_gain2, bias_p, layer, batch, cast):
    M = z.shape[0]
    nb = M // batch // WINDOW
    job = _cast_job(cast[0], cast[1], batch * nb, lambda b, i: b * nb + i)
    return pl.pallas_call(
        _mix_b_prompt_kernel,
        grid=(batch, nb),
        in_specs=[pl.BlockSpec(memory_space=pltpu.SMEM),
                  pl.BlockSpec((WINDOW, B_WIDTH), lambda b, i: (b * nb + i, Z_BQ // B_WIDTH)),
                  pl.BlockSpec((WINDOW, B_KV_WIDTH), lambda b, i: (b * nb + i, Z_BK // B_KV_WIDTH)),
                  pl.BlockSpec((WINDOW, B_KV_WIDTH), lambda b, i: (b * nb + i, Z_BV // B_KV_WIDTH)),
                  pl.BlockSpec((None, LANES, WINDOW), lambda b, i: (layer, 0, 0)),
                  pl.BlockSpec((None, 1, LANES), lambda b, i: (layer, 0, 0)),
                  pl.BlockSpec((B_HEADS, WINDOW, WINDOW), lambda b, i: (0, 0, 0)),
                  job[0]],
        out_specs=(pl.BlockSpec((WINDOW, B_WIDTH), lambda b, i: (b * nb + i, 0)),
                   pl.BlockSpec((1, WINDOW, B_KV_WIDTH), lambda b, i: (b, 0, 0)),
                   pl.BlockSpec((1, WINDOW, B_KV_WIDTH), lambda b, i: (b, 0, 0)),
                   job[1]),
        out_shape=(jax.ShapeDtypeStruct((M, B_WIDTH), bf16),
                   jax.ShapeDtypeStruct((batch, WINDOW, B_KV_WIDTH), f32),
                   jax.ShapeDtypeStruct((batch, WINDOW, B_KV_WIDTH), f32),
                   job[2]),
        scratch_shapes=[pltpu.VMEM((2 * WINDOW, LANES), f32),
                        pltpu.VMEM((2 * WINDOW, LANES), f32)],
        compiler_params=_cparams(("arbitrary", "arbitrary")),
        name="mix_b_prompt",
    )(sinks, z, z, z, q_gain_t, k_gain2, bias_p, cast[0])


def _mix_c_prompt_kernel(q_ref, k_ref, v0_ref, v1_ref, o0_ref, o1_ref, g_ref, gb_ref, hgt_ref,
                         out_ref, c_out, n_out, m_out, c_s, n_s, m_s):
    L = MLSTM_L
    c = pl.program_id(1)

    @pl.when(c == 0)
    def _():
        c_s[...] = jnp.zeros(c_s.shape, f32)
        n_s[...] = jnp.zeros(n_s.shape, f32)
        m_s[...] = jnp.zeros(m_s.shape, f32)

    row = lax.broadcasted_iota(jnp.int32, (L, L), 0)
    col = lax.broadcasted_iota(jnp.int32, (L, L), 1)
    tri = col <= row
    tri_f = tri.astype(f32)
    seqs = range(q_ref.shape[0])
    units = [(bi, h) for bi in seqs for h in range(C_HEADS)]
    g, b = {}, {}
    for bi in seqs:
        g[bi] = g_ref[bi] + gb_ref[...]
        b[bi] = jnp.dot(tri_f, jax.nn.log_sigmoid(g[bi]), precision=lax.Precision.HIGHEST,
                        preferred_element_type=f32)
    bT = {bi: b[bi].T for bi in seqs}
    vis = row <= col
    nt = (((1,), (1,)), ((), ()))
    tn = (((0,), (0,)), ((), ()))
    qb, kh, vb, ct_prev, n_prev, st, qct, qn_lin = {}, {}, {}, {}, {}, {}, {}, {}
    for u in units:
        bi, h = u
        v_ref = v0_ref if h < 2 else v1_ref
        vsl = slice((h % 2) * C_V_DIM, (h % 2 + 1) * C_V_DIM)
        qb[u] = q_ref[bi, :, h * C_QK_DIM:(h + 1) * C_QK_DIM].astype(bf16)
        kh[u] = k_ref[bi, :, h * C_QK_DIM:(h + 1) * C_QK_DIM] * (C_QK_DIM ** -0.5)
        vb[u] = v_ref[bi, :, vsl].astype(bf16)
        ct_prev[u] = c_s[bi, h]
        n_prev[u] = n_s[bi, h]
        st[u] = lax.dot_general(kh[u].astype(bf16), qb[u], nt, preferred_element_type=f32)
        qct[u] = lax.dot_general(ct_prev[u].astype(bf16), qb[u], nt, preferred_element_type=f32)
        n8 = jnp.broadcast_to(n_prev[u], (SUBLANES, C_QK_DIM)).astype(bf16)
        qn_lin[u] = lax.dot_general(n8, qb[u], nt, preferred_element_type=f32)[0:1, :]
    pt, iw, inv, kw, decay, m_new = {}, {}, {}, {}, {}, {}
    for u in units:
        bi, h = u
        b_row = bT[bi][C_HEADS + h:C_HEADS + h + 1, :]
        src = g[bi][:, h:h + 1] - b[bi][:, C_HEADS + h:C_HEADS + h + 1]
        dlog = jnp.where(vis, b_row + src, -jnp.inf)
        m_prev = m_s[bi, h][0:1, 0:1]
        inter = b_row + m_prev
        mt = jnp.maximum(inter, jnp.max(dlog, axis=0, keepdims=True))
        dt = jnp.exp(dlog - mt)
        pt[u] = st[u] * dt
        iw[u] = jnp.exp(inter - mt)
        qn = iw[u] * qn_lin[u] + jnp.sum(pt[u], axis=0, keepdims=True)
        inv[u] = 1.0 / jnp.maximum(jnp.abs(qn), jnp.exp(-mt))
        m_new[u] = mt[:, L - 1:L]
        decay[u] = jnp.exp(b_row[:, L - 1:L] + m_prev - m_new[u])
        kw[u] = kh[u] * dt[:, L - 1:L]
    svt = {u: lax.dot_general(vb[u], pt[u].astype(bf16), tn, preferred_element_type=f32) for u in units}
    kvt = {u: lax.dot_general(vb[u], kw[u].astype(bf16), tn, preferred_element_type=f32) for u in units}
    for u in units:
        bi, h = u
        o_ref = o0_ref if h < 2 else o1_ref
        vsl = slice((h % 2) * C_V_DIM, (h % 2 + 1) * C_V_DIM)
        ht = (iw[u] * qct[u] + svt[u]) * inv[u]
        ms = jnp.mean(ht * ht, axis=0, keepdims=True)
        hn = (ht * lax.rsqrt(ms + EPS) * hgt_ref[h]).T
        c_s[bi, h] = decay[u] * ct_prev[u] + kvt[u]
        n_s[bi, h] = decay[u] * n_prev[u] + jnp.sum(kw[u], axis=0, keepdims=True)
        m_s[bi, h] = jnp.broadcast_to(m_new[u], (SUBLANES, LANES))
        osl = slice(h * C_V_DIM, (h + 1) * C_V_DIM)
        out_ref[bi, :, osl] = (hn * jax.nn.sigmoid(o_ref[bi, :, vsl])).astype(bf16)

    @pl.when(c == pl.num_programs(1) - 1)
    def _():
        eye = (row == col).astype(f32)
        for bi, h in units:
            c_out[bi, h] = lax.dot_general(c_s[bi, h], eye, tn, precision=lax.Precision.HIGHEST,
                                           preferred_element_type=f32)
        n_out[...] = n_s[...]
        m_out[...] = m_s[...]


def _mix_c_prompt(z, gate_bias, h_gain, layer, batch):
    M = z.shape[0]
    L = MLSTM_L
    T = M // batch
    z3 = z.reshape(batch, T, Z_WIDTH)
    ns = C_SEQS if batch % C_SEQS == 0 else 1
    col = lambda width, off: pl.BlockSpec((ns, L, width), lambda b, c: (b, c, off // width))
    state = lambda *shape: pl.BlockSpec((ns,) + shape, lambda b, c: (b,) + (0,) * len(shape))
    out, c_st, n_st, m_st = pl.pallas_call(
        _mix_c_prompt_kernel,
        grid=(batch // ns, T // L),
        in_specs=[col(C_QK_WIDTH, Z_CQ), col(C_QK_WIDTH, Z_CK),
                  col(C_PAIR, Z_CV), col(C_PAIR, Z_CV + C_PAIR),
                  col(C_PAIR, Z_CO), col(C_PAIR, Z_CO + C_PAIR),
                  col(LANES, Z_GATE),
                  pl.BlockSpec((None, 1, LANES), lambda b, c: (layer, 0, 0)),
                  pl.BlockSpec((None, C_HEADS, C_V_DIM, L), lambda b, c: (layer, 0, 0, 0))],
        out_specs=(pl.BlockSpec((ns, L, C_WIDTH), lambda b, c: (b, c, 0)),
                   state(C_HEADS, C_QK_DIM, C_V_DIM),
                   state(C_HEADS, 1, C_QK_DIM),
                   state(C_HEADS, SUBLANES, LANES)),
        out_shape=(jax.ShapeDtypeStruct((batch, T, C_WIDTH), bf16),
                   jax.ShapeDtypeStruct((batch, C_HEADS, C_QK_DIM, C_V_DIM), f32),
                   jax.ShapeDtypeStruct((batch, C_HEADS, 1, C_QK_DIM), f32),
                   jax.ShapeDtypeStruct((batch, C_HEADS, SUBLANES, LANES), f32)),
        scratch_shapes=[pltpu.VMEM((ns, C_HEADS, C_V_DIM, C_QK_DIM), f32),
                        pltpu.VMEM((ns, C_HEADS, 1, C_QK_DIM), f32),
                        pltpu.VMEM((ns, C_HEADS, SUBLANES, LANES), f32)],
        compiler_params=_cparams(("arbitrary", "arbitrary")),
        name="mix_c_prompt",
    )(z3, z3, z3, z3, z3, z3, z3, gate_bias, h_gain)
    return out.reshape(M, C_WIDTH), c_st, n_st, m_st


_BC_SHARED = (4, 5, 6, 7, 12, 13)


def _mix_bc_sample_kernel(*refs):
    rows = [_mix_bc_sample_row(*[ref if k in _BC_SHARED else ref.at[pl.ds(r, 1)] for k, ref in enumerate(refs)])
            for r in range(refs[0].shape[0])]
    while rows:
        rows = [row for row in rows if next(row, True) is None]


def _mix_bc_sample_row(zr_ref, q_ref, kp_ref, vp_ref, qg_ref, kg_ref, sink_ref, bias_ref,
                       cq_ref, ck_ref, cv_ref, co_ref, gb_ref, hg_ref, c0_ref, n0_ref, m0_ref,
                       bo_ref, kn_ref, co_out, c_out, n_out, m_out):
    W = WINDOW
    zr = zr_ref[0]

    qn = _rms(q_ref[0], qg_ref[...])
    hrow = lax.broadcasted_iota(jnp.int32, (B_HEADS, LANES), 0)
    lane = lax.broadcasted_iota(jnp.int32, (B_HEADS, LANES), 1)
    own = (hrow < GQA) == (lane < B_HEAD_DIM)
    q2 = jnp.where(own, jnp.concatenate([qn, qn], axis=-1), 0.0)
    k_new = _norm_head_pairs(zr[:, Z_BK:Z_BK + B_KV_WIDTH], kg_ref[...])
    v_new = zr[:, Z_BV:Z_BV + B_KV_WIDTH]
    kn_ref[0] = k_new
    scale = B_HEAD_DIM ** -0.5
    s_past = lax.dot_general(q2.astype(bf16), kp_ref[0].astype(bf16), (((1,), (1,)), ((), ())),
                             preferred_element_type=f32) * scale + bias_ref[:, 0:W]
    yield
    s_past = jnp.where(lane >= 1, s_past, NEG_INF)
    s_new = jnp.sum(q2 * k_new, axis=-1, keepdims=True) * scale + bias_ref[:, W:W + 1]
    sk = sink_ref[...]
    mx = jnp.maximum(jnp.maximum(jnp.max(s_past, axis=-1, keepdims=True), s_new), sk)
    p_past = jnp.exp(s_past - mx)
    p_new = jnp.exp(s_new - mx)
    den = jnp.sum(p_past, axis=-1, keepdims=True) + p_new + jnp.exp(sk - mx)
    inv = 1.0 / den
    o2 = (jnp.dot((p_past * inv).astype(bf16), vp_ref[0].astype(bf16), preferred_element_type=f32)
          + (p_new * inv) * v_new)
    o2_sw = pltpu.roll(o2, B_HEAD_DIM, 1)
    bo_ref[0] = jnp.where(hrow < GQA, o2, o2_sw)[:, 0:B_HEAD_DIM].astype(bf16)

    g = zr[:, Z_GATE:Z_GATE + LANES] + gb_ref[...]
    lf = jax.nn.log_sigmoid(g)
    hrow4 = lax.broadcasted_iota(jnp.int32, (C_HEADS, LANES), 0)
    lane4 = lax.broadcasted_iota(jnp.int32, (C_HEADS, LANES), 1)

    def head_column(row_vec, first_lane):
        spread = jnp.broadcast_to(row_vec, (C_HEADS, LANES))
        return jnp.sum(jnp.where(lane4 == hrow4 + first_lane, spread, 0.0), axis=-1, keepdims=True)

    ig = head_column(g, 0)
    b = head_column(lf, C_HEADS)
    m0 = m0_ref[0]
    m_prev = jnp.sum(jnp.where(lane4[:, 0:C_HEADS] == hrow4[:, 0:C_HEADS],
                               jnp.broadcast_to(m0, (C_HEADS, C_HEADS)), 0.0), axis=-1, keepdims=True)
    inter = b + m_prev
    mt = jnp.maximum(inter, ig)
    q4 = cq_ref[0]
    k4 = ck_ref[0] * (C_QK_DIM ** -0.5)
    v4 = cv_ref[0]
    n4 = n0_ref[0]
    c0 = c0_ref[0]
    s = jnp.sum(q4 * k4, axis=-1, keepdims=True) * jnp.exp(ig - mt)
    iw = jnp.exp(inter - mt)
    blk_row = lax.broadcasted_iota(jnp.int32, (C_HEADS, C_QK_WIDTH), 0)
    blk_lane = lax.broadcasted_iota(jnp.int32, (C_HEADS, C_QK_WIDTH), 1) // C_QK_DIM
    pad4 = jnp.zeros((SUBLANES - C_HEADS, C_QK_WIDTH), f32)

    def head_blocks(x4):
        tiled = jnp.concatenate([x4] * C_HEADS, axis=-1)
        return jnp.concatenate([jnp.where(blk_lane == blk_row, tiled, 0.0), pad4], axis=0).astype(bf16)

    c0s = c0.reshape(C_QK_WIDTH, C_V_DIM)
    qc = jnp.dot(head_blocks(q4), c0s.astype(bf16), preferred_element_type=f32)[0:C_HEADS, :]
    yield
    num = iw * qc + s * v4
    qn_ = iw * jnp.sum(q4 * n4, axis=-1, keepdims=True) + s
    hh = num / jnp.maximum(jnp.abs(qn_), jnp.exp(-mt))
    decay = iw
    kw = k4 * jnp.exp(ig - mt)
    v8 = jnp.concatenate([v4, jnp.zeros((SUBLANES - C_HEADS, C_V_DIM), f32)], axis=0).astype(bf16)
    outer = lax.dot_general(head_blocks(kw), v8, (((0,), (0,)), ((), ())),
                            preferred_element_type=f32)
    for h in range(C_HEADS):
        c_out[0, h] = decay[h:h + 1, :] * c0[h] + outer[h * C_QK_DIM:(h + 1) * C_QK_DIM, :]
    n_out[0] = decay * n4 + kw
    m_out[0] = jnp.sum(jnp.where(lane4 == hrow4, jnp.broadcast_to(mt, (C_HEADS, LANES)), 0.0),
                       axis=0, keepdims=True)
    co_out[0] = (_rms(hh, hg_ref[...]) * jax.nn.sigmoid(co_ref[0])).astype(bf16)


def _mix_bc_sample(z, k_past, v_past, q_gain, k_gain2, sinks, bias_s, gate_bias, h_gain4,
                   c0, n0, m0, layer):
    R = z.shape[0]
    zr = z.reshape(R, 1, Z_WIDTH)
    q = z[:, Z_BQ:Z_BQ + B_WIDTH].reshape(R, B_HEADS, B_HEAD_DIM)
    cq = z[:, Z_CQ:Z_CQ + C_QK_WIDTH].reshape(R, C_HEADS, C_QK_DIM)
    ck = z[:, Z_CK:Z_CK + C_QK_WIDTH].reshape(R, C_HEADS, C_QK_DIM)
    cv = z[:, Z_CV:Z_CV + C_WIDTH].reshape(R, C_HEADS, C_V_DIM)
    co = z[:, Z_CO:Z_CO + C_WIDTH].reshape(R, C_HEADS, C_V_DIM)
    rb = DEC_ROWS if R % DEC_ROWS == 0 else 1
    row3 = lambda n: pl.BlockSpec((rb, 1, n), lambda r: (r, 0, 0))
    return pl.pallas_call(
        _mix_bc_sample_kernel,
        grid=(R // rb,),
        in_specs=[row3(Z_WIDTH),
                  pl.BlockSpec((rb, B_HEADS, B_HEAD_DIM), lambda r: (r, 0, 0)),
                  pl.BlockSpec((None, rb, WINDOW, B_KV_WIDTH), lambda r: (layer, r, 0, 0)),
                  pl.BlockSpec((None, rb, WINDOW, B_KV_WIDTH), lambda r: (layer, r, 0, 0)),
                  pl.BlockSpec((None, 1, B_HEAD_DIM), lambda r: (layer, 0, 0)),
                  pl.BlockSpec((None, 1, LANES), lambda r: (layer, 0, 0)),
                  pl.BlockSpec((None, B_HEADS, 1), lambda r: (layer, 0, 0)),
                  pl.BlockSpec((B_HEADS, 2 * WINDOW), lambda r: (0, 0)),
                  pl.BlockSpec((rb, C_HEADS, C_QK_DIM), lambda r: (r, 0, 0)),
                  pl.BlockSpec((rb, C_HEADS, C_QK_DIM), lambda r: (r, 0, 0)),
                  pl.BlockSpec((rb, C_HEADS, C_V_DIM), lambda r: (r, 0, 0)),
                  pl.BlockSpec((rb, C_HEADS, C_V_DIM), lambda r: (r, 0, 0)),
                  pl.BlockSpec((None, 1, LANES), lambda r: (layer, 0, 0)),
                  pl.BlockSpec((None, C_HEADS, C_V_DIM), lambda r: (layer, 0, 0)),
                  pl.BlockSpec((None, rb, C_HEADS, C_QK_DIM, C_V_DIM), lambda r: (layer, r, 0, 0, 0)),
                  pl.BlockSpec((None, rb, C_HEADS, C_QK_DIM), lambda r: (layer, r, 0, 0)),
                  pl.BlockSpec((None, rb, 1, C_HEADS), lambda r: (layer, r, 0, 0))],
        out_specs=(pl.BlockSpec((rb, B_HEADS, B_HEAD_DIM), lambda r: (r, 0, 0)),
                   row3(B_KV_WIDTH),
                   pl.BlockSpec((rb, C_HEADS, C_V_DIM), lambda r: (r, 0, 0)),
                   pl.BlockSpec((rb, C_HEADS, C_QK_DIM, C_V_DIM), lambda r: (r, 0, 0, 0)),
                   pl.BlockSpec((rb, C_HEADS, C_QK_DIM), lambda r: (r, 0, 0)),
                   row3(LANES)),
        out_shape=(jax.ShapeDtypeStruct((R, B_HEADS, B_HEAD_DIM), bf16),
                   jax.ShapeDtypeStruct((R, 1, B_KV_WIDTH), f32),
                   jax.ShapeDtypeStruct((R, C_HEADS, C_V_DIM), bf16),
                   jax.ShapeDtypeStruct((R, C_HEADS, C_QK_DIM, C_V_DIM), f32),
                   jax.ShapeDtypeStruct((R, C_HEADS, C_QK_DIM), f32),
                   jax.ShapeDtypeStruct((R, 1, LANES), f32)),
        compiler_params=_cparams(("parallel",)),
        name="mix_bc_sample",
    )(zr, q, k_past, v_past, q_gain, k_gain2, sinks, bias_s, cq, ck, cv, co, gate_bias, h_gain4,
      c0, n0, m0)


def _prep_w_in(w_in):
    pad = jnp.zeros(w_in.shape[:-1] + (Z_WIDTH - w_in.shape[-1],), w_in.dtype)
    return jnp.concatenate([w_in, pad], axis=-1).astype(bf16)


def _row_tile(m, pref):
    return pref if m % pref == 0 else m


def kernel(x_prompt, x_sample, cache_swa_k, cache_swa_v, state_mlstm_C, state_mlstm_n, state_mlstm_m, state_ffn_conv, rel_bias, norm1, w_in, a_v_gain, a_spatial_w, a_spatial_b, b_q_gain, b_k_gain, b_sinks, c_gate_bias, c_h_gain, w_out, norm2, w_up, ffn_conv_w, ffn_conv_b, w_down):
    depth = w_in.shape[0]
    Bp, T, _ = x_prompt.shape
    R = x_sample.shape[0]
    assert x_sample.shape[1] == 1 and T % CHUNK == 0 and T % MLSTM_L == 0

    w_in_p = _prep_w_in(w_in)
    norm1_3 = norm1.reshape(depth, 1, D_MODEL)
    norm2_3 = norm2.reshape(depth, 1, D_MODEL)
    v_gain3 = a_v_gain.reshape(depth, 1, A_WIDTH)
    bs_full = jnp.repeat(jnp.swapaxes(a_spatial_b, 1, 2), A_DIM, axis=-1)
    ws0 = jnp.repeat(a_spatial_w[:, :, 0, 0], A_DIM, axis=-1).reshape(depth, 1, A_WIDTH)
    bs0 = jnp.repeat(a_spatial_b[:, :, 0], A_DIM, axis=-1).reshape(depth, 1, A_WIDTH)
    q_gain3 = b_q_gain.reshape(depth, 1, B_HEAD_DIM)
    q_gain_t = jnp.broadcast_to(jnp.tile(b_q_gain, (1, 2)).reshape(depth, LANES, 1), (depth, LANES, WINDOW))
    k_gain2 = jnp.tile(b_k_gain, (1, 2)).reshape(depth, 1, LANES)
    sinks3 = b_sinks.reshape(depth, B_HEADS, 1)
    gate_b = jnp.pad(c_gate_bias, ((0, 0), (0, LANES - 2 * C_HEADS))).reshape(depth, 1, LANES)
    h_gain_t = jnp.broadcast_to(c_h_gain.reshape(depth, C_HEADS, C_V_DIM, 1), (depth, C_HEADS, C_V_DIM, MLSTM_L))
    h_gain4 = c_h_gain.reshape(depth, C_HEADS, C_V_DIM)
    conv_b3 = ffn_conv_b.reshape(depth, 1, 2 * D_FF)
    k_cache = cache_swa_k.reshape(depth, R, WINDOW, B_KV_WIDTH)
    v_cache = cache_swa_v.reshape(depth, R, WINDOW, B_KV_WIDTH)
    m_state = state_mlstm_m.reshape(depth, R, 1, C_HEADS)
    conv_hist = jnp.swapaxes(state_ffn_conv, 1, 2)

    bias_p, bias_s = _bias_tables(rel_bias)

    Mp = Bp * T
    tm = _row_tile(T, 1024)
    tm_io = _row_tile(T, 512)
    tm_down = _row_tile(T, 256)
    a_rows = _row_tile(T, 512)
    xp = x_prompt.reshape(Mp, D_MODEL)
    xs = x_sample.reshape(R, D_MODEL)
    P = [[] for _ in range(6)]
    S = [[] for _ in range(7)]
    for l in range(depth):
        z, w_out_l = _mm_in(xp, norm1_3, w_in_p, l, tm_io, cast=(w_out, l))
        a_o = _mix_a_prompt(z, v_gain3, a_spatial_w, bs_full, l, a_rows)
        b_o, k_last, v_last, w_up_l = _mix_b_prompt(z, b_sinks[l], q_gain_t, k_gain2, bias_p, l, Bp, (w_up, l))
        c_o, c_st, n_st, m_st = _mix_c_prompt(z, gate_b, h_gain_t, l, Bp)
        x1 = _mm_out(a_o, b_o, c_o, xp, w_out_l, tm_io)
        act, cs, w_down_l = _mm_up_prompt(x1, norm2_3, w_up_l, ffn_conv_w, conv_b3, l, Bp, tm, (w_down, l))
        xp = _mm_down(act, x1, w_down_l, tm_down)
        P[0].append(k_last.reshape(Bp, WINDOW, B_KV_HEADS, B_HEAD_DIM))
        P[1].append(v_last.reshape(Bp, WINDOW, B_KV_HEADS, B_HEAD_DIM))
        P[2].append(c_st)
        P[3].append(n_st.reshape(Bp, C_HEADS, C_QK_DIM))
        P[4].append(m_st[:, :, 0, 0])
        seq_tiles = T // tm
        tail = cs[seq_tiles - 1::seq_tiles, :, SUBLANES - (CONV_W - 1):, :]
        P[5].append(jnp.swapaxes(tail, 1, 2).reshape(Bp, CONV_W - 1, 2 * D_FF))

        zs = _mm_in(xs, norm1_3, w_in_p, l, R)
        a_s, vn_s = _mix_a_sample(zs, v_gain3, ws0, bs0, l)
        b_s, kn_s, c_s, c_new, n_new, m_new = _mix_bc_sample(
            zs, k_cache, v_cache, q_gain3, k_gain2, sinks3, bias_s, gate_b, h_gain4,
            state_mlstm_C, state_mlstm_n, m_state, l)
        x1s = _mm_out(a_s, b_s.reshape(R, B_WIDTH), c_s.reshape(R, C_WIDTH), xs, w_out_l, R)
        act_s, zg_s, za_s = _mm_up_sample(x1s, norm2_3, w_up_l, ffn_conv_w, conv_b3, conv_hist, l)
        xs = _mm_down(act_s, x1s, w_down_l, R)
        S[0].append(vn_s.reshape(R, 1, A_WIDTH))
        S[1].append(kn_s.reshape(R, 1, B_KV_HEADS, B_HEAD_DIM))
        S[2].append(zs[:, Z_BV:Z_BV + B_KV_WIDTH].reshape(R, 1, B_KV_HEADS, B_HEAD_DIM))
        S[3].append(c_new)
        S[4].append(n_new)
        S[5].append(m_new[:, 0, 0:C_HEADS])
        z_new = jnp.concatenate([zg_s, za_s], axis=-1)
        S[6].append(jnp.stack([state_ffn_conv[l][:, CONV_W - 2], z_new], axis=1))

    st = lambda lst: jnp.stack(lst, axis=0)
    return (xp.reshape(Bp, T, D_MODEL), xs.reshape(R, 1, D_MODEL),
            st(P[0]), st(P[1]), st(S[1]), st(S[2]),
            st(P[2]), st(P[3]), st(P[4]),
            st(S[3]), st(S[4]), st(S[5]),
            st(P[5]), st(S[6]),
            st(S[0]))
```

```python
import functools
import math

import numpy as np
import jax
import jax.numpy as jnp
from jax import lax
from jax.experimental import pallas as pl
from jax.experimental.pallas import tpu as pltpu

f32 = jnp.float32
bf16 = jnp.bfloat16

D_MODEL = 2048
EPS = 1e-6
NEG_INF = -1e30
SQRT_HALF = 0.7071067811865476

A_GROUPS = 4
A_DIM = 128
A_WIDTH = 512
CHUNK = 128
B_HEADS = 16
B_KV_HEADS = 2
B_HEAD_DIM = 64
GQA = 8
B_WIDTH = 1024
B_KV_WIDTH = 128
WINDOW = 128
N_BUCKETS = 32
MAX_DISTANCE = 128
C_HEADS = 4
C_QK_DIM = 64
C_V_DIM = 128
C_QK_WIDTH = 256
C_WIDTH = 512
IN_SPLITS = (A_WIDTH, A_WIDTH, B_WIDTH, B_KV_WIDTH, B_KV_WIDTH,
             C_QK_WIDTH, C_QK_WIDTH, C_WIDTH, C_WIDTH, C_HEADS, C_HEADS)
IN_OFFSETS = tuple(int(o) for o in np.cumsum(IN_SPLITS)[:-1])
D_FF = 5632
CONV_W = 3

Z_WIDTH = 4096
Z_AU, Z_AV, Z_BQ, Z_BK, Z_BV, Z_CQ, Z_CK, Z_CV, Z_CO, Z_GATE = (0,) + IN_OFFSETS[:9]
C_PAIR = 2 * C_V_DIM

MLSTM_L = 128
UP_CHUNK = 256
UP_ROWS = 512
UP_TILE_DECODE = 1408
C_SEQS = 2
DEC_ROWS = 8
MM_TILE_N = {(D_MODEL, Z_WIDTH): Z_WIDTH, (D_MODEL, D_MODEL): D_MODEL, (D_MODEL, 2 * D_FF): 512,
             (D_FF, D_MODEL): D_MODEL}
LANES = 128
SUBLANES = 8
VMEM_LIMIT = 52 * 1024 * 1024


def _cparams(sem, flags=None):
    return pltpu.CompilerParams(dimension_semantics=sem, vmem_limit_bytes=VMEM_LIMIT, flags=flags)


def _gelu(x):
    return 0.5 * x * (1.0 + lax.erf(x * SQRT_HALF))


def _rms(x, gain):
    return x * lax.rsqrt(jnp.mean(x * x, axis=-1, keepdims=True) + EPS) * gain


def _norm_head_pairs(x, gain2):
    lo = lax.broadcasted_iota(jnp.int32, x.shape, 1) < B_HEAD_DIM
    x2 = x * x
    s_lo = jnp.sum(jnp.where(lo, x2, 0.0), axis=-1, keepdims=True)
    s_hi = jnp.sum(jnp.where(lo, 0.0, x2), axis=-1, keepdims=True)
    ms = jnp.where(lo, s_lo, s_hi) * (1.0 / B_HEAD_DIM)
    return x * lax.rsqrt(ms + EPS) * gain2


def _t5_bucket_np(dist):
    n = np.maximum(dist, 0)
    max_exact = N_BUCKETS // 2
    nf = np.maximum(n, 1).astype(np.float32)
    large = max_exact + (np.log(nf / np.float32(max_exact)) / np.float32(math.log(MAX_DISTANCE / max_exact))
                         * np.float32(N_BUCKETS - max_exact)).astype(np.int32)
    return np.where(n < max_exact, n, np.minimum(large, N_BUCKETS - 1)).astype(np.int32)


def _bias_kernel(rb_ref, bkp_ref, bks_ref, op_ref, os_ref):
    bkp = bkp_ref[...]
    bks = bks_ref[...]
    for h in range(B_HEADS):
        accp = jnp.zeros(bkp.shape, f32)
        accs = jnp.zeros(bks.shape, f32)
        for b in range(N_BUCKETS):
            val = rb_ref[b, h]
            accp = jnp.where(bkp == b, val, accp)
            accs = jnp.where(bks == b, val, accs)
        op_ref[h] = accp
        os_ref[h:h + 1, :] = accs[0:1, :]


def _bias_tables(rel_bias):
    cj = np.arange(WINDOW)[:, None]
    qi = np.arange(WINDOW)[None, :]
    bkp = _t5_bucket_np((qi - cj) % WINDOW)
    j = np.arange(2 * WINDOW)
    dist_s = np.where(j < WINDOW, WINDOW - j, 0)
    bks = np.broadcast_to(_t5_bucket_np(dist_s)[None, :], (SUBLANES, 2 * WINDOW)).copy()
    return pl.pallas_call(
        _bias_kernel,
        out_shape=(jax.ShapeDtypeStruct((B_HEADS, WINDOW, WINDOW), f32),
                   jax.ShapeDtypeStruct((B_HEADS, 2 * WINDOW), f32)),
        in_specs=[pl.BlockSpec(memory_space=pltpu.SMEM),
                  pl.BlockSpec(memory_space=pltpu.VMEM),
                  pl.BlockSpec(memory_space=pltpu.VMEM)],
        out_specs=(pl.BlockSpec(memory_space=pltpu.VMEM), pl.BlockSpec(memory_space=pltpu.VMEM)),
        name="bias_tables",
    )(rel_bias, jnp.asarray(bkp), jnp.asarray(bks))


def _cast_job(src, layer, n_chunks, chunk_of_step):
    _, k, n = src.shape
    rows = k // n_chunks
    return (pl.BlockSpec((None, rows, n), lambda *g: (layer, chunk_of_step(*g), 0)),
            pl.BlockSpec((rows, n), lambda *g: (chunk_of_step(*g), 0)),
            jax.ShapeDtypeStruct((k, n), bf16))


def _cast_chunk(src_ref, dst_ref):
    dst_ref[...] = src_ref[...].astype(bf16)


def _mm_in_kernel(*refs, cast):
    if cast:
        x_ref, g_ref, w_ref, src_ref, z_ref, dst_ref, h_ref = refs
        _cast_chunk(src_ref, dst_ref)
    else:
        x_ref, g_ref, w_ref, z_ref, h_ref = refs

    @pl.when(pl.program_id(1) == 0)
    def _():
        h_ref[...] = _rms(x_ref[...], g_ref[...]).astype(bf16)

    z_ref[...] = jnp.dot(h_ref[...], w_ref[...], preferred_element_type=f32)


def _mm_in(x, gain, w, layer, tm, cast=None):
    M = x.shape[0]
    tn = MM_TILE_N[w.shape[-2:]]
    nj = Z_WIDTH // tn
    in_specs = [pl.BlockSpec((tm, D_MODEL), lambda i, j: (i, 0)),
                pl.BlockSpec((None, 1, D_MODEL), lambda i, j: (layer, 0, 0)),
                pl.BlockSpec((None, D_MODEL, tn), lambda i, j: (layer, 0, j),
                             pipeline_mode=pl.Buffered(1 if nj == 1 else 2))]
    out_specs = [pl.BlockSpec((tm, tn), lambda i, j: (i, j))]
    out_shape = [jax.ShapeDtypeStruct((M, Z_WIDTH), f32)]
    args = [x, gain, w]
    if cast:
        job = _cast_job(cast[0], cast[1], (M // tm) * nj, lambda i, j: i * nj + j)
        in_specs.append(job[0]); out_specs.append(job[1]); out_shape.append(job[2]); args.append(cast[0])
    out = pl.pallas_call(
        functools.partial(_mm_in_kernel, cast=bool(cast)),
        grid=(M // tm, nj),
        in_specs=in_specs, out_specs=out_specs, out_shape=out_shape,
        scratch_shapes=[pltpu.VMEM((tm, D_MODEL), bf16)],
        compiler_params=_cparams(("arbitrary", "arbitrary")),
        name="mm_in",
    )(*args)
    return out if cast else out[0]


def _mm_out_kernel(a_ref, b_ref, c_ref, x_ref, w_ref, o_ref, lhs_ref):
    @pl.when(pl.program_id(1) == 0)
    def _():
        lhs_ref[:, 0:A_WIDTH] = a_ref[...]
        lhs_ref[:, A_WIDTH:A_WIDTH + B_WIDTH] = b_ref[...]
        lhs_ref[:, A_WIDTH + B_WIDTH:D_MODEL] = c_ref[...]

    o_ref[...] = x_ref[...] + jnp.dot(lhs_ref[...], w_ref[...], preferred_element_type=f32)


def _mm_out(a, b, c, x, w, tm):
    M = x.shape[0]
    tn = MM_TILE_N[w.shape]
    return pl.pallas_call(
        _mm_out_kernel,
        grid=(M // tm, D_MODEL // tn),
        in_specs=[pl.BlockSpec((tm, A_WIDTH), lambda i, j: (i, 0)),
                  pl.BlockSpec((tm, B_WIDTH), lambda i, j: (i, 0)),
                  pl.BlockSpec((tm, C_WIDTH), lambda i, j: (i, 0)),
                  pl.BlockSpec((tm, tn), lambda i, j: (i, j)),
                  pl.BlockSpec((D_MODEL, tn), lambda i, j: (0, j),
                               pipeline_mode=pl.Buffered(1 if tn == D_MODEL else 2))],
        out_specs=pl.BlockSpec((tm, tn), lambda i, j: (i, j)),
        out_shape=jax.ShapeDtypeStruct((M, D_MODEL), f32),
        scratch_shapes=[pltpu.VMEM((tm, D_MODEL), bf16)],
        compiler_params=_cparams(("parallel", "arbitrary")),
        name="mm_out",
    )(a, b, c, x, w)


def _mm_down_kernel(a_ref, x_ref, w_ref, o_ref):
    o_ref[...] = x_ref[...] + jnp.dot(a_ref[...], w_ref[...], preferred_element_type=f32)


def _mm_down(act, x, w, tm):
    M = x.shape[0]
    tn = MM_TILE_N[w.shape]
    return pl.pallas_call(
        _mm_down_kernel,
        grid=(M // tm, D_MODEL // tn),
        in_specs=[pl.BlockSpec((tm, D_FF), lambda i, j: (i, 0)),
                  pl.BlockSpec((tm, tn), lambda i, j: (i, j)),
                  pl.BlockSpec((D_FF, tn), lambda i, j: (0, j),
                               pipeline_mode=pl.Buffered(1 if tn == D_MODEL else 2))],
        out_specs=pl.BlockSpec((tm, tn), lambda i, j: (i, j)),
        out_shape=jax.ShapeDtypeStruct((M, D_MODEL), f32),
        compiler_params=_cparams(("parallel", "arbitrary")),
        name="mm_down",
    )(act, x, w)


def _silu(x):
    return x * jax.nn.sigmoid(x)


def _mm_up_prompt_kernel(x_ref, n2_ref, wg_ref, wa_ref, cwg_ref, cwa_ref, cbg_ref, cba_ref, src_ref,
                         act_ref, cs_ref, dst_ref, h_ref, carry_ref, zb_ref,
                         *, tm, nj, n_steps, tiles_per_seq):
    _cast_chunk(src_ref, dst_ref)
    s = pl.program_id(0)
    sa = jnp.minimum(s, n_steps - 1)
    ia = sa // nj
    ja = sa % nj
    tf = act_ref.shape[1]

    @pl.when(s == 0)
    def _():
        zb_ref[...] = jnp.zeros(zb_ref.shape, f32)
        carry_ref[...] = jnp.zeros(carry_ref.shape, f32)

    @pl.when(ja == 0)
    def _():
        h_ref[...] = _rms(x_ref[...], n2_ref[...]).astype(bf16)

    chunks = [slice(c0, c0 + UP_CHUNK) for c0 in range(0, tf, UP_CHUNK)]

    rblocks = [(r0, min(UP_ROWS, tm - r0)) for r0 in range(0, tm, UP_ROWS)]

    def conv(idx, cw_ref, cb_ref, cs, r0, nr):
        zz = zb_ref[idx, r0:r0 + SUBLANES + nr, cs]
        z1 = pltpu.roll(zz, 1, 0)[SUBLANES:]
        z2 = pltpu.roll(zz, 2, 0)[SUBLANES:]
        return (cb_ref[:, cs] + z2 * cw_ref[0:1, cs] + z1 * cw_ref[1:2, cs] + zz[SUBLANES:] * cw_ref[2:3, cs])

    for cs in chunks:
        for r0, nr in rblocks:
            g = conv(0, cwg_ref, cbg_ref, cs, r0, nr)
            a = conv(1, cwa_ref, cba_ref, cs, r0, nr)
            act_ref[r0:r0 + nr, cs] = (_silu(g) * a).astype(bf16)

    seq_start = (ia % tiles_per_seq) == 0
    for cs in chunks:
        for r0, nr in rblocks:
            for idx, w_ref in ((0, wg_ref), (1, wa_ref)):
                z = jnp.dot(h_ref[r0:r0 + nr, :], w_ref[:, cs], preferred_element_type=f32)
                if r0 == 0:
                    zb_ref[idx, 0:SUBLANES, cs] = jnp.where(seq_start, 0.0, carry_ref[idx, ja, :, cs])
                zb_ref[idx, SUBLANES + r0:SUBLANES + r0 + nr, cs] = z
                if r0 + nr == tm:
                    tail = z[nr - SUBLANES:nr, :]
                    carry_ref[idx, ja, :, cs] = tail
                    cs_ref[0, idx, :, cs] = tail


def _mm_up_prompt(x, norm2, w_up, conv_w, conv_b, layer, batch, tm, cast):
    M = x.shape[0]
    seq = M // batch
    tiles_per_seq = seq // tm
    tf = MM_TILE_N[w_up.shape]
    nj = D_FF // tf
    n_steps = (M // tm) * nj
    job = _cast_job(cast[0], cast[1], n_steps, lambda s: jnp.minimum(s, n_steps - 1))
    kern = functools.partial(_mm_up_prompt_kernel, tm=tm, nj=nj, n_steps=n_steps, tiles_per_seq=tiles_per_seq)
    ia = lambda s: jnp.minimum(s, n_steps - 1) // nj
    ja = lambda s: jnp.minimum(s, n_steps - 1) % nj
    ib = lambda s: jnp.maximum(s - 1, 0) // nj
    jb = lambda s: jnp.maximum(s - 1, 0) % nj
    return pl.pallas_call(
        kern,
        grid=(n_steps + 1,),
        in_specs=[pl.BlockSpec((tm, D_MODEL), lambda s: (ia(s), 0)),
                  pl.BlockSpec((None, 1, D_MODEL), lambda s: (layer, 0, 0)),
                  pl.BlockSpec((D_MODEL, tf), lambda s: (0, ja(s))),
                  pl.BlockSpec((D_MODEL, tf), lambda s: (0, nj + ja(s))),
                  pl.BlockSpec((None, CONV_W, tf), lambda s: (layer, 0, jb(s))),
                  pl.BlockSpec((None, CONV_W, tf), lambda s: (layer, 0, nj + jb(s))),
                  pl.BlockSpec((None, 1, tf), lambda s: (layer, 0, jb(s))),
                  pl.BlockSpec((None, 1, tf), lambda s: (layer, 0, nj + jb(s))),
                  job[0]],
        out_specs=(pl.BlockSpec((tm, tf), lambda s: (ib(s), jb(s))),
                   pl.BlockSpec((1, 2, SUBLANES, tf), lambda s: (ia(s), 0, 0, ja(s))),
                   job[1]),
        out_shape=(jax.ShapeDtypeStruct((M, D_FF), bf16),
                   jax.ShapeDtypeStruct((M // tm, 2, SUBLANES, D_FF), f32),
                   job[2]),
        scratch_shapes=[pltpu.VMEM((tm, D_MODEL), bf16),
                        pltpu.VMEM((2, nj, SUBLANES, tf), f32),
                        pltpu.VMEM((2, tm + SUBLANES, tf), f32)],
        compiler_params=_cparams(("arbitrary",)),
        name="mm_up_prompt",
    )(x, norm2, w_up, w_up, conv_w, conv_w, conv_b, conv_b, cast[0])


def _mm_up_sample_kernel(x_ref, n2_ref, wg_ref, wa_ref, cwg_ref, cwa_ref, cbg_ref, cba_ref,
                         b0g_ref, b1g_ref, b0a_ref, b1a_ref, act_ref, zg_ref, za_ref, h_ref):
    @pl.when(pl.program_id(0) == 0)
    def _():
        h_ref[...] = _rms(x_ref[...], n2_ref[...]).astype(bf16)

    def conv_half(w_ref, cw_ref, cb_ref, b0_ref, b1_ref, z_ref):
        z = jnp.dot(h_ref[...], w_ref[...], preferred_element_type=f32)
        z_ref[...] = z
        return (cb_ref[...] + b0_ref[...] * cw_ref[0:1, :] + b1_ref[...] * cw_ref[1:2, :]
                + z * cw_ref[2:3, :])

    g = conv_half(wg_ref, cwg_ref, cbg_ref, b0g_ref, b1g_ref, zg_ref)
    a = conv_half(wa_ref, cwa_ref, cba_ref, b0a_ref, b1a_ref, za_ref)
    act_ref[...] = (_silu(g) * a).astype(bf16)


def _mm_up_sample(x, norm2, w_up, conv_w, conv_b, buf, layer):
    M = x.shape[0]
    tf = UP_TILE_DECODE
    nj = D_FF // tf
    wspec = lambda off: pl.BlockSpec((D_MODEL, tf), lambda j: (0, off + j))
    cwspec = lambda off: pl.BlockSpec((None, CONV_W, tf), lambda j: (layer, 0, off + j))
    cbspec = lambda off: pl.BlockSpec((None, 1, tf), lambda j: (layer, 0, off + j))
    bufspec = lambda row, off: pl.BlockSpec((None, None, M, tf), lambda j: (layer, row, 0, off + j))
    return pl.pallas_call(
        _mm_up_sample_kernel,
        grid=(nj,),
        in_specs=[pl.BlockSpec((M, D_MODEL), lambda j: (0, 0)),
                  pl.BlockSpec((None, 1, D_MODEL), lambda j: (layer, 0, 0)),
                  wspec(0), wspec(nj), cwspec(0), cwspec(nj), cbspec(0), cbspec(nj),
                  bufspec(0, 0), bufspec(1, 0), bufspec(0, nj), bufspec(1, nj)],
        out_specs=(pl.BlockSpec((M, tf), lambda j: (0, j)),
                   pl.BlockSpec((M, tf), lambda j: (0, j)),
                   pl.BlockSpec((M, tf), lambda j: (0, j))),
        out_shape=(jax.ShapeDtypeStruct((M, D_FF), bf16),
                   jax.ShapeDtypeStruct((M, D_FF), f32),
                   jax.ShapeDtypeStruct((M, D_FF), f32)),
        scratch_shapes=[pltpu.VMEM((M, D_MODEL), bf16)],
        compiler_params=_cparams(("arbitrary",)),
        name="mm_up_sample",
    )(x, norm2, w_up, w_up, conv_w, conv_w, conv_b, conv_b, buf, buf, buf, buf)


def _mix_a_prompt_kernel(u_ref, v_ref, vg_ref, ws_ref, bs_ref, o_ref):
    row = lax.broadcasted_iota(jnp.int32, (CHUNK, CHUNK), 0)
    col = lax.broadcasted_iota(jnp.int32, (CHUNK, CHUNK), 1)
    tri = col <= row
    for g in range(A_GROUPS):
        sl = slice(g * A_DIM, (g + 1) * A_DIM)
        ws = jnp.where(tri, ws_ref[g], 0.0).astype(bf16)
        for r0 in range(0, u_ref.shape[0], CHUNK):
            rs = slice(r0, r0 + CHUNK)
            vn = _rms(_gelu(v_ref[rs, sl]), vg_ref[:, sl])
            mixv = jnp.dot(ws, vn.astype(bf16), preferred_element_type=f32) + bs_ref[:, sl]
            o_ref[rs, sl] = (_gelu(u_ref[rs, sl]) * mixv).astype(bf16)


def _mix_a_prompt(z, v_gain, w_s, b_s_full, layer, rows):
    M = z.shape[0]
    return pl.pallas_call(
        _mix_a_prompt_kernel,
        grid=(M // rows,),
        in_specs=[pl.BlockSpec((rows, A_WIDTH), lambda r: (r, Z_AU // A_WIDTH)),
                  pl.BlockSpec((rows, A_WIDTH), lambda r: (r, Z_AV // A_WIDTH)),
                  pl.BlockSpec((None, 1, A_WIDTH), lambda r: (layer, 0, 0)),
                  pl.BlockSpec((None, A_GROUPS, CHUNK, CHUNK), lambda r: (layer, 0, 0, 0)),
                  pl.BlockSpec((None, CHUNK, A_WIDTH), lambda r: (layer, 0, 0))],
        out_specs=pl.BlockSpec((rows, A_WIDTH), lambda r: (r, 0)),
        out_shape=jax.ShapeDtypeStruct((M, A_WIDTH), bf16),
        compiler_params=_cparams(("parallel",)),
        name="mix_a_prompt",
    )(z, z, v_gain, w_s, b_s_full)


def _mix_a_sample_kernel(u_ref, v_ref, vg_ref, ws_ref, bs_ref, o_ref, vn_ref):
    for g in range(A_GROUPS):
        sl = slice(g * A_DIM, (g + 1) * A_DIM)
        vn = _rms(_gelu(v_ref[:, sl]), vg_ref[:, sl])
        vn_ref[:, sl] = vn
        mixv = ws_ref[:, sl] * vn + bs_ref[:, sl]
        o_ref[:, sl] = (_gelu(u_ref[:, sl]) * mixv).astype(bf16)


def _mix_a_sample(z, v_gain, ws0, bs0, layer):
    M = z.shape[0]
    vec = pl.BlockSpec((None, 1, A_WIDTH), lambda r: (layer, 0, 0))
    return pl.pallas_call(
        _mix_a_sample_kernel,
        grid=(1,),
        in_specs=[pl.BlockSpec((M, A_WIDTH), lambda r: (0, Z_AU // A_WIDTH)),
                  pl.BlockSpec((M, A_WIDTH), lambda r: (0, Z_AV // A_WIDTH)),
                  vec, vec, vec],
        out_specs=(pl.BlockSpec((M, A_WIDTH), lambda r: (0, 0)),
                   pl.BlockSpec((M, A_WIDTH), lambda r: (0, 0))),
        out_shape=(jax.ShapeDtypeStruct((M, A_WIDTH), bf16),
                   jax.ShapeDtypeStruct((M, A_WIDTH), f32)),
        compiler_params=_cparams(("arbitrary",)),
        name="mix_a_sample",
    )(z, z, v_gain, ws0, bs0)


def _mix_b_prompt_kernel(sink_ref, q_ref, k_ref, v_ref, qg_ref, kg_ref, bias_ref, src_ref,
                         o_ref, klast_ref, vlast_ref, dst_ref, kcat_ref, vcat_ref):
    _cast_chunk(src_ref, dst_ref)
    i = pl.program_id(1)
    W = WINDOW
    kn = _norm_head_pairs(k_ref[...], kg_ref[...])
    v = v_ref[...]
    klast_ref[0] = kn
    vlast_ref[0] = v

    @pl.when(i == 0)
    def _():
        kcat_ref[0:W, :] = jnp.zeros((W, LANES), f32)
        vcat_ref[0:W, :] = jnp.zeros((W, LANES), f32)

    @pl.when(i > 0)
    def _():
        kcat_ref[0:W, :] = kcat_ref[W:2 * W, :]
        vcat_ref[0:W, :] = vcat_ref[W:2 * W, :]

    kcat_ref[W:2 * W, :] = kn
    vcat_ref[W:2 * W, :] = v
    kc = kcat_ref[...]
    vc = vcat_ref[...]
    kc_sw = pltpu.roll(kc, B_HEAD_DIM, 1)
    vc_sw = pltpu.roll(vc, B_HEAD_DIM, 1)
    lo = lax.broadcasted_iota(jnp.int32, (2 * W, LANES), 1) < B_HEAD_DIM

    cj = lax.broadcasted_iota(jnp.int32, (W, W), 0)
    qi = lax.broadcasted_iota(jnp.int32, (W, W), 1)
    own = cj <= qi
    keep = own | (i > 0)
    q_gain = qg_ref[...] * (B_HEAD_DIM ** -0.5)

    tiles = GQA // 2
    nt = (((1,), (1,)), ((), ()))
    tn = (((0,), (0,)), ((), ()))
    scores, values = [], []
    for kvh in range(B_KV_HEADS):
        k_src, k_alt = (kc, kc_sw) if kvh == 0 else (kc_sw, kc)
        v_src, v_alt = (vc, vc_sw) if kvh == 0 else (vc_sw, vc)
        k_even = jnp.where(lo, k_src, 0.0).astype(bf16)
        k_odd = jnp.where(lo, 0.0, k_alt).astype(bf16)
        values.append((jnp.where(lo, v_src, 0.0).astype(bf16), jnp.where(lo, 0.0, v_alt).astype(bf16)))
        qts = []
        for t in range(tiles):
            c0 = (kvh * tiles + t) * LANES
            qt = q_ref[:, c0:c0 + LANES].T
            q2 = qt * qt
            halves = []
            for r0 in (0, B_HEAD_DIM):
                ms = jnp.mean(q2[r0:r0 + B_HEAD_DIM, :], axis=0, keepdims=True)
                halves.append(qt[r0:r0 + B_HEAD_DIM, :] * lax.rsqrt(ms + EPS))
            qts.append(jnp.concatenate(halves, axis=0) * q_gain)
        qst = jnp.concatenate(qts, axis=1).astype(bf16)
        scores.append((jnp.dot(k_even, qst, preferred_element_type=f32),
                       jnp.dot(k_odd, qst, preferred_element_type=f32)))
    for kvh in range(B_KV_HEADS):
        p_par = []
        for par in range(2):
            blocks = []
            for t in range(tiles):
                h = kvh * GQA + 2 * t + par
                sb = scores[kvh][par][:, t * W:(t + 1) * W]
                s = jnp.where(own, sb[W:2 * W, :], sb[0:W, :]) + bias_ref[h]
                s = jnp.where(keep, s, NEG_INF)
                sk = sink_ref[h]
                mx = jnp.maximum(jnp.max(s, axis=0, keepdims=True), sk)
                p = jnp.exp(s - mx)
                den = jnp.sum(p, axis=0, keepdims=True) + jnp.exp(sk - mx)
                p = p * (1.0 / den)
                blocks.append(jnp.concatenate([jnp.where(own, 0.0, p), jnp.where(own, p, 0.0)],
                                              axis=0).astype(bf16))
            p_par.append(jnp.concatenate(blocks, axis=1))
        ot = (lax.dot_general(values[kvh][0], p_par[0], tn, preferred_element_type=f32)
              + lax.dot_general(values[kvh][1], p_par[1], tn, preferred_element_type=f32))
        for t in range(tiles):
            c0 = (kvh * tiles + t) * LANES
            o_ref[:, c0:c0 + LANES] = ot[:, t * W:(t + 1) * W].T.astype(bf16)


def _mix_b_prompt(z, sinks, q_gain_t, k_gain2, bias_p, layer, batch, cast):
    M = z.shape[0]
    nb = M // batch // WINDOW
    job = _cast_job(cast[0], cast[1], batch * nb, lambda b, i: b * nb + i)
    return pl.pallas_call(
        _mix_b_prompt_kernel,
        grid=(batch, nb),
        in_specs=[pl.BlockSpec(memory_space=pltpu.SMEM),
                  pl.BlockSpec((WINDOW, B_WIDTH), lambda b, i: (b * nb + i, Z_BQ // B_WIDTH)),
                  pl.BlockSpec((WINDOW, B_KV_WIDTH), lambda b, i: (b * nb + i, Z_BK // B_KV_WIDTH)),
                  pl.BlockSpec((WINDOW, B_KV_WIDTH), lambda b, i: (b * nb + i, Z_BV // B_KV_WIDTH)),
                  pl.BlockSpec((None, LANES, WINDOW), lambda b, i: (layer, 0, 0)),
                  pl.BlockSpec((None, 1, LANES), lambda b, i: (layer, 0, 0)),
                  pl.BlockSpec((B_HEADS, WINDOW, WINDOW), lambda b, i: (0, 0, 0)),
                  job[0]],
        out_specs=(pl.BlockSpec((WINDOW, B_WIDTH), lambda b, i: (b * nb + i, 0)),
                   pl.BlockSpec((1, WINDOW, B_KV_WIDTH), lambda b, i: (b, 0, 0)),
                   pl.BlockSpec((1, WINDOW, B_KV_WIDTH), lambda b, i: (b, 0, 0)),
                   job[1]),
        out_shape=(jax.ShapeDtypeStruct((M, B_WIDTH), bf16),
                   jax.ShapeDtypeStruct((batch, WINDOW, B_KV_WIDTH), f32),
                   jax.ShapeDtypeStruct((batch, WINDOW, B_KV_WIDTH), f32),
                   job[2]),
        scratch_shapes=[pltpu.VMEM((2 * WINDOW, LANES), f32),
                        pltpu.VMEM((2 * WINDOW, LANES), f32)],
        compiler_params=_cparams(("arbitrary", "arbitrary")),
        name="mix_b_prompt",
    )(sinks, z, z, z, q_gain_t, k_gain2, bias_p, cast[0])


def _mix_c_prompt_kernel(q_ref, k_ref, v0_ref, v1_ref, o0_ref, o1_ref, g_ref, gb_ref, hgt_ref,
                         out_ref, c_out, n_out, m_out, c_s, n_s, m_s):
    L = MLSTM_L
    c = pl.program_id(1)

    @pl.when(c == 0)
    def _():
        c_s[...] = jnp.zeros(c_s.shape, f32)
        n_s[...] = jnp.zeros(n_s.shape, f32)
        m_s[...] = jnp.zeros(m_s.shape, f32)

    row = lax.broadcasted_iota(jnp.int32, (L, L), 0)
    col = lax.broadcasted_iota(jnp.int32, (L, L), 1)
    tri = col <= row
    tri_f = tri.astype(f32)
    seqs = range(q_ref.shape[0])
    units = [(bi, h) for bi in seqs for h in range(C_HEADS)]
    g, b = {}, {}
    for bi in seqs:
        g[bi] = g_ref[bi] + gb_ref[...]
        b[bi] = jnp.dot(tri_f, jax.nn.log_sigmoid(g[bi]), precision=lax.Precision.HIGHEST,
                        preferred_element_type=f32)
    bT = {bi: b[bi].T for bi in seqs}
    vis = row <= col
    nt = (((1,), (1,)), ((), ()))
    tn = (((0,), (0,)), ((), ()))
    qb, kh, vb, ct_prev, n_prev, st, qct, qn_lin = {}, {}, {}, {}, {}, {}, {}, {}
    for u in units:
        bi, h = u
        v_ref = v0_ref if h < 2 else v1_ref
        vsl = slice((h % 2) * C_V_DIM, (h % 2 + 1) * C_V_DIM)
        qb[u] = q_ref[bi, :, h * C_QK_DIM:(h + 1) * C_QK_DIM].astype(bf16)
        kh[u] = k_ref[bi, :, h * C_QK_DIM:(h + 1) * C_QK_DIM] * (C_QK_DIM ** -0.5)
        vb[u] = v_ref[bi, :, vsl].astype(bf16)
        ct_prev[u] = c_s[bi, h]
        n_prev[u] = n_s[bi, h]
        st[u] = lax.dot_general(kh[u].astype(bf16), qb[u], nt, preferred_element_type=f32)
        qct[u] = lax.dot_general(ct_prev[u].astype(bf16), qb[u], nt, preferred_element_type=f32)
        n8 = jnp.broadcast_to(n_prev[u], (SUBLANES, C_QK_DIM)).astype(bf16)
        qn_lin[u] = lax.dot_general(n8, qb[u], nt, preferred_element_type=f32)[0:1, :]
    pt, iw, inv, kw, decay, m_new = {}, {}, {}, {}, {}, {}
    for u in units:
        bi, h = u
        b_row = bT[bi][C_HEADS + h:C_HEADS + h + 1, :]
        src = g[bi][:, h:h + 1] - b[bi][:, C_HEADS + h:C_HEADS + h + 1]
        dlog = jnp.where(vis, b_row + src, -jnp.inf)
        m_prev = m_s[bi, h][0:1, 0:1]
        inter = b_row + m_prev
        mt = jnp.maximum(inter, jnp.max(dlog, axis=0, keepdims=True))
        dt = jnp.exp(dlog - mt)
        pt[u] = st[u] * dt
        iw[u] = jnp.exp(inter - mt)
        qn = iw[u] * qn_lin[u] + jnp.sum(pt[u], axis=0, keepdims=True)
        inv[u] = 1.0 / jnp.maximum(jnp.abs(qn), jnp.exp(-mt))
        m_new[u] = mt[:, L - 1:L]
        decay[u] = jnp.exp(b_row[:, L - 1:L] + m_prev - m_new[u])
        kw[u] = kh[u] * dt[:, L - 1:L]
    svt = {u: lax.dot_general(vb[u], pt[u].astype(bf16), tn, preferred_element_type=f32) for u in units}
    kvt = {u: lax.dot_general(vb[u], kw[u].astype(bf16), tn, preferred_element_type=f32) for u in units}
    for u in units:
        bi, h = u
        o_ref = o0_ref if h < 2 else o1_ref
        vsl = slice((h % 2) * C_V_DIM, (h % 2 + 1) * C_V_DIM)
        ht = (iw[u] * qct[u] + svt[u]) * inv[u]
        ms = jnp.mean(ht * ht, axis=0, keepdims=True)
        hn = (ht * lax.rsqrt(ms + EPS) * hgt_ref[h]).T
        c_s[bi, h] = decay[u] * ct_prev[u] + kvt[u]
        n_s[bi, h] = decay[u] * n_prev[u] + jnp.sum(kw[u], axis=0, keepdims=True)
        m_s[bi, h] = jnp.broadcast_to(m_new[u], (SUBLANES, LANES))
        osl = slice(h * C_V_DIM, (h + 1) * C_V_DIM)
        out_ref[bi, :, osl] = (hn * jax.nn.sigmoid(o_ref[bi, :, vsl])).astype(bf16)

    @pl.when(c == pl.num_programs(1) - 1)
    def _():
        eye = (row == col).astype(f32)
        for bi, h in units:
            c_out[bi, h] = lax.dot_general(c_s[bi, h], eye, tn, precision=lax.Precision.HIGHEST,
                                           preferred_element_type=f32)
        n_out[...] = n_s[...]
        m_out[...] = m_s[...]


def _mix_c_prompt(z, gate_bias, h_gain, layer, batch):
    M = z.shape[0]
    L = MLSTM_L
    T = M // batch
    z3 = z.reshape(batch, T, Z_WIDTH)
    ns = C_SEQS if batch % C_SEQS == 0 else 1
    col = lambda width, off: pl.BlockSpec((ns, L, width), lambda b, c: (b, c, off // width))
    state = lambda *shape: pl.BlockSpec((ns,) + shape, lambda b, c: (b,) + (0,) * len(shape))
    out, c_st, n_st, m_st = pl.pallas_call(
        _mix_c_prompt_kernel,
        grid=(batch // ns, T // L),
        in_specs=[col(C_QK_WIDTH, Z_CQ), col(C_QK_WIDTH, Z_CK),
                  col(C_PAIR, Z_CV), col(C_PAIR, Z_CV + C_PAIR),
                  col(C_PAIR, Z_CO), col(C_PAIR, Z_CO + C_PAIR),
                  col(LANES, Z_GATE),
                  pl.BlockSpec((None, 1, LANES), lambda b, c: (layer, 0, 0)),
                  pl.BlockSpec((None, C_HEADS, C_V_DIM, L), lambda b, c: (layer, 0, 0, 0))],
        out_specs=(pl.BlockSpec((ns, L, C_WIDTH), lambda b, c: (b, c, 0)),
                   state(C_HEADS, C_QK_DIM, C_V_DIM),
                   state(C_HEADS, 1, C_QK_DIM),
                   state(C_HEADS, SUBLANES, LANES)),
        out_shape=(jax.ShapeDtypeStruct((batch, T, C_WIDTH), bf16),
                   jax.ShapeDtypeStruct((batch, C_HEADS, C_QK_DIM, C_V_DIM), f32),
                   jax.ShapeDtypeStruct((batch, C_HEADS, 1, C_QK_DIM), f32),
                   jax.ShapeDtypeStruct((batch, C_HEADS, SUBLANES, LANES), f32)),
        scratch_shapes=[pltpu.VMEM((ns, C_HEADS, C_V_DIM, C_QK_DIM), f32),
                        pltpu.VMEM((ns, C_HEADS, 1, C_QK_DIM), f32),
                        pltpu.VMEM((ns, C_HEADS, SUBLANES, LANES), f32)],
        compiler_params=_cparams(("arbitrary", "arbitrary")),
        name="mix_c_prompt",
    )(z3, z3, z3, z3, z3, z3, z3, gate_bias, h_gain)
    return out.reshape(M, C_WIDTH), c_st, n_st, m_st


_BC_SHARED = (4, 5, 6, 7, 12, 13)


def _mix_bc_sample_kernel(*refs):
    rows = [_mix_bc_sample_row(*[ref if k in _BC_SHARED else ref.at[pl.ds(r, 1)] for k, ref in enumerate(refs)])
            for r in range(refs[0].shape[0])]
    while rows:
        rows = [row for row in rows if next(row, True) is None]


def _mix_bc_sample_row(zr_ref, q_ref, kp_ref, vp_ref, qg_ref, kg_ref, sink_ref, bias_ref,
                       cq_ref, ck_ref, cv_ref, co_ref, gb_ref, hg_ref, c0_ref, n0_ref, m0_ref,
                       bo_ref, kn_ref, co_out, c_out, n_out, m_out):
    W = WINDOW
    zr = zr_ref[0]

    qn = _rms(q_ref[0], qg_ref[...])
    hrow = lax.broadcasted_iota(jnp.int32, (B_HEADS, LANES), 0)
    lane = lax.broadcasted_iota(jnp.int32, (B_HEADS, LANES), 1)
    own = (hrow < GQA) == (lane < B_HEAD_DIM)
    q2 = jnp.where(own, jnp.concatenate([qn, qn], axis=-1), 0.0)
    k_new = _norm_head_pairs(zr[:, Z_BK:Z_BK + B_KV_WIDTH], kg_ref[...])
    v_new = zr[:, Z_BV:Z_BV + B_KV_WIDTH]
    kn_ref[0] = k_new
    scale = B_HEAD_DIM ** -0.5
    s_past = lax.dot_general(q2.astype(bf16), kp_ref[0].astype(bf16), (((1,), (1,)), ((), ())),
                             preferred_element_type=f32) * scale + bias_ref[:, 0:W]
    yield
    s_past = jnp.where(lane >= 1, s_past, NEG_INF)
    s_new = jnp.sum(q2 * k_new, axis=-1, keepdims=True) * scale + bias_ref[:, W:W + 1]
    sk = sink_ref[...]
    mx = jnp.maximum(jnp.maximum(jnp.max(s_past, axis=-1, keepdims=True), s_new), sk)
    p_past = jnp.exp(s_past - mx)
    p_new = jnp.exp(s_new - mx)
    den = jnp.sum(p_past, axis=-1, keepdims=True) + p_new + jnp.exp(sk - mx)
    inv = 1.0 / den
    o2 = (jnp.dot((p_past * inv).astype(bf16), vp_ref[0].astype(bf16), preferred_element_type=f32)
          + (p_new * inv) * v_new)
    o2_sw = pltpu.roll(o2, B_HEAD_DIM, 1)
    bo_ref[0] = jnp.where(hrow < GQA, o2, o2_sw)[:, 0:B_HEAD_DIM].astype(bf16)

    g = zr[:, Z_GATE:Z_GATE + LANES] + gb_ref[...]
    lf = jax.nn.log_sigmoid(g)
    hrow4 = lax.broadcasted_iota(jnp.int32, (C_HEADS, LANES), 0)
    lane4 = lax.broadcasted_iota(jnp.int32, (C_HEADS, LANES), 1)

    def head_column(row_vec, first_lane):
        spread = jnp.broadcast_to(row_vec, (C_HEADS, LANES))
        return jnp.sum(jnp.where(lane4 == hrow4 + first_lane, spread, 0.0), axis=-1, keepdims=True)

    ig = head_column(g, 0)
    b = head_column(lf, C_HEADS)
    m0 = m0_ref[0]
    m_prev = jnp.sum(jnp.where(lane4[:, 0:C_HEADS] == hrow4[:, 0:C_HEADS],
                               jnp.broadcast_to(m0, (C_HEADS, C_HEADS)), 0.0), axis=-1, keepdims=True)
    inter = b + m_prev
    mt = jnp.maximum(inter, ig)
    q4 = cq_ref[0]
    k4 = ck_ref[0] * (C_QK_DIM ** -0.5)
    v4 = cv_ref[0]
    n4 = n0_ref[0]
    c0 = c0_ref[0]
    s = jnp.sum(q4 * k4, axis=-1, keepdims=True) * jnp.exp(ig - mt)
    iw = jnp.exp(inter - mt)
    blk_row = lax.broadcasted_iota(jnp.int32, (C_HEADS, C_QK_WIDTH), 0)
    blk_lane = lax.broadcasted_iota(jnp.int32, (C_HEADS, C_QK_WIDTH), 1) // C_QK_DIM
    pad4 = jnp.zeros((SUBLANES - C_HEADS, C_QK_WIDTH), f32)

    def head_blocks(x4):
        tiled = jnp.concatenate([x4] * C_HEADS, axis=-1)
        return jnp.concatenate([jnp.where(blk_lane == blk_row, tiled, 0.0), pad4], axis=0).astype(bf16)

    c0s = c0.reshape(C_QK_WIDTH, C_V_DIM)
    qc = jnp.dot(head_blocks(q4), c0s.astype(bf16), preferred_element_type=f32)[0:C_HEADS, :]
    yield
    num = iw * qc + s * v4
    qn_ = iw * jnp.sum(q4 * n4, axis=-1, keepdims=True) + s
    hh = num / jnp.maximum(jnp.abs(qn_), jnp.exp(-mt))
    decay = iw
    kw = k4 * jnp.exp(ig - mt)
    v8 = jnp.concatenate([v4, jnp.zeros((SUBLANES - C_HEADS, C_V_DIM), f32)], axis=0).astype(bf16)
    outer = lax.dot_general(head_blocks(kw), v8, (((0,), (0,)), ((), ())),
                            preferred_element_type=f32)
    for h in range(C_HEADS):
        c_out[0, h] = decay[h:h + 1, :] * c0[h] + outer[h * C_QK_DIM:(h + 1) * C_QK_DIM, :]
    n_out[0] = decay * n4 + kw
    m_out[0] = jnp.sum(jnp.where(lane4 == hrow4, jnp.broadcast_to(mt, (C_HEADS, LANES)), 0.0),
                       axis=0, keepdims=True)
    co_out[0] = (_rms(hh, hg_ref[...]) * jax.nn.sigmoid(co_ref[0])).astype(bf16)


def _mix_bc_sample(z, k_past, v_past, q_gain, k_gain2, sinks, bias_s, gate_bias, h_gain4,
                   c0, n0, m0, layer):
    R = z.shape[0]
    zr = z.reshape(R, 1, Z_WIDTH)
    q = z[:, Z_BQ:Z_BQ + B_WIDTH].reshape(R, B_HEADS, B_HEAD_DIM)
    cq = z[:, Z_CQ:Z_CQ + C_QK_WIDTH].reshape(R, C_HEADS, C_QK_DIM)
    ck = z[:, Z_CK:Z_CK + C_QK_WIDTH].reshape(R, C_HEADS, C_QK_DIM)
    cv = z[:, Z_CV:Z_CV + C_WIDTH].reshape(R, C_HEADS, C_V_DIM)
    co = z[:, Z_CO:Z_CO + C_WIDTH].reshape(R, C_HEADS, C_V_DIM)
    rb = DEC_ROWS if R % DEC_ROWS == 0 else 1
    row3 = lambda n: pl.BlockSpec((rb, 1, n), lambda r: (r, 0, 0))
    return pl.pallas_call(
        _mix_bc_sample_kernel,
        grid=(R // rb,),
        in_specs=[row3(Z_WIDTH),
                  pl.BlockSpec((rb, B_HEADS, B_HEAD_DIM), lambda r: (r, 0, 0)),
                  pl.BlockSpec((None, rb, WINDOW, B_KV_WIDTH), lambda r: (layer, r, 0, 0)),
                  pl.BlockSpec((None, rb, WINDOW, B_KV_WIDTH), lambda r: (layer, r, 0, 0)),
                  pl.BlockSpec((None, 1, B_HEAD_DIM), lambda r: (layer, 0, 0)),
                  pl.BlockSpec((None, 1, LANES), lambda r: (layer, 0, 0)),
                  pl.BlockSpec((None, B_HEADS, 1), lambda r: (layer, 0, 0)),
                  pl.BlockSpec((B_HEADS, 2 * WINDOW), lambda r: (0, 0)),
                  pl.BlockSpec((rb, C_HEADS, C_QK_DIM), lambda r: (r, 0, 0)),
                  pl.BlockSpec((rb, C_HEADS, C_QK_DIM), lambda r: (r, 0, 0)),
                  pl.BlockSpec((rb, C_HEADS, C_V_DIM), lambda r: (r, 0, 0)),
                  pl.BlockSpec((rb, C_HEADS, C_V_DIM), lambda r: (r, 0, 0)),
                  pl.BlockSpec((None, 1, LANES), lambda r: (layer, 0, 0)),
                  pl.BlockSpec((None, C_HEADS, C_V_DIM), lambda r: (layer, 0, 0)),
                  pl.BlockSpec((None, rb, C_HEADS, C_QK_DIM, C_V_DIM), lambda r: (layer, r, 0, 0, 0)),
                  pl.BlockSpec((None, rb, C_HEADS, C_QK_DIM), lambda r: (layer, r, 0, 0)),
                  pl.BlockSpec((None, rb, 1, C_HEADS), lambda r: (layer, r, 0, 0))],
        out_specs=(pl.BlockSpec((rb, B_HEADS, B_HEAD_DIM), lambda r: (r, 0, 0)),
                   row3(B_KV_WIDTH),
                   pl.BlockSpec((rb, C_HEADS, C_V_DIM), lambda r: (r, 0, 0)),
                   pl.BlockSpec((rb, C_HEADS, C_QK_DIM, C_V_DIM), lambda r: (r, 0, 0, 0)),
                   pl.BlockSpec((rb, C_HEADS, C_QK_DIM), lambda r: (r, 0, 0)),
                   row3(LANES)),
        out_shape=(jax.ShapeDtypeStruct((R, B_HEADS, B_HEAD_DIM), bf16),
                   jax.ShapeDtypeStruct((R, 1, B_KV_WIDTH), f32),
                   jax.ShapeDtypeStruct((R, C_HEADS, C_V_DIM), bf16),
                   jax.ShapeDtypeStruct((R, C_HEADS, C_QK_DIM, C_V_DIM), f32),
                   jax.ShapeDtypeStruct((R, C_HEADS, C_QK_DIM), f32),
                   jax.ShapeDtypeStruct((R, 1, LANES), f32)),
        compiler_params=_cparams(("parallel",)),
        name="mix_bc_sample",
    )(zr, q, k_past, v_past, q_gain, k_gain2, sinks, bias_s, cq, ck, cv, co, gate_bias, h_gain4,
      c0, n0, m0)


def _prep_w_in(w_in):
    pad = jnp.zeros(w_in.shape[:-1] + (Z_WIDTH - w_in.shape[-1],), w_in.dtype)
    return jnp.concatenate([w_in, pad], axis=-1).astype(bf16)


def _row_tile(m, pref):
    return pref if m % pref == 0 else m


def kernel(x_prompt, x_sample, cache_swa_k, cache_swa_v, state_mlstm_C, state_mlstm_n, state_mlstm_m, state_ffn_conv, rel_bias, norm1, w_in, a_v_gain, a_spatial_w, a_spatial_b, b_q_gain, b_k_gain, b_sinks, c_gate_bias, c_h_gain, w_out, norm2, w_up, ffn_conv_w, ffn_conv_b, w_down):
    depth = w_in.shape[0]
    Bp, T, _ = x_prompt.shape
    R = x_sample.shape[0]
    assert x_sample.shape[1] == 1 and T % CHUNK == 0 and T % MLSTM_L == 0

    w_in_p = _prep_w_in(w_in)
    norm1_3 = norm1.reshape(depth, 1, D_MODEL)
    norm2_3 = norm2.reshape(depth, 1, D_MODEL)
    v_gain3 = a_v_gain.reshape(depth, 1, A_WIDTH)
    bs_full = jnp.repeat(jnp.swapaxes(a_spatial_b, 1, 2), A_DIM, axis=-1)
    ws0 = jnp.repeat(a_spatial_w[:, :, 0, 0], A_DIM, axis=-1).reshape(depth, 1, A_WIDTH)
    bs0 = jnp.repeat(a_spatial_b[:, :, 0], A_DIM, axis=-1).reshape(depth, 1, A_WIDTH)
    q_gain3 = b_q_gain.reshape(depth, 1, B_HEAD_DIM)
    q_gain_t = jnp.broadcast_to(jnp.tile(b_q_gain, (1, 2)).reshape(depth, LANES, 1), (depth, LANES, WINDOW))
    k_gain2 = jnp.tile(b_k_gain, (1, 2)).reshape(depth, 1, LANES)
    sinks3 = b_sinks.reshape(depth, B_HEADS, 1)
    gate_b = jnp.pad(c_gate_bias, ((0, 0), (0, LANES - 2 * C_HEADS))).reshape(depth, 1, LANES)
    h_gain_t = jnp.broadcast_to(c_h_gain.reshape(depth, C_HEADS, C_V_DIM, 1), (depth, C_HEADS, C_V_DIM, MLSTM_L))
    h_gain4 = c_h_gain.reshape(depth, C_HEADS, C_V_DIM)
    conv_b3 = ffn_conv_b.reshape(depth, 1, 2 * D_FF)
    k_cache = cache_swa_k.reshape(depth, R, WINDOW, B_KV_WIDTH)
    v_cache = cache_swa_v.reshape(depth, R, WINDOW, B_KV_WIDTH)
    m_state = state_mlstm_m.reshape(depth, R, 1, C_HEADS)
    conv_hist = jnp.swapaxes(state_ffn_conv, 1, 2)

    bias_p, bias_s = _bias_tables(rel_bias)

    Mp = Bp * T
    tm = _row_tile(T, 1024)
    tm_io = _row_tile(T, 512)
    tm_down = _row_tile(T, 256)
    a_rows = _row_tile(T, 1024)
    xp = x_prompt.reshape(Mp, D_MODEL)
    xs = x_sample.reshape(R, D_MODEL)
    P = [[] for _ in range(6)]
    S = [[] for _ in range(7)]
    for l in range(depth):
        z, w_out_l = _mm_in(xp, norm1_3, w_in_p, l, tm_io, cast=(w_out, l))
        a_o = _mix_a_prompt(z, v_gain3, a_spatial_w, bs_full, l, a_rows)
        b_o, k_last, v_last, w_up_l = _mix_b_prompt(z, b_sinks[l], q_gain_t, k_gain2, bias_p, l, Bp, (w_up, l))
        c_o, c_st, n_st, m_st = _mix_c_prompt(z, gate_b, h_gain_t, l, Bp)
        x1 = _mm_out(a_o, b_o, c_o, xp, w_out_l, tm_io)
        act, cs, w_down_l = _mm_up_prompt(x1, norm2_3, w_up_l, ffn_conv_w, conv_b3, l, Bp, tm, (w_down, l))
        xp = _mm_down(act, x1, w_down_l, tm_down)
        P[0].append(k_last.reshape(Bp, WINDOW, B_KV_HEADS, B_HEAD_DIM))
        P[1].append(v_last.reshape(Bp, WINDOW, B_KV_HEADS, B_HEAD_DIM))
        P[2].append(c_st)
        P[3].append(n_st.reshape(Bp, C_HEADS, C_QK_DIM))
        P[4].append(m_st[:, :, 0, 0])
        seq_tiles = T // tm
        tail = cs[seq_tiles - 1::seq_tiles, :, SUBLANES - (CONV_W - 1):, :]
        P[5].append(jnp.swapaxes(tail, 1, 2).reshape(Bp, CONV_W - 1, 2 * D_FF))

        zs = _mm_in(xs, norm1_3, w_in_p, l, R)
        a_s, vn_s = _mix_a_sample(zs, v_gain3, ws0, bs0, l)
        b_s, kn_s, c_s, c_new, n_new, m_new = _mix_bc_sample(
            zs, k_cache, v_cache, q_gain3, k_gain2, sinks3, bias_s, gate_b, h_gain4,
            state_mlstm_C, state_mlstm_n, m_state, l)
        x1s = _mm_out(a_s, b_s.reshape(R, B_WIDTH), c_s.reshape(R, C_WIDTH), xs, w_out_l, R)
        act_s, zg_s, za_s = _mm_up_sample(x1s, norm2_3, w_up_l, ffn_conv_w, conv_b3, conv_hist, l)
        xs = _mm_down(act_s, x1s, w_down_l, R)
        S[0].append(vn_s.reshape(R, 1, A_WIDTH))
        S[1].append(kn_s.reshape(R, 1, B_KV_HEADS, B_HEAD_DIM))
        S[2].append(zs[:, Z_BV:Z_BV + B_KV_WIDTH].reshape(R, 1, B_KV_HEADS, B_HEAD_DIM))
        S[3].append(c_new)
        S[4].append(n_new)
        S[5].append(m_new[:, 0, 0:C_HEADS])
        z_new = jnp.concatenate([zg_s, za_s], axis=-1)
        S[6].append(jnp.stack([state_ffn_conv[l][:, CONV_W - 2], z_new], axis=1))

    st = lambda lst: jnp.stack(lst, axis=0)
    return (xp.reshape(Bp, T, D_MODEL), xs.reshape(R, 1, D_MODEL),
            st(P[0]), st(P[1]), st(S[1]), st(S[2]),
            st(P[2]), st(P[3]), st(P[4]),
            st(S[3]), st(S[4]), st(S[5]),
            st(P[5]), st(S[6]),
            st(S[0]))
```

```python
import functools
import math

import numpy as np
import jax
import jax.numpy as jnp
from jax import lax
from jax.experimental import pallas as pl
from jax.experimental.pallas import tpu as pltpu

f32 = jnp.float32
bf16 = jnp.bfloat16

D_MODEL = 2048
EPS = 1e-6
NEG_INF = -1e30
SQRT_HALF = 0.7071067811865476

A_GROUPS = 4
A_DIM = 128
A_WIDTH = 512
CHUNK = 128
B_HEADS = 16
B_KV_HEADS = 2
B_HEAD_DIM = 64
GQA = 8
B_WIDTH = 1024
B_KV_WIDTH = 128
WINDOW = 128
N_BUCKETS = 32
MAX_DISTANCE = 128
C_HEADS = 4
C_QK_DIM = 64
C_V_DIM = 128
C_QK_WIDTH = 256
C_WIDTH = 512
IN_SPLITS = (A_WIDTH, A_WIDTH, B_WIDTH, B_KV_WIDTH, B_KV_WIDTH,
             C_QK_WIDTH, C_QK_WIDTH, C_WIDTH, C_WIDTH, C_HEADS, C_HEADS)
IN_OFFSETS = tuple(int(o) for o in np.cumsum(IN_SPLITS)[:-1])
D_FF = 5632
CONV_W = 3

Z_WIDTH = 4096
Z_AU, Z_AV, Z_BQ, Z_BK, Z_BV, Z_CQ, Z_CK, Z_CV, Z_CO, Z_GATE = (0,) + IN_OFFSETS[:9]
C_PAIR = 2 * C_V_DIM

MLSTM_L = 128
UP_CHUNK = 256
UP_ROWS = 128
UP_TILE_DECODE = 1408
C_SEQS = 2
DEC_ROWS = 8
MM_TILE_N = {(D_MODEL, Z_WIDTH): Z_WIDTH, (D_MODEL, D_MODEL): D_MODEL, (D_MODEL, 2 * D_FF): 512,
             (D_FF, D_MODEL): D_MODEL}
LANES = 128
SUBLANES = 8
VMEM_LIMIT = 52 * 1024 * 1024


def _cparams(sem, flags=None):
    return pltpu.CompilerParams(dimension_semantics=sem, vmem_limit_bytes=VMEM_LIMIT, flags=flags)


def _gelu(x):
    return 0.5 * x * (1.0 + lax.erf(x * SQRT_HALF))


def _rms(x, gain):
    return x * lax.rsqrt(jnp.mean(x * x, axis=-1, keepdims=True) + EPS) * gain


def _norm_head_pairs(x, gain2):
    lo = lax.broadcasted_iota(jnp.int32, x.shape, 1) < B_HEAD_DIM
    x2 = x * x
    s_lo = jnp.sum(jnp.where(lo, x2, 0.0), axis=-1, keepdims=True)
    s_hi = jnp.sum(jnp.where(lo, 0.0, x2), axis=-1, keepdims=True)
    ms = jnp.where(lo, s_lo, s_hi) * (1.0 / B_HEAD_DIM)
    return x * lax.rsqrt(ms + EPS) * gain2


def _t5_bucket_np(dist):
    n = np.maximum(dist, 0)
    max_exact = N_BUCKETS // 2
    nf = np.maximum(n, 1).astype(np.float32)
    large = max_exact + (np.log(nf / np.float32(max_exact)) / np.float32(math.log(MAX_DISTANCE / max_exact))
                         * np.float32(N_BUCKETS - max_exact)).astype(np.int32)
    return np.where(n < max_exact, n, np.minimum(large, N_BUCKETS - 1)).astype(np.int32)


def _bias_kernel(rb_ref, bkp_ref, bks_ref, op_ref, os_ref):
    bkp = bkp_ref[...]
    bks = bks_ref[...]
    for h in range(B_HEADS):
        accp = jnp.zeros(bkp.shape, f32)
        accs = jnp.zeros(bks.shape, f32)
        for b in range(N_BUCKETS):
            val = rb_ref[b, h]
            accp = jnp.where(bkp == b, val, accp)
            accs = jnp.where(bks == b, val, accs)
        op_ref[h] = accp
        os_ref[h:h + 1, :] = accs[0:1, :]


def _bias_tables(rel_bias):
    cj = np.arange(WINDOW)[:, None]
    qi = np.arange(WINDOW)[None, :]
    bkp = _t5_bucket_np((qi - cj) % WINDOW)
    j = np.arange(2 * WINDOW)
    dist_s = np.where(j < WINDOW, WINDOW - j, 0)
    bks = np.broadcast_to(_t5_bucket_np(dist_s)[None, :], (SUBLANES, 2 * WINDOW)).copy()
    return pl.pallas_call(
        _bias_kernel,
        out_shape=(jax.ShapeDtypeStruct((B_HEADS, WINDOW, WINDOW), f32),
                   jax.ShapeDtypeStruct((B_HEADS, 2 * WINDOW), f32)),
        in_specs=[pl.BlockSpec(memory_space=pltpu.SMEM),
                  pl.BlockSpec(memory_space=pltpu.VMEM),
                  pl.BlockSpec(memory_space=pltpu.VMEM)],
        out_specs=(pl.BlockSpec(memory_space=pltpu.VMEM), pl.BlockSpec(memory_space=pltpu.VMEM)),
        name="bias_tables",
    )(rel_bias, jnp.asarray(bkp), jnp.asarray(bks))


def _cast_job(src, layer, n_chunks, chunk_of_step):
    _, k, n = src.shape
    rows = k // n_chunks
    return (pl.BlockSpec((None, rows, n), lambda *g: (layer, chunk_of_step(*g), 0)),
            pl.BlockSpec((rows, n), lambda *g: (chunk_of_step(*g), 0)),
            jax.ShapeDtypeStruct((k, n), bf16))


def _cast_chunk(src_ref, dst_ref):
    dst_ref[...] = src_ref[...].astype(bf16)


def _mm_in_kernel(*refs, cast):
    if cast:
        x_ref, g_ref, w_ref, src_ref, z_ref, dst_ref, h_ref = refs
        _cast_chunk(src_ref, dst_ref)
    else:
        x_ref, g_ref, w_ref, z_ref, h_ref = refs

    @pl.when(pl.program_id(1) == 0)
    def _():
        h_ref[...] = _rms(x_ref[...], g_ref[...]).astype(bf16)

    z_ref[...] = jnp.dot(h_ref[...], w_ref[...], preferred_element_type=f32)


def _mm_in(x, gain, w, layer, tm, cast=None):
    M = x.shape[0]
    tn = MM_TILE_N[w.shape[-2:]]
    nj = Z_WIDTH // tn
    in_specs = [pl.BlockSpec((tm, D_MODEL), lambda i, j: (i, 0)),
                pl.BlockSpec((None, 1, D_MODEL), lambda i, j: (layer, 0, 0)),
                pl.BlockSpec((None, D_MODEL, tn), lambda i, j: (layer, 0, j),
                             pipeline_mode=pl.Buffered(1 if nj == 1 else 2))]
    out_specs = [pl.BlockSpec((tm, tn), lambda i, j: (i, j))]
    out_shape = [jax.ShapeDtypeStruct((M, Z_WIDTH), f32)]
    args = [x, gain, w]
    if cast:
        job = _cast_job(cast[0], cast[1], (M // tm) * nj, lambda i, j: i * nj + j)
        in_specs.append(job[0]); out_specs.append(job[1]); out_shape.append(job[2]); args.append(cast[0])
    out = pl.pallas_call(
        functools.partial(_mm_in_kernel, cast=bool(cast)),
        grid=(M // tm, nj),
        in_specs=in_specs, out_specs=out_specs, out_shape=out_shape,
        scratch_shapes=[pltpu.VMEM((tm, D_MODEL), bf16)],
        compiler_params=_cparams(("arbitrary", "arbitrary")),
        name="mm_in",
    )(*args)
    return out if cast else out[0]


def _mm_out_kernel(a_ref, b_ref, c_ref, x_ref, w_ref, o_ref, lhs_ref):
    @pl.when(pl.program_id(1) == 0)
    def _():
        lhs_ref[:, 0:A_WIDTH] = a_ref[...]
        lhs_ref[:, A_WIDTH:A_WIDTH + B_WIDTH] = b_ref[...]
        lhs_ref[:, A_WIDTH + B_WIDTH:D_MODEL] = c_ref[...]

    o_ref[...] = x_ref[...] + jnp.dot(lhs_ref[...], w_ref[...], preferred_element_type=f32)


def _mm_out(a, b, c, x, w, tm):
    M = x.shape[0]
    tn = MM_TILE_N[w.shape]
    return pl.pallas_call(
        _mm_out_kernel,
        grid=(M // tm, D_MODEL // tn),
        in_specs=[pl.BlockSpec((tm, A_WIDTH), lambda i, j: (i, 0)),
                  pl.BlockSpec((tm, B_WIDTH), lambda i, j: (i, 0)),
                  pl.BlockSpec((tm, C_WIDTH), lambda i, j: (i, 0)),
                  pl.BlockSpec((tm, tn), lambda i, j: (i, j)),
                  pl.BlockSpec((D_MODEL, tn), lambda i, j: (0, j),
                               pipeline_mode=pl.Buffered(1 if tn == D_MODEL else 2))],
        out_specs=pl.BlockSpec((tm, tn), lambda i, j: (i, j)),
        out_shape=jax.ShapeDtypeStruct((M, D_MODEL), f32),
        scratch_shapes=[pltpu.VMEM((tm, D_MODEL), bf16)],
        compiler_params=_cparams(("parallel", "arbitrary")),
        name="mm_out",
    )(a, b, c, x, w)


def _mm_down_kernel(a_ref, x_ref, w_ref, o_ref):
    o_ref[...] = x_ref[...] + jnp.dot(a_ref[...], w_ref[...], preferred_element_type=f32)


def _mm_down(act, x, w, tm):
    M = x.shape[0]
    tn = MM_TILE_N[w.shape]
    return pl.pallas_call(
        _mm_down_kernel,
        grid=(M // tm, D_MODEL // tn),
        in_specs=[pl.BlockSpec((tm, D_FF), lambda i, j: (i, 0)),
                  pl.BlockSpec((tm, tn), lambda i, j: (i, j)),
                  pl.BlockSpec((D_FF, tn), lambda i, j: (0, j),
                               pipeline_mode=pl.Buffered(1 if tn == D_MODEL else 2))],
        out_specs=pl.BlockSpec((tm, tn), lambda i, j: (i, j)),
        out_shape=jax.ShapeDtypeStruct((M, D_MODEL), f32),
        compiler_params=_cparams(("parallel", "arbitrary")),
        name="mm_down",
    )(act, x, w)


def _silu(x):
    return x * jax.nn.sigmoid(x)


def _mm_up_prompt_kernel(x_ref, n2_ref, wg_ref, wa_ref, cwg_ref, cwa_ref, cbg_ref, cba_ref, src_ref,
                         act_ref, cs_ref, dst_ref, h_ref, carry_ref, zb_ref,
                         *, tm, nj, n_steps, tiles_per_seq):
    _cast_chunk(src_ref, dst_ref)
    s = pl.program_id(0)
    sa = jnp.minimum(s, n_steps - 1)
    ia = sa // nj
    ja = sa % nj
    tf = act_ref.shape[1]

    @pl.when(s == 0)
    def _():
        zb_ref[...] = jnp.zeros(zb_ref.shape, f32)
        carry_ref[...] = jnp.zeros(carry_ref.shape, f32)

    @pl.when(ja == 0)
    def _():
        h_ref[...] = _rms(x_ref[...], n2_ref[...]).astype(bf16)

    chunks = [slice(c0, c0 + UP_CHUNK) for c0 in range(0, tf, UP_CHUNK)]

    rblocks = [(r0, min(UP_ROWS, tm - r0)) for r0 in range(0, tm, UP_ROWS)]

    def conv(idx, cw_ref, cb_ref, cs, r0, nr):
        zz = zb_ref[idx, r0:r0 + SUBLANES + nr, cs]
        z1 = pltpu.roll(zz, 1, 0)[SUBLANES:]
        z2 = pltpu.roll(zz, 2, 0)[SUBLANES:]
        return (cb_ref[:, cs] + z2 * cw_ref[0:1, cs] + z1 * cw_ref[1:2, cs] + zz[SUBLANES:] * cw_ref[2:3, cs])

    for cs in chunks:
        for r0, nr in rblocks:
            g = conv(0, cwg_ref, cbg_ref, cs, r0, nr)
            a = conv(1, cwa_ref, cba_ref, cs, r0, nr)
            act_ref[r0:r0 + nr, cs] = (_silu(g) * a).astype(bf16)

    seq_start = (ia % tiles_per_seq) == 0
    for cs in chunks:
        for r0, nr in rblocks:
            for idx, w_ref in ((0, wg_ref), (1, wa_ref)):
                z = jnp.dot(h_ref[r0:r0 + nr, :], w_ref[:, cs], preferred_element_type=f32)
                if r0 == 0:
                    zb_ref[idx, 0:SUBLANES, cs] = jnp.where(seq_start, 0.0, carry_ref[idx, ja, :, cs])
                zb_ref[idx, SUBLANES + r0:SUBLANES + r0 + nr, cs] = z
                if r0 + nr == tm:
                    tail = z[nr - SUBLANES:nr, :]
                    carry_ref[idx, ja, :, cs] = tail
                    cs_ref[0, idx, :, cs] = tail


def _mm_up_prompt(x, norm2, w_up, conv_w, conv_b, layer, batch, tm, cast):
    M = x.shape[0]
    seq = M // batch
    tiles_per_seq = seq // tm
    tf = MM_TILE_N[w_up.shape]
    nj = D_FF // tf
    n_steps = (M // tm) * nj
    job = _cast_job(cast[0], cast[1], n_steps, lambda s: jnp.minimum(s, n_steps - 1))
    kern = functools.partial(_mm_up_prompt_kernel, tm=tm, nj=nj, n_steps=n_steps, tiles_per_seq=tiles_per_seq)
    ia = lambda s: jnp.minimum(s, n_steps - 1) // nj
    ja = lambda s: jnp.minimum(s, n_steps - 1) % nj
    ib = lambda s: jnp.maximum(s - 1, 0) // nj
    jb = lambda s: jnp.maximum(s - 1, 0) % nj
    return pl.pallas_call(
        kern,
        grid=(n_steps + 1,),
        in_specs=[pl.BlockSpec((tm, D_MODEL), lambda s: (ia(s), 0)),
                  pl.BlockSpec((None, 1, D_MODEL), lambda s: (layer, 0, 0)),
                  pl.BlockSpec((D_MODEL, tf), lambda s: (0, ja(s))),
                  pl.BlockSpec((D_MODEL, tf), lambda s: (0, nj + ja(s))),
                  pl.BlockSpec((None, CONV_W, tf), lambda s: (layer, 0, jb(s))),
                  pl.BlockSpec((None, CONV_W, tf), lambda s: (layer, 0, nj + jb(s))),
                  pl.BlockSpec((None, 1, tf), lambda s: (layer, 0, jb(s))),
                  pl.BlockSpec((None, 1, tf), lambda s: (layer, 0, nj + jb(s))),
                  job[0]],
        out_specs=(pl.BlockSpec((tm, tf), lambda s: (ib(s), jb(s))),
                   pl.BlockSpec((1, 2, SUBLANES, tf), lambda s: (ia(s), 0, 0, ja(s))),
                   job[1]),
        out_shape=(jax.ShapeDtypeStruct((M, D_FF), bf16),
                   jax.ShapeDtypeStruct((M // tm, 2, SUBLANES, D_FF), f32),
                   job[2]),
        scratch_shapes=[pltpu.VMEM((tm, D_MODEL), bf16),
                        pltpu.VMEM((2, nj, SUBLANES, tf), f32),
                        pltpu.VMEM((2, tm + SUBLANES, tf), f32)],
        compiler_params=_cparams(("arbitrary",)),
        name="mm_up_prompt",
    )(x, norm2, w_up, w_up, conv_w, conv_w, conv_b, conv_b, cast[0])


def _mm_up_sample_kernel(x_ref, n2_ref, wg_ref, wa_ref, cwg_ref, cwa_ref, cbg_ref, cba_ref,
                         b0g_ref, b1g_ref, b0a_ref, b1a_ref, act_ref, zg_ref, za_ref, h_ref):
    @pl.when(pl.program_id(0) == 0)
    def _():
        h_ref[...] = _rms(x_ref[...], n2_ref[...]).astype(bf16)

    def conv_half(w_ref, cw_ref, cb_ref, b0_ref, b1_ref, z_ref):
        z = jnp.dot(h_ref[...], w_ref[...], preferred_element_type=f32)
        z_ref[...] = z
        return (cb_ref[...] + b0_ref[...] * cw_ref[0:1, :] + b1_ref[...] * cw_ref[1:2, :]
                + z * cw_ref[2:3, :])

    g = conv_half(wg_ref, cwg_ref, cbg_ref, b0g_ref, b1g_ref, zg_ref)
    a = conv_half(wa_ref, cwa_ref, cba_ref, b0a_ref, b1a_ref, za_ref)
    act_ref[...] = (_silu(g) * a).astype(bf16)


def _mm_up_sample(x, norm2, w_up, conv_w, conv_b, buf, layer):
    M = x.shape[0]
    tf = UP_TILE_DECODE
    nj = D_FF // tf
    wspec = lambda off: pl.BlockSpec((D_MODEL, tf), lambda j: (0, off + j))
    cwspec = lambda off: pl.BlockSpec((None, CONV_W, tf), lambda j: (layer, 0, off + j))
    cbspec = lambda off: pl.BlockSpec((None, 1, tf), lambda j: (layer, 0, off + j))
    bufspec = lambda row, off: pl.BlockSpec((None, None, M, tf), lambda j: (layer, row, 0, off + j))
    return pl.pallas_call(
        _mm_up_sample_kernel,
        grid=(nj,),
        in_specs=[pl.BlockSpec((M, D_MODEL), lambda j: (0, 0)),
                  pl.BlockSpec((None, 1, D_MODEL), lambda j: (layer, 0, 0)),
                  wspec(0), wspec(nj), cwspec(0), cwspec(nj), cbspec(0), cbspec(nj),
                  bufspec(0, 0), bufspec(1, 0), bufspec(0, nj), bufspec(1, nj)],
        out_specs=(pl.BlockSpec((M, tf), lambda j: (0, j)),
                   pl.BlockSpec((M, tf), lambda j: (0, j)),
                   pl.BlockSpec((M, tf), lambda j: (0, j))),
        out_shape=(jax.ShapeDtypeStruct((M, D_FF), bf16),
                   jax.ShapeDtypeStruct((M, D_FF), f32),
                   jax.ShapeDtypeStruct((M, D_FF), f32)),
        scratch_shapes=[pltpu.VMEM((M, D_MODEL), bf16)],
        compiler_params=_cparams(("arbitrary",)),
        name="mm_up_sample",
    )(x, norm2, w_up, w_up, conv_w, conv_w, conv_b, conv_b, buf, buf, buf, buf)


def _mix_a_prompt_kernel(u_ref, v_ref, vg_ref, ws_ref, bs_ref, o_ref):
    row = lax.broadcasted_iota(jnp.int32, (CHUNK, CHUNK), 0)
    col = lax.broadcasted_iota(jnp.int32, (CHUNK, CHUNK), 1)
    tri = col <= row
    for g in range(A_GROUPS):
        sl = slice(g * A_DIM, (g + 1) * A_DIM)
        ws = jnp.where(tri, ws_ref[g], 0.0).astype(bf16)
        for r0 in range(0, u_ref.shape[0], CHUNK):
            rs = slice(r0, r0 + CHUNK)
            vn = _rms(_gelu(v_ref[rs, sl]), vg_ref[:, sl])
            mixv = jnp.dot(ws, vn.astype(bf16), preferred_element_type=f32) + bs_ref[:, sl]
            o_ref[rs, sl] = (_gelu(u_ref[rs, sl]) * mixv).astype(bf16)


def _mix_a_prompt(z, v_gain, w_s, b_s_full, layer, rows):
    M = z.shape[0]
    return pl.pallas_call(
        _mix_a_prompt_kernel,
        grid=(M // rows,),
        in_specs=[pl.BlockSpec((rows, A_WIDTH), lambda r: (r, Z_AU // A_WIDTH)),
                  pl.BlockSpec((rows, A_WIDTH), lambda r: (r, Z_AV // A_WIDTH)),
                  pl.BlockSpec((None, 1, A_WIDTH), lambda r: (layer, 0, 0)),
                  pl.BlockSpec((None, A_GROUPS, CHUNK, CHUNK), lambda r: (layer, 0, 0, 0)),
                  pl.BlockSpec((None, CHUNK, A_WIDTH), lambda r: (layer, 0, 0))],
        out_specs=pl.BlockSpec((rows, A_WIDTH), lambda r: (r, 0)),
        out_shape=jax.ShapeDtypeStruct((M, A_WIDTH), bf16),
        compiler_params=_cparams(("parallel",)),
        name="mix_a_prompt",
    )(z, z, v_gain, w_s, b_s_full)


def _mix_a_sample_kernel(u_ref, v_ref, vg_ref, ws_ref, bs_ref, o_ref, vn_ref):
    for g in range(A_GROUPS):
        sl = slice(g * A_DIM, (g + 1) * A_DIM)
        vn = _rms(_gelu(v_ref[:, sl]), vg_ref[:, sl])
        vn_ref[:, sl] = vn
        mixv = ws_ref[:, sl] * vn + bs_ref[:, sl]
        o_ref[:, sl] = (_gelu(u_ref[:, sl]) * mixv).astype(bf16)


def _mix_a_sample(z, v_gain, ws0, bs0, layer):
    M = z.shape[0]
    vec = pl.BlockSpec((None, 1, A_WIDTH), lambda r: (layer, 0, 0))
    return pl.pallas_call(
        _mix_a_sample_kernel,
        grid=(1,),
        in_specs=[pl.BlockSpec((M, A_WIDTH), lambda r: (0, Z_AU // A_WIDTH)),
                  pl.BlockSpec((M, A_WIDTH), lambda r: (0, Z_AV // A_WIDTH)),
                  vec, vec, vec],
        out_specs=(pl.BlockSpec((M, A_WIDTH), lambda r: (0, 0)),
                   pl.BlockSpec((M, A_WIDTH), lambda r: (0, 0))),
        out_shape=(jax.ShapeDtypeStruct((M, A_WIDTH), bf16),
                   jax.ShapeDtypeStruct((M, A_WIDTH), f32)),
        compiler_params=_cparams(("arbitrary",)),
        name="mix_a_sample",
    )(z, z, v_gain, ws0, bs0)


def _mix_b_prompt_kernel(sink_ref, q_ref, k_ref, v_ref, qg_ref, kg_ref, bias_ref, src_ref,
                         o_ref, klast_ref, vlast_ref, dst_ref, kcat_ref, vcat_ref):
    _cast_chunk(src_ref, dst_ref)
    i = pl.program_id(1)
    W = WINDOW
    kn = _norm_head_pairs(k_ref[...], kg_ref[...])
    v = v_ref[...]
    klast_ref[0] = kn
    vlast_ref[0] = v

    @pl.when(i == 0)
    def _():
        kcat_ref[0:W, :] = jnp.zeros((W, LANES), f32)
        vcat_ref[0:W, :] = jnp.zeros((W, LANES), f32)

    @pl.when(i > 0)
    def _():
        kcat_ref[0:W, :] = kcat_ref[W:2 * W, :]
        vcat_ref[0:W, :] = vcat_ref[W:2 * W, :]

    kcat_ref[W:2 * W, :] = kn
    vcat_ref[W:2 * W, :] = v
    kc = kcat_ref[...]
    vc = vcat_ref[...]
    kc_sw = pltpu.roll(kc, B_HEAD_DIM, 1)
    vc_sw = pltpu.roll(vc, B_HEAD_DIM, 1)
    lo = lax.broadcasted_iota(jnp.int32, (2 * W, LANES), 1) < B_HEAD_DIM

    cj = lax.broadcasted_iota(jnp.int32, (W, W), 0)
    qi = lax.broadcasted_iota(jnp.int32, (W, W), 1)
    own = cj <= qi
    keep = own | (i > 0)
    q_gain = qg_ref[...] * (B_HEAD_DIM ** -0.5)

    tiles = GQA // 2
    nt = (((1,), (1,)), ((), ()))
    tn = (((0,), (0,)), ((), ()))
    scores, values = [], []
    for kvh in range(B_KV_HEADS):
        k_src, k_alt = (kc, kc_sw) if kvh == 0 else (kc_sw, kc)
        v_src, v_alt = (vc, vc_sw) if kvh == 0 else (vc_sw, vc)
        k_even = jnp.where(lo, k_src, 0.0).astype(bf16)
        k_odd = jnp.where(lo, 0.0, k_alt).astype(bf16)
        values.append((jnp.where(lo, v_src, 0.0).astype(bf16), jnp.where(lo, 0.0, v_alt).astype(bf16)))
        qts = []
        for t in range(tiles):
            c0 = (kvh * tiles + t) * LANES
            qt = q_ref[:, c0:c0 + LANES].T
            q2 = qt * qt
            halves = []
            for r0 in (0, B_HEAD_DIM):
                ms = jnp.mean(q2[r0:r0 + B_HEAD_DIM, :], axis=0, keepdims=True)
                halves.append(qt[r0:r0 + B_HEAD_DIM, :] * lax.rsqrt(ms + EPS))
            qts.append(jnp.concatenate(halves, axis=0) * q_gain)
        qst = jnp.concatenate(qts, axis=1).astype(bf16)
        scores.append((jnp.dot(k_even, qst, preferred_element_type=f32),
                       jnp.dot(k_odd, qst, preferred_element_type=f32)))
    for kvh in range(B_KV_HEADS):
        p_par = []
        for par in range(2):
            blocks = []
            for t in range(tiles):
                h = kvh * GQA + 2 * t + par
                sb = scores[kvh][par][:, t * W:(t + 1) * W]
                s = jnp.where(own, sb[W:2 * W, :], sb[0:W, :]) + bias_ref[h]
                s = jnp.where(keep, s, NEG_INF)
                sk = sink_ref[h]
                mx = jnp.maximum(jnp.max(s, axis=0, keepdims=True), sk)
                p = jnp.exp(s - mx)
                den = jnp.sum(p, axis=0, keepdims=True) + jnp.exp(sk - mx)
                p = p * (1.0 / den)
                blocks.append(jnp.concatenate([jnp.where(own, 0.0, p), jnp.where(own, p, 0.0)],
                                              axis=0).astype(bf16))
            p_par.append(jnp.concatenate(blocks, axis=1))
        ot = (lax.dot_general(values[kvh][0], p_par[0], tn, preferred_element_type=f32)
              + lax.dot_general(values[kvh][1], p_par[1], tn, preferred_element_type=f32))
        for t in range(tiles):
            c0 = (kvh * tiles + t) * LANES
            o_ref[:, c0:c0 + LANES] = ot[:, t * W:(t + 1) * W].T.astype(bf16)


def _mix_b_prompt(z, sinks, q_gain_t, k_gain2, bias_p, layer, batch, cast):
    M = z.shape[0]
    nb = M // batch // WINDOW
    job = _cast_job(cast[0], cast[1], batch * nb, lambda b, i: b * nb + i)
    return pl.pallas_call(
        _mix_b_prompt_kernel,
        grid=(batch, nb),
        in_specs=[pl.BlockSpec(memory_space=pltpu.SMEM),
                  pl.BlockSpec((WINDOW, B_WIDTH), lambda b, i: (b * nb + i, Z_BQ // B_WIDTH)),
                  pl.BlockSpec((WINDOW, B_KV_WIDTH), lambda b, i: (b * nb + i, Z_BK // B_KV_WIDTH)),
                  pl.BlockSpec((WINDOW, B_KV_WIDTH), lambda b, i: (b * nb + i, Z_BV // B_KV_WIDTH)),
                  pl.BlockSpec((None, LANES, WINDOW), lambda b, i: (layer, 0, 0)),
                  pl.BlockSpec((None, 1, LANES), lambda b, i: (layer, 0, 0)),
                  pl.BlockSpec((B_HEADS, WINDOW, WINDOW), lambda b, i: (0, 0, 0)),
                  job[0]],
        out_specs=(pl.BlockSpec((WINDOW, B_WIDTH), lambda b, i: (b * nb + i, 0)),
                   pl.BlockSpec((1, WINDOW, B_KV_WIDTH), lambda b, i: (b, 0, 0)),
                   pl.BlockSpec((1, WINDOW, B_KV_WIDTH), lambda b, i: (b, 0, 0)),
                   job[1]),
        out_shape=(jax.ShapeDtypeStruct((M, B_WIDTH), bf16),
                   jax.ShapeDtypeStruct((batch, WINDOW, B_KV_WIDTH), f32),
                   jax.ShapeDtypeStruct((batch, WINDOW, B_KV_WIDTH), f32),
                   job[2]),
        scratch_shapes=[pltpu.VMEM((2 * WINDOW, LANES), f32),
                        pltpu.VMEM((2 * WINDOW, LANES), f32)],
        compiler_params=_cparams(("arbitrary", "arbitrary")),
        name="mix_b_prompt",
    )(sinks, z, z, z, q_gain_t, k_gain2, bias_p, cast[0])


def _mix_c_prompt_kernel(q_ref, k_ref, v0_ref, v1_ref, o0_ref, o1_ref, g_ref, gb_ref, hgt_ref,
                         out_ref, c_out, n_out, m_out, c_s, n_s, m_s):
    L = MLSTM_L
    c = pl.program_id(1)

    @pl.when(c == 0)
    def _():
        c_s[...] = jnp.zeros(c_s.shape, f32)
        n_s[...] = jnp.zeros(n_s.shape, f32)
        m_s[...] = jnp.zeros(m_s.shape, f32)

    row = lax.broadcasted_iota(jnp.int32, (L, L), 0)
    col = lax.broadcasted_iota(jnp.int32, (L, L), 1)
    tri = col <= row
    tri_f = tri.astype(f32)
    seqs = range(q_ref.shape[0])
    units = [(bi, h) for bi in seqs for h in range(C_HEADS)]
    g, b = {}, {}
    for bi in seqs:
        g[bi] = g_ref[bi] + gb_ref[...]
        b[bi] = jnp.dot(tri_f, jax.nn.log_sigmoid(g[bi]), precision=lax.Precision.HIGHEST,
                        preferred_element_type=f32)
    bT = {bi: b[bi].T for bi in seqs}
    vis = row <= col
    nt = (((1,), (1,)), ((), ()))
    tn = (((0,), (0,)), ((), ()))
    qb, kh, vb, ct_prev, n_prev, st, qct, qn_lin = {}, {}, {}, {}, {}, {}, {}, {}
    for u in units:
        bi, h = u
        v_ref = v0_ref if h < 2 else v1_ref
        vsl = slice((h % 2) * C_V_DIM, (h % 2 + 1) * C_V_DIM)
        qb[u] = q_ref[bi, :, h * C_QK_DIM:(h + 1) * C_QK_DIM].astype(bf16)
        kh[u] = k_ref[bi, :, h * C_QK_DIM:(h + 1) * C_QK_DIM] * (C_QK_DIM ** -0.5)
        vb[u] = v_ref[bi, :, vsl].astype(bf16)
        ct_prev[u] = c_s[bi, h]
        n_prev[u] = n_s[bi, h]
        st[u] = lax.dot_general(kh[u].astype(bf16), qb[u], nt, preferred_element_type=f32)
        qct[u] = lax.dot_general(ct_prev[u].astype(bf16), qb[u], nt, preferred_element_type=f32)
        n8 = jnp.broadcast_to(n_prev[u], (SUBLANES, C_QK_DIM)).astype(bf16)
        qn_lin[u] = lax.dot_general(n8, qb[u], nt, preferred_element_type=f32)[0:1, :]
    pt, iw, inv, kw, decay, m_new = {}, {}, {}, {}, {}, {}
    for u in units:
        bi, h = u
        b_row = bT[bi][C_HEADS + h:C_HEADS + h + 1, :]
        src = g[bi][:, h:h + 1] - b[bi][:, C_HEADS + h:C_HEADS + h + 1]
        dlog = jnp.where(vis, b_row + src, -jnp.inf)
        m_prev = m_s[bi, h][0:1, 0:1]
        inter = b_row + m_prev
        mt = jnp.maximum(inter, jnp.max(dlog, axis=0, keepdims=True))
        dt = jnp.exp(dlog - mt)
        pt[u] = st[u] * dt
        iw[u] = jnp.exp(inter - mt)
        qn = iw[u] * qn_lin[u] + jnp.sum(pt[u], axis=0, keepdims=True)
        inv[u] = 1.0 / jnp.maximum(jnp.abs(qn), jnp.exp(-mt))
        m_new[u] = mt[:, L - 1:L]
        decay[u] = jnp.exp(b_row[:, L - 1:L] + m_prev - m_new[u])
        kw[u] = kh[u] * dt[:, L - 1:L]
    svt = {u: lax.dot_general(vb[u], pt[u].astype(bf16), tn, preferred_element_type=f32) for u in units}
    kvt = {u: lax.dot_general(vb[u], kw[u].astype(bf16), tn, preferred_element_type=f32) for u in units}
    for u in units:
        bi, h = u
        o_ref = o0_ref if h < 2 else o1_ref
        vsl = slice((h % 2) * C_V_DIM, (h % 2 + 1) * C_V_DIM)
        ht = (iw[u] * qct[u] + svt[u]) * inv[u]
        ms = jnp.mean(ht * ht, axis=0, keepdims=True)
        hn = (ht * lax.rsqrt(ms + EPS) * hgt_ref[h]).T
        c_s[bi, h] = decay[u] * ct_prev[u] + kvt[u]
        n_s[bi, h] = decay[u] * n_prev[u] + jnp.sum(kw[u], axis=0, keepdims=True)
        m_s[bi, h] = jnp.broadcast_to(m_new[u], (SUBLANES, LANES))
        osl = slice(h * C_V_DIM, (h + 1) * C_V_DIM)
        out_ref[bi, :, osl] = (hn * jax.nn.sigmoid(o_ref[bi, :, vsl])).astype(bf16)

    @pl.when(c == pl.num_programs(1) - 1)
    def _():
        eye = (row == col).astype(f32)
        for bi, h in units:
            c_out[bi, h] = lax.dot_general(c_s[bi, h], eye, tn, precision=lax.Precision.HIGHEST,
                                           preferred_element_type=f32)
        n_out[...] = n_s[...]
        m_out[...] = m_s[...]


def _mix_c_prompt(z, gate_bias, h_gain, layer, batch):
    M = z.shape[0]
    L = MLSTM_L
    T = M // batch
    z3 = z.reshape(batch, T, Z_WIDTH)
    ns = C_SEQS if batch % C_SEQS == 0 else 1
    col = lambda width, off: pl.BlockSpec((ns, L, width), lambda b, c: (b, c, off // width))
    state = lambda *shape: pl.BlockSpec((ns,) + shape, lambda b, c: (b,) + (0,) * len(shape))
    out, c_st, n_st, m_st = pl.pallas_call(
        _mix_c_prompt_kernel,
        grid=(batch // ns, T // L),
        in_specs=[col(C_QK_WIDTH, Z_CQ), col(C_QK_WIDTH, Z_CK),
                  col(C_PAIR, Z_CV), col(C_PAIR, Z_CV + C_PAIR),
                  col(C_PAIR, Z_CO), col(C_PAIR, Z_CO + C_PAIR),
                  col(LANES, Z_GATE),
                  pl.BlockSpec((None, 1, LANES), lambda b, c: (layer, 0, 0)),
                  pl.BlockSpec((None, C_HEADS, C_V_DIM, L), lambda b, c: (layer, 0, 0, 0))],
        out_specs=(pl.BlockSpec((ns, L, C_WIDTH), lambda b, c: (b, c, 0)),
                   state(C_HEADS, C_QK_DIM, C_V_DIM),
                   state(C_HEADS, 1, C_QK_DIM),
                   state(C_HEADS, SUBLANES, LANES)),
        out_shape=(jax.ShapeDtypeStruct((batch, T, C_WIDTH), bf16),
                   jax.ShapeDtypeStruct((batch, C_HEADS, C_QK_DIM, C_V_DIM), f32),
                   jax.ShapeDtypeStruct((batch, C_HEADS, 1, C_QK_DIM), f32),
                   jax.ShapeDtypeStruct((batch, C_HEADS, SUBLANES, LANES), f32)),
        scratch_shapes=[pltpu.VMEM((ns, C_HEADS, C_V_DIM, C_QK_DIM), f32),
                        pltpu.VMEM((ns, C_HEADS, 1, C_QK_DIM), f32),
                        pltpu.VMEM((ns, C_HEADS, SUBLANES, LANES), f32)],
        compiler_params=_cparams(("arbitrary", "arbitrary")),
        name="mix_c_prompt",
    )(z3, z3, z3, z3, z3, z3, z3, gate_bias, h_gain)
    return out.reshape(M, C_WIDTH), c_st, n_st, m_st


_BC_SHARED = (4, 5, 6, 7, 12, 13)


def _mix_bc_sample_kernel(*refs):
    rows = [_mix_bc_sample_row(*[ref if k in _BC_SHARED else ref.at[pl.ds(r, 1)] for k, ref in enumerate(refs)])
            for r in range(refs[0].shape[0])]
    while rows:
        rows = [row for row in rows if next(row, True) is None]


def _mix_bc_sample_row(zr_ref, q_ref, kp_ref, vp_ref, qg_ref, kg_ref, sink_ref, bias_ref,
                       cq_ref, ck_ref, cv_ref, co_ref, gb_ref, hg_ref, c0_ref, n0_ref, m0_ref,
                       bo_ref, kn_ref, co_out, c_out, n_out, m_out):
    W = WINDOW
    zr = zr_ref[0]

    qn = _rms(q_ref[0], qg_ref[...])
    hrow = lax.broadcasted_iota(jnp.int32, (B_HEADS, LANES), 0)
    lane = lax.broadcasted_iota(jnp.int32, (B_HEADS, LANES), 1)
    own = (hrow < GQA) == (lane < B_HEAD_DIM)
    q2 = jnp.where(own, jnp.concatenate([qn, qn], axis=-1), 0.0)
    k_new = _norm_head_pairs(zr[:, Z_BK:Z_BK + B_KV_WIDTH], kg_ref[...])
    v_new = zr[:, Z_BV:Z_BV + B_KV_WIDTH]
    kn_ref[0] = k_new
    scale = B_HEAD_DIM ** -0.5
    s_past = lax.dot_general(q2.astype(bf16), kp_ref[0].astype(bf16), (((1,), (1,)), ((), ())),
                             preferred_element_type=f32) * scale + bias_ref[:, 0:W]
    yield
    s_past = jnp.where(lane >= 1, s_past, NEG_INF)
    s_new = jnp.sum(q2 * k_new, axis=-1, keepdims=True) * scale + bias_ref[:, W:W + 1]
    sk = sink_ref[...]
    mx = jnp.maximum(jnp.maximum(jnp.max(s_past, axis=-1, keepdims=True), s_new), sk)
    p_past = jnp.exp(s_past - mx)
    p_new = jnp.exp(s_new - mx)
    den = jnp.sum(p_past, axis=-1, keepdims=True) + p_new + jnp.exp(sk - mx)
    inv = 1.0 / den
    o2 = (jnp.dot((p_past * inv).astype(bf16), vp_ref[0].astype(bf16), preferred_element_type=f32)
          + (p_new * inv) * v_new)
    o2_sw = pltpu.roll(o2, B_HEAD_DIM, 1)
    bo_ref[0] = jnp.where(hrow < GQA, o2, o2_sw)[:, 0:B_HEAD_DIM].astype(bf16)

    g = zr[:, Z_GATE:Z_GATE + LANES] + gb_ref[...]
    lf = jax.nn.log_sigmoid(g)
    hrow4 = lax.broadcasted_iota(jnp.int32, (C_HEADS, LANES), 0)
    lane4 = lax.broadcasted_iota(jnp.int32, (C_HEADS, LANES), 1)

    def head_column(row_vec, first_lane):
        spread = jnp.broadcast_to(row_vec, (C_HEADS, LANES))
        return jnp.sum(jnp.where(lane4 == hrow4 + first_lane, spread, 0.0), axis=-1, keepdims=True)

    ig = head_column(g, 0)
    b = head_column(lf, C_HEADS)
    m0 = m0_ref[0]
    m_prev = jnp.sum(jnp.where(lane4[:, 0:C_HEADS] == hrow4[:, 0:C_HEADS],
                               jnp.broadcast_to(m0, (C_HEADS, C_HEADS)), 0.0), axis=-1, keepdims=True)
    inter = b + m_prev
    mt = jnp.maximum(inter, ig)
    q4 = cq_ref[0]
    k4 = ck_ref[0] * (C_QK_DIM ** -0.5)
    v4 = cv_ref[0]
    n4 = n0_ref[0]
    c0 = c0_ref[0]
    s = jnp.sum(q4 * k4, axis=-1, keepdims=True) * jnp.exp(ig - mt)
    iw = jnp.exp(inter - mt)
    blk_row = lax.broadcasted_iota(jnp.int32, (C_HEADS, C_QK_WIDTH), 0)
    blk_lane = lax.broadcasted_iota(jnp.int32, (C_HEADS, C_QK_WIDTH), 1) // C_QK_DIM
    pad4 = jnp.zeros((SUBLANES - C_HEADS, C_QK_WIDTH), f32)

    def head_blocks(x4):
        tiled = jnp.concatenate([x4] * C_HEADS, axis=-1)
        return jnp.concatenate([jnp.where(blk_lane == blk_row, tiled, 0.0), pad4], axis=0).astype(bf16)

    c0s = c0.reshape(C_QK_WIDTH, C_V_DIM)
    qc = jnp.dot(head_blocks(q4), c0s.astype(bf16), preferred_element_type=f32)[0:C_HEADS, :]
    yield
    num = iw * qc + s * v4
    qn_ = iw * jnp.sum(q4 * n4, axis=-1, keepdims=True) + s
    hh = num / jnp.maximum(jnp.abs(qn_), jnp.exp(-mt))
    decay = iw
    kw = k4 * jnp.exp(ig - mt)
    v8 = jnp.concatenate([v4, jnp.zeros((SUBLANES - C_HEADS, C_V_DIM), f32)], axis=0).astype(bf16)
    outer = lax.dot_general(head_blocks(kw), v8, (((0,), (0,)), ((), ())),
                            preferred_element_type=f32)
    for h in range(C_HEADS):
        c_out[0, h] = decay[h:h + 1, :] * c0[h] + outer[h * C_QK_DIM:(h + 1) * C_QK_DIM, :]
    n_out[0] = decay * n4 + kw
    m_out[0] = jnp.sum(jnp.where(lane4 == hrow4, jnp.broadcast_to(mt, (C_HEADS, LANES)), 0.0),
                       axis=0, keepdims=True)
    co_out[0] = (_rms(hh, hg_ref[...]) * jax.nn.sigmoid(co_ref[0])).astype(bf16)


def _mix_bc_sample(z, k_past, v_past, q_gain, k_gain2, sinks, bias_s, gate_bias, h_gain4,
                   c0, n0, m0, layer):
    R = z.shape[0]
    zr = z.reshape(R, 1, Z_WIDTH)
    q = z[:, Z_BQ:Z_BQ + B_WIDTH].reshape(R, B_HEADS, B_HEAD_DIM)
    cq = z[:, Z_CQ:Z_CQ + C_QK_WIDTH].reshape(R, C_HEADS, C_QK_DIM)
    ck = z[:, Z_CK:Z_CK + C_QK_WIDTH].reshape(R, C_HEADS, C_QK_DIM)
    cv = z[:, Z_CV:Z_CV + C_WIDTH].reshape(R, C_HEADS, C_V_DIM)
    co = z[:, Z_CO:Z_CO + C_WIDTH].reshape(R, C_HEADS, C_V_DIM)
    rb = DEC_ROWS if R % DEC_ROWS == 0 else 1
    row3 = lambda n: pl.BlockSpec((rb, 1, n), lambda r: (r, 0, 0))
    return pl.pallas_call(
        _mix_bc_sample_kernel,
        grid=(R // rb,),
        in_specs=[row3(Z_WIDTH),
                  pl.BlockSpec((rb, B_HEADS, B_HEAD_DIM), lambda r: (r, 0, 0)),
                  pl.BlockSpec((None, rb, WINDOW, B_KV_WIDTH), lambda r: (layer, r, 0, 0)),
                  pl.BlockSpec((None, rb, WINDOW, B_KV_WIDTH), lambda r: (layer, r, 0, 0)),
                  pl.BlockSpec((None, 1, B_HEAD_DIM), lambda r: (layer, 0, 0)),
                  pl.BlockSpec((None, 1, LANES), lambda r: (layer, 0, 0)),
                  pl.BlockSpec((None, B_HEADS, 1), lambda r: (layer, 0, 0)),
                  pl.BlockSpec((B_HEADS, 2 * WINDOW), lambda r: (0, 0)),
                  pl.BlockSpec((rb, C_HEADS, C_QK_DIM), lambda r: (r, 0, 0)),
                  pl.BlockSpec((rb, C_HEADS, C_QK_DIM), lambda r: (r, 0, 0)),
                  pl.BlockSpec((rb, C_HEADS, C_V_DIM), lambda r: (r, 0, 0)),
                  pl.BlockSpec((rb, C_HEADS, C_V_DIM), lambda r: (r, 0, 0)),
                  pl.BlockSpec((None, 1, LANES), lambda r: (layer, 0, 0)),
                  pl.BlockSpec((None, C_HEADS, C_V_DIM), lambda r: (layer, 0, 0)),
                  pl.BlockSpec((None, rb, C_HEADS, C_QK_DIM, C_V_DIM), lambda r: (layer, r, 0, 0, 0)),
                  pl.BlockSpec((None, rb, C_HEADS, C_QK_DIM), lambda r: (layer, r, 0, 0)),
                  pl.BlockSpec((None, rb, 1, C_HEADS), lambda r: (layer, r, 0, 0))],
        out_specs=(pl.BlockSpec((rb, B_HEADS, B_HEAD_DIM), lambda r: (r, 0, 0)),
                   row3(B_KV_WIDTH),
                   pl.BlockSpec((rb, C_HEADS, C_V_DIM), lambda r: (r, 0, 0)),
                   pl.BlockSpec((rb, C_HEADS, C_QK_DIM, C_V_DIM), lambda r: (r, 0, 0, 0)),
                   pl.BlockSpec((rb, C_HEADS, C_QK_DIM), lambda r: (r, 0, 0)),
                   row3(LANES)),
        out_shape=(jax.ShapeDtypeStruct((R, B_HEADS, B_HEAD_DIM), bf16),
                   jax.ShapeDtypeStruct((R, 1, B_KV_WIDTH), f32),
                   jax.ShapeDtypeStruct((R, C_HEADS, C_V_DIM), bf16),
                   jax.ShapeDtypeStruct((R, C_HEADS, C_QK_DIM, C_V_DIM), f32),
                   jax.ShapeDtypeStruct((R, C_HEADS, C_QK_DIM), f32),
                   jax.ShapeDtypeStruct((R, 1, LANES), f32)),
        compiler_params=_cparams(("parallel",)),
        name="mix_bc_sample",
    )(zr, q, k_past, v_past, q_gain, k_gain2, sinks, bias_s, cq, ck, cv, co, gate_bias, h_gain4,
      c0, n0, m0)


def _prep_w_in(w_in):
    pad = jnp.zeros(w_in.shape[:-1] + (Z_WIDTH - w_in.shape[-1],), w_in.dtype)
    return jnp.concatenate([w_in, pad], axis=-1).astype(bf16)


def _row_tile(m, pref):
    return pref if m % pref == 0 else m


def kernel(x_prompt, x_sample, cache_swa_k, cache_swa_v, state_mlstm_C, state_mlstm_n, state_mlstm_m, state_ffn_conv, rel_bias, norm1, w_in, a_v_gain, a_spatial_w, a_spatial_b, b_q_gain, b_k_gain, b_sinks, c_gate_bias, c_h_gain, w_out, norm2, w_up, ffn_conv_w, ffn_conv_b, w_down):
    depth = w_in.shape[0]
    Bp, T, _ = x_prompt.shape
    R = x_sample.shape[0]
    assert x_sample.shape[1] == 1 and T % CHUNK == 0 and T % MLSTM_L == 0

    w_in_p = _prep_w_in(w_in)
    norm1_3 = norm1.reshape(depth, 1, D_MODEL)
    norm2_3 = norm2.reshape(depth, 1, D_MODEL)
    v_gain3 = a_v_gain.reshape(depth, 1, A_WIDTH)
    bs_full = jnp.repeat(jnp.swapaxes(a_spatial_b, 1, 2), A_DIM, axis=-1)
    ws0 = jnp.repeat(a_spatial_w[:, :, 0, 0], A_DIM, axis=-1).reshape(depth, 1, A_WIDTH)
    bs0 = jnp.repeat(a_spatial_b[:, :, 0], A_DIM, axis=-1).reshape(depth, 1, A_WIDTH)
    q_gain3 = b_q_gain.reshape(depth, 1, B_HEAD_DIM)
    q_gain_t = jnp.broadcast_to(jnp.tile(b_q_gain, (1, 2)).reshape(depth, LANES, 1), (depth, LANES, WINDOW))
    k_gain2 = jnp.tile(b_k_gain, (1, 2)).reshape(depth, 1, LANES)
    sinks3 = b_sinks.reshape(depth, B_HEADS, 1)
    gate_b = jnp.pad(c_gate_bias, ((0, 0), (0, LANES - 2 * C_HEADS))).reshape(depth, 1, LANES)
    h_gain_t = jnp.broadcast_to(c_h_gain.reshape(depth, C_HEADS, C_V_DIM, 1), (depth, C_HEADS, C_V_DIM, MLSTM_L))
    h_gain4 = c_h_gain.reshape(depth, C_HEADS, C_V_DIM)
    conv_b3 = ffn_conv_b.reshape(depth, 1, 2 * D_FF)
    k_cache = cache_swa_k.reshape(depth, R, WINDOW, B_KV_WIDTH)
    v_cache = cache_swa_v.reshape(depth, R, WINDOW, B_KV_WIDTH)
    m_state = state_mlstm_m.reshape(depth, R, 1, C_HEADS)
    conv_hist = jnp.swapaxes(state_ffn_conv, 1, 2)

    bias_p, bias_s = _bias_tables(rel_bias)

    Mp = Bp * T
    tm = _row_tile(T, 1024)
    tm_io = _row_tile(T, 512)
    tm_down = _row_tile(T, 256)
    a_rows = _row_tile(T, 1024)
    xp = x_prompt.reshape(Mp, D_MODEL)
    xs = x_sample.reshape(R, D_MODEL)
    P = [[] for _ in range(6)]
    S = [[] for _ in range(7)]
    for l in range(depth):
        z, w_out_l = _mm_in(xp, norm1_3, w_in_p, l, tm_io, cast=(w_out, l))
        a_o = _mix_a_prompt(z, v_gain3, a_spatial_w, bs_full, l, a_rows)
        b_o, k_last, v_last, w_up_l = _mix_b_prompt(z, b_sinks[l], q_gain_t, k_gain2, bias_p, l, Bp, (w_up, l))
        c_o, c_st, n_st, m_st = _mix_c_prompt(z, gate_b, h_gain_t, l, Bp)
        x1 = _mm_out(a_o, b_o, c_o, xp, w_out_l, tm_io)
        act, cs, w_down_l = _mm_up_prompt(x1, norm2_3, w_up_l, ffn_conv_w, conv_b3, l, Bp, tm, (w_down, l))
        xp = _mm_down(act, x1, w_down_l, tm_down)
        P[0].append(k_last.reshape(Bp, WINDOW, B_KV_HEADS, B_HEAD_DIM))
        P[1].append(v_last.reshape(Bp, WINDOW, B_KV_HEADS, B_HEAD_DIM))
        P[2].append(c_st)
        P[3].append(n_st.reshape(Bp, C_HEADS, C_QK_DIM))
        P[4].append(m_st[:, :, 0, 0])
        seq_tiles = T // tm
        tail = cs[seq_tiles - 1::seq_tiles, :, SUBLANES - (CONV_W - 1):, :]
        P[5].append(jnp.swapaxes(tail, 1, 2).reshape(Bp, CONV_W - 1, 2 * D_FF))

        zs = _mm_in(xs, norm1_3, w_in_p, l, R)
        a_s, vn_s = _mix_a_sample(zs, v_gain3, ws0, bs0, l)
        b_s, kn_s, c_s, c_new, n_new, m_new = _mix_bc_sample(
            zs, k_cache, v_cache, q_gain3, k_gain2, sinks3, bias_s, gate_b, h_gain4,
            state_mlstm_C, state_mlstm_n, m_state, l)
        x1s = _mm_out(a_s, b_s.reshape(R, B_WIDTH), c_s.reshape(R, C_WIDTH), xs, w_out_l, R)
        act_s, zg_s, za_s = _mm_up_sample(x1s, norm2_3, w_up_l, ffn_conv_w, conv_b3, conv_hist, l)
        xs = _mm_down(act_s, x1s, w_down_l, R)
        S[0].append(vn_s.reshape(R, 1, A_WIDTH))
        S[1].append(kn_s.reshape(R, 1, B_KV_HEADS, B_HEAD_DIM))
        S[2].append(zs[:, Z_BV:Z_BV + B_KV_WIDTH].reshape(R, 1, B_KV_HEADS, B_HEAD_DIM))
        S[3].append(c_new)
        S[4].append(n_new)
        S[5].append(m_new[:, 0, 0:C_HEADS])
        z_new = jnp.concatenate([zg_s, za_s], axis=-1)
        S[6].append(jnp.stack([state_ffn_conv[l][:, CONV_W - 2], z_new], axis=1))

    st = lambda lst: jnp.stack(lst, axis=0)
    return (xp.reshape(Bp, T, D_MODEL), xs.reshape(R, 1, D_MODEL),
            st(P[0]), st(P[1]), st(S[1]), st(S[2]),
            st(P[2]), st(P[3]), st(P[4]),
            st(S[3]), st(S[4]), st(S[5]),
            st(P[5]), st(S[6]),
            st(S[0]))
```

```python
import functools
import math

import numpy as np
import jax
import jax.numpy as jnp
from jax import lax
from jax.experimental import pallas as pl
from jax.experimental.pallas import tpu as pltpu

f32 = jnp.float32
bf16 = jnp.bfloat16

D_MODEL = 2048
EPS = 1e-6
NEG_INF = -1e30
SQRT_HALF = 0.7071067811865476

A_GROUPS = 4
A_DIM = 128
A_WIDTH = 512
CHUNK = 128
B_HEADS = 16
B_KV_HEADS = 2
B_HEAD_DIM = 64
GQA = 8
B_WIDTH = 1024
B_KV_WIDTH = 128
WINDOW = 128
N_BUCKETS = 32
MAX_DISTANCE = 128
C_HEADS = 4
C_QK_DIM = 64
C_V_DIM = 128
C_QK_WIDTH = 256
C_WIDTH = 512
IN_SPLITS = (A_WIDTH, A_WIDTH, B_WIDTH, B_KV_WIDTH, B_KV_WIDTH,
             C_QK_WIDTH, C_QK_WIDTH, C_WIDTH, C_WIDTH, C_HEADS, C_HEADS)
IN_OFFSETS = tuple(int(o) for o in np.cumsum(IN_SPLITS)[:-1])
D_FF = 5632
CONV_W = 3

Z_WIDTH = 4096
Z_AU, Z_AV, Z_BQ, Z_BK, Z_BV, Z_CQ, Z_CK, Z_CV, Z_CO, Z_GATE = (0,) + IN_OFFSETS[:9]
C_PAIR = 2 * C_V_DIM

MLSTM_L = 128
UP_CHUNK = 256
UP_ROWS = 128
UP_TILE_DECODE = 1408
C_SEQS = 4
DEC_ROWS = 8
MM_TILE_N = {(D_MODEL, Z_WIDTH): Z_WIDTH, (D_MODEL, D_MODEL): D_MODEL, (D_MODEL, 2 * D_FF): 512,
             (D_FF, D_MODEL): D_MODEL}
LANES = 128
SUBLANES = 8
VMEM_LIMIT = 52 * 1024 * 1024


def _cparams(sem, flags=None):
    return pltpu.CompilerParams(dimension_semantics=sem, vmem_limit_bytes=VMEM_LIMIT, flags=flags)


def _gelu(x):
    return 0.5 * x * (1.0 + lax.erf(x * SQRT_HALF))


def _rms(x, gain):
    return x * lax.rsqrt(jnp.mean(x * x, axis=-1, keepdims=True) + EPS) * gain


def _norm_head_pairs(x, gain2):
    lo = lax.broadcasted_iota(jnp.int32, x.shape, 1) < B_HEAD_DIM
    x2 = x * x
    s_lo = jnp.sum(jnp.where(lo, x2, 0.0), axis=-1, keepdims=True)
    s_hi = jnp.sum(jnp.where(lo, 0.0, x2), axis=-1, keepdims=True)
    ms = jnp.where(lo, s_lo, s_hi) * (1.0 / B_HEAD_DIM)
    return x * lax.rsqrt(ms + EPS) * gain2


def _t5_bucket_np(dist):
    n = np.maximum(dist, 0)
    max_exact = N_BUCKETS // 2
    nf = np.maximum(n, 1).astype(np.float32)
    large = max_exact + (np.log(nf / np.float32(max_exact)) / np.float32(math.log(MAX_DISTANCE / max_exact))
                         * np.float32(N_BUCKETS - max_exact)).astype(np.int32)
    return np.where(n < max_exact, n, np.minimum(large, N_BUCKETS - 1)).astype(np.int32)


def _bias_kernel(rb_ref, bkp_ref, bks_ref, op_ref, os_ref):
    bkp = bkp_ref[...]
    bks = bks_ref[...]
    for h in range(B_HEADS):
        accp = jnp.zeros(bkp.shape, f32)
        accs = jnp.zeros(bks.shape, f32)
        for b in range(N_BUCKETS):
            val = rb_ref[b, h]
            accp = jnp.where(bkp == b, val, accp)
            accs = jnp.where(bks == b, val, accs)
        op_ref[h] = accp
        os_ref[h:h + 1, :] = accs[0:1, :]


def _bias_tables(rel_bias):
    cj = np.arange(WINDOW)[:, None]
    qi = np.arange(WINDOW)[None, :]
    bkp = _t5_bucket_np((qi - cj) % WINDOW)
    j = np.arange(2 * WINDOW)
    dist_s = np.where(j < WINDOW, WINDOW - j, 0)
    bks = np.broadcast_to(_t5_bucket_np(dist_s)[None, :], (SUBLANES, 2 * WINDOW)).copy()
    return pl.pallas_call(
        _bias_kernel,
        out_shape=(jax.ShapeDtypeStruct((B_HEADS, WINDOW, WINDOW), f32),
                   jax.ShapeDtypeStruct((B_HEADS, 2 * WINDOW), f32)),
        in_specs=[pl.BlockSpec(memory_space=pltpu.SMEM),
                  pl.BlockSpec(memory_space=pltpu.VMEM),
                  pl.BlockSpec(memory_space=pltpu.VMEM)],
        out_specs=(pl.BlockSpec(memory_space=pltpu.VMEM), pl.BlockSpec(memory_space=pltpu.VMEM)),
        name="bias_tables",
    )(rel_bias, jnp.asarray(bkp), jnp.asarray(bks))


def _cast_job(src, layer, n_chunks, chunk_of_step):
    _, k, n = src.shape
    rows = k // n_chunks
    return (pl.BlockSpec((None, rows, n), lambda *g: (layer, chunk_of_step(*g), 0)),
            pl.BlockSpec((rows, n), lambda *g: (chunk_of_step(*g), 0)),
            jax.ShapeDtypeStruct((k, n), bf16))


def _cast_chunk(src_ref, dst_ref):
    dst_ref[...] = src_ref[...].astype(bf16)


def _mm_in_kernel(*refs, cast):
    if cast:
        x_ref, g_ref, w_ref, src_ref, z_ref, dst_ref, h_ref = refs
        _cast_chunk(src_ref, dst_ref)
    else:
        x_ref, g_ref, w_ref, z_ref, h_ref = refs

    @pl.when(pl.program_id(1) == 0)
    def _():
        h_ref[...] = _rms(x_ref[...], g_ref[...]).astype(bf16)

    z_ref[...] = jnp.dot(h_ref[...], w_ref[...], preferred_element_type=f32)


def _mm_in(x, gain, w, layer, tm, cast=None):
    M = x.shape[0]
    tn = MM_TILE_N[w.shape[-2:]]
    nj = Z_WIDTH // tn
    in_specs = [pl.BlockSpec((tm, D_MODEL), lambda i, j: (i, 0)),
                pl.BlockSpec((None, 1, D_MODEL), lambda i, j: (layer, 0, 0)),
                pl.BlockSpec((None, D_MODEL, tn), lambda i, j: (layer, 0, j),
                             pipeline_mode=pl.Buffered(1 if nj == 1 else 2))]
    out_specs = [pl.BlockSpec((tm, tn), lambda i, j: (i, j))]
    out_shape = [jax.ShapeDtypeStruct((M, Z_WIDTH), f32)]
    args = [x, gain, w]
    if cast:
        job = _cast_job(cast[0], cast[1], (M // tm) * nj, lambda i, j: i * nj + j)
        in_specs.append(job[0]); out_specs.append(job[1]); out_shape.append(job[2]); args.append(cast[0])
    out = pl.pallas_call(
        functools.partial(_mm_in_kernel, cast=bool(cast)),
        grid=(M // tm, nj),
        in_specs=in_specs, out_specs=out_specs, out_shape=out_shape,
        scratch_shapes=[pltpu.VMEM((tm, D_MODEL), bf16)],
        compiler_params=_cparams(("arbitrary", "arbitrary")),
        name="mm_in",
    )(*args)
    return out if cast else out[0]


def _mm_out_kernel(a_ref, b_ref, c_ref, x_ref, w_ref, o_ref, lhs_ref):
    @pl.when(pl.program_id(1) == 0)
    def _():
        lhs_ref[:, 0:A_WIDTH] = a_ref[...]
        lhs_ref[:, A_WIDTH:A_WIDTH + B_WIDTH] = b_ref[...]
        lhs_ref[:, A_WIDTH + B_WIDTH:D_MODEL] = c_ref[...]

    o_ref[...] = x_ref[...] + jnp.dot(lhs_ref[...], w_ref[...], preferred_element_type=f32)


def _mm_out(a, b, c, x, w, tm):
    M = x.shape[0]
    tn = MM_TILE_N[w.shape]
    return pl.pallas_call(
        _mm_out_kernel,
        grid=(M // tm, D_MODEL // tn),
        in_specs=[pl.BlockSpec((tm, A_WIDTH), lambda i, j: (i, 0)),
                  pl.BlockSpec((tm, B_WIDTH), lambda i, j: (i, 0)),
                  pl.BlockSpec((tm, C_WIDTH), lambda i, j: (i, 0)),
                  pl.BlockSpec((tm, tn), lambda i, j: (i, j)),
                  pl.BlockSpec((D_MODEL, tn), lambda i, j: (0, j),
                               pipeline_mode=pl.Buffered(1 if tn == D_MODEL else 2))],
        out_specs=pl.BlockSpec((tm, tn), lambda i, j: (i, j)),
        out_shape=jax.ShapeDtypeStruct((M, D_MODEL), f32),
        scratch_shapes=[pltpu.VMEM((tm, D_MODEL), bf16)],
        compiler_params=_cparams(("parallel", "arbitrary")),
        name="mm_out",
    )(a, b, c, x, w)


def _mm_down_kernel(a_ref, x_ref, w_ref, o_ref):
    o_ref[...] = x_ref[...] + jnp.dot(a_ref[...], w_ref[...], preferred_element_type=f32)


def _mm_down(act, x, w, tm):
    M = x.shape[0]
    tn = MM_TILE_N[w.shape]
    return pl.pallas_call(
        _mm_down_kernel,
        grid=(M // tm, D_MODEL // tn),
        in_specs=[pl.BlockSpec((tm, D_FF), lambda i, j: (i, 0)),
                  pl.BlockSpec((tm, tn), lambda i, j: (i, j)),
                  pl.BlockSpec((D_FF, tn), lambda i, j: (0, j),
                               pipeline_mode=pl.Buffered(1 if tn == D_MODEL else 2))],
        out_specs=pl.BlockSpec((tm, tn), lambda i, j: (i, j)),
        out_shape=jax.ShapeDtypeStruct((M, D_MODEL), f32),
        compiler_params=_cparams(("parallel", "arbitrary")),
        name="mm_down",
    )(act, x, w)


def _silu(x):
    return x * jax.nn.sigmoid(x)


def _mm_up_prompt_kernel(x_ref, n2_ref, wg_ref, wa_ref, cwg_ref, cwa_ref, cbg_ref, cba_ref, src_ref,
                         act_ref, cs_ref, dst_ref, h_ref, carry_ref, zb_ref,
                         *, tm, nj, n_steps, tiles_per_seq):
    _cast_chunk(src_ref, dst_ref)
    s = pl.program_id(0)
    sa = jnp.minimum(s, n_steps - 1)
    ia = sa // nj
    ja = sa % nj
    tf = act_ref.shape[1]

    @pl.when(s == 0)
    def _():
        zb_ref[...] = jnp.zeros(zb_ref.shape, f32)
        carry_ref[...] = jnp.zeros(carry_ref.shape, f32)

    @pl.when(ja == 0)
    def _():
        h_ref[...] = _rms(x_ref[...], n2_ref[...]).astype(bf16)

    chunks = [slice(c0, c0 + UP_CHUNK) for c0 in range(0, tf, UP_CHUNK)]

    rblocks = [(r0, min(UP_ROWS, tm - r0)) for r0 in range(0, tm, UP_ROWS)]

    def conv(idx, cw_ref, cb_ref, cs, r0, nr):
        zz = zb_ref[idx, r0:r0 + SUBLANES + nr, cs]
        z1 = pltpu.roll(zz, 1, 0)[SUBLANES:]
        z2 = pltpu.roll(zz, 2, 0)[SUBLANES:]
        return (cb_ref[:, cs] + z2 * cw_ref[0:1, cs] + z1 * cw_ref[1:2, cs] + zz[SUBLANES:] * cw_ref[2:3, cs])

    for cs in chunks:
        for r0, nr in rblocks:
            g = conv(0, cwg_ref, cbg_ref, cs, r0, nr)
            a = conv(1, cwa_ref, cba_ref, cs, r0, nr)
            act_ref[r0:r0 + nr, cs] = (_silu(g) * a).astype(bf16)

    seq_start = (ia % tiles_per_seq) == 0
    for cs in chunks:
        for r0, nr in rblocks:
            for idx, w_ref in ((0, wg_ref), (1, wa_ref)):
                z = jnp.dot(h_ref[r0:r0 + nr, :], w_ref[:, cs], preferred_element_type=f32)
                if r0 == 0:
                    zb_ref[idx, 0:SUBLANES, cs] = jnp.where(seq_start, 0.0, carry_ref[idx, ja, :, cs])
                zb_ref[idx, SUBLANES + r0:SUBLANES + r0 + nr, cs] = z
                if r0 + nr == tm:
                    tail = z[nr - SUBLANES:nr, :]
                    carry_ref[idx, ja, :, cs] = tail
                    cs_ref[0, idx, :, cs] = tail


def _mm_up_prompt(x, norm2, w_up, conv_w, conv_b, layer, batch, tm, cast):
    M = x.shape[0]
    seq = M // batch
    tiles_per_seq = seq // tm
    tf = MM_TILE_N[w_up.shape]
    nj = D_FF // tf
    n_steps = (M // tm) * nj
    job = _cast_job(cast[0], cast[1], n_steps, lambda s: jnp.minimum(s, n_steps - 1))
    kern = functools.partial(_mm_up_prompt_kernel, tm=tm, nj=nj, n_steps=n_steps, tiles_per_seq=tiles_per_seq)
    ia = lambda s: jnp.minimum(s, n_steps - 1) // nj
    ja = lambda s: jnp.minimum(s, n_steps - 1) % nj
    ib = lambda s: jnp.maximum(s - 1, 0) // nj
    jb = lambda s: jnp.maximum(s - 1, 0) % nj
    return pl.pallas_call(
        kern,
        grid=(n_steps + 1,),
        in_specs=[pl.BlockSpec((tm, D_MODEL), lambda s: (ia(s), 0)),
                  pl.BlockSpec((None, 1, D_MODEL), lambda s: (layer, 0, 0)),
                  pl.BlockSpec((D_MODEL, tf), lambda s: (0, ja(s))),
                  pl.BlockSpec((D_MODEL, tf), lambda s: (0, nj + ja(s))),
                  pl.BlockSpec((None, CONV_W, tf), lambda s: (layer, 0, jb(s))),
                  pl.BlockSpec((None, CONV_W, tf), lambda s: (layer, 0, nj + jb(s))),
                  pl.BlockSpec((None, 1, tf), lambda s: (layer, 0, jb(s))),
                  pl.BlockSpec((None, 1, tf), lambda s: (layer, 0, nj + jb(s))),
                  job[0]],
        out_specs=(pl.BlockSpec((tm, tf), lambda s: (ib(s), jb(s))),
                   pl.BlockSpec((1, 2, SUBLANES, tf), lambda s: (ia(s), 0, 0, ja(s))),
                   job[1]),
        out_shape=(jax.ShapeDtypeStruct((M, D_FF), bf16),
                   jax.ShapeDtypeStruct((M // tm, 2, SUBLANES, D_FF), f32),
                   job[2]),
        scratch_shapes=[pltpu.VMEM((tm, D_MODEL), bf16),
                        pltpu.VMEM((2, nj, SUBLANES, tf), f32),
                        pltpu.VMEM((2, tm + SUBLANES, tf), f32)],
        compiler_params=_cparams(("arbitrary",)),
        name="mm_up_prompt",
    )(x, norm2, w_up, w_up, conv_w, conv_w, conv_b, conv_b, cast[0])


def _mm_up_sample_kernel(x_ref, n2_ref, wg_ref, wa_ref, cwg_ref, cwa_ref, cbg_ref, cba_ref,
                         b0g_ref, b1g_ref, b0a_ref, b1a_ref, act_ref, zg_ref, za_ref, h_ref):
    @pl.when(pl.program_id(0) == 0)
    def _():
        h_ref[...] = _rms(x_ref[...], n2_ref[...]).astype(bf16)

    def conv_half(w_ref, cw_ref, cb_ref, b0_ref, b1_ref, z_ref):
        z = jnp.dot(h_ref[...], w_ref[...], preferred_element_type=f32)
        z_ref[...] = z
        return (cb_ref[...] + b0_ref[...] * cw_ref[0:1, :] + b1_ref[...] * cw_ref[1:2, :]
                + z * cw_ref[2:3, :])

    g = conv_half(wg_ref, cwg_ref, cbg_ref, b0g_ref, b1g_ref, zg_ref)
    a = conv_half(wa_ref, cwa_ref, cba_ref, b0a_ref, b1a_ref, za_ref)
    act_ref[...] = (_silu(g) * a).astype(bf16)


def _mm_up_sample(x, norm2, w_up, conv_w, conv_b, buf, layer):
    M = x.shape[0]
    tf = UP_TILE_DECODE
    nj = D_FF // tf
    wspec = lambda off: pl.BlockSpec((D_MODEL, tf), lambda j: (0, off + j))
    cwspec = lambda off: pl.BlockSpec((None, CONV_W, tf), lambda j: (layer, 0, off + j))
    cbspec = lambda off: pl.BlockSpec((None, 1, tf), lambda j: (layer, 0, off + j))
    bufspec = lambda row, off: pl.BlockSpec((None, None, M, tf), lambda j: (layer, row, 0, off + j))
    return pl.pallas_call(
        _mm_up_sample_kernel,
        grid=(nj,),
        in_specs=[pl.BlockSpec((M, D_MODEL), lambda j: (0, 0)),
                  pl.BlockSpec((None, 1, D_MODEL), lambda j: (layer, 0, 0)),
                  wspec(0), wspec(nj), cwspec(0), cwspec(nj), cbspec(0), cbspec(nj),
                  bufspec(0, 0), bufspec(1, 0), bufspec(0, nj), bufspec(1, nj)],
        out_specs=(pl.BlockSpec((M, tf), lambda j: (0, j)),
                   pl.BlockSpec((M, tf), lambda j: (0, j)),
                   pl.BlockSpec((M, tf), lambda j: (0, j))),
        out_shape=(jax.ShapeDtypeStruct((M, D_FF), bf16),
                   jax.ShapeDtypeStruct((M, D_FF), f32),
                   jax.ShapeDtypeStruct((M, D_FF), f32)),
        scratch_shapes=[pltpu.VMEM((M, D_MODEL), bf16)],
        compiler_params=_cparams(("arbitrary",)),
        name="mm_up_sample",
    )(x, norm2, w_up, w_up, conv_w, conv_w, conv_b, conv_b, buf, buf, buf, buf)


def _mix_a_prompt_kernel(u_ref, v_ref, vg_ref, ws_ref, bs_ref, o_ref):
    row = lax.broadcasted_iota(jnp.int32, (CHUNK, CHUNK), 0)
    col = lax.broadcasted_iota(jnp.int32, (CHUNK, CHUNK), 1)
    tri = col <= row
    for g in range(A_GROUPS):
        sl = slice(g * A_DIM, (g + 1) * A_DIM)
        ws = jnp.where(tri, ws_ref[g], 0.0).astype(bf16)
        for r0 in range(0, u_ref.shape[0], CHUNK):
            rs = slice(r0, r0 + CHUNK)
            vn = _rms(_gelu(v_ref[rs, sl]), vg_ref[:, sl])
            mixv = jnp.dot(ws, vn.astype(bf16), preferred_element_type=f32) + bs_ref[:, sl]
            o_ref[rs, sl] = (_gelu(u_ref[rs, sl]) * mixv).astype(bf16)


def _mix_a_prompt(z, v_gain, w_s, b_s_full, layer, rows):
    M = z.shape[0]
    return pl.pallas_call(
        _mix_a_prompt_kernel,
        grid=(M // rows,),
        in_specs=[pl.BlockSpec((rows, A_WIDTH), lambda r: (r, Z_AU // A_WIDTH)),
                  pl.BlockSpec((rows, A_WIDTH), lambda r: (r, Z_AV // A_WIDTH)),
                  pl.BlockSpec((None, 1, A_WIDTH), lambda r: (layer, 0, 0)),
                  pl.BlockSpec((None, A_GROUPS, CHUNK, CHUNK), lambda r: (layer, 0, 0, 0)),
                  pl.BlockSpec((None, CHUNK, A_WIDTH), lambda r: (layer, 0, 0))],
        out_specs=pl.BlockSpec((rows, A_WIDTH), lambda r: (r, 0)),
        out_shape=jax.ShapeDtypeStruct((M, A_WIDTH), bf16),
        compiler_params=_cparams(("parallel",)),
        name="mix_a_prompt",
    )(z, z, v_gain, w_s, b_s_full)


def _mix_a_sample_kernel(u_ref, v_ref, vg_ref, ws_ref, bs_ref, o_ref, vn_ref):
    for g in range(A_GROUPS):
        sl = slice(g * A_DIM, (g + 1) * A_DIM)
        vn = _rms(_gelu(v_ref[:, sl]), vg_ref[:, sl])
        vn_ref[:, sl] = vn
        mixv = ws_ref[:, sl] * vn + bs_ref[:, sl]
        o_ref[:, sl] = (_gelu(u_ref[:, sl]) * mixv).astype(bf16)


def _mix_a_sample(z, v_gain, ws0, bs0, layer):
    M = z.shape[0]
    vec = pl.BlockSpec((None, 1, A_WIDTH), lambda r: (layer, 0, 0))
    return pl.pallas_call(
        _mix_a_sample_kernel,
        grid=(1,),
        in_specs=[pl.BlockSpec((M, A_WIDTH), lambda r: (0, Z_AU // A_WIDTH)),
                  pl.BlockSpec((M, A_WIDTH), lambda r: (0, Z_AV // A_WIDTH)),
                  vec, vec, vec],
        out_specs=(pl.BlockSpec((M, A_WIDTH), lambda r: (0, 0)),
                   pl.BlockSpec((M, A_WIDTH), lambda r: (0, 0))),
        out_shape=(jax.ShapeDtypeStruct((M, A_WIDTH), bf16),
                   jax.ShapeDtypeStruct((M, A_WIDTH), f32)),
        compiler_params=_cparams(("arbitrary",)),
        name="mix_a_sample",
    )(z, z, v_gain, ws0, bs0)


def _mix_b_prompt_kernel(sink_ref, q_ref, k_ref, v_ref, qg_ref, kg_ref, bias_ref, src_ref,
                         o_ref, klast_ref, vlast_ref, dst_ref, kcat_ref, vcat_ref):
    _cast_chunk(src_ref, dst_ref)
    i = pl.program_id(1)
    W = WINDOW
    kn = _norm_head_pairs(k_ref[...], kg_ref[...])
    v = v_ref[...]
    klast_ref[0] = kn
    vlast_ref[0] = v

    @pl.when(i == 0)
    def _():
        kcat_ref[0:W, :] = jnp.zeros((W, LANES), f32)
        vcat_ref[0:W, :] = jnp.zeros((W, LANES), f32)

    @pl.when(i > 0)
    def _():
        kcat_ref[0:W, :] = kcat_ref[W:2 * W, :]
        vcat_ref[0:W, :] = vcat_ref[W:2 * W, :]

    kcat_ref[W:2 * W, :] = kn
    vcat_ref[W:2 * W, :] = v
    kc = kcat_ref[...]
    vc = vcat_ref[...]
    kc_sw = pltpu.roll(kc, B_HEAD_DIM, 1)
    vc_sw = pltpu.roll(vc, B_HEAD_DIM, 1)
    lo = lax.broadcasted_iota(jnp.int32, (2 * W, LANES), 1) < B_HEAD_DIM

    cj = lax.broadcasted_iota(jnp.int32, (W, W), 0)
    qi = lax.broadcasted_iota(jnp.int32, (W, W), 1)
    own = cj <= qi
    keep = own | (i > 0)
    q_gain = qg_ref[...] * (B_HEAD_DIM ** -0.5)

    tiles = GQA // 2
    nt = (((1,), (1,)), ((), ()))
    tn = (((0,), (0,)), ((), ()))
    scores, values = [], []
    for kvh in range(B_KV_HEADS):
        k_src, k_alt = (kc, kc_sw) if kvh == 0 else (kc_sw, kc)
        v_src, v_alt = (vc, vc_sw) if kvh == 0 else (vc_sw, vc)
        k_even = jnp.where(lo, k_src, 0.0).astype(bf16)
        k_odd = jnp.where(lo, 0.0, k_alt).astype(bf16)
        values.append((jnp.where(lo, v_src, 0.0).astype(bf16), jnp.where(lo, 0.0, v_alt).astype(bf16)))
        qts = []
        for t in range(tiles):
            c0 = (kvh * tiles + t) * LANES
            qt = q_ref[:, c0:c0 + LANES].T
            q2 = qt * qt
            halves = []
            for r0 in (0, B_HEAD_DIM):
                ms = jnp.mean(q2[r0:r0 + B_HEAD_DIM, :], axis=0, keepdims=True)
                halves.append(qt[r0:r0 + B_HEAD_DIM, :] * lax.rsqrt(ms + EPS))
            qts.append(jnp.concatenate(halves, axis=0) * q_gain)
        qst = jnp.concatenate(qts, axis=1).astype(bf16)
        scores.append((jnp.dot(k_even, qst, preferred_element_type=f32),
                       jnp.dot(k_odd, qst, preferred_element_type=f32)))
    for kvh in range(B_KV_HEADS):
        p_par = []
        for par in range(2):
            blocks = []
            for t in range(tiles):
                h = kvh * GQA + 2 * t + par
                sb = scores[kvh][par][:, t * W:(t + 1) * W]
                s = jnp.where(own, sb[W:2 * W, :], sb[0:W, :]) + bias_ref[h]
                s = jnp.where(keep, s, NEG_INF)
                sk = sink_ref[h]
                mx = jnp.maximum(jnp.max(s, axis=0, keepdims=True), sk)
                p = jnp.exp(s - mx)
                den = jnp.sum(p, axis=0, keepdims=True) + jnp.exp(sk - mx)
                p = p * (1.0 / den)
                blocks.append(jnp.concatenate([jnp.where(own, 0.0, p), jnp.where(own, p, 0.0)],
                                              axis=0).astype(bf16))
            p_par.append(jnp.concatenate(blocks, axis=1))
        ot = (lax.dot_general(values[kvh][0], p_par[0], tn, preferred_element_type=f32)
              + lax.dot_general(values[kvh][1], p_par[1], tn, preferred_element_type=f32))
        for t in range(tiles):
            c0 = (kvh * tiles + t) * LANES
            o_ref[:, c0:c0 + LANES] = ot[:, t * W:(t + 1) * W].T.astype(bf16)


def _mix_b_prompt(z, sinks, q_gain_t, k_gain2, bias_p, layer, batch, cast):
    M = z.shape[0]
    nb = M // batch // WINDOW
    job = _cast_job(cast[0], cast[1], batch * nb, lambda b, i: b * nb + i)
    return pl.pallas_call(
        _mix_b_prompt_kernel,
        grid=(batch, nb),
        in_specs=[pl.BlockSpec(memory_space=pltpu.SMEM),
                  pl.BlockSpec((WINDOW, B_WIDTH), lambda b, i: (b * nb + i, Z_BQ // B_WIDTH)),
                  pl.BlockSpec((WINDOW, B_KV_WIDTH), lambda b, i: (b * nb + i, Z_BK // B_KV_WIDTH)),
                  pl.BlockSpec((WINDOW, B_KV_WIDTH), lambda b, i: (b * nb + i, Z_BV // B_KV_WIDTH)),
                  pl.BlockSpec((None, LANES, WINDOW), lambda b, i: (layer, 0, 0)),
                  pl.BlockSpec((None, 1, LANES), lambda b, i: (layer, 0, 0)),
                  pl.BlockSpec((B_HEADS, WINDOW, WINDOW), lambda b, i: (0, 0, 0)),
                  job[0]],
        out_specs=(pl.BlockSpec((WINDOW, B_WIDTH), lambda b, i: (b * nb + i, 0)),
                   pl.BlockSpec((1, WINDOW, B_KV_WIDTH), lambda b, i: (b, 0, 0)),
                   pl.BlockSpec((1, WINDOW, B_KV_WIDTH), lambda b, i: (b, 0, 0)),
                   job[1]),
        out_shape=(jax.ShapeDtypeStruct((M, B_WIDTH), bf16),
                   jax.ShapeDtypeStruct((batch, WINDOW, B_KV_WIDTH), f32),
                   jax.ShapeDtypeStruct((batch, WINDOW, B_KV_WIDTH), f32),
                   job[2]),
        scratch_shapes=[pltpu.VMEM((2 * WINDOW, LANES), f32),
                        pltpu.VMEM((2 * WINDOW, LANES), f32)],
        compiler_params=_cparams(("arbitrary", "arbitrary")),
        name="mix_b_prompt",
    )(sinks, z, z, z, q_gain_t, k_gain2, bias_p, cast[0])


def _mix_c_prompt_kernel(q_ref, k_ref, v0_ref, v1_ref, o0_ref, o1_ref, g_ref, gb_ref, hgt_ref,
                         out_ref, c_out, n_out, m_out, c_s, n_s, m_s):
    L = MLSTM_L
    c = pl.program_id(1)

    @pl.when(c == 0)
    def _():
        c_s[...] = jnp.zeros(c_s.shape, f32)
        n_s[...] = jnp.zeros(n_s.shape, f32)
        m_s[...] = jnp.zeros(m_s.shape, f32)

    row = lax.broadcasted_iota(jnp.int32, (L, L), 0)
    col = lax.broadcasted_iota(jnp.int32, (L, L), 1)
    tri = col <= row
    tri_f = tri.astype(f32)
    seqs = range(q_ref.shape[0])
    units = [(bi, h) for bi in seqs for h in range(C_HEADS)]
    g, b = {}, {}
    for bi in seqs:
        g[bi] = g_ref[bi] + gb_ref[...]
        b[bi] = jnp.dot(tri_f, jax.nn.log_sigmoid(g[bi]), precision=lax.Precision.HIGHEST,
                        preferred_element_type=f32)
    bT = {bi: b[bi].T for bi in seqs}
    vis = row <= col
    nt = (((1,), (1,)), ((), ()))
    tn = (((0,), (0,)), ((), ()))
    qb, kh, vb, ct_prev, n_prev, st, qct, qn_lin = {}, {}, {}, {}, {}, {}, {}, {}
    for u in units:
        bi, h = u
        v_ref = v0_ref if h < 2 else v1_ref
        vsl = slice((h % 2) * C_V_DIM, (h % 2 + 1) * C_V_DIM)
        qb[u] = q_ref[bi, :, h * C_QK_DIM:(h + 1) * C_QK_DIM].astype(bf16)
        kh[u] = k_ref[bi, :, h * C_QK_DIM:(h + 1) * C_QK_DIM] * (C_QK_DIM ** -0.5)
        vb[u] = v_ref[bi, :, vsl].astype(bf16)
        ct_prev[u] = c_s[bi, h]
        n_prev[u] = n_s[bi, h]
        st[u] = lax.dot_general(kh[u].astype(bf16), qb[u], nt, preferred_element_type=f32)
        qct[u] = lax.dot_general(ct_prev[u].astype(bf16), qb[u], nt, preferred_element_type=f32)
        n8 = jnp.broadcast_to(n_prev[u], (SUBLANES, C_QK_DIM)).astype(bf16)
        qn_lin[u] = lax.dot_general(n8, qb[u], nt, preferred_element_type=f32)[0:1, :]
    pt, iw, inv, kw, decay, m_new = {}, {}, {}, {}, {}, {}
    for u in units:
        bi, h = u
        b_row = bT[bi][C_HEADS + h:C_HEADS + h + 1, :]
        src = g[bi][:, h:h + 1] - b[bi][:, C_HEADS + h:C_HEADS + h + 1]
        dlog = jnp.where(vis, b_row + src, -jnp.inf)
        m_prev = m_s[bi, h][0:1, 0:1]
        inter = b_row + m_prev
        mt = jnp.maximum(inter, jnp.max(dlog, axis=0, keepdims=True))
        dt = jnp.exp(dlog - mt)
        pt[u] = st[u] * dt
        iw[u] = jnp.exp(inter - mt)
        qn = iw[u] * qn_lin[u] + jnp.sum(pt[u], axis=0, keepdims=True)
        inv[u] = 1.0 / jnp.maximum(jnp.abs(qn), jnp.exp(-mt))
        m_new[u] = mt[:, L - 1:L]
        decay[u] = jnp.exp(b_row[:, L - 1:L] + m_prev - m_new[u])
        kw[u] = kh[u] * dt[:, L - 1:L]
    svt = {u: lax.dot_general(vb[u], pt[u].astype(bf16), tn, preferred_element_type=f32) for u in units}
    kvt = {u: lax.dot_general(vb[u], kw[u].astype(bf16), tn, preferred_element_type=f32) for u in units}
    for u in units:
        bi, h = u
        o_ref = o0_ref if h < 2 else o1_ref
        vsl = slice((h % 2) * C_V_DIM, (h % 2 + 1) * C_V_DIM)
        ht = (iw[u] * qct[u] + svt[u]) * inv[u]
        ms = jnp.mean(ht * ht, axis=0, keepdims=True)
        hn = (ht * lax.rsqrt(ms + EPS) * hgt_ref[h]).T
        c_s[bi, h] = decay[u] * ct_prev[u] + kvt[u]
        n_s[bi, h] = decay[u] * n_prev[u] + jnp.sum(kw[u], axis=0, keepdims=True)
        m_s[bi, h] = jnp.broadcast_to(m_new[u], (SUBLANES, LANES))
        osl = slice(h * C_V_DIM, (h + 1) * C_V_DIM)
        out_ref[bi, :, osl] = (hn * jax.nn.sigmoid(o_ref[bi, :, vsl])).astype(bf16)

    @pl.when(c == pl.num_programs(1) - 1)
    def _():
        eye = (row == col).astype(f32)
        for bi, h in units:
            c_out[bi, h] = lax.dot_general(c_s[bi, h], eye, tn, precision=lax.Precision.HIGHEST,
                                           preferred_element_type=f32)
        n_out[...] = n_s[...]
        m_out[...] = m_s[...]


def _mix_c_prompt(z, gate_bias, h_gain, layer, batch):
    M = z.shape[0]
    L = MLSTM_L
    T = M // batch
    z3 = z.reshape(batch, T, Z_WIDTH)
    ns = C_SEQS if batch % C_SEQS == 0 else 1
    col = lambda width, off: pl.BlockSpec((ns, L, width), lambda b, c: (b, c, off // width))
    state = lambda *shape: pl.BlockSpec((ns,) + shape, lambda b, c: (b,) + (0,) * len(shape))
    out, c_st, n_st, m_st = pl.pallas_call(
        _mix_c_prompt_kernel,
        grid=(batch // ns, T // L),
        in_specs=[col(C_QK_WIDTH, Z_CQ), col(C_QK_WIDTH, Z_CK),
                  col(C_PAIR, Z_CV), col(C_PAIR, Z_CV + C_PAIR),
                  col(C_PAIR, Z_CO), col(C_PAIR, Z_CO + C_PAIR),
                  col(LANES, Z_GATE),
                  pl.BlockSpec((None, 1, LANES), lambda b, c: (layer, 0, 0)),
                  pl.BlockSpec((None, C_HEADS, C_V_DIM, L), lambda b, c: (layer, 0, 0, 0))],
        out_specs=(pl.BlockSpec((ns, L, C_WIDTH), lambda b, c: (b, c, 0)),
                   state(C_HEADS, C_QK_DIM, C_V_DIM),
                   state(C_HEADS, 1, C_QK_DIM),
                   state(C_HEADS, SUBLANES, LANES)),
        out_shape=(jax.ShapeDtypeStruct((batch, T, C_WIDTH), bf16),
                   jax.ShapeDtypeStruct((batch, C_HEADS, C_QK_DIM, C_V_DIM), f32),
                   jax.ShapeDtypeStruct((batch, C_HEADS, 1, C_QK_DIM), f32),
                   jax.ShapeDtypeStruct((batch, C_HEADS, SUBLANES, LANES), f32)),
        scratch_shapes=[pltpu.VMEM((ns, C_HEADS, C_V_DIM, C_QK_DIM), f32),
                        pltpu.VMEM((ns, C_HEADS, 1, C_QK_DIM), f32),
                        pltpu.VMEM((ns, C_HEADS, SUBLANES, LANES), f32)],
        compiler_params=_cparams(("arbitrary", "arbitrary")),
        name="mix_c_prompt",
    )(z3, z3, z3, z3, z3, z3, z3, gate_bias, h_gain)
    return out.reshape(M, C_WIDTH), c_st, n_st, m_st


_BC_SHARED = (4, 5, 6, 7, 12, 13)


def _mix_bc_sample_kernel(*refs):
    rows = [_mix_bc_sample_row(*[ref if k in _BC_SHARED else ref.at[pl.ds(r, 1)] for k, ref in enumerate(refs)])
            for r in range(refs[0].shape[0])]
    while rows:
        rows = [row for row in rows if next(row, True) is None]


def _mix_bc_sample_row(zr_ref, q_ref, kp_ref, vp_ref, qg_ref, kg_ref, sink_ref, bias_ref,
                       cq_ref, ck_ref, cv_ref, co_ref, gb_ref, hg_ref, c0_ref, n0_ref, m0_ref,
                       bo_ref, kn_ref, co_out, c_out, n_out, m_out):
    W = WINDOW
    zr = zr_ref[0]

    qn = _rms(q_ref[0], qg_ref[...])
    hrow = lax.broadcasted_iota(jnp.int32, (B_HEADS, LANES), 0)
    lane = lax.broadcasted_iota(jnp.int32, (B_HEADS, LANES), 1)
    own = (hrow < GQA) == (lane < B_HEAD_DIM)
    q2 = jnp.where(own, jnp.concatenate([qn, qn], axis=-1), 0.0)
    k_new = _norm_head_pairs(zr[:, Z_BK:Z_BK + B_KV_WIDTH], kg_ref[...])
    v_new = zr[:, Z_BV:Z_BV + B_KV_WIDTH]
    kn_ref[0] = k_new
    scale = B_HEAD_DIM ** -0.5
    s_past = lax.dot_general(q2.astype(bf16), kp_ref[0].astype(bf16), (((1,), (1,)), ((), ())),
                             preferred_element_type=f32) * scale + bias_ref[:, 0:W]
    yield
    s_past = jnp.where(lane >= 1, s_past, NEG_INF)
    s_new = jnp.sum(q2 * k_new, axis=-1, keepdims=True) * scale + bias_ref[:, W:W + 1]
    sk = sink_ref[...]
    mx = jnp.maximum(jnp.maximum(jnp.max(s_past, axis=-1, keepdims=True), s_new), sk)
    p_past = jnp.exp(s_past - mx)
    p_new = jnp.exp(s_new - mx)
    den = jnp.sum(p_past, axis=-1, keepdims=True) + p_new + jnp.exp(sk - mx)
    inv = 1.0 / den
    o2 = (jnp.dot((p_past * inv).astype(bf16), vp_ref[0].astype(bf16), preferred_element_type=f32)
          + (p_new * inv) * v_new)
    o2_sw = pltpu.roll(o2, B_HEAD_DIM, 1)
    bo_ref[0] = jnp.where(hrow < GQA, o2, o2_sw)[:, 0:B_HEAD_DIM].astype(bf16)

    g = zr[:, Z_GATE:Z_GATE + LANES] + gb_ref[...]
    lf = jax.nn.log_sigmoid(g)
    hrow4 = lax.broadcasted_iota(jnp.int32, (C_HEADS, LANES), 0)
    lane4 = lax.broadcasted_iota(jnp.int32, (C_HEADS, LANES), 1)

    def head_column(row_vec, first_lane):
        spread = jnp.broadcast_to(row_vec, (C_HEADS, LANES))
        return jnp.sum(jnp.where(lane4 == hrow4 + first_lane, spread, 0.0), axis=-1, keepdims=True)

    ig = head_column(g, 0)
    b = head_column(lf, C_HEADS)
    m0 = m0_ref[0]
    m_prev = jnp.sum(jnp.where(lane4[:, 0:C_HEADS] == hrow4[:, 0:C_HEADS],
                               jnp.broadcast_to(m0, (C_HEADS, C_HEADS)), 0.0), axis=-1, keepdims=True)
    inter = b + m_prev
    mt = jnp.maximum(inter, ig)
    q4 = cq_ref[0]
    k4 = ck_ref[0] * (C_QK_DIM ** -0.5)
    v4 = cv_ref[0]
    n4 = n0_ref[0]
    c0 = c0_ref[0]
    s = jnp.sum(q4 * k4, axis=-1, keepdims=True) * jnp.exp(ig - mt)
    iw = jnp.exp(inter - mt)
    blk_row = lax.broadcasted_iota(jnp.int32, (C_HEADS, C_QK_WIDTH), 0)
    blk_lane = lax.broadcasted_iota(jnp.int32, (C_HEADS, C_QK_WIDTH), 1) // C_QK_DIM
    pad4 = jnp.zeros((SUBLANES - C_HEADS, C_QK_WIDTH), f32)

    def head_blocks(x4):
        tiled = jnp.concatenate([x4] * C_HEADS, axis=-1)
        return jnp.concatenate([jnp.where(blk_lane == blk_row, tiled, 0.0), pad4], axis=0).astype(bf16)

    c0s = c0.reshape(C_QK_WIDTH, C_V_DIM)
    qc = jnp.dot(head_blocks(q4), c0s.astype(bf16), preferred_element_type=f32)[0:C_HEADS, :]
    yield
    num = iw * qc + s * v4
    qn_ = iw * jnp.sum(q4 * n4, axis=-1, keepdims=True) + s
    hh = num / jnp.maximum(jnp.abs(qn_), jnp.exp(-mt))
    decay = iw
    kw = k4 * jnp.exp(ig - mt)
    v8 = jnp.concatenate([v4, jnp.zeros((SUBLANES - C_HEADS, C_V_DIM), f32)], axis=0).astype(bf16)
    outer = lax.dot_general(head_blocks(kw), v8, (((0,), (0,)), ((), ())),
                            preferred_element_type=f32)
    for h in range(C_HEADS):
        c_out[0, h] = decay[h:h + 1, :] * c0[h] + outer[h * C_QK_DIM:(h + 1) * C_QK_DIM, :]
    n_out[0] = decay * n4 + kw
    m_out[0] = jnp.sum(jnp.where(lane4 == hrow4, jnp.broadcast_to(mt, (C_HEADS, LANES)), 0.0),
                       axis=0, keepdims=True)
    co_out[0] = (_rms(hh, hg_ref[...]) * jax.nn.sigmoid(co_ref[0])).astype(bf16)


def _mix_bc_sample(z, k_past, v_past, q_gain, k_gain2, sinks, bias_s, gate_bias, h_gain4,
                   c0, n0, m0, layer):
    R = z.shape[0]
    zr = z.reshape(R, 1, Z_WIDTH)
    q = z[:, Z_BQ:Z_BQ + B_WIDTH].reshape(R, B_HEADS, B_HEAD_DIM)
    cq = z[:, Z_CQ:Z_CQ + C_QK_WIDTH].reshape(R, C_HEADS, C_QK_DIM)
    ck = z[:, Z_CK:Z_CK + C_QK_WIDTH].reshape(R, C_HEADS, C_QK_DIM)
    cv = z[:, Z_CV:Z_CV + C_WIDTH].reshape(R, C_HEADS, C_V_DIM)
    co = z[:, Z_CO:Z_CO + C_WIDTH].reshape(R, C_HEADS, C_V_DIM)
    rb = DEC_ROWS if R % DEC_ROWS == 0 else 1
    row3 = lambda n: pl.BlockSpec((rb, 1, n), lambda r: (r, 0, 0))
    return pl.pallas_call(
        _mix_bc_sample_kernel,
        grid=(R // rb,),
        in_specs=[row3(Z_WIDTH),
                  pl.BlockSpec((rb, B_HEADS, B_HEAD_DIM), lambda r: (r, 0, 0)),
                  pl.BlockSpec((None, rb, WINDOW, B_KV_WIDTH), lambda r: (layer, r, 0, 0)),
                  pl.BlockSpec((None, rb, WINDOW, B_KV_WIDTH), lambda r: (layer, r, 0, 0)),
                  pl.BlockSpec((None, 1, B_HEAD_DIM), lambda r: (layer, 0, 0)),
                  pl.BlockSpec((None, 1, LANES), lambda r: (layer, 0, 0)),
                  pl.BlockSpec((None, B_HEADS, 1), lambda r: (layer, 0, 0)),
                  pl.BlockSpec((B_HEADS, 2 * WINDOW), lambda r: (0, 0)),
                  pl.BlockSpec((rb, C_HEADS, C_QK_DIM), lambda r: (r, 0, 0)),
                  pl.BlockSpec((rb, C_HEADS, C_QK_DIM), lambda r: (r, 0, 0)),
                  pl.BlockSpec((rb, C_HEADS, C_V_DIM), lambda r: (r, 0, 0)),
                  pl.BlockSpec((rb, C_HEADS, C_V_DIM), lambda r: (r, 0, 0)),
                  pl.BlockSpec((None, 1, LANES), lambda r: (layer, 0, 0)),
                  pl.BlockSpec((None, C_HEADS, C_V_DIM), lambda r: (layer, 0, 0)),
                  pl.BlockSpec((None, rb, C_HEADS, C_QK_DIM, C_V_DIM), lambda r: (layer, r, 0, 0, 0)),
                  pl.BlockSpec((None, rb, C_HEADS, C_QK_DIM), lambda r: (layer, r, 0, 0)),
                  pl.BlockSpec((None, rb, 1, C_HEADS), lambda r: (layer, r, 0, 0))],
        out_specs=(pl.BlockSpec((rb, B_HEADS, B_HEAD_DIM), lambda r: (r, 0, 0)),
                   row3(B_KV_WIDTH),
                   pl.BlockSpec((rb, C_HEADS, C_V_DIM), lambda r: (r, 0, 0)),
                   pl.BlockSpec((rb, C_HEADS, C_QK_DIM, C_V_DIM), lambda r: (r, 0, 0, 0)),
                   pl.BlockSpec((rb, C_HEADS, C_QK_DIM), lambda r: (r, 0, 0)),
                   row3(LANES)),
        out_shape=(jax.ShapeDtypeStruct((R, B_HEADS, B_HEAD_DIM), bf16),
                   jax.ShapeDtypeStruct((R, 1, B_KV_WIDTH), f32),
                   jax.ShapeDtypeStruct((R, C_HEADS, C_V_DIM), bf16),
                   jax.ShapeDtypeStruct((R, C_HEADS, C_QK_DIM, C_V_DIM), f32),
                   jax.ShapeDtypeStruct((R, C_HEADS, C_QK_DIM), f32),
                   jax.ShapeDtypeStruct((R, 1, LANES), f32)),
        compiler_params=_cparams(("parallel",)),
        name="mix_bc_sample",
    )(zr, q, k_past, v_past, q_gain, k_gain2, sinks, bias_s, cq, ck, cv, co, gate_bias, h_gain4,
      c0, n0, m0)


def _prep_w_in(w_in):
    pad = jnp.zeros(w_in.shape[:-1] + (Z_WIDTH - w_in.shape[-1],), w_in.dtype)
    return jnp.concatenate([w_in, pad], axis=-1).astype(bf16)


def _row_tile(m, pref):
    return pref if m % pref == 0 else m


def kernel(x_prompt, x_sample, cache_swa_k, cache_swa_v, state_mlstm_C, state_mlstm_n, state_mlstm_m, state_ffn_conv, rel_bias, norm1, w_in, a_v_gain, a_spatial_w, a_spatial_b, b_q_gain, b_k_gain, b_sinks, c_gate_bias, c_h_gain, w_out, norm2, w_up, ffn_conv_w, ffn_conv_b, w_down):
    depth = w_in.shape[0]
    Bp, T, _ = x_prompt.shape
    R = x_sample.shape[0]
    assert x_sample.shape[1] == 1 and T % CHUNK == 0 and T % MLSTM_L == 0

    w_in_p = _prep_w_in(w_in)
    norm1_3 = norm1.reshape(depth, 1, D_MODEL)
    norm2_3 = norm2.reshape(depth, 1, D_MODEL)
    v_gain3 = a_v_gain.reshape(depth, 1, A_WIDTH)
    bs_full = jnp.repeat(jnp.swapaxes(a_spatial_b, 1, 2), A_DIM, axis=-1)
    ws0 = jnp.repeat(a_spatial_w[:, :, 0, 0], A_DIM, axis=-1).reshape(depth, 1, A_WIDTH)
    bs0 = jnp.repeat(a_spatial_b[:, :, 0], A_DIM, axis=-1).reshape(depth, 1, A_WIDTH)
    q_gain3 = b_q_gain.reshape(depth, 1, B_HEAD_DIM)
    q_gain_t = jnp.broadcast_to(jnp.tile(b_q_gain, (1, 2)).reshape(depth, LANES, 1), (depth, LANES, WINDOW))
    k_gain2 = jnp.tile(b_k_gain, (1, 2)).reshape(depth, 1, LANES)
    sinks3 = b_sinks.reshape(depth, B_HEADS, 1)
    gate_b = jnp.pad(c_gate_bias, ((0, 0), (0, LANES - 2 * C_HEADS))).reshape(depth, 1, LANES)
    h_gain_t = jnp.broadcast_to(c_h_gain.reshape(depth, C_HEADS, C_V_DIM, 1), (depth, C_HEADS, C_V_DIM, MLSTM_L))
    h_gain4 = c_h_gain.reshape(depth, C_HEADS, C_V_DIM)
    conv_b3 = ffn_conv_b.reshape(depth, 1, 2 * D_FF)
    k_cache = cache_swa_k.reshape(depth, R, WINDOW, B_KV_WIDTH)
    v_cache = cache_swa_v.reshape(depth, R, WINDOW, B_KV_WIDTH)
    m_state = state_mlstm_m.reshape(depth, R, 1, C_HEADS)
    conv_hist = jnp.swapaxes(state_ffn_conv, 1, 2)

    bias_p, bias_s = _bias_tables(rel_bias)

    Mp = Bp * T
    tm = _row_tile(T, 1024)
    tm_io = _row_tile(T, 512)
    tm_down = _row_tile(T, 256)
    a_rows = _row_tile(T, 1024)
    xp = x_prompt.reshape(Mp, D_MODEL)
    xs = x_sample.reshape(R, D_MODEL)
    P = [[] for _ in range(6)]
    S = [[] for _ in range(7)]
    for l in range(depth):
        z, w_out_l = _mm_in(xp, norm1_3, w_in_p, l, tm_io, cast=(w_out, l))
        a_o = _mix_a_prompt(z, v_gain3, a_spatial_w, bs_full, l, a_rows)
        b_o, k_last, v_last, w_up_l = _mix_b_prompt(z, b_sinks[l], q_gain_t, k_gain2, bias_p, l, Bp, (w_up, l))
        c_o, c_st, n_st, m_st = _mix_c_prompt(z, gate_b, h_gain_t, l, Bp)
        x1 = _mm_out(a_o, b_o, c_o, xp, w_out_l, tm_io)
        act, cs, w_down_l = _mm_up_prompt(x1, norm2_3, w_up_l, ffn_conv_w, conv_b3, l, Bp, tm, (w_down, l))
        xp = _mm_down(act, x1, w_down_l, tm_down)
        P[0].append(k_last.reshape(Bp, WINDOW, B_KV_HEADS, B_HEAD_DIM))
        P[1].append(v_last.reshape(Bp, WINDOW, B_KV_HEADS, B_HEAD_DIM))
        P[2].append(c_st)
        P[3].append(n_st.reshape(Bp, C_HEADS, C_QK_DIM))
        P[4].append(m_st[:, :, 0, 0])
        seq_tiles = T // tm
        tail = cs[seq_tiles - 1::seq_tiles, :, SUBLANES - (CONV_W - 1):, :]
        P[5].append(jnp.swapaxes(tail, 1, 2).reshape(Bp, CONV_W - 1, 2 * D_FF))

        zs = _mm_in(xs, norm1_3, w_in_p, l, R)
        a_s, vn_s = _mix_a_sample(zs, v_gain3, ws0, bs0, l)
        b_s, kn_s, c_s, c_new, n_new, m_new = _mix_bc_sample(
            zs, k_cache, v_cache, q_gain3, k_gain2, sinks3, bias_s, gate_b, h_gain4,
            state_mlstm_C, state_mlstm_n, m_state, l)
        x1s = _mm_out(a_s, b_s.reshape(R, B_WIDTH), c_s.reshape(R, C_WIDTH), xs, w_out_l, R)
        act_s, zg_s, za_s = _mm_up_sample(x1s, norm2_3, w_up_l, ffn_conv_w, conv_b3, conv_hist, l)
        xs = _mm_down(act_s, x1s, w_down_l, R)
        S[0].append(vn_s.reshape(R, 1, A_WIDTH))
        S[1].append(kn_s.reshape(R, 1, B_KV_HEADS, B_HEAD_DIM))
        S[2].append(zs[:, Z_BV:Z_BV + B_KV_WIDTH].reshape(R, 1, B_KV_HEADS, B_HEAD_DIM))
        S[3].append(c_new)
        S[4].append(n_new)
        S[5].append(m_new[:, 0, 0:C_HEADS])
        z_new = jnp.concatenate([zg_s, za_s], axis=-1)
        S[6].append(jnp.stack([state_ffn_conv[l][:, CONV_W - 2], z_new], axis=1))

    st = lambda lst: jnp.stack(lst, axis=0)
    return (xp.reshape(Bp, T, D_MODEL), xs.reshape(R, 1, D_MODEL),
            st(P[0]), st(P[1]), st(S[1]), st(S[2]),
            st(P[2]), st(P[3]), st(P[4]),
            st(S[3]), st(S[4]), st(S[5]),
            st(P[5]), st(S[6]),
            st(S[0]))
```
